```python
import math
import jax, jax.numpy as jnp
from jax import lax
import numpy as np

D_MODEL = 2048
BATCH = 4
SEQ = 2048
DEPTH = 4
DEC_BATCH = 128
DEC_SEQ = 1
PAST_LEN = 16384
PAGE_SIZE = 128

N_EVEN = (DEPTH + 1) // 2
N_ODD = DEPTH // 2
D_FF = 5632
D_CONV = D_MODEL // 2
CONV_W = 3
MLSTM_HEADS = 4
MLSTM_DK = D_MODEL // 16
MLSTM_DV = D_MODEL // 8
MLSTM_CHUNK = 64
F_BIAS_LO = 3.0
F_BIAS_HI = 6.0
NEG_BIG = -1e30
D_POOL = D_MODEL // 2
POOL_WINDOWS = (2, 4, 8, 16)
POOL_GROUPS = 4
POOL_GW = D_POOL // POOL_GROUPS
POOL_BUF = max(POOL_WINDOWS) - 1
D_GMLP = D_MODEL // 2
GMLP_GROUPS = 4
GMLP_GW = D_GMLP // GMLP_GROUPS
GMLP_CHUNK = 128
ALPHA = (2 * DEPTH) ** 0.25
BETA = (8 * DEPTH) ** -0.25
LN_EPS = 1e-5
EVEN_SIZES = (D_CONV, D_CONV, D_CONV, MLSTM_HEADS * MLSTM_DK, MLSTM_HEADS * MLSTM_DK,
              MLSTM_HEADS * MLSTM_DV, MLSTM_HEADS * MLSTM_DV, MLSTM_HEADS, MLSTM_HEADS)
EVEN_IN = sum(EVEN_SIZES)
EVEN_MIX = D_CONV + MLSTM_HEADS * MLSTM_DV
ODD_SIZES = (D_POOL, D_GMLP, D_GMLP)
ODD_IN = sum(ODD_SIZES)
ODD_MIX = D_POOL + D_GMLP

kernel_name = 'hybrid_conv_mlstm_pool_gmlp_decoder_step'


def _split(a, sizes):
    idx = np.cumsum(sizes)[:-1].tolist()
    return jnp.split(a, idx, axis=-1)


def layer_norm(x, g, b=None):
    x32 = x.astype(jnp.float32)
    mu = x32.mean(-1, keepdims=True)
    var = jnp.square(x32 - mu).mean(-1, keepdims=True)
    y = (x32 - mu) * lax.rsqrt(var + LN_EPS) * g.astype(jnp.float32)
    if b is not None:
        y = y + b.astype(jnp.float32)
    return y.astype(x.dtype)


def swiglu(x, w_in, w_out):
    gate, up = jnp.split(x @ w_in, 2, axis=-1)
    return (jax.nn.silu(gate) * up) @ w_out


def mlstm_chunkwise(q, k, v, log_i, log_f, C0, n0, m0):
    B, S = q.shape[:2]
    L = min(MLSTM_CHUNK, S)
    NC = -(-S // L)
    pad = NC * L - S
    if pad:
        pw = ((0, 0), (0, pad), (0, 0), (0, 0))
        q, k, v = jnp.pad(q, pw), jnp.pad(k, pw), jnp.pad(v, pw)
        log_i = jnp.pad(log_i, ((0, 0), (0, pad), (0, 0)), constant_values=NEG_BIG)
        log_f = jnp.pad(log_f, ((0, 0), (0, pad), (0, 0)))

    def to_chunks(a):
        a = a.reshape((B, NC, L) + a.shape[2:])
        return jnp.moveaxis(jnp.moveaxis(a, 1, 0), 3, 2)

    causal = jnp.tril(jnp.ones((L, L), dtype=bool))

    def step(carry, xs):
        C, n, m = carry
        qc, kc, vc, li, lf = xs
        b = jnp.cumsum(lf, axis=-1)
        D = b[..., :, None] - b[..., None, :] + li[..., None, :]
        D = jnp.where(causal, D, -jnp.inf)
        inter = b + m[..., None]
        m_t = jnp.maximum(inter, D.max(-1))
        a_inter = jnp.exp(inter - m_t)
        W = jnp.exp(D - m_t[..., None]) * jnp.einsum('bhtd,bhsd->bhts', qc, kc)
        num = a_inter[..., None] * jnp.einsum('bhtd,bhde->bhte', qc, C) + jnp.einsum('bhts,bhse->bhte', W, vc)
        den = a_inter * jnp.einsum('bhtd,bhd->bht', qc, n) + W.sum(-1)
        h = num / jnp.maximum(jnp.abs(den), jnp.exp(-m_t))[..., None]
        m_new = m_t[..., -1]
        w_end = jnp.exp(b[..., -1:] - b + li - m_new[..., None])
        decay = jnp.exp(b[..., -1] + m - m_new)
        C_new = decay[..., None, None] * C + jnp.einsum('bhs,bhsd,bhse->bhde', w_end, kc, vc)
        n_new = decay[..., None] * n + jnp.einsum('bhs,bhsd->bhd', w_end, kc)
        return (C_new, n_new, m_new), h

    xs = (to_chunks(q), to_chunks(k), to_chunks(v), to_chunks(log_i), to_chunks(log_f))
    (C1, n1, m1), hs = lax.scan(step, (C0, n0, m0), xs)
    hs = jnp.swapaxes(jnp.moveaxis(hs, 0, 1), 2, 3).reshape(B, NC * L, q.shape[2], v.shape[-1])
    return hs[:, :S], C1, n1, m1


def even_mixer(x, conv_prefix, C0, n0, m0, w_in, b_gates, w_conv, mh_g, w_out):
    B, S, _ = x.shape
    H, DK, DV = MLSTM_HEADS, MLSTM_DK, MLSTM_DV
    f32 = jnp.float32
    bg, cg, xin, q, k, v, og, ig, fg = _split(x @ w_in, EVEN_SIZES)
    full = jnp.concatenate([conv_prefix.astype(x.dtype), cg * xin], axis=1)
    conv_out = sum(full[:, j:j + S] * w_conv[j] for j in range(CONV_W))
    y_conv = bg * conv_out
    conv_state = full[:, -(CONV_W - 1):]
    q = q.reshape(B, S, H, DK).astype(f32)
    k = k.reshape(B, S, H, DK).astype(f32) * (DK ** -0.5)
    v = v.reshape(B, S, H, DV).astype(f32)
    log_i = ig.astype(f32) + b_gates[0].astype(f32)
    log_f = jax.nn.log_sigmoid(fg.astype(f32) + b_gates[1].astype(f32))
    h, C1, n1, m1 = mlstm_chunkwise(q, k, v, log_i, log_f, C0.astype(f32), n0.astype(f32), m0.astype(f32))
    h = layer_norm(h, mh_g.reshape(H, DV))
    h = h.reshape(B, S, H * DV) * jax.nn.sigmoid(og.astype(f32))
    y = jnp.concatenate([y_conv, h.astype(x.dtype)], axis=-1) @ w_out
    return y, conv_state, C1.astype(C0.dtype), n1.astype(n0.dtype), m1.astype(m0.dtype)


def pool_mix(p, prefix, start_pos, w_pool, scale):
    B, S, _ = p.shape
    full = jnp.concatenate([prefix.astype(p.dtype), p], axis=1).astype(jnp.float32)
    cs = jnp.pad(jnp.cumsum(full, axis=1), ((0, 0), (1, 0), (0, 0)))
    pos = start_pos + jnp.arange(S)
    outs = []
    for g, w in enumerate(POOL_WINDOWS):
        sl = slice(g * POOL_GW, (g + 1) * POOL_GW)
        win = cs[:, POOL_BUF + 1:POOL_BUF + 1 + S, sl] - cs[:, POOL_BUF + 1 - w:POOL_BUF + 1 - w + S, sl]
        cnt = jnp.minimum(w, pos + 1).astype(jnp.float32)[None, :, None]
        outs.append(win / cnt)
    pooled = jnp.stack(outs, axis=2)
    diff = (pooled - p.reshape(B, S, POOL_GROUPS, POOL_GW).astype(jnp.float32)).astype(p.dtype)
    y = jnp.einsum('bsgc,gcd->bsgd', diff, w_pool).reshape(B, S, D_POOL) * scale
    return y, full[:, -POOL_BUF:].astype(p.dtype)


def gmlp_mix(u, v, w_s, b_s):
    B, S, _ = u.shape
    L = min(GMLP_CHUNK, S)
    NC = -(-S // L)
    pad = NC * L - S
    vp = jnp.pad(v, ((0, 0), (0, pad), (0, 0))).reshape(B, NC, L, GMLP_GROUPS, GMLP_GW)
    ws = jnp.where(jnp.tril(jnp.ones((L, L), dtype=bool)), w_s[:, :L, :L], 0)
    sv = jnp.einsum('gij,bnjgc->bnigc', ws, vp) + b_s[:, :L].T[None, None, :, :, None]
    return u * sv.reshape(B, NC * L, D_GMLP)[:, :S]


def odd_mixer(x, pool_prefix, start_pos, w_in, w_pool, pool_scale, gm_g, gm_b, w_s, b_s, w_out):
    B, S, _ = x.shape
    p, u, v = _split(x @ w_in, ODD_SIZES)
    y_pool, pool_state = pool_mix(p, pool_prefix, start_pos, w_pool, pool_scale)
    u = jax.nn.gelu(u)
    v = layer_norm(jax.nn.gelu(v), gm_g, gm_b)
    y_g = gmlp_mix(u, v, w_s, b_s)
    L = min(GMLP_CHUNK, S)
    v_rows = v[:, ((S - 1) // L) * L:]
    y = jnp.concatenate([y_pool.astype(x.dtype), y_g], axis=-1) @ w_out
    return y, pool_state, v_rows


def trunk(x, conv_st, C_st, n_st, m_st, pool_st, start_pos,
          ln_g, ln_b, w_ffn_in, w_ffn_out, w_in_even, b_gates_even, w_conv, mh_norm_g, w_out_even,
          w_in_odd, w_pool, pool_scale, gm_ln_g, gm_ln_b, w_spatial, b_spatial, w_out_odd):
    conv_new, C_new, n_new, m_new, pool_new, gv_new = [], [], [], [], [], []
    for layer in range(DEPTH):
        x = layer_norm(ALPHA * x + 0.5 * swiglu(x, w_ffn_in[layer, 0], w_ffn_out[layer, 0]),
                       ln_g[layer, 0], ln_b[layer, 0])
        j = layer // 2
        if layer % 2 == 0:
            mix, cs, C1, n1, m1 = even_mixer(x, conv_st[j], C_st[j], n_st[j], m_st[j], w_in_even[j],
                                             b_gates_even[j], w_conv[j], mh_norm_g[j], w_out_even[j])
            conv_new.append(cs)
            C_new.append(C1)
            n_new.append(n1)
            m_new.append(m1)
        else:
            mix, ps, gv = odd_mixer(x, pool_st[j], start_pos, w_in_odd[j], w_pool[j], pool_scale[j],
                                    gm_ln_g[j], gm_ln_b[j], w_spatial[j], b_spatial[j], w_out_odd[j])
            pool_new.append(ps)
            gv_new.append(gv)
        x = layer_norm(ALPHA * x + mix, ln_g[layer, 1], ln_b[layer, 1])
        x = layer_norm(ALPHA * x + 0.5 * swiglu(x, w_ffn_in[layer, 1], w_ffn_out[layer, 1]),
                       ln_g[layer, 2], ln_b[layer, 2])
    return (x, jnp.stack(conv_new), jnp.stack(C_new), jnp.stack(n_new), jnp.stack(m_new),
            jnp.stack(pool_new), jnp.stack(gv_new))


def setup_inputs(seed: int = 0) -> dict:
    key = jax.random.key(seed)
    ks = jax.random.split(key, 32)
    H, DK, DV = MLSTM_HEADS, MLSTM_DK, MLSTM_DV
    nrm = lambda k, shape, s=1.0: s * jax.random.normal(k, shape, jnp.float32)
    f_base = jnp.stack([jnp.zeros((H,), jnp.float32), jnp.linspace(F_BIAS_LO, F_BIAS_HI, H)])
    return {
        'x_prompt': nrm(ks[0], (BATCH, SEQ, D_MODEL)),
        'x_sample': nrm(ks[1], (DEC_BATCH, DEC_SEQ, D_MODEL)),
        'state_conv': nrm(ks[2], (N_EVEN, DEC_BATCH, CONV_W - 1, D_CONV)),
        'state_mlstm_C': nrm(ks[3], (N_EVEN, DEC_BATCH, H, DK, DV), 0.3),
        'state_mlstm_n': nrm(ks[4], (N_EVEN, DEC_BATCH, H, DK), 0.3),
        'state_mlstm_m': jax.random.uniform(ks[5], (N_EVEN, DEC_BATCH, H), jnp.float32, 0.0, 3.0),
        'state_pool': nrm(ks[6], (N_ODD, DEC_BATCH, POOL_BUF, D_POOL)),
        'ln_g': 1.0 + nrm(ks[7], (DEPTH, 3, D_MODEL), 0.02),
        'ln_b': nrm(ks[8], (DEPTH, 3, D_MODEL), 0.02),
        'w_ffn_in': nrm(ks[9], (DEPTH, 2, D_MODEL, 2 * D_FF), D_MODEL ** -0.5),
        'w_ffn_out': nrm(ks[10], (DEPTH, 2, D_FF, D_MODEL), BETA * D_FF ** -0.5),
        'w_in_even': nrm(ks[11], (N_EVEN, D_MODEL, EVEN_IN), D_MODEL ** -0.5),
        'b_gates_even': f_base[None] + nrm(ks[12], (N_EVEN, 2, H), 0.1),
        'w_conv': nrm(ks[13], (N_EVEN, CONV_W, D_CONV), CONV_W ** -0.5),
        'mh_norm_g': 1.0 + nrm(ks[14], (N_EVEN, H * DV), 0.02),
        'w_out_even': nrm(ks[15], (N_EVEN, EVEN_MIX, D_MODEL), BETA * EVEN_MIX ** -0.5),
        'w_in_odd': nrm(ks[16], (N_ODD, D_MODEL, ODD_IN), D_MODEL ** -0.5),
        'w_pool': nrm(ks[17], (N_ODD, POOL_GROUPS, POOL_GW, POOL_GW), POOL_GW ** -0.5),
        'pool_scale': 1.0 + nrm(ks[18], (N_ODD, D_POOL), 0.1),
        'gm_ln_g': 1.0 + nrm(ks[19], (N_ODD, D_GMLP), 0.02),
        'gm_ln_b': nrm(ks[20], (N_ODD, D_GMLP), 0.02),
        'w_spatial': nrm(ks[21], (N_ODD, GMLP_GROUPS, GMLP_CHUNK, GMLP_CHUNK), 0.05),
        'b_spatial': 1.0 + nrm(ks[22], (N_ODD, GMLP_GROUPS, GMLP_CHUNK), 0.1),
        'w_out_odd': nrm(ks[23], (N_ODD, ODD_MIX, D_MODEL), BETA * ODD_MIX ** -0.5),
    }


def reference(x_prompt, x_sample, state_conv, state_mlstm_C, state_mlstm_n, state_mlstm_m, state_pool,
              ln_g, ln_b, w_ffn_in, w_ffn_out, w_in_even, b_gates_even, w_conv, mh_norm_g, w_out_even,
              w_in_odd, w_pool, pool_scale, gm_ln_g, gm_ln_b, w_spatial, b_spatial, w_out_odd):
    Bp = x_prompt.shape[0]
    H, DK, DV = MLSTM_HEADS, MLSTM_DK, MLSTM_DV
    conv0 = jnp.zeros((N_EVEN, Bp, CONV_W - 1, D_CONV), x_prompt.dtype)
    C0 = jnp.zeros((N_EVEN, Bp, H, DK, DV), state_mlstm_C.dtype)
    n0 = jnp.zeros((N_EVEN, Bp, H, DK), state_mlstm_n.dtype)
    m0 = jnp.zeros((N_EVEN, Bp, H), state_mlstm_m.dtype)
    pool0 = jnp.zeros((N_ODD, Bp, POOL_BUF, D_POOL), x_prompt.dtype)
    y_prompt, conv_p, C_p, n_p, m_p, pool_p, gv_p = trunk(
        x_prompt, conv0, C0, n0, m0, pool0, 0,
        ln_g, ln_b, w_ffn_in, w_ffn_out, w_in_even, b_gates_even, w_conv, mh_norm_g, w_out_even,
        w_in_odd, w_pool, pool_scale, gm_ln_g, gm_ln_b, w_spatial, b_spatial, w_out_odd)
    y_sample, conv_s, C_s, n_s, m_s, pool_s, gv_s = trunk(
        x_sample, state_conv, state_mlstm_C, state_mlstm_n, state_mlstm_m, state_pool, PAST_LEN,
        ln_g, ln_b, w_ffn_in, w_ffn_out, w_in_even, b_gates_even, w_conv, mh_norm_g, w_out_even,
        w_in_odd, w_pool, pool_scale, gm_ln_g, gm_ln_b, w_spatial, b_spatial, w_out_odd)
    return (y_prompt, y_sample, conv_p, conv_s, C_p, C_s, n_p, n_s, m_p, m_s, pool_p, pool_s, gv_p, gv_s)
```

```python
import functools

import jax
import jax.numpy as jnp
from jax import lax
from jax.experimental import pallas as pl
from jax.experimental.pallas import tpu as pltpu

F32 = jnp.float32

D_MODEL = 2048
BATCH = 4
SEQ = 2048
DEPTH = 4
DEC_BATCH = 128
PAST_LEN = 16384
D_FF = 5632
D_CONV = 1024
CONV_W = 3
HEADS = 4
DK = 128
DV = 256
D_POOL = 1024
POOL_WINDOWS = (2, 4, 8, 16)
POOL_GW = 256
POOL_BUF = 15
D_GMLP = 1024
GMLP_GW = 256
GMLP_CHUNK = 128
ALPHA = (2 * DEPTH) ** 0.25
LN_EPS = 1e-5
EVEN_MAIN = 3 * D_CONV + 2 * HEADS * DK + 2 * HEADS * DV
ODD_IN = D_POOL + 2 * D_GMLP

MP = BATCH * SEQ
M = MP + DEC_BATCH

LANES = 128
SUBLANES = 8
VMEM_BYTES_V7X = 64 * 1024 * 1024

TM = 832
TF = 256
TN = 512
TK = 512
LC = 256
RO = 256
SB = 8
HIST = 16


def _vmem_limit(nbytes):
    return int(min(VMEM_BYTES_V7X - 4 * 1024 * 1024, nbytes))


def _layer_norm(y, g, b):
    mu = jnp.mean(y, axis=-1, keepdims=True)
    yc = y - mu
    var = jnp.mean(yc * yc, axis=-1, keepdims=True)
    out = yc * lax.rsqrt(var + LN_EPS) * g
    if b is not None:
        out = out + b
    return out


def _log_sigmoid(x):
    return -(jnp.maximum(-x, 0.0) + jnp.log1p(jnp.exp(-jnp.abs(x))))


def _dot(a, b):
    return jnp.dot(a, b, preferred_element_type=F32)


def _dot_nt(a, b):
    return lax.dot_general(a, b, (((1,), (1,)), ((), ())), preferred_element_type=F32)


def _ffn_kernel(x_ref, wg_ref, wu_ref, wo_ref, g_ref, b_ref, o_ref, *, nf):
    f = pl.program_id(1)
    x = x_ref[...]
    gate = _dot(x, wg_ref[...])
    up = _dot(x, wu_ref[...])
    h = gate * jax.nn.sigmoid(gate) * up
    part = _dot(h, wo_ref[...])

    @pl.when(f == 0)
    def _():
        o_ref[...] = part

    @pl.when(f > 0)
    def _():
        o_ref[...] += part

    @pl.when(f == nf - 1)
    def _():
        y = ALPHA * x_ref[...] + 0.5 * o_ref[...]
        o_ref[...] = _layer_norm(y, g_ref[...], b_ref[...])


def _ffn(x, w_ffn_in, w_ffn_out, ln_g, ln_b, layer, which, ln_idx):
    nf = D_FF // TF
    est = 4 * (4 * TM * D_MODEL + 2 * 3 * D_MODEL * TF + TM * D_MODEL + 4 * TM * TF) + (4 << 20)
    return pl.pallas_call(
        functools.partial(_ffn_kernel, nf=nf),
        grid=(M // TM, nf),
        in_specs=[
            pl.BlockSpec((TM, D_MODEL), lambda i, f: (i, 0)),
            pl.BlockSpec((None, None, D_MODEL, TF), lambda i, f: (layer, which, 0, f)),
            pl.BlockSpec((None, None, D_MODEL, TF), lambda i, f: (layer, which, 0, nf + f)),
            pl.BlockSpec((None, None, TF, D_MODEL), lambda i, f: (layer, which, f, 0)),
            pl.BlockSpec((None, 1, D_MODEL), lambda i, f: (ln_idx, 0, 0)),
            pl.BlockSpec((None, 1, D_MODEL), lambda i, f: (ln_idx, 0, 0)),
        ],
        out_specs=pl.BlockSpec((TM, D_MODEL), lambda i, f: (i, 0)),
        out_shape=jax.ShapeDtypeStruct((M, D_MODEL), F32),
        compiler_params=pltpu.CompilerParams(
            dimension_semantics=("parallel", "arbitrary"), vmem_limit_bytes=_vmem_limit(est)),
        name="ffn_ln",
    )(x, w_ffn_in, w_ffn_in, w_ffn_out, ln_g, ln_b)


def _proj_kernel(x_ref, w_ref, o_ref):
    o_ref[...] = _dot(x_ref[...], w_ref[...])


def _proj(x, w, layer_idx, n_cols, tn, name):
    est = 4 * (2 * TM * D_MODEL + 2 * D_MODEL * tn + 3 * TM * tn) + (4 << 20)
    return pl.pallas_call(
        _proj_kernel,
        grid=(M // TM, n_cols // tn),
        in_specs=[
            pl.BlockSpec((TM, D_MODEL), lambda i, n: (i, 0)),
            pl.BlockSpec((None, D_MODEL, tn), lambda i, n: (layer_idx, 0, n)),
        ],
        out_specs=pl.BlockSpec((TM, tn), lambda i, n: (i, n)),
        out_shape=jax.ShapeDtypeStruct((M, n_cols), F32),
        compiler_params=pltpu.CompilerParams(
            dimension_semantics=("parallel", "arbitrary"), vmem_limit_bytes=_vmem_limit(est)),
        name=name,
    )(x, w)


def _outproj_kernel(x_ref, a_ref, w_ref, g_ref, b_ref, o_ref, *, nk):
    k = pl.program_id(1)
    part = _dot(a_ref[...], w_ref[...])

    @pl.when(k == 0)
    def _():
        o_ref[...] = part

    @pl.when(k > 0)
    def _():
        o_ref[...] += part

    @pl.when(k == nk - 1)
    def _():
        y = ALPHA * x_ref[...] + o_ref[...]
        o_ref[...] = _layer_norm(y, g_ref[...], b_ref[...])


def _outproj(x, mix, w_out, ln_g, ln_b, layer_idx, ln_idx):
    nk = D_MODEL // TK
    est = 4 * (7 * TM * D_MODEL + 2 * TM * TK + 2 * TK * D_MODEL) + (4 << 20)
    return pl.pallas_call(
        functools.partial(_outproj_kernel, nk=nk),
        grid=(M // TM, nk),
        in_specs=[
            pl.BlockSpec((TM, D_MODEL), lambda i, k: (i, 0)),
            pl.BlockSpec((TM, TK), lambda i, k: (i, k)),
            pl.BlockSpec((None, TK, D_MODEL), lambda i, k: (layer_idx, k, 0)),
            pl.BlockSpec((None, 1, D_MODEL), lambda i, k: (ln_idx, 0, 0)),
            pl.BlockSpec((None, 1, D_MODEL), lambda i, k: (ln_idx, 0, 0)),
        ],
        out_specs=pl.BlockSpec((TM, D_MODEL), lambda i, k: (i, 0)),
        out_shape=jax.ShapeDtypeStruct((M, D_MODEL), F32),
        compiler_params=pltpu.CompilerParams(
            dimension_semantics=("parallel", "arbitrary"), vmem_limit_bytes=_vmem_limit(est)),
        name="outproj_ln",
    )(x, mix, w_out, ln_g, ln_b)


def _split3(x):
    h1 = x.astype(jnp.bfloat16).astype(F32)
    r = x - h1
    h2 = r.astype(jnp.bfloat16).astype(F32)
    return h1, h2, r - h2


def _even_prompt_kernel(bg_ref, cg_ref, xin_ref, q_ref, k_ref, v_ref, og_ref, gt_ref,
                        wc_ref, bgate_ref, mhg_ref,
                        mix_ref, conv_ref, c_out_ref, n_out_ref, m_out_ref,
                        cbuf, c_sc, n_sc, m_sc):
    c = pl.program_id(1)
    L = LC

    @pl.when(c == 0)
    def _():
        cbuf[0:SUBLANES, :] = jnp.zeros((SUBLANES, D_CONV), F32)
        c_sc[...] = jnp.zeros_like(c_sc)
        n_sc[...] = jnp.zeros_like(n_sc)
        m_sc[...] = jnp.zeros_like(m_sc)

    cx = cg_ref[...] * xin_ref[...]
    cbuf[SUBLANES:SUBLANES + L, :] = cx
    c1 = cbuf[SUBLANES - 1:SUBLANES - 1 + L, :]
    c2 = cbuf[SUBLANES - 2:SUBLANES - 2 + L, :]
    conv = c2 * wc_ref[0:1, :] + c1 * wc_ref[1:2, :] + cx * wc_ref[2:3, :]
    mix_ref[:, 0:D_CONV] = bg_ref[...] * conv
    conv_ref[...] = cbuf[SUBLANES + L - 2:SUBLANES + L, :]
    cbuf[0:SUBLANES, :] = cbuf[L:L + SUBLANES, :]

    gt = gt_ref[...]
    li_all = gt[:, 0:HEADS] + bgate_ref[0:1, :]
    lf_all = _log_sigmoid(gt[:, HEADS:2 * HEADS] + bgate_ref[1:2, :])
    row = lax.broadcasted_iota(jnp.int32, (L, L), 0)
    col = lax.broadcasted_iota(jnp.int32, (L, L), 1)
    causal = row >= col
    tril = causal.astype(F32)
    f1, f2, f3 = _split3(lf_all)
    b_all = _dot(tril, f1) + _dot(tril, f2) + _dot(tril, f3)
    z = jnp.concatenate([li_all, b_all, jnp.zeros((L, LANES - 2 * HEADS), F32)], axis=1)
    zt = z.T

    for h in range(HEADS):
        li_c = li_all[:, h:h + 1]
        b_c = b_all[:, h:h + 1]
        li_r = zt[h:h + 1, :]
        b_r = zt[HEADS + h:HEADS + h + 1, :]
        m0 = m_sc[h, 0:1, 0:1]
        dm = jnp.where(causal, b_c - b_r + li_r, -jnp.inf)
        inter = b_c + m0
        m_t = jnp.maximum(inter, jnp.max(dm, axis=-1, keepdims=True))
        a_int = jnp.exp(inter - m_t)
        qh = q_ref[:, h * DK:(h + 1) * DK]
        kh = k_ref[:, h * DK:(h + 1) * DK] * (DK ** -0.5)
        vh = v_ref[:, h * DV:(h + 1) * DV]
        w = jnp.exp(dm - m_t) * _dot_nt(qh, kh)
        ch = c_sc[h]
        n_row = n_sc[h:h + 1, :]
        num = a_int * _dot(qh, ch) + _dot(w, vh)
        den = a_int * jnp.sum(qh * n_row, axis=-1, keepdims=True) + jnp.sum(w, axis=-1, keepdims=True)
        hh = num / jnp.maximum(jnp.abs(den), jnp.exp(-m_t))
        m_new = m_t[L - 1:L, :]
        b_last = b_c[L - 1:L, :]
        w_end = jnp.exp(b_last - b_c + li_c - m_new)
        decay = jnp.exp(b_last + m0 - m_new)
        wk = w_end * kh
        c_new = decay * ch + _dot(wk.T, vh)
        n_new = decay * n_row + jnp.sum(wk, axis=0, keepdims=True)
        c_sc[h] = c_new
        n_sc[h:h + 1, :] = n_new
        m_sc[h] = jnp.broadcast_to(m_new, (SUBLANES, LANES))
        c_out_ref[h] = c_new
        n_out_ref[h:h + 1, :] = n_new
        m_out_ref[0:1, h:h + 1] = m_new
        hn = _layer_norm(hh, mhg_ref[0:1, h * DV:(h + 1) * DV], None)
        og = og_ref[:, h * DV:(h + 1) * DV]
        mix_ref[:, D_CONV + h * DV:D_CONV + (h + 1) * DV] = hn * jax.nn.sigmoid(og)


def _even_prompt(proj, gates, w_conv, b_gates, mh_g, j):
    nck = SEQ // LC
    r = lambda b, c: b * nck + c
    est = 4 * (2 * (5 * LC * 1024 + 2 * LC * 512 + LC * 128) + 2 * LC * 2048
               + 3 * HEADS * DK * DV + 16 * LC * LC) + (8 << 20)
    return pl.pallas_call(
        _even_prompt_kernel,
        grid=(BATCH, nck),
        in_specs=[
            pl.BlockSpec((LC, D_CONV), lambda b, c: (r(b, c), 0)),
            pl.BlockSpec((LC, D_CONV), lambda b, c: (r(b, c), 1)),
            pl.BlockSpec((LC, D_CONV), lambda b, c: (r(b, c), 2)),
            pl.BlockSpec((LC, HEADS * DK), lambda b, c: (r(b, c), 6)),
            pl.BlockSpec((LC, HEADS * DK), lambda b, c: (r(b, c), 7)),
            pl.BlockSpec((LC, HEADS * DV), lambda b, c: (r(b, c), 4)),
            pl.BlockSpec((LC, HEADS * DV), lambda b, c: (r(b, c), 5)),
            pl.BlockSpec((LC, LANES), lambda b, c: (r(b, c), 0)),
            pl.BlockSpec((None, CONV_W, D_CONV), lambda b, c: (j, 0, 0)),
            pl.BlockSpec((None, 2, HEADS), lambda b, c: (j, 0, 0)),
            pl.BlockSpec((None, 1, HEADS * DV), lambda b, c: (j, 0, 0)),
        ],
        out_specs=[
            pl.BlockSpec((LC, D_MODEL), lambda b, c: (r(b, c), 0)),
            pl.BlockSpec((None, CONV_W - 1, D_CONV), lambda b, c: (b, 0, 0)),
            pl.BlockSpec((None, HEADS, DK, DV), lambda b, c: (b, 0, 0, 0)),
            pl.BlockSpec((None, HEADS, DK), lambda b, c: (b, 0, 0)),
            pl.BlockSpec((None, 1, HEADS), lambda b, c: (b, 0, 0)),
        ],
        out_shape=[
            jax.ShapeDtypeStruct((M, D_MODEL), F32),
            jax.ShapeDtypeStruct((BATCH, CONV_W - 1, D_CONV), F32),
            jax.ShapeDtypeStruct((BATCH, HEADS, DK, DV), F32),
            jax.ShapeDtypeStruct((BATCH, HEADS, DK), F32),
            jax.ShapeDtypeStruct((BATCH, 1, HEADS), F32),
        ],
        scratch_shapes=[
            pltpu.VMEM((LC + SUBLANES, D_CONV), F32),
            pltpu.VMEM((HEADS, DK, DV), F32),
            pltpu.VMEM((HEADS, DK), F32),
            pltpu.VMEM((HEADS, SUBLANES, LANES), F32),
        ],
        compiler_params=pltpu.CompilerParams(
            dimension_semantics=("arbitrary", "arbitrary"), vmem_limit_bytes=_vmem_limit(est)),
        name="even_prompt",
    )(proj, proj, proj, proj, proj, proj, proj, gates, w_conv, b_gates, mh_g)


def _sample_gates(gt, bgate_ref, m):
    li = gt[:, 0:HEADS] + bgate_ref[0:1, :]
    lf = _log_sigmoid(gt[:, HEADS:2 * HEADS] + bgate_ref[1:2, :])
    inter = lf + m
    m_t = jnp.maximum(inter, li)
    return jnp.exp(inter - m_t), jnp.exp(li - m_t), m_t


def _even_sample_state_kernel(c_ref, qcol_ref, kcol_ref, v_ref, gt_ref, m_ref, bgate_ref, *rest):
    c_out_ref, num_ref = rest[-2], rest[-1]
    a, wgt, _ = _sample_gates(gt_ref[...], bgate_ref, m_ref[...])
    for bi in range(SB):
        for h in range(HEADS):
            a_s = a[bi:bi + 1, h:h + 1]
            w_s = wgt[bi:bi + 1, h:h + 1]
            ch = c_ref[bi, h]
            kc = kcol_ref[bi, h] * (DK ** -0.5)
            qc = qcol_ref[bi, h]
            vr = v_ref[bi:bi + 1, h * DV:(h + 1) * DV]
            c_out_ref[bi, h] = a_s * ch + (w_s * kc) * vr
            num_ref[bi:bi + 1, h * DV:(h + 1) * DV] = jnp.sum(qc * ch, axis=0, keepdims=True)


def _even_sample_state(state_c, qcol, kcol, v_s, gates_s, state_m, b_gates, j, c_prev):
    blk5 = (None, SB, HEADS, DK, DV)
    in_specs = [
        pl.BlockSpec(blk5, lambda i: (j, i, 0, 0, 0)),
        pl.BlockSpec((SB, HEADS, DK, 1), lambda i: (i, 0, 0, 0)),
        pl.BlockSpec((SB, HEADS, DK, 1), lambda i: (i, 0, 0, 0)),
        pl.BlockSpec((SB, HEADS * DV), lambda i: (i, 0)),
        pl.BlockSpec((SB, LANES), lambda i: (i, 0)),
        pl.BlockSpec((None, SB, HEADS), lambda i: (j, i, 0)),
        pl.BlockSpec((None, 2, HEADS), lambda i: (j, 0, 0)),
    ]
    args = [state_c, qcol, kcol, v_s, gates_s, state_m, b_gates]
    aliases = {}
    if c_prev is not None:
        in_specs.append(pl.BlockSpec(memory_space=pl.ANY))
        args.append(c_prev)
        aliases = {len(args) - 1: 0}
    est = 4 * (4 * SB * HEADS * DK * DV + 4 * SB * HEADS * DK * LANES) + (8 << 20)
    return pl.pallas_call(
        _even_sample_state_kernel,
        grid=(DEC_BATCH // SB,),
        in_specs=in_specs,
        out_specs=[
            pl.BlockSpec(blk5, lambda i: (j, i, 0, 0, 0)),
            pl.BlockSpec((SB, HEADS * DV), lambda i: (i, 0)),
        ],
        out_shape=[
            jax.ShapeDtypeStruct(state_c.shape, F32),
            jax.ShapeDtypeStruct((DEC_BATCH, HEADS * DV), F32),
        ],
        input_output_aliases=aliases,
        compiler_params=pltpu.CompilerParams(
            dimension_semantics=("arbitrary",), vmem_limit_bytes=_vmem_limit(est)),
        name="even_sample_state",
    )(*args)


def _even_sample_kernel(bg_ref, cg_ref, xin_ref, q_ref, k_ref, v_ref, og_ref, gt_ref,
                        num_ref, cst_ref, n_ref, m_ref, wc_ref, bgate_ref, mhg_ref, mixin_hbm,
                        mix_ref, conv_ref, n_out_ref, m_out_ref):
    del mixin_hbm
    cx = cg_ref[...] * xin_ref[...]
    st0 = cst_ref[:, 0:D_CONV]
    st1 = cst_ref[:, D_CONV:2 * D_CONV]
    conv = st0 * wc_ref[0:1, :] + st1 * wc_ref[1:2, :] + cx * wc_ref[2:3, :]
    mix_ref[:, 0:D_CONV] = bg_ref[...] * conv
    conv_ref[:, 0:D_CONV] = st1
    conv_ref[:, D_CONV:2 * D_CONV] = cx

    a, wgt, m_t = _sample_gates(gt_ref[...], bgate_ref, m_ref[...])
    m_out_ref[...] = m_t
    floor = jnp.exp(-m_t)
    for h in range(HEADS):
        a_h = a[:, h:h + 1]
        w_h = wgt[:, h:h + 1]
        qh = q_ref[:, h * DK:(h + 1) * DK]
        kh = k_ref[:, h * DK:(h + 1) * DK] * (DK ** -0.5)
        vh = v_ref[:, h * DV:(h + 1) * DV]
        nh = n_ref[:, h * DK:(h + 1) * DK]
        wt = w_h * jnp.sum(qh * kh, axis=-1, keepdims=True)
        num = a_h * num_ref[:, h * DV:(h + 1) * DV] + wt * vh
        den = a_h * jnp.sum(qh * nh, axis=-1, keepdims=True) + wt
        hh = num / jnp.maximum(jnp.abs(den), floor[:, h:h + 1])
        n_out_ref[:, h * DK:(h + 1) * DK] = a_h * nh + w_h * kh
        hn = _layer_norm(hh, mhg_ref[0:1, h * DV:(h + 1) * DV], None)
        og = og_ref[:, h * DV:(h + 1) * DV]
        mix_ref[:, D_CONV + h * DV:D_CONV + (h + 1) * DV] = hn * jax.nn.sigmoid(og)


def _even_sample(proj, gates, num, conv_st, n_st, m_st, w_conv, b_gates, mh_g, mixin, j):
    nb = DEC_BATCH
    rb = MP // nb
    est = 4 * 2 * (5 * nb * 1024 + 2 * nb * 512 + nb * 128 + nb * 1024 + nb * 2048 + nb * 512
                   + nb * 2048 + nb * 2048 + nb * 512) + (8 << 20)
    return pl.pallas_call(
        _even_sample_kernel,
        grid=(1,),
        in_specs=[
            pl.BlockSpec((nb, D_CONV), lambda i: (rb, 0)),
            pl.BlockSpec((nb, D_CONV), lambda i: (rb, 1)),
            pl.BlockSpec((nb, D_CONV), lambda i: (rb, 2)),
            pl.BlockSpec((nb, HEADS * DK), lambda i: (rb, 6)),
            pl.BlockSpec((nb, HEADS * DK), lambda i: (rb, 7)),
            pl.BlockSpec((nb, HEADS * DV), lambda i: (rb, 4)),
            pl.BlockSpec((nb, HEADS * DV), lambda i: (rb, 5)),
            pl.BlockSpec((nb, LANES), lambda i: (rb, 0)),
            pl.BlockSpec((nb, HEADS * DV), lambda i: (0, 0)),
            pl.BlockSpec((None, nb, 2 * D_CONV), lambda i: (j, 0, 0)),
            pl.BlockSpec((None, nb, HEADS * DK), lambda i: (j, 0, 0)),
            pl.BlockSpec((None, nb, HEADS), lambda i: (j, 0, 0)),
            pl.BlockSpec((None, CONV_W, D_CONV), lambda i: (j, 0, 0)),
            pl.BlockSpec((None, 2, HEADS), lambda i: (j, 0, 0)),
            pl.BlockSpec((None, 1, HEADS * DV), lambda i: (j, 0, 0)),
            pl.BlockSpec(memory_space=pl.ANY),
        ],
        out_specs=[
            pl.BlockSpec((nb, D_MODEL), lambda i: (rb, 0)),
            pl.BlockSpec((nb, 2 * D_CONV), lambda i: (0, 0)),
            pl.BlockSpec((nb, HEADS * DK), lambda i: (0, 0)),
            pl.BlockSpec((nb, HEADS), lambda i: (0, 0)),
        ],
        out_shape=[
            jax.ShapeDtypeStruct((M, D_MODEL), F32),
            jax.ShapeDtypeStruct((nb, 2 * D_CONV), F32),
            jax.ShapeDtypeStruct((nb, HEADS * DK), F32),
            jax.ShapeDtypeStruct((nb, HEADS), F32),
        ],
        input_output_aliases={15: 0},
        compiler_params=pltpu.CompilerParams(
            dimension_semantics=("arbitrary",), vmem_limit_bytes=_vmem_limit(est)),
        name="even_sample",
    )(proj, proj, proj, proj, proj, proj, proj, gates, num, conv_st, n_st, m_st,
      w_conv, b_gates, mh_g, mixin)


def _gmlp_norm(u_raw, v_raw, g, b):
    u = jax.nn.gelu(u_raw)
    vn = _layer_norm(jax.nn.gelu(v_raw), g, b)
    return u, vn


def _odd_prompt_kernel(p_ref, u_ref, v_ref, wp_ref, sc_ref, gmg_ref, gmb_ref, ws_ref, bst_ref,
                       mix_ref, pool_ref, gv_ref, pbuf):
    s = pl.program_id(1)
    R = RO

    @pl.when(s == 0)
    def _():
        pbuf[0:HIST, :] = jnp.zeros((HIST, D_POOL), F32)

    p = p_ref[...]
    pbuf[HIST:HIST + R, :] = p
    pos = s * R + lax.broadcasted_iota(jnp.int32, (R, 1), 0)
    for g, w in enumerate(POOL_WINDOWS):
        lo, hi = g * POOL_GW, (g + 1) * POOL_GW
        win = p[:, lo:hi]
        for jj in range(1, w):
            win = win + pbuf[HIST - jj:HIST - jj + R, lo:hi]
        cnt = jnp.minimum(w, pos + 1).astype(F32)
        diff = win / cnt - p[:, lo:hi]
        mix_ref[:, lo:hi] = _dot(diff, wp_ref[g]) * sc_ref[0:1, lo:hi]
    pool_ref[...] = pbuf[HIST + R - POOL_BUF:HIST + R, :]
    pbuf[0:HIST, :] = pbuf[R:R + HIST, :]

    u, vn = _gmlp_norm(u_ref[...], v_ref[...], gmg_ref[...], gmb_ref[...])
    L = GMLP_CHUNK
    tril = lax.broadcasted_iota(jnp.int32, (L, L), 0) >= lax.broadcasted_iota(jnp.int32, (L, L), 1)
    for g in range(D_GMLP // GMLP_GW):
        lo, hi = g * GMLP_GW, (g + 1) * GMLP_GW
        ws = jnp.where(tril, ws_ref[g], 0.0)
        bcol = bst_ref[:, g:g + 1]
        for ck in range(R // L):
            r0, r1 = ck * L, (ck + 1) * L
            sv = _dot(ws, vn[r0:r1, lo:hi]) + bcol
            mix_ref[r0:r1, D_POOL + lo:D_POOL + hi] = u[r0:r1, lo:hi] * sv
    gv_ref[...] = vn[R - L:R, :]


def _odd_prompt(proj, w_pool, pool_scale, gm_g, gm_b, w_spatial, bs_t, j):
    nrb = SEQ // RO
    r = lambda b, s: b * nrb + s
    est = 4 * (2 * 3 * RO * 1024 + 2 * RO * 2048 + 2 * 4 * 256 * 256 + 2 * 4 * 128 * 128
               + (RO + HIST) * 1024 + 8 * RO * 1024) + (8 << 20)
    return pl.pallas_call(
        _odd_prompt_kernel,
        grid=(BATCH, nrb),
        in_specs=[
            pl.BlockSpec((RO, D_POOL), lambda b, s: (r(b, s), 0)),
            pl.BlockSpec((RO, D_GMLP), lambda b, s: (r(b, s), 1)),
            pl.BlockSpec((RO, D_GMLP), lambda b, s: (r(b, s), 2)),
            pl.BlockSpec((None, 4, POOL_GW, POOL_GW), lambda b, s: (j, 0, 0, 0)),
            pl.BlockSpec((None, 1, D_POOL), lambda b, s: (j, 0, 0)),
            pl.BlockSpec((None, 1, D_GMLP), lambda b, s: (j, 0, 0)),
            pl.BlockSpec((None, 1, D_GMLP), lambda b, s: (j, 0, 0)),
            pl.BlockSpec((None, 4, GMLP_CHUNK, GMLP_CHUNK), lambda b, s: (j, 0, 0, 0)),
            pl.BlockSpec((None, GMLP_CHUNK, 4), lambda b, s: (j, 0, 0)),
        ],
        out_specs=[
            pl.BlockSpec((RO, D_MODEL), lambda b, s: (r(b, s), 0)),
            pl.BlockSpec((None, POOL_BUF, D_POOL), lambda b, s: (b, 0, 0)),
            pl.BlockSpec((None, GMLP_CHUNK, D_GMLP), lambda b, s: (b, 0, 0)),
        ],
        out_shape=[
            jax.ShapeDtypeStruct((M, D_MODEL), F32),
            jax.ShapeDtypeStruct((BATCH, POOL_BUF, D_POOL), F32),
            jax.ShapeDtypeStruct((BATCH, GMLP_CHUNK, D_GMLP), F32),
        ],
        scratch_shapes=[pltpu.VMEM((RO + HIST, D_POOL), F32)],
        compiler_params=pltpu.CompilerParams(
            dimension_semantics=("arbitrary", "arbitrary"), vmem_limit_bytes=_vmem_limit(est)),
        name="odd_prompt",
    )(proj, proj, proj, w_pool, pool_scale, gm_g, gm_b, w_spatial, bs_t)


def _odd_sample_pool_kernel(p_ref, st_ref, diff_ref, pool_ref):
    for bi in range(SB):
        prow = p_ref[bi:bi + 1, :]
        pool_ref[bi, 0:POOL_BUF - 1, :] = st_ref[bi, 1:POOL_BUF, :]
        pool_ref[bi, POOL_BUF - 1:POOL_BUF, :] = prow
        for g, w in enumerate(POOL_WINDOWS):
            lo, hi = g * POOL_GW, (g + 1) * POOL_GW
            win = prow[:, lo:hi] + jnp.sum(st_ref[bi, POOL_BUF - (w - 1):POOL_BUF, lo:hi],
                                           axis=0, keepdims=True)
            cnt = float(min(w, PAST_LEN + 1))
            diff_ref[bi:bi + 1, lo:hi] = win / cnt - prow[:, lo:hi]


def _odd_sample_pool(proj, state_pool, j):
    rb = MP // SB
    return pl.pallas_call(
        _odd_sample_pool_kernel,
        grid=(DEC_BATCH // SB,),
        in_specs=[
            pl.BlockSpec((SB, D_POOL), lambda i: (rb + i, 0)),
            pl.BlockSpec((None, SB, POOL_BUF, D_POOL), lambda i: (j, i, 0, 0)),
        ],
        out_specs=[
            pl.BlockSpec((SB, D_POOL), lambda i: (i, 0)),
            pl.BlockSpec((SB, POOL_BUF, D_POOL), lambda i: (i, 0, 0)),
        ],
        out_shape=[
            jax.ShapeDtypeStruct((DEC_BATCH, D_POOL), F32),
            jax.ShapeDtypeStruct((DEC_BATCH, POOL_BUF, D_POOL), F32),
        ],
        compiler_params=pltpu.CompilerParams(dimension_semantics=("arbitrary",)),
        name="odd_sample_pool",
    )(proj, state_pool)


def _odd_sample_kernel(diff_ref, u_ref, v_ref, wp_ref, sc_ref, gmg_ref, gmb_ref, ws_ref, bst_ref,
                       mixin_hbm, mix_ref, gv_ref):
    del mixin_hbm
    diff = diff_ref[...]
    for g in range(len(POOL_WINDOWS)):
        lo, hi = g * POOL_GW, (g + 1) * POOL_GW
        mix_ref[:, lo:hi] = _dot(diff[:, lo:hi], wp_ref[g]) * sc_ref[0:1, lo:hi]
    u, vn = _gmlp_norm(u_ref[...], v_ref[...], gmg_ref[...], gmb_ref[...])
    gv_ref[...] = vn
    for g in range(D_GMLP // GMLP_GW):
        lo, hi = g * GMLP_GW, (g + 1) * GMLP_GW
        sv = ws_ref[g, 0:1, 0:1] * vn[:, lo:hi] + bst_ref[0:1, g:g + 1]
        mix_ref[:, D_POOL + lo:D_POOL + hi] = u[:, lo:hi] * sv


def _odd_sample(diff, proj, w_pool, pool_scale, gm_g, gm_b, w_spatial, bs_t, mixin, j):
    nb = DEC_BATCH
    rb = MP // nb
    est = 4 * 2 * (3 * nb * 1024 + 4 * 256 * 256 + 4 * 128 * 128 + nb * 2048 + nb * 1024) + (8 << 20)
    return pl.pallas_call(
        _odd_sample_kernel,
        grid=(1,),
        in_specs=[
            pl.BlockSpec((nb, D_POOL), lambda i: (0, 0)),
            pl.BlockSpec((nb, D_GMLP), lambda i: (rb, 1)),
            pl.BlockSpec((nb, D_GMLP), lambda i: (rb, 2)),
            pl.BlockSpec((None, 4, POOL_GW, POOL_GW), lambda i: (j, 0, 0, 0)),
            pl.BlockSpec((None, 1, D_POOL), lambda i: (j, 0, 0)),
            pl.BlockSpec((None, 1, D_GMLP), lambda i: (j, 0, 0)),
            pl.BlockSpec((None, 1, D_GMLP), lambda i: (j, 0, 0)),
            pl.BlockSpec((None, 4, GMLP_CHUNK, GMLP_CHUNK), lambda i: (j, 0, 0, 0)),
            pl.BlockSpec((None, GMLP_CHUNK, 4), lambda i: (j, 0, 0)),
            pl.BlockSpec(memory_space=pl.ANY),
        ],
        out_specs=[
            pl.BlockSpec((nb, D_MODEL), lambda i: (rb, 0)),
            pl.BlockSpec((nb, D_GMLP), lambda i: (0, 0)),
        ],
        out_shape=[
            jax.ShapeDtypeStruct((M, D_MODEL), F32),
            jax.ShapeDtypeStruct((nb, D_GMLP), F32),
        ],
        input_output_aliases={9: 0},
        compiler_params=pltpu.CompilerParams(
            dimension_semantics=("arbitrary",), vmem_limit_bytes=_vmem_limit(est)),
        name="odd_sample",
    )(diff, proj, proj, w_pool, pool_scale, gm_g, gm_b, w_spatial, bs_t, mixin)


def kernel(x_prompt, x_sample, state_conv, state_mlstm_C, state_mlstm_n, state_mlstm_m, state_pool,
           ln_g, ln_b, w_ffn_in, w_ffn_out, w_in_even, b_gates_even, w_conv, mh_norm_g, w_out_even,
           w_in_odd, w_pool, pool_scale, gm_ln_g, gm_ln_b, w_spatial, b_spatial, w_out_odd):
    n_even, n_odd = w_in_even.shape[0], w_in_odd.shape[0]
    x = jnp.concatenate([x_prompt.reshape(MP, D_MODEL), x_sample.reshape(DEC_BATCH, D_MODEL)], axis=0)

    ln_g3 = ln_g.reshape(DEPTH * 3, 1, D_MODEL)
    ln_b3 = ln_b.reshape(DEPTH * 3, 1, D_MODEL)
    w_gate = jnp.pad(w_in_even[:, :, EVEN_MAIN:], ((0, 0), (0, 0), (0, LANES - 2 * HEADS)))
    mh_g3 = mh_norm_g.reshape(n_even, 1, HEADS * DV)
    conv_st = state_conv.reshape(n_even, DEC_BATCH, (CONV_W - 1) * D_CONV)
    n_st = state_mlstm_n.reshape(n_even, DEC_BATCH, HEADS * DK)
    scale3 = pool_scale.reshape(n_odd, 1, D_POOL)
    gm_g3 = gm_ln_g.reshape(n_odd, 1, D_GMLP)
    gm_b3 = gm_ln_b.reshape(n_odd, 1, D_GMLP)
    bs_t = jnp.swapaxes(b_spatial, 1, 2)

    conv_p, conv_s, c_p, n_p, n_s, m_p, m_s = [], [], [], [], [], [], []
    pool_p, pool_s, gv_p, gv_s = [], [], [], []
    c_s = None

    for layer in range(DEPTH):
        j = layer // 2
        x = _ffn(x, w_ffn_in, w_ffn_out, ln_g3, ln_b3, layer, 0, 3 * layer)
        if layer % 2 == 0:
            proj = _proj(x, w_in_even, j, EVEN_MAIN, TN, "proj_even")
            gates = _proj(x, w_gate, j, LANES, LANES, "proj_gates")
            mixin, cv, cc, nn, mm = _even_prompt(proj, gates, w_conv, b_gates_even, mh_g3, j)
            conv_p.append(cv)
            c_p.append(cc)
            n_p.append(nn)
            m_p.append(mm.reshape(BATCH, HEADS))
            q_s = proj[MP:, 3 * D_CONV:3 * D_CONV + HEADS * DK]
            k_s = proj[MP:, 3 * D_CONV + HEADS * DK:3 * D_CONV + 2 * HEADS * DK]
            v_s = proj[MP:, 4 * D_CONV:4 * D_CONV + HEADS * DV]
            gates_s = gates[MP:]
            qcol = q_s.reshape(DEC_BATCH, HEADS, DK, 1)
            kcol = k_s.reshape(DEC_BATCH, HEADS, DK, 1)
            c_s, num = _even_sample_state(state_mlstm_C, qcol, kcol, v_s, gates_s, state_mlstm_m,
                                          b_gates_even, j, c_s)
            mixin, cvs, nns, mms = _even_sample(proj, gates, num, conv_st, n_st, state_mlstm_m,
                                                w_conv, b_gates_even, mh_g3, mixin, j)
            conv_s.append(cvs.reshape(DEC_BATCH, CONV_W - 1, D_CONV))
            n_s.append(nns.reshape(DEC_BATCH, HEADS, DK))
            m_s.append(mms)
            x = _outproj(x, mixin, w_out_even, ln_g3, ln_b3, j, 3 * layer + 1)
        else:
            proj = _proj(x, w_in_odd, j, ODD_IN, TN, "proj_odd")
            mixin, pp, gv = _odd_prompt(proj, w_pool, scale3, gm_g3, gm_b3, w_spatial, bs_t, j)
            pool_p.append(pp)
            gv_p.append(gv)
            diff, pps = _odd_sample_pool(proj, state_pool, j)
            mixin, gvs = _odd_sample(diff, proj, w_pool, scale3, gm_g3, gm_b3, w_spatial, bs_t, mixin, j)
            pool_s.append(pps)
            gv_s.append(gvs.reshape(DEC_BATCH, 1, D_GMLP))
            x = _outproj(x, mixin, w_out_odd, ln_g3, ln_b3, j, 3 * layer + 1)
        x = _ffn(x, w_ffn_in, w_ffn_out, ln_g3, ln_b3, layer, 1, 3 * layer + 2)

    y_prompt = x[:MP].reshape(BATCH, SEQ, D_MODEL)
    y_sample = x[MP:].reshape(DEC_BATCH, 1, D_MODEL)
    return (y_prompt, y_sample,
            jnp.stack(conv_p), jnp.stack(conv_s),
            jnp.stack(c_p), c_s,
            jnp.stack(n_p), jnp.stack(n_s),
            jnp.stack(m_p), jnp.stack(m_s),
            jnp.stack(pool_p), jnp.stack(pool_s),
            jnp.stack(gv_p), jnp.stack(gv_s))
```

```python
import functools

import jax
import jax.numpy as jnp
from jax import lax
from jax.experimental import pallas as pl
from jax.experimental.pallas import tpu as pltpu

F32 = jnp.float32

D_MODEL = 2048
BATCH = 4
SEQ = 2048
DEPTH = 4
DEC_BATCH = 128
PAST_LEN = 16384
D_FF = 5632
D_CONV = 1024
CONV_W = 3
HEADS = 4
DK = 128
DV = 256
D_POOL = 1024
POOL_WINDOWS = (2, 4, 8, 16)
POOL_GW = 256
POOL_BUF = 15
D_GMLP = 1024
GMLP_GW = 256
GMLP_CHUNK = 128
ALPHA = (2 * DEPTH) ** 0.25
LN_EPS = 1e-5
EVEN_MAIN = 3 * D_CONV + 2 * HEADS * DK + 2 * HEADS * DV
ODD_IN = D_POOL + 2 * D_GMLP

MP = BATCH * SEQ
M = MP + DEC_BATCH

LANES = 128
SUBLANES = 8
VMEM_BYTES_V7X = 64 * 1024 * 1024

TM = 832
TF = 256
TN = 512
TMP = 1664
TMO = 416
LC = 256
RO = 256
SB = 8
HIST = 16


def _vmem_limit(nbytes):
    return int(min(VMEM_BYTES_V7X - 4 * 1024 * 1024, nbytes))


def _layer_norm(y, g, b):
    mu = jnp.mean(y, axis=-1, keepdims=True)
    yc = y - mu
    var = jnp.mean(yc * yc, axis=-1, keepdims=True)
    out = yc * lax.rsqrt(var + LN_EPS) * g
    if b is not None:
        out = out + b
    return out


def _log_sigmoid(x):
    return -(jnp.maximum(-x, 0.0) + jnp.log1p(jnp.exp(-jnp.abs(x))))


def _dot(a, b):
    return jnp.dot(a, b, preferred_element_type=F32)


def _dot_nt(a, b):
    return lax.dot_general(a, b, (((1,), (1,)), ((), ())), preferred_element_type=F32)


def _ffn_kernel(x_ref, wg_ref, wu_ref, wo_ref, g_ref, b_ref, o_ref, *, nf):
    f = pl.program_id(1)

    @pl.when(f == 0)
    def _():
        o_ref[...] = jnp.zeros_like(o_ref)

    x = x_ref[...]
    gate = _dot(x, wg_ref[...])
    up = _dot(x, wu_ref[...])
    h = gate * jax.nn.sigmoid(gate) * up
    o_ref[...] += _dot(h, wo_ref[...])

    @pl.when(f == nf - 1)
    def _():
        y = ALPHA * x_ref[...] + 0.5 * o_ref[...]
        o_ref[...] = _layer_norm(y, g_ref[...], b_ref[...])


def _ffn(x, w_ffn_in, w_ffn_out, ln_g, ln_b, layer, which, ln_idx):
    nf = D_FF // TF
    est = 4 * (4 * TM * D_MODEL + 2 * 3 * D_MODEL * TF + TM * D_MODEL + 4 * TM * TF) + (4 << 20)
    return pl.pallas_call(
        functools.partial(_ffn_kernel, nf=nf),
        grid=(M // TM, nf),
        in_specs=[
            pl.BlockSpec((TM, D_MODEL), lambda i, f: (i, 0)),
            pl.BlockSpec((None, None, D_MODEL, TF), lambda i, f: (layer, which, 0, f)),
            pl.BlockSpec((None, None, D_MODEL, TF), lambda i, f: (layer, which, 0, nf + f)),
            pl.BlockSpec((None, None, TF, D_MODEL), lambda i, f: (layer, which, f, 0)),
            pl.BlockSpec((None, 1, D_MODEL), lambda i, f: (ln_idx, 0, 0)),
            pl.BlockSpec((None, 1, D_MODEL), lambda i, f: (ln_idx, 0, 0)),
        ],
        out_specs=pl.BlockSpec((TM, D_MODEL), lambda i, f: (i, 0)),
        out_shape=jax.ShapeDtypeStruct((M, D_MODEL), F32),
        compiler_params=pltpu.CompilerParams(
            dimension_semantics=("parallel", "arbitrary"), vmem_limit_bytes=_vmem_limit(est)),
        name="ffn_ln",
    )(x, w_ffn_in, w_ffn_in, w_ffn_out, ln_g, ln_b)


def _proj_kernel(x_ref, w_ref, o_ref):
    o_ref[...] = _dot(x_ref[...], w_ref[...])


def _proj(x, w, layer_idx, n_cols, tn, name):
    est = 4 * (2 * TMP * D_MODEL + 2 * D_MODEL * tn + 3 * TMP * tn) + (4 << 20)
    return pl.pallas_call(
        _proj_kernel,
        grid=(M // TMP, n_cols // tn),
        in_specs=[
            pl.BlockSpec((TMP, D_MODEL), lambda i, n: (i, 0)),
            pl.BlockSpec((None, D_MODEL, tn), lambda i, n: (layer_idx, 0, n)),
        ],
        out_specs=pl.BlockSpec((TMP, tn), lambda i, n: (i, n)),
        out_shape=jax.ShapeDtypeStruct((M, n_cols), F32),
        compiler_params=pltpu.CompilerParams(
            dimension_semantics=("parallel", "arbitrary"), vmem_limit_bytes=_vmem_limit(est)),
        name=name,
    )(x, w)


def _outproj_kernel(x_ref, a_ref, w_ref, g_ref, b_ref, o_ref):
    y = ALPHA * x_ref[...] + _dot(a_ref[...], w_ref[...])
    o_ref[...] = _layer_norm(y, g_ref[...], b_ref[...])


def _outproj(x, mix, w_out, ln_g, ln_b, layer_idx, ln_idx):
    est = 4 * (D_MODEL * D_MODEL + 9 * TMO * D_MODEL) + (4 << 20)
    return pl.pallas_call(
        _outproj_kernel,
        grid=(M // TMO,),
        in_specs=[
            pl.BlockSpec((TMO, D_MODEL), lambda i: (i, 0)),
            pl.BlockSpec((TMO, D_MODEL), lambda i: (i, 0)),
            pl.BlockSpec((None, D_MODEL, D_MODEL), lambda i: (layer_idx, 0, 0),
                         pipeline_mode=pl.Buffered(1)),
            pl.BlockSpec((None, 1, D_MODEL), lambda i: (ln_idx, 0, 0)),
            pl.BlockSpec((None, 1, D_MODEL), lambda i: (ln_idx, 0, 0)),
        ],
        out_specs=pl.BlockSpec((TMO, D_MODEL), lambda i: (i, 0)),
        out_shape=jax.ShapeDtypeStruct((M, D_MODEL), F32),
        compiler_params=pltpu.CompilerParams(
            dimension_semantics=("parallel",), vmem_limit_bytes=_vmem_limit(est)),
        name="outproj_ln",
    )(x, mix, w_out, ln_g, ln_b)


def _split3(x):
    h1 = x.astype(jnp.bfloat16).astype(F32)
    r = x - h1
    h2 = r.astype(jnp.bfloat16).astype(F32)
    return h1, h2, r - h2


def _even_prompt_kernel(bg_ref, cg_ref, xin_ref, q_ref, k_ref, v_ref, og_ref, gt_ref,
                        wc_ref, bgate_ref, mhg_ref,
                        mix_ref, conv_ref, c_out_ref, n_out_ref, m_out_ref,
                        cbuf, c_sc, n_sc, m_sc):
    c = pl.program_id(1)
    L = LC

    @pl.when(c == 0)
    def _():
        cbuf[0:SUBLANES, :] = jnp.zeros((SUBLANES, D_CONV), F32)
        c_sc[...] = jnp.zeros_like(c_sc)
        n_sc[...] = jnp.zeros_like(n_sc)
        m_sc[...] = jnp.zeros_like(m_sc)

    cx = cg_ref[...] * xin_ref[...]
    cbuf[SUBLANES:SUBLANES + L, :] = cx
    c1 = cbuf[SUBLANES - 1:SUBLANES - 1 + L, :]
    c2 = cbuf[SUBLANES - 2:SUBLANES - 2 + L, :]
    conv = c2 * wc_ref[0:1, :] + c1 * wc_ref[1:2, :] + cx * wc_ref[2:3, :]
    mix_ref[:, 0:D_CONV] = bg_ref[...] * conv
    conv_ref[...] = cbuf[SUBLANES + L - 2:SUBLANES + L, :]
    cbuf[0:SUBLANES, :] = cbuf[L:L + SUBLANES, :]

    gt = gt_ref[...]
    li_all = gt[:, 0:HEADS] + bgate_ref[0:1, :]
    lf_all = _log_sigmoid(gt[:, HEADS:2 * HEADS] + bgate_ref[1:2, :])
    row = lax.broadcasted_iota(jnp.int32, (L, L), 0)
    col = lax.broadcasted_iota(jnp.int32, (L, L), 1)
    causal = row >= col
    tril = causal.astype(F32)
    f1, f2, f3 = _split3(lf_all)
    b_all = _dot(tril, f1) + _dot(tril, f2) + _dot(tril, f3)
    z = jnp.concatenate([li_all, b_all, jnp.zeros((L, LANES - 2 * HEADS), F32)], axis=1)
    zt = z.T

    for h in range(HEADS):
        li_c = li_all[:, h:h + 1]
        b_c = b_all[:, h:h + 1]
        li_r = zt[h:h + 1, :]
        b_r = zt[HEADS + h:HEADS + h + 1, :]
        m0 = m_sc[h, 0:1, 0:1]
        dm = jnp.where(causal, b_c - b_r + li_r, -jnp.inf)
        inter = b_c + m0
        m_t = jnp.maximum(inter, jnp.max(dm, axis=-1, keepdims=True))
        a_int = jnp.exp(inter - m_t)
        qh = q_ref[:, h * DK:(h + 1) * DK]
        kh = k_ref[:, h * DK:(h + 1) * DK] * (DK ** -0.5)
        vh = v_ref[:, h * DV:(h + 1) * DV]
        w = jnp.exp(dm - m_t) * _dot_nt(qh, kh)
        ch = c_sc[h]
        n_row = n_sc[h:h + 1, :]
        num = a_int * _dot(qh, ch) + _dot(w, vh)
        den = a_int * jnp.sum(qh * n_row, axis=-1, keepdims=True) + jnp.sum(w, axis=-1, keepdims=True)
        hh = num / jnp.maximum(jnp.abs(den), jnp.exp(-m_t))
        m_new = m_t[L - 1:L, :]
        b_last = b_c[L - 1:L, :]
        w_end = jnp.exp(b_last - b_c + li_c - m_new)
        decay = jnp.exp(b_last + m0 - m_new)
        wk = w_end * kh
        c_new = decay * ch + _dot(wk.T, vh)
        n_new = decay * n_row + jnp.sum(wk, axis=0, keepdims=True)
        c_sc[h] = c_new
        n_sc[h:h + 1, :] = n_new
        m_sc[h] = jnp.broadcast_to(m_new, (SUBLANES, LANES))
        c_out_ref[h] = c_new
        n_out_ref[h:h + 1, :] = n_new
        m_out_ref[0:1, h:h + 1] = m_new
        hn = _layer_norm(hh, mhg_ref[0:1, h * DV:(h + 1) * DV], None)
        og = og_ref[:, h * DV:(h + 1) * DV]
        mix_ref[:, D_CONV + h * DV:D_CONV + (h + 1) * DV] = hn * jax.nn.sigmoid(og)


def _even_prompt(proj, gates, w_conv, b_gates, mh_g, j):
    nck = SEQ // LC
    r = lambda b, c: b * nck + c
    est = 4 * (2 * (5 * LC * 1024 + 2 * LC * 512 + LC * 128) + 2 * LC * 2048
               + 3 * HEADS * DK * DV + 16 * LC * LC) + (8 << 20)
    return pl.pallas_call(
        _even_prompt_kernel,
        grid=(BATCH, nck),
        in_specs=[
            pl.BlockSpec((LC, D_CONV), lambda b, c: (r(b, c), 0)),
            pl.BlockSpec((LC, D_CONV), lambda b, c: (r(b, c), 1)),
            pl.BlockSpec((LC, D_CONV), lambda b, c: (r(b, c), 2)),
            pl.BlockSpec((LC, HEADS * DK), lambda b, c: (r(b, c), 6)),
            pl.BlockSpec((LC, HEADS * DK), lambda b, c: (r(b, c), 7)),
            pl.BlockSpec((LC, HEADS * DV), lambda b, c: (r(b, c), 4)),
            pl.BlockSpec((LC, HEADS * DV), lambda b, c: (r(b, c), 5)),
            pl.BlockSpec((LC, LANES), lambda b, c: (r(b, c), 0)),
            pl.BlockSpec((None, CONV_W, D_CONV), lambda b, c: (j, 0, 0)),
            pl.BlockSpec((None, 2, HEADS), lambda b, c: (j, 0, 0)),
            pl.BlockSpec((None, 1, HEADS * DV), lambda b, c: (j, 0, 0)),
        ],
        out_specs=[
            pl.BlockSpec((LC, D_MODEL), lambda b, c: (r(b, c), 0)),
            pl.BlockSpec((None, CONV_W - 1, D_CONV), lambda b, c: (b, 0, 0)),
            pl.BlockSpec((None, HEADS, DK, DV), lambda b, c: (b, 0, 0, 0)),
            pl.BlockSpec((None, HEADS, DK), lambda b, c: (b, 0, 0)),
            pl.BlockSpec((None, 1, HEADS), lambda b, c: (b, 0, 0)),
        ],
        out_shape=[
            jax.ShapeDtypeStruct((M, D_MODEL), F32),
            jax.ShapeDtypeStruct((BATCH, CONV_W - 1, D_CONV), F32),
            jax.ShapeDtypeStruct((BATCH, HEADS, DK, DV), F32),
            jax.ShapeDtypeStruct((BATCH, HEADS, DK), F32),
            jax.ShapeDtypeStruct((BATCH, 1, HEADS), F32),
        ],
        scratch_shapes=[
            pltpu.VMEM((LC + SUBLANES, D_CONV), F32),
            pltpu.VMEM((HEADS, DK, DV), F32),
            pltpu.VMEM((HEADS, DK), F32),
            pltpu.VMEM((HEADS, SUBLANES, LANES), F32),
        ],
        compiler_params=pltpu.CompilerParams(
            dimension_semantics=("arbitrary", "arbitrary"), vmem_limit_bytes=_vmem_limit(est)),
        name="even_prompt",
    )(proj, proj, proj, proj, proj, proj, proj, gates, w_conv, b_gates, mh_g)


def _sample_gates(gt, bgate_ref, m):
    li = gt[:, 0:HEADS] + bgate_ref[0:1, :]
    lf = _log_sigmoid(gt[:, HEADS:2 * HEADS] + bgate_ref[1:2, :])
    inter = lf + m
    m_t = jnp.maximum(inter, li)
    return jnp.exp(inter - m_t), jnp.exp(li - m_t), m_t


def _even_sample_state_kernel(c_ref, qcol_ref, kcol_ref, v_ref, gt_ref, m_ref, bgate_ref, *rest):
    c_out_ref, num_ref = rest[-2], rest[-1]
    a, wgt, _ = _sample_gates(gt_ref[...], bgate_ref, m_ref[...])
    for bi in range(SB):
        for h in range(HEADS):
            a_s = a[bi:bi + 1, h:h + 1]
            w_s = wgt[bi:bi + 1, h:h + 1]
            ch = c_ref[bi, h]
            kc = kcol_ref[bi, h] * (DK ** -0.5)
            qc = qcol_ref[bi, h]
            vr = v_ref[bi:bi + 1, h * DV:(h + 1) * DV]
            c_out_ref[bi, h] = a_s * ch + (w_s * kc) * vr
            num_ref[bi:bi + 1, h * DV:(h + 1) * DV] = jnp.sum(qc * ch, axis=0, keepdims=True)


def _even_sample_state(state_c, qcol, kcol, v_s, gates_s, state_m, b_gates, j, c_prev):
    blk5 = (None, SB, HEADS, DK, DV)
    in_specs = [
        pl.BlockSpec(blk5, lambda i: (j, i, 0, 0, 0)),
        pl.BlockSpec((SB, HEADS, DK, 1), lambda i: (i, 0, 0, 0)),
        pl.BlockSpec((SB, HEADS, DK, 1), lambda i: (i, 0, 0, 0)),
        pl.BlockSpec((SB, HEADS * DV), lambda i: (i, 0)),
        pl.BlockSpec((SB, LANES), lambda i: (i, 0)),
        pl.BlockSpec((None, SB, HEADS), lambda i: (j, i, 0)),
        pl.BlockSpec((None, 2, HEADS), lambda i: (j, 0, 0)),
    ]
    args = [state_c, qcol, kcol, v_s, gates_s, state_m, b_gates]
    aliases = {}
    if c_prev is not None:
        in_specs.append(pl.BlockSpec(memory_space=pl.ANY))
        args.append(c_prev)
        aliases = {len(args) - 1: 0}
    est = 4 * (4 * SB * HEADS * DK * DV + 4 * SB * HEADS * DK * LANES) + (8 << 20)
    return pl.pallas_call(
        _even_sample_state_kernel,
        grid=(DEC_BATCH // SB,),
        in_specs=in_specs,
        out_specs=[
            pl.BlockSpec(blk5, lambda i: (j, i, 0, 0, 0)),
            pl.BlockSpec((SB, HEADS * DV), lambda i: (i, 0)),
        ],
        out_shape=[
            jax.ShapeDtypeStruct(state_c.shape, F32),
            jax.ShapeDtypeStruct((DEC_BATCH, HEADS * DV), F32),
        ],
        input_output_aliases=aliases,
        compiler_params=pltpu.CompilerParams(
            dimension_semantics=("arbitrary",), vmem_limit_bytes=_vmem_limit(est)),
        name="even_sample_state",
    )(*args)


def _even_sample_kernel(bg_ref, cg_ref, xin_ref, q_ref, k_ref, v_ref, og_ref, gt_ref,
                        num_ref, cst_ref, n_ref, m_ref, wc_ref, bgate_ref, mhg_ref, mixin_hbm,
                        mix_ref, conv_ref, n_out_ref, m_out_ref):
    del mixin_hbm
    cx = cg_ref[...] * xin_ref[...]
    st0 = cst_ref[:, 0:D_CONV]
    st1 = cst_ref[:, D_CONV:2 * D_CONV]
    conv = st0 * wc_ref[0:1, :] + st1 * wc_ref[1:2, :] + cx * wc_ref[2:3, :]
    mix_ref[:, 0:D_CONV] = bg_ref[...] * conv
    conv_ref[:, 0:D_CONV] = st1
    conv_ref[:, D_CONV:2 * D_CONV] = cx

    a, wgt, m_t = _sample_gates(gt_ref[...], bgate_ref, m_ref[...])
    m_out_ref[...] = m_t
    floor = jnp.exp(-m_t)
    for h in range(HEADS):
        a_h = a[:, h:h + 1]
        w_h = wgt[:, h:h + 1]
        qh = q_ref[:, h * DK:(h + 1) * DK]
        kh = k_ref[:, h * DK:(h + 1) * DK] * (DK ** -0.5)
        vh = v_ref[:, h * DV:(h + 1) * DV]
        nh = n_ref[:, h * DK:(h + 1) * DK]
        wt = w_h * jnp.sum(qh * kh, axis=-1, keepdims=True)
        num = a_h * num_ref[:, h * DV:(h + 1) * DV] + wt * vh
        den = a_h * jnp.sum(qh * nh, axis=-1, keepdims=True) + wt
        hh = num / jnp.maximum(jnp.abs(den), floor[:, h:h + 1])
        n_out_ref[:, h * DK:(h + 1) * DK] = a_h * nh + w_h * kh
        hn = _layer_norm(hh, mhg_ref[0:1, h * DV:(h + 1) * DV], None)
        og = og_ref[:, h * DV:(h + 1) * DV]
        mix_ref[:, D_CONV + h * DV:D_CONV + (h + 1) * DV] = hn * jax.nn.sigmoid(og)


def _even_sample(proj, gates, num, conv_st, n_st, m_st, w_conv, b_gates, mh_g, mixin, j):
    nb = DEC_BATCH
    rb = MP // nb
    est = 4 * 2 * (5 * nb * 1024 + 2 * nb * 512 + nb * 128 + nb * 1024 + nb * 2048 + nb * 512
                   + nb * 2048 + nb * 2048 + nb * 512) + (8 << 20)
    return pl.pallas_call(
        _even_sample_kernel,
        grid=(1,),
        in_specs=[
            pl.BlockSpec((nb, D_CONV), lambda i: (rb, 0)),
            pl.BlockSpec((nb, D_CONV), lambda i: (rb, 1)),
            pl.BlockSpec((nb, D_CONV), lambda i: (rb, 2)),
            pl.BlockSpec((nb, HEADS * DK), lambda i: (rb, 6)),
            pl.BlockSpec((nb, HEADS * DK), lambda i: (rb, 7)),
            pl.BlockSpec((nb, HEADS * DV), lambda i: (rb, 4)),
            pl.BlockSpec((nb, HEADS * DV), lambda i: (rb, 5)),
            pl.BlockSpec((nb, LANES), lambda i: (rb, 0)),
            pl.BlockSpec((nb, HEADS * DV), lambda i: (0, 0)),
            pl.BlockSpec((None, nb, 2 * D_CONV), lambda i: (j, 0, 0)),
            pl.BlockSpec((None, nb, HEADS * DK), lambda i: (j, 0, 0)),
            pl.BlockSpec((None, nb, HEADS), lambda i: (j, 0, 0)),
            pl.BlockSpec((None, CONV_W, D_CONV), lambda i: (j, 0, 0)),
            pl.BlockSpec((None, 2, HEADS), lambda i: (j, 0, 0)),
            pl.BlockSpec((None, 1, HEADS * DV), lambda i: (j, 0, 0)),
            pl.BlockSpec(memory_space=pl.ANY),
        ],
        out_specs=[
            pl.BlockSpec((nb, D_MODEL), lambda i: (rb, 0)),
            pl.BlockSpec((nb, 2 * D_CONV), lambda i: (0, 0)),
            pl.BlockSpec((nb, HEADS * DK), lambda i: (0, 0)),
            pl.BlockSpec((nb, HEADS), lambda i: (0, 0)),
        ],
        out_shape=[
            jax.ShapeDtypeStruct((M, D_MODEL), F32),
            jax.ShapeDtypeStruct((nb, 2 * D_CONV), F32),
            jax.ShapeDtypeStruct((nb, HEADS * DK), F32),
            jax.ShapeDtypeStruct((nb, HEADS), F32),
        ],
        input_output_aliases={15: 0},
        compiler_params=pltpu.CompilerParams(
            dimension_semantics=("arbitrary",), vmem_limit_bytes=_vmem_limit(est)),
        name="even_sample",
    )(proj, proj, proj, proj, proj, proj, proj, gates, num, conv_st, n_st, m_st,
      w_conv, b_gates, mh_g, mixin)


def _gmlp_norm(u_raw, v_raw, g, b):
    u = jax.nn.gelu(u_raw)
    vn = _layer_norm(jax.nn.gelu(v_raw), g, b)
    return u, vn


def _odd_prompt_kernel(p_ref, u_ref, v_ref, wp_ref, sc_ref, gmg_ref, gmb_ref, ws_ref, bst_ref,
                       mix_ref, pool_ref, gv_ref, pbuf):
    s = pl.program_id(1)
    R = RO

    @pl.when(s == 0)
    def _():
        pbuf[0:HIST, :] = jnp.zeros((HIST, D_POOL), F32)

    p = p_ref[...]
    pbuf[HIST:HIST + R, :] = p
    pos = s * R + lax.broadcasted_iota(jnp.int32, (R, 1), 0)
    for g, w in enumerate(POOL_WINDOWS):
        lo, hi = g * POOL_GW, (g + 1) * POOL_GW
        win = p[:, lo:hi]
        for jj in range(1, w):
            win = win + pbuf[HIST - jj:HIST - jj + R, lo:hi]
        cnt = jnp.minimum(w, pos + 1).astype(F32)
        diff = win / cnt - p[:, lo:hi]
        mix_ref[:, lo:hi] = _dot(diff, wp_ref[g]) * sc_ref[0:1, lo:hi]
    pool_ref[...] = pbuf[HIST + R - POOL_BUF:HIST + R, :]
    pbuf[0:HIST, :] = pbuf[R:R + HIST, :]

    u, vn = _gmlp_norm(u_ref[...], v_ref[...], gmg_ref[...], gmb_ref[...])
    L = GMLP_CHUNK
    tril = lax.broadcasted_iota(jnp.int32, (L, L), 0) >= lax.broadcasted_iota(jnp.int32, (L, L), 1)
    for g in range(D_GMLP // GMLP_GW):
        lo, hi = g * GMLP_GW, (g + 1) * GMLP_GW
        ws = jnp.where(tril, ws_ref[g], 0.0)
        bcol = bst_ref[:, g:g + 1]
        for ck in range(R // L):
            r0, r1 = ck * L, (ck + 1) * L
            sv = _dot(ws, vn[r0:r1, lo:hi]) + bcol
            mix_ref[r0:r1, D_POOL + lo:D_POOL + hi] = u[r0:r1, lo:hi] * sv
    gv_ref[...] = vn[R - L:R, :]


def _odd_prompt(proj, w_pool, pool_scale, gm_g, gm_b, w_spatial, bs_t, j):
    nrb = SEQ // RO
    r = lambda b, s: b * nrb + s
    est = 4 * (2 * 3 * RO * 1024 + 2 * RO * 2048 + 2 * 4 * 256 * 256 + 2 * 4 * 128 * 128
               + (RO + HIST) * 1024 + 8 * RO * 1024) + (8 << 20)
    return pl.pallas_call(
        _odd_prompt_kernel,
        grid=(BATCH, nrb),
        in_specs=[
            pl.BlockSpec((RO, D_POOL), lambda b, s: (r(b, s), 0)),
            pl.BlockSpec((RO, D_GMLP), lambda b, s: (r(b, s), 1)),
            pl.BlockSpec((RO, D_GMLP), lambda b, s: (r(b, s), 2)),
            pl.BlockSpec((None, 4, POOL_GW, POOL_GW), lambda b, s: (j, 0, 0, 0)),
            pl.BlockSpec((None, 1, D_POOL), lambda b, s: (j, 0, 0)),
            pl.BlockSpec((None, 1, D_GMLP), lambda b, s: (j, 0, 0)),
            pl.BlockSpec((None, 1, D_GMLP), lambda b, s: (j, 0, 0)),
            pl.BlockSpec((None, 4, GMLP_CHUNK, GMLP_CHUNK), lambda b, s: (j, 0, 0, 0)),
            pl.BlockSpec((None, GMLP_CHUNK, 4), lambda b, s: (j, 0, 0)),
        ],
        out_specs=[
            pl.BlockSpec((RO, D_MODEL), lambda b, s: (r(b, s), 0)),
            pl.BlockSpec((None, POOL_BUF, D_POOL), lambda b, s: (b, 0, 0)),
            pl.BlockSpec((None, GMLP_CHUNK, D_GMLP), lambda b, s: (b, 0, 0)),
        ],
        out_shape=[
            jax.ShapeDtypeStruct((M, D_MODEL), F32),
            jax.ShapeDtypeStruct((BATCH, POOL_BUF, D_POOL), F32),
            jax.ShapeDtypeStruct((BATCH, GMLP_CHUNK, D_GMLP), F32),
        ],
        scratch_shapes=[pltpu.VMEM((RO + HIST, D_POOL), F32)],
        compiler_params=pltpu.CompilerParams(
            dimension_semantics=("arbitrary", "arbitrary"), vmem_limit_bytes=_vmem_limit(est)),
        name="odd_prompt",
    )(proj, proj, proj, w_pool, pool_scale, gm_g, gm_b, w_spatial, bs_t)


def _odd_sample_pool_kernel(p_ref, st_ref, diff_ref, pool_ref):
    for bi in range(SB):
        prow = p_ref[bi:bi + 1, :]
        pool_ref[bi, 0:POOL_BUF - 1, :] = st_ref[bi, 1:POOL_BUF, :]
        pool_ref[bi, POOL_BUF - 1:POOL_BUF, :] = prow
        for g, w in enumerate(POOL_WINDOWS):
            lo, hi = g * POOL_GW, (g + 1) * POOL_GW
            win = prow[:, lo:hi] + jnp.sum(st_ref[bi, POOL_BUF - (w - 1):POOL_BUF, lo:hi],
                                           axis=0, keepdims=True)
            cnt = float(min(w, PAST_LEN + 1))
            diff_ref[bi:bi + 1, lo:hi] = win / cnt - prow[:, lo:hi]


def _odd_sample_pool(proj, state_pool, j):
    rb = MP // SB
    return pl.pallas_call(
        _odd_sample_pool_kernel,
        grid=(DEC_BATCH // SB,),
        in_specs=[
            pl.BlockSpec((SB, D_POOL), lambda i: (rb + i, 0)),
            pl.BlockSpec((None, SB, POOL_BUF, D_POOL), lambda i: (j, i, 0, 0)),
        ],
        out_specs=[
            pl.BlockSpec((SB, D_POOL), lambda i: (i, 0)),
            pl.BlockSpec((SB, POOL_BUF, D_POOL), lambda i: (i, 0, 0)),
        ],
        out_shape=[
            jax.ShapeDtypeStruct((DEC_BATCH, D_POOL), F32),
            jax.ShapeDtypeStruct((DEC_BATCH, POOL_BUF, D_POOL), F32),
        ],
        compiler_params=pltpu.CompilerParams(dimension_semantics=("arbitrary",)),
        name="odd_sample_pool",
    )(proj, state_pool)


def _odd_sample_kernel(diff_ref, u_ref, v_ref, wp_ref, sc_ref, gmg_ref, gmb_ref, ws_ref, bst_ref,
                       mixin_hbm, mix_ref, gv_ref):
    del mixin_hbm
    diff = diff_ref[...]
    for g in range(len(POOL_WINDOWS)):
        lo, hi = g * POOL_GW, (g + 1) * POOL_GW
        mix_ref[:, lo:hi] = _dot(diff[:, lo:hi], wp_ref[g]) * sc_ref[0:1, lo:hi]
    u, vn = _gmlp_norm(u_ref[...], v_ref[...], gmg_ref[...], gmb_ref[...])
    gv_ref[...] = vn
    for g in range(D_GMLP // GMLP_GW):
        lo, hi = g * GMLP_GW, (g + 1) * GMLP_GW
        sv = ws_ref[g, 0:1, 0:1] * vn[:, lo:hi] + bst_ref[0:1, g:g + 1]
        mix_ref[:, D_POOL + lo:D_POOL + hi] = u[:, lo:hi] * sv


def _odd_sample(diff, proj, w_pool, pool_scale, gm_g, gm_b, w_spatial, bs_t, mixin, j):
    nb = DEC_BATCH
    rb = MP // nb
    est = 4 * 2 * (3 * nb * 1024 + 4 * 256 * 256 + 4 * 128 * 128 + nb * 2048 + nb * 1024) + (8 << 20)
    return pl.pallas_call(
        _odd_sample_kernel,
        grid=(1,),
        in_specs=[
            pl.BlockSpec((nb, D_POOL), lambda i: (0, 0)),
            pl.BlockSpec((nb, D_GMLP), lambda i: (rb, 1)),
            pl.BlockSpec((nb, D_GMLP), lambda i: (rb, 2)),
            pl.BlockSpec((None, 4, POOL_GW, POOL_GW), lambda i: (j, 0, 0, 0)),
            pl.BlockSpec((None, 1, D_POOL), lambda i: (j, 0, 0)),
            pl.BlockSpec((None, 1, D_GMLP), lambda i: (j, 0, 0)),
            pl.BlockSpec((None, 1, D_GMLP), lambda i: (j, 0, 0)),
            pl.BlockSpec((None, 4, GMLP_CHUNK, GMLP_CHUNK), lambda i: (j, 0, 0, 0)),
            pl.BlockSpec((None, GMLP_CHUNK, 4), lambda i: (j, 0, 0)),
            pl.BlockSpec(memory_space=pl.ANY),
        ],
        out_specs=[
            pl.BlockSpec((nb, D_MODEL), lambda i: (rb, 0)),
            pl.BlockSpec((nb, D_GMLP), lambda i: (0, 0)),
        ],
        out_shape=[
            jax.ShapeDtypeStruct((M, D_MODEL), F32),
            jax.ShapeDtypeStruct((nb, D_GMLP), F32),
        ],
        input_output_aliases={9: 0},
        compiler_params=pltpu.CompilerParams(
            dimension_semantics=("arbitrary",), vmem_limit_bytes=_vmem_limit(est)),
        name="odd_sample",
    )(diff, proj, proj, w_pool, pool_scale, gm_g, gm_b, w_spatial, bs_t, mixin)


def kernel(x_prompt, x_sample, state_conv, state_mlstm_C, state_mlstm_n, state_mlstm_m, state_pool,
           ln_g, ln_b, w_ffn_in, w_ffn_out, w_in_even, b_gates_even, w_conv, mh_norm_g, w_out_even,
           w_in_odd, w_pool, pool_scale, gm_ln_g, gm_ln_b, w_spatial, b_spatial, w_out_odd):
    n_even, n_odd = w_in_even.shape[0], w_in_odd.shape[0]
    x = jnp.concatenate([x_prompt.reshape(MP, D_MODEL), x_sample.reshape(DEC_BATCH, D_MODEL)], axis=0)

    ln_g3 = ln_g.reshape(DEPTH * 3, 1, D_MODEL)
    ln_b3 = ln_b.reshape(DEPTH * 3, 1, D_MODEL)
    w_gate = jnp.pad(w_in_even[:, :, EVEN_MAIN:], ((0, 0), (0, 0), (0, LANES - 2 * HEADS)))
    mh_g3 = mh_norm_g.reshape(n_even, 1, HEADS * DV)
    conv_st = state_conv.reshape(n_even, DEC_BATCH, (CONV_W - 1) * D_CONV)
    n_st = state_mlstm_n.reshape(n_even, DEC_BATCH, HEADS * DK)
    scale3 = pool_scale.reshape(n_odd, 1, D_POOL)
    gm_g3 = gm_ln_g.reshape(n_odd, 1, D_GMLP)
    gm_b3 = gm_ln_b.reshape(n_odd, 1, D_GMLP)
    bs_t = jnp.swapaxes(b_spatial, 1, 2)

    conv_p, conv_s, c_p, n_p, n_s, m_p, m_s = [], [], [], [], [], [], []
    pool_p, pool_s, gv_p, gv_s = [], [], [], []
    c_s = None

    for layer in range(DEPTH):
        j = layer // 2
        x = _ffn(x, w_ffn_in, w_ffn_out, ln_g3, ln_b3, layer, 0, 3 * layer)
        if layer % 2 == 0:
            proj = _proj(x, w_in_even, j, EVEN_MAIN, TN, "proj_even")
            gates = _proj(x, w_gate, j, LANES, LANES, "proj_gates")
            mixin, cv, cc, nn, mm = _even_prompt(proj, gates, w_conv, b_gates_even, mh_g3, j)
            conv_p.append(cv)
            c_p.append(cc)
            n_p.append(nn)
            m_p.append(mm.reshape(BATCH, HEADS))
            q_s = proj[MP:, 3 * D_CONV:3 * D_CONV + HEADS * DK]
            k_s = proj[MP:, 3 * D_CONV + HEADS * DK:3 * D_CONV + 2 * HEADS * DK]
            v_s = proj[MP:, 4 * D_CONV:4 * D_CONV + HEADS * DV]
            gates_s = gates[MP:]
            qcol = q_s.reshape(DEC_BATCH, HEADS, DK, 1)
            kcol = k_s.reshape(DEC_BATCH, HEADS, DK, 1)
            c_s, num = _even_sample_state(state_mlstm_C, qcol, kcol, v_s, gates_s, state_mlstm_m,
                                          b_gates_even, j, c_s)
            mixin, cvs, nns, mms = _even_sample(proj, gates, num, conv_st, n_st, state_mlstm_m,
                                                w_conv, b_gates_even, mh_g3, mixin, j)
            conv_s.append(cvs.reshape(DEC_BATCH, CONV_W - 1, D_CONV))
            n_s.append(nns.reshape(DEC_BATCH, HEADS, DK))
            m_s.append(mms)
            x = _outproj(x, mixin, w_out_even, ln_g3, ln_b3, j, 3 * layer + 1)
        else:
            proj = _proj(x, w_in_odd, j, ODD_IN, TN, "proj_odd")
            mixin, pp, gv = _odd_prompt(proj, w_pool, scale3, gm_g3, gm_b3, w_spatial, bs_t, j)
            pool_p.append(pp)
            gv_p.append(gv)
            diff, pps = _odd_sample_pool(proj, state_pool, j)
            mixin, gvs = _odd_sample(diff, proj, w_pool, scale3, gm_g3, gm_b3, w_spatial, bs_t, mixin, j)
            pool_s.append(pps)
            gv_s.append(gvs.reshape(DEC_BATCH, 1, D_GMLP))
            x = _outproj(x, mixin, w_out_odd, ln_g3, ln_b3, j, 3 * layer + 1)
        x = _ffn(x, w_ffn_in, w_ffn_out, ln_g3, ln_b3, layer, 1, 3 * layer + 2)

    y_prompt = x[:MP].reshape(BATCH, SEQ, D_MODEL)
    y_sample = x[MP:].reshape(DEC_BATCH, 1, D_MODEL)
    return (y_prompt, y_sample,
            jnp.stack(conv_p), jnp.stack(conv_s),
            jnp.stack(c_p), c_s,
            jnp.stack(n_p), jnp.stack(n_s),
            jnp.stack(m_p), jnp.stack(m_s),
            jnp.stack(pool_p), jnp.stack(pool_s),
            jnp.stack(gv_p), jnp.stack(gv_s))
```

```python
import functools

import jax
import jax.numpy as jnp
from jax import lax
from jax.experimental import pallas as pl
from jax.experimental.pallas import tpu as pltpu

F32 = jnp.float32

D_MODEL = 2048
BATCH = 4
SEQ = 2048
DEPTH = 4
DEC_BATCH = 128
PAST_LEN = 16384
D_FF = 5632
D_CONV = 1024
CONV_W = 3
HEADS = 4
DK = 128
DV = 256
D_POOL = 1024
POOL_WINDOWS = (2, 4, 8, 16)
POOL_GW = 256
POOL_BUF = 15
D_GMLP = 1024
GMLP_GW = 256
GMLP_CHUNK = 128
ALPHA = (2 * DEPTH) ** 0.25
LN_EPS = 1e-5
EVEN_MAIN = 3 * D_CONV + 2 * HEADS * DK + 2 * HEADS * DV
ODD_IN = D_POOL + 2 * D_GMLP

MP = BATCH * SEQ
M = MP + DEC_BATCH

LANES = 128
SUBLANES = 8
VMEM_BYTES_V7X = 64 * 1024 * 1024

TM = 1040
TF = 256
TN = 512
TMP = 1664
TMO = 416
LC = 256
RO = 256
SB = 8
HIST = 16


def _vmem_limit(nbytes):
    return int(min(VMEM_BYTES_V7X - 4 * 1024 * 1024, nbytes))


def _layer_norm(y, g, b):
    mu = jnp.mean(y, axis=-1, keepdims=True)
    yc = y - mu
    var = jnp.mean(yc * yc, axis=-1, keepdims=True)
    out = yc * lax.rsqrt(var + LN_EPS) * g
    if b is not None:
        out = out + b
    return out


def _log_sigmoid(x):
    return -(jnp.maximum(-x, 0.0) + jnp.log1p(jnp.exp(-jnp.abs(x))))


def _dot(a, b):
    return jnp.dot(a, b, preferred_element_type=F32)


def _dot_nt(a, b):
    return lax.dot_general(a, b, (((1,), (1,)), ((), ())), preferred_element_type=F32)


def _ffn_kernel(x_ref, wg_ref, wu_ref, wo_ref, g_ref, b_ref, o_ref, *, nf):
    f = pl.program_id(1)

    @pl.when(f == 0)
    def _():
        o_ref[...] = jnp.zeros_like(o_ref)

    x = x_ref[...]
    gate = _dot(x, wg_ref[...])
    up = _dot(x, wu_ref[...])
    h = gate * jax.nn.sigmoid(gate) * up
    for n in range(D_MODEL // TN):
        o_ref[:, n * TN:(n + 1) * TN] += _dot(h, wo_ref[:, n * TN:(n + 1) * TN])

    @pl.when(f == nf - 1)
    def _():
        y = ALPHA * x_ref[...] + 0.5 * o_ref[...]
        o_ref[...] = _layer_norm(y, g_ref[...], b_ref[...])


def _ffn(x, w_ffn_in, w_ffn_out, ln_g, ln_b, layer, which, ln_idx):
    nf = D_FF // TF
    est = 4 * (4 * TM * D_MODEL + 2 * 3 * D_MODEL * TF + TM * D_MODEL + 4 * TM * TF) + (4 << 20)
    return pl.pallas_call(
        functools.partial(_ffn_kernel, nf=nf),
        grid=(M // TM, nf),
        in_specs=[
            pl.BlockSpec((TM, D_MODEL), lambda i, f: (i, 0)),
            pl.BlockSpec((None, None, D_MODEL, TF), lambda i, f: (layer, which, 0, f)),
            pl.BlockSpec((None, None, D_MODEL, TF), lambda i, f: (layer, which, 0, nf + f)),
            pl.BlockSpec((None, None, TF, D_MODEL), lambda i, f: (layer, which, f, 0)),
            pl.BlockSpec((None, 1, D_MODEL), lambda i, f: (ln_idx, 0, 0)),
            pl.BlockSpec((None, 1, D_MODEL), lambda i, f: (ln_idx, 0, 0)),
        ],
        out_specs=pl.BlockSpec((TM, D_MODEL), lambda i, f: (i, 0)),
        out_shape=jax.ShapeDtypeStruct((M, D_MODEL), F32),
        compiler_params=pltpu.CompilerParams(
            dimension_semantics=("parallel", "arbitrary"), vmem_limit_bytes=_vmem_limit(est)),
        name="ffn_ln",
    )(x, w_ffn_in, w_ffn_in, w_ffn_out, ln_g, ln_b)


def _proj_kernel(x_ref, w_ref, o_ref, *, w_transposed):
    if w_transposed:
        o_ref[...] = _dot_nt(x_ref[...], w_ref[...])
    else:
        o_ref[...] = _dot(x_ref[...], w_ref[...])


def _proj(x, w, layer_idx, n_cols, tn, name, w_transposed=False):
    est = 4 * (2 * TMP * D_MODEL + 2 * D_MODEL * tn + 3 * TMP * tn) + (4 << 20)
    if w_transposed:
        w_spec = pl.BlockSpec((None, tn, D_MODEL), lambda i, n: (layer_idx, n, 0))
    else:
        w_spec = pl.BlockSpec((None, D_MODEL, tn), lambda i, n: (layer_idx, 0, n))
    return pl.pallas_call(
        functools.partial(_proj_kernel, w_transposed=w_transposed),
        grid=(M // TMP, n_cols // tn),
        in_specs=[
            pl.BlockSpec((TMP, D_MODEL), lambda i, n: (i, 0)),
            w_spec,
        ],
        out_specs=pl.BlockSpec((TMP, tn), lambda i, n: (i, n)),
        out_shape=jax.ShapeDtypeStruct((M, n_cols), F32),
        compiler_params=pltpu.CompilerParams(
            dimension_semantics=("parallel", "arbitrary"), vmem_limit_bytes=_vmem_limit(est)),
        name=name,
    )(x, w)


def _outproj_kernel(x_ref, a_ref, w_ref, g_ref, b_ref, o_ref):
    y = ALPHA * x_ref[...] + _dot(a_ref[...], w_ref[...])
    o_ref[...] = _layer_norm(y, g_ref[...], b_ref[...])


def _outproj(x, mix, w_out, ln_g, ln_b, layer_idx, ln_idx):
    est = 4 * (D_MODEL * D_MODEL + 9 * TMO * D_MODEL) + (4 << 20)
    return pl.pallas_call(
        _outproj_kernel,
        grid=(M // TMO,),
        in_specs=[
            pl.BlockSpec((TMO, D_MODEL), lambda i: (i, 0)),
            pl.BlockSpec((TMO, D_MODEL), lambda i: (i, 0)),
            pl.BlockSpec((None, D_MODEL, D_MODEL), lambda i: (layer_idx, 0, 0),
                         pipeline_mode=pl.Buffered(1)),
            pl.BlockSpec((None, 1, D_MODEL), lambda i: (ln_idx, 0, 0)),
            pl.BlockSpec((None, 1, D_MODEL), lambda i: (ln_idx, 0, 0)),
        ],
        out_specs=pl.BlockSpec((TMO, D_MODEL), lambda i: (i, 0)),
        out_shape=jax.ShapeDtypeStruct((M, D_MODEL), F32),
        compiler_params=pltpu.CompilerParams(
            dimension_semantics=("parallel",), vmem_limit_bytes=_vmem_limit(est)),
        name="outproj_ln",
    )(x, mix, w_out, ln_g, ln_b)


def _split3(x):
    h1 = x.astype(jnp.bfloat16).astype(F32)
    r = x - h1
    h2 = r.astype(jnp.bfloat16).astype(F32)
    return h1, h2, r - h2


def _even_prompt_kernel(bg_ref, cg_ref, xin_ref, q_ref, k_ref, v_ref, og_ref, gt_ref,
                        wc_ref, bgate_ref, mhg_ref,
                        mix_ref, conv_ref, c_out_ref, n_out_ref, m_out_ref,
                        cbuf, c_sc, n_sc, m_sc):
    c = pl.program_id(1)
    L = LC

    @pl.when(c == 0)
    def _():
        cbuf[0:SUBLANES, :] = jnp.zeros((SUBLANES, D_CONV), F32)
        c_sc[...] = jnp.zeros_like(c_sc)
        n_sc[...] = jnp.zeros_like(n_sc)
        m_sc[...] = jnp.zeros_like(m_sc)

    cx = cg_ref[...] * xin_ref[...]
    cbuf[SUBLANES:SUBLANES + L, :] = cx
    c1 = cbuf[SUBLANES - 1:SUBLANES - 1 + L, :]
    c2 = cbuf[SUBLANES - 2:SUBLANES - 2 + L, :]
    conv = c2 * wc_ref[0:1, :] + c1 * wc_ref[1:2, :] + cx * wc_ref[2:3, :]
    mix_ref[:, 0:D_CONV] = bg_ref[...] * conv
    conv_ref[...] = cbuf[SUBLANES + L - 2:SUBLANES + L, :]
    cbuf[0:SUBLANES, :] = cbuf[L:L + SUBLANES, :]

    gt = gt_ref[...]
    li_all = gt[:, 0:HEADS] + bgate_ref[0:1, :]
    lf_all = _log_sigmoid(gt[:, HEADS:2 * HEADS] + bgate_ref[1:2, :])
    row = lax.broadcasted_iota(jnp.int32, (L, L), 0)
    col = lax.broadcasted_iota(jnp.int32, (L, L), 1)
    causal = row >= col
    tril = causal.astype(F32)
    f1, f2, f3 = _split3(lf_all)
    b_all = _dot(tril, f1) + _dot(tril, f2) + _dot(tril, f3)
    z = jnp.concatenate([li_all, b_all, jnp.zeros((L, LANES - 2 * HEADS), F32)], axis=1)
    zt = z.T

    for h in range(HEADS):
        li_c = li_all[:, h:h + 1]
        b_c = b_all[:, h:h + 1]
        li_r = zt[h:h + 1, :]
        b_r = zt[HEADS + h:HEADS + h + 1, :]
        m0 = m_sc[h, 0:1, 0:1]
        dm = jnp.where(causal, b_c - b_r + li_r, -jnp.inf)
        inter = b_c + m0
        m_t = jnp.maximum(inter, jnp.max(dm, axis=-1, keepdims=True))
        a_int = jnp.exp(inter - m_t)
        qh = q_ref[:, h * DK:(h + 1) * DK]
        kh = k_ref[:, h * DK:(h + 1) * DK] * (DK ** -0.5)
        vh = v_ref[:, h * DV:(h + 1) * DV]
        w = jnp.exp(dm - m_t) * _dot_nt(qh, kh)
        ch = c_sc[h]
        n_row = n_sc[h:h + 1, :]
        num = a_int * _dot(qh, ch) + _dot(w, vh)
        den = a_int * jnp.sum(qh * n_row, axis=-1, keepdims=True) + jnp.sum(w, axis=-1, keepdims=True)
        hh = num / jnp.maximum(jnp.abs(den), jnp.exp(-m_t))
        m_new = m_t[L - 1:L, :]
        b_last = b_c[L - 1:L, :]
        w_end = jnp.exp(b_last - b_c + li_c - m_new)
        decay = jnp.exp(b_last + m0 - m_new)
        wk = w_end * kh
        c_new = decay * ch + _dot(wk.T, vh)
        n_new = decay * n_row + jnp.sum(wk, axis=0, keepdims=True)
        c_sc[h] = c_new
        n_sc[h:h + 1, :] = n_new
        m_sc[h] = jnp.broadcast_to(m_new, (SUBLANES, LANES))
        c_out_ref[h] = c_new
        n_out_ref[h:h + 1, :] = n_new
        m_out_ref[0:1, h:h + 1] = m_new
        hn = _layer_norm(hh, mhg_ref[0:1, h * DV:(h + 1) * DV], None)
        og = og_ref[:, h * DV:(h + 1) * DV]
        mix_ref[:, D_CONV + h * DV:D_CONV + (h + 1) * DV] = hn * jax.nn.sigmoid(og)


def _even_prompt(proj, gates, w_conv, b_gates, mh_g, j):
    nck = SEQ // LC
    r = lambda b, c: b * nck + c
    est = 4 * (2 * (5 * LC * 1024 + 2 * LC * 512 + LC * 128) + 2 * LC * 2048
               + 3 * HEADS * DK * DV + 16 * LC * LC) + (8 << 20)
    return pl.pallas_call(
        _even_prompt_kernel,
        grid=(BATCH, nck),
        in_specs=[
            pl.BlockSpec((LC, D_CONV), lambda b, c: (r(b, c), 0)),
            pl.BlockSpec((LC, D_CONV), lambda b, c: (r(b, c), 1)),
            pl.BlockSpec((LC, D_CONV), lambda b, c: (r(b, c), 2)),
            pl.BlockSpec((LC, HEADS * DK), lambda b, c: (r(b, c), 6)),
            pl.BlockSpec((LC, HEADS * DK), lambda b, c: (r(b, c), 7)),
            pl.BlockSpec((LC, HEADS * DV), lambda b, c: (r(b, c), 4)),
            pl.BlockSpec((LC, HEADS * DV), lambda b, c: (r(b, c), 5)),
            pl.BlockSpec((LC, LANES), lambda b, c: (r(b, c), 0)),
            pl.BlockSpec((None, CONV_W, D_CONV), lambda b, c: (j, 0, 0)),
            pl.BlockSpec((None, 2, HEADS), lambda b, c: (j, 0, 0)),
            pl.BlockSpec((None, 1, HEADS * DV), lambda b, c: (j, 0, 0)),
        ],
        out_specs=[
            pl.BlockSpec((LC, D_MODEL), lambda b, c: (r(b, c), 0)),
            pl.BlockSpec((None, CONV_W - 1, D_CONV), lambda b, c: (b, 0, 0)),
            pl.BlockSpec((None, HEADS, DK, DV), lambda b, c: (b, 0, 0, 0)),
            pl.BlockSpec((None, HEADS, DK), lambda b, c: (b, 0, 0)),
            pl.BlockSpec((None, 1, HEADS), lambda b, c: (b, 0, 0)),
        ],
        out_shape=[
            jax.ShapeDtypeStruct((M, D_MODEL), F32),
            jax.ShapeDtypeStruct((BATCH, CONV_W - 1, D_CONV), F32),
            jax.ShapeDtypeStruct((BATCH, HEADS, DK, DV), F32),
            jax.ShapeDtypeStruct((BATCH, HEADS, DK), F32),
            jax.ShapeDtypeStruct((BATCH, 1, HEADS), F32),
        ],
        scratch_shapes=[
            pltpu.VMEM((LC + SUBLANES, D_CONV), F32),
            pltpu.VMEM((HEADS, DK, DV), F32),
            pltpu.VMEM((HEADS, DK), F32),
            pltpu.VMEM((HEADS, SUBLANES, LANES), F32),
        ],
        compiler_params=pltpu.CompilerParams(
            dimension_semantics=("arbitrary", "arbitrary"), vmem_limit_bytes=_vmem_limit(est)),
        name="even_prompt",
    )(proj, proj, proj, proj, proj, proj, proj, gates, w_conv, b_gates, mh_g)


def _sample_gates(gt, bgate_ref, m):
    li = gt[:, 0:HEADS] + bgate_ref[0:1, :]
    lf = _log_sigmoid(gt[:, HEADS:2 * HEADS] + bgate_ref[1:2, :])
    inter = lf + m
    m_t = jnp.maximum(inter, li)
    return jnp.exp(inter - m_t), jnp.exp(li - m_t), m_t


def _even_sample_state_kernel(c_ref, qcol_ref, kcol_ref, v_ref, gt_ref, m_ref, bgate_ref, *rest):
    c_out_ref, num_ref = rest[-2], rest[-1]
    a, wgt, _ = _sample_gates(gt_ref[...], bgate_ref, m_ref[...])
    for bi in range(SB):
        for h in range(HEADS):
            a_s = a[bi:bi + 1, h:h + 1]
            w_s = wgt[bi:bi + 1, h:h + 1]
            ch = c_ref[bi, h]
            kc = kcol_ref[bi, h] * (DK ** -0.5)
            qc = qcol_ref[bi, h]
            vr = v_ref[bi:bi + 1, h * DV:(h + 1) * DV]
            c_out_ref[bi, h] = a_s * ch + (w_s * kc) * vr
            num_ref[bi:bi + 1, h * DV:(h + 1) * DV] = jnp.sum(qc * ch, axis=0, keepdims=True)


def _even_sample_state(state_c, qcol, kcol, v_s, gates_s, state_m, b_gates, j, c_prev):
    blk5 = (None, SB, HEADS, DK, DV)
    in_specs = [
        pl.BlockSpec(blk5, lambda i: (j, i, 0, 0, 0)),
        pl.BlockSpec((SB, HEADS, DK, 1), lambda i: (i, 0, 0, 0)),
        pl.BlockSpec((SB, HEADS, DK, 1), lambda i: (i, 0, 0, 0)),
        pl.BlockSpec((SB, HEADS * DV), lambda i: (i, 0)),
        pl.BlockSpec((SB, LANES), lambda i: (i, 0)),
        pl.BlockSpec((None, SB, HEADS), lambda i: (j, i, 0)),
        pl.BlockSpec((None, 2, HEADS), lambda i: (j, 0, 0)),
    ]
    args = [state_c, qcol, kcol, v_s, gates_s, state_m, b_gates]
    aliases = {}
    if c_prev is not None:
        in_specs.append(pl.BlockSpec(memory_space=pl.ANY))
        args.append(c_prev)
        aliases = {len(args) - 1: 0}
    est = 4 * (4 * SB * HEADS * DK * DV + 4 * SB * HEADS * DK * LANES) + (8 << 20)
    return pl.pallas_call(
        _even_sample_state_kernel,
        grid=(DEC_BATCH // SB,),
        in_specs=in_specs,
        out_specs=[
            pl.BlockSpec(blk5, lambda i: (j, i, 0, 0, 0)),
            pl.BlockSpec((SB, HEADS * DV), lambda i: (i, 0)),
        ],
        out_shape=[
            jax.ShapeDtypeStruct(state_c.shape, F32),
            jax.ShapeDtypeStruct((DEC_BATCH, HEADS * DV), F32),
        ],
        input_output_aliases=aliases,
        compiler_params=pltpu.CompilerParams(
            dimension_semantics=("arbitrary",), vmem_limit_bytes=_vmem_limit(est)),
        name="even_sample_state",
    )(*args)


def _even_sample_kernel(bg_ref, cg_ref, xin_ref, q_ref, k_ref, v_ref, og_ref, gt_ref,
                        num_ref, cst_ref, n_ref, m_ref, wc_ref, bgate_ref, mhg_ref, mixin_hbm,
                        mix_ref, conv_ref, n_out_ref, m_out_ref):
    del mixin_hbm
    cx = cg_ref[...] * xin_ref[...]
    st0 = cst_ref[:, 0:D_CONV]
    st1 = cst_ref[:, D_CONV:2 * D_CONV]
    conv = st0 * wc_ref[0:1, :] + st1 * wc_ref[1:2, :] + cx * wc_ref[2:3, :]
    mix_ref[:, 0:D_CONV] = bg_ref[...] * conv
    conv_ref[:, 0:D_CONV] = st1
    conv_ref[:, D_CONV:2 * D_CONV] = cx

    a, wgt, m_t = _sample_gates(gt_ref[...], bgate_ref, m_ref[...])
    m_out_ref[...] = m_t
    floor = jnp.exp(-m_t)
    for h in range(HEADS):
        a_h = a[:, h:h + 1]
        w_h = wgt[:, h:h + 1]
        qh = q_ref[:, h * DK:(h + 1) * DK]
        kh = k_ref[:, h * DK:(h + 1) * DK] * (DK ** -0.5)
        vh = v_ref[:, h * DV:(h + 1) * DV]
        nh = n_ref[:, h * DK:(h + 1) * DK]
        wt = w_h * jnp.sum(qh * kh, axis=-1, keepdims=True)
        num = a_h * num_ref[:, h * DV:(h + 1) * DV] + wt * vh
        den = a_h * jnp.sum(qh * nh, axis=-1, keepdims=True) + wt
        hh = num / jnp.maximum(jnp.abs(den), floor[:, h:h + 1])
        n_out_ref[:, h * DK:(h + 1) * DK] = a_h * nh + w_h * kh
        hn = _layer_norm(hh, mhg_ref[0:1, h * DV:(h + 1) * DV], None)
        og = og_ref[:, h * DV:(h + 1) * DV]
        mix_ref[:, D_CONV + h * DV:D_CONV + (h + 1) * DV] = hn * jax.nn.sigmoid(og)


def _even_sample(proj, gates, num, conv_st, n_st, m_st, w_conv, b_gates, mh_g, mixin, j):
    nb = DEC_BATCH
    rb = MP // nb
    est = 4 * 2 * (5 * nb * 1024 + 2 * nb * 512 + nb * 128 + nb * 1024 + nb * 2048 + nb * 512
                   + nb * 2048 + nb * 2048 + nb * 512) + (8 << 20)
    return pl.pallas_call(
        _even_sample_kernel,
        grid=(1,),
        in_specs=[
            pl.BlockSpec((nb, D_CONV), lambda i: (rb, 0)),
            pl.BlockSpec((nb, D_CONV), lambda i: (rb, 1)),
            pl.BlockSpec((nb, D_CONV), lambda i: (rb, 2)),
            pl.BlockSpec((nb, HEADS * DK), lambda i: (rb, 6)),
            pl.BlockSpec((nb, HEADS * DK), lambda i: (rb, 7)),
            pl.BlockSpec((nb, HEADS * DV), lambda i: (rb, 4)),
            pl.BlockSpec((nb, HEADS * DV), lambda i: (rb, 5)),
            pl.BlockSpec((nb, LANES), lambda i: (rb, 0)),
            pl.BlockSpec((nb, HEADS * DV), lambda i: (0, 0)),
            pl.BlockSpec((None, nb, 2 * D_CONV), lambda i: (j, 0, 0)),
            pl.BlockSpec((None, nb, HEADS * DK), lambda i: (j, 0, 0)),
            pl.BlockSpec((None, nb, HEADS), lambda i: (j, 0, 0)),
            pl.BlockSpec((None, CONV_W, D_CONV), lambda i: (j, 0, 0)),
            pl.BlockSpec((None, 2, HEADS), lambda i: (j, 0, 0)),
            pl.BlockSpec((None, 1, HEADS * DV), lambda i: (j, 0, 0)),
            pl.BlockSpec(memory_space=pl.ANY),
        ],
        out_specs=[
            pl.BlockSpec((nb, D_MODEL), lambda i: (rb, 0)),
            pl.BlockSpec((nb, 2 * D_CONV), lambda i: (0, 0)),
            pl.BlockSpec((nb, HEADS * DK), lambda i: (0, 0)),
            pl.BlockSpec((nb, HEADS), lambda i: (0, 0)),
        ],
        out_shape=[
            jax.ShapeDtypeStruct((M, D_MODEL), F32),
            jax.ShapeDtypeStruct((nb, 2 * D_CONV), F32),
            jax.ShapeDtypeStruct((nb, HEADS * DK), F32),
            jax.ShapeDtypeStruct((nb, HEADS), F32),
        ],
        input_output_aliases={15: 0},
        compiler_params=pltpu.CompilerParams(
            dimension_semantics=("arbitrary",), vmem_limit_bytes=_vmem_limit(est)),
        name="even_sample",
    )(proj, proj, proj, proj, proj, proj, proj, gates, num, conv_st, n_st, m_st,
      w_conv, b_gates, mh_g, mixin)


def _gmlp_norm(u_raw, v_raw, g, b):
    u = jax.nn.gelu(u_raw)
    vn = _layer_norm(jax.nn.gelu(v_raw), g, b)
    return u, vn


def _odd_prompt_kernel(p_ref, u_ref, v_ref, wp_ref, sc_ref, gmg_ref, gmb_ref, ws_ref, bst_ref,
                       mix_ref, pool_ref, gv_ref, pbuf):
    s = pl.program_id(1)
    R = RO

    @pl.when(s == 0)
    def _():
        pbuf[0:HIST, :] = jnp.zeros((HIST, D_POOL), F32)

    p = p_ref[...]
    pbuf[HIST:HIST + R, :] = p
    pos = s * R + lax.broadcasted_iota(jnp.int32, (R, 1), 0)
    for g, w in enumerate(POOL_WINDOWS):
        lo, hi = g * POOL_GW, (g + 1) * POOL_GW
        win = p[:, lo:hi]
        for jj in range(1, w):
            win = win + pbuf[HIST - jj:HIST - jj + R, lo:hi]
        cnt = jnp.minimum(w, pos + 1).astype(F32)
        diff = win / cnt - p[:, lo:hi]
        mix_ref[:, lo:hi] = _dot(diff, wp_ref[g]) * sc_ref[0:1, lo:hi]
    pool_ref[...] = pbuf[HIST + R - POOL_BUF:HIST + R, :]
    pbuf[0:HIST, :] = pbuf[R:R + HIST, :]

    u, vn = _gmlp_norm(u_ref[...], v_ref[...], gmg_ref[...], gmb_ref[...])
    L = GMLP_CHUNK
    tril = lax.broadcasted_iota(jnp.int32, (L, L), 0) >= lax.broadcasted_iota(jnp.int32, (L, L), 1)
    for g in range(D_GMLP // GMLP_GW):
        lo, hi = g * GMLP_GW, (g + 1) * GMLP_GW
        ws = jnp.where(tril, ws_ref[g], 0.0)
        bcol = bst_ref[:, g:g + 1]
        for ck in range(R // L):
            r0, r1 = ck * L, (ck + 1) * L
            sv = _dot(ws, vn[r0:r1, lo:hi]) + bcol
            mix_ref[r0:r1, D_POOL + lo:D_POOL + hi] = u[r0:r1, lo:hi] * sv
    gv_ref[...] = vn[R - L:R, :]


def _odd_prompt(proj, w_pool, pool_scale, gm_g, gm_b, w_spatial, bs_t, j):
    nrb = SEQ // RO
    r = lambda b, s: b * nrb + s
    est = 4 * (2 * 3 * RO * 1024 + 2 * RO * 2048 + 2 * 4 * 256 * 256 + 2 * 4 * 128 * 128
               + (RO + HIST) * 1024 + 8 * RO * 1024) + (8 << 20)
    return pl.pallas_call(
        _odd_prompt_kernel,
        grid=(BATCH, nrb),
        in_specs=[
            pl.BlockSpec((RO, D_POOL), lambda b, s: (r(b, s), 0)),
            pl.BlockSpec((RO, D_GMLP), lambda b, s: (r(b, s), 1)),
            pl.BlockSpec((RO, D_GMLP), lambda b, s: (r(b, s), 2)),
            pl.BlockSpec((None, 4, POOL_GW, POOL_GW), lambda b, s: (j, 0, 0, 0)),
            pl.BlockSpec((None, 1, D_POOL), lambda b, s: (j, 0, 0)),
            pl.BlockSpec((None, 1, D_GMLP), lambda b, s: (j, 0, 0)),
            pl.BlockSpec((None, 1, D_GMLP), lambda b, s: (j, 0, 0)),
            pl.BlockSpec((None, 4, GMLP_CHUNK, GMLP_CHUNK), lambda b, s: (j, 0, 0, 0)),
            pl.BlockSpec((None, GMLP_CHUNK, 4), lambda b, s: (j, 0, 0)),
        ],
        out_specs=[
            pl.BlockSpec((RO, D_MODEL), lambda b, s: (r(b, s), 0)),
            pl.BlockSpec((None, POOL_BUF, D_POOL), lambda b, s: (b, 0, 0)),
            pl.BlockSpec((None, GMLP_CHUNK, D_GMLP), lambda b, s: (b, 0, 0)),
        ],
        out_shape=[
            jax.ShapeDtypeStruct((M, D_MODEL), F32),
            jax.ShapeDtypeStruct((BATCH, POOL_BUF, D_POOL), F32),
            jax.ShapeDtypeStruct((BATCH, GMLP_CHUNK, D_GMLP), F32),
        ],
        scratch_shapes=[pltpu.VMEM((RO + HIST, D_POOL), F32)],
        compiler_params=pltpu.CompilerParams(
            dimension_semantics=("arbitrary", "arbitrary"), vmem_limit_bytes=_vmem_limit(est)),
        name="odd_prompt",
    )(proj, proj, proj, w_pool, pool_scale, gm_g, gm_b, w_spatial, bs_t)


def _odd_sample_kernel(p_ref, u_ref, v_ref, st_ref, wp_ref, sc_ref, gmg_ref, gmb_ref, ws_ref, bst_ref,
                       *rest):
    mix_ref, pool_ref, gv_ref = rest[-3:]
    p = p_ref[...]
    for r in range(POOL_BUF - 1):
        pool_ref[r] = st_ref[r + 1]
    pool_ref[POOL_BUF - 1] = p
    for g, w in enumerate(POOL_WINDOWS):
        lo, hi = g * POOL_GW, (g + 1) * POOL_GW
        win = p[:, lo:hi]
        for jj in range(1, w):
            win = win + st_ref[POOL_BUF - jj, :, lo:hi]
        cnt = float(min(w, PAST_LEN + 1))
        diff = win / cnt - p[:, lo:hi]
        mix_ref[:, lo:hi] = _dot(diff, wp_ref[g]) * sc_ref[0:1, lo:hi]
    u, vn = _gmlp_norm(u_ref[...], v_ref[...], gmg_ref[...], gmb_ref[...])
    gv_ref[...] = vn
    for g in range(D_GMLP // GMLP_GW):
        lo, hi = g * GMLP_GW, (g + 1) * GMLP_GW
        sv = ws_ref[g, 0:1, 0:1] * vn[:, lo:hi] + bst_ref[0:1, g:g + 1]
        mix_ref[:, D_POOL + lo:D_POOL + hi] = u[:, lo:hi] * sv


def _odd_sample(proj, pool_st, w_pool, pool_scale, gm_g, gm_b, w_spatial, bs_t, mixin, pool_prev, j):
    nb = DEC_BATCH
    rb = MP // nb
    st_blk = (None, POOL_BUF, nb, D_POOL)
    in_specs = [
        pl.BlockSpec((nb, D_POOL), lambda i: (rb, 0)),
        pl.BlockSpec((nb, D_GMLP), lambda i: (rb, 1)),
        pl.BlockSpec((nb, D_GMLP), lambda i: (rb, 2)),
        pl.BlockSpec(st_blk, lambda i: (j, 0, 0, 0)),
        pl.BlockSpec((None, 4, POOL_GW, POOL_GW), lambda i: (j, 0, 0, 0)),
        pl.BlockSpec((None, 1, D_POOL), lambda i: (j, 0, 0)),
        pl.BlockSpec((None, 1, D_GMLP), lambda i: (j, 0, 0)),
        pl.BlockSpec((None, 1, D_GMLP), lambda i: (j, 0, 0)),
        pl.BlockSpec((None, 4, GMLP_CHUNK, GMLP_CHUNK), lambda i: (j, 0, 0, 0)),
        pl.BlockSpec((None, GMLP_CHUNK, 4), lambda i: (j, 0, 0)),
        pl.BlockSpec(memory_space=pl.ANY),
    ]
    args = [proj, proj, proj, pool_st, w_pool, pool_scale, gm_g, gm_b, w_spatial, bs_t, mixin]
    aliases = {len(args) - 1: 0}
    if pool_prev is not None:
        in_specs.append(pl.BlockSpec(memory_space=pl.ANY))
        args.append(pool_prev)
        aliases[len(args) - 1] = 1
    est = 4 * (4 * POOL_BUF * nb * D_POOL + 2 * (3 * nb * 1024 + 4 * 256 * 256 + 4 * 128 * 128
                                                + nb * 2048 + nb * 1024)) + (8 << 20)
    return pl.pallas_call(
        _odd_sample_kernel,
        grid=(1,),
        in_specs=in_specs,
        out_specs=[
            pl.BlockSpec((nb, D_MODEL), lambda i: (rb, 0)),
            pl.BlockSpec(st_blk, lambda i: (j, 0, 0, 0)),
            pl.BlockSpec((nb, D_GMLP), lambda i: (0, 0)),
        ],
        out_shape=[
            jax.ShapeDtypeStruct((M, D_MODEL), F32),
            jax.ShapeDtypeStruct(pool_st.shape, F32),
            jax.ShapeDtypeStruct((nb, D_GMLP), F32),
        ],
        input_output_aliases=aliases,
        compiler_params=pltpu.CompilerParams(
            dimension_semantics=("arbitrary",), vmem_limit_bytes=_vmem_limit(est)),
        name="odd_sample",
    )(*args)


def kernel(x_prompt, x_sample, state_conv, state_mlstm_C, state_mlstm_n, state_mlstm_m, state_pool,
           ln_g, ln_b, w_ffn_in, w_ffn_out, w_in_even, b_gates_even, w_conv, mh_norm_g, w_out_even,
           w_in_odd, w_pool, pool_scale, gm_ln_g, gm_ln_b, w_spatial, b_spatial, w_out_odd):
    n_even, n_odd = w_in_even.shape[0], w_in_odd.shape[0]
    x = jnp.concatenate([x_prompt.reshape(MP, D_MODEL), x_sample.reshape(DEC_BATCH, D_MODEL)], axis=0)

    ln_g3 = ln_g.reshape(DEPTH * 3, 1, D_MODEL)
    ln_b3 = ln_b.reshape(DEPTH * 3, 1, D_MODEL)
    w_even_t = jnp.swapaxes(w_in_even, 1, 2)
    w_gate_t = jnp.pad(w_even_t[:, EVEN_MAIN:, :], ((0, 0), (0, LANES - 2 * HEADS), (0, 0)))
    pool_st = jnp.swapaxes(state_pool, 1, 2)
    mh_g3 = mh_norm_g.reshape(n_even, 1, HEADS * DV)
    conv_st = state_conv.reshape(n_even, DEC_BATCH, (CONV_W - 1) * D_CONV)
    n_st = state_mlstm_n.reshape(n_even, DEC_BATCH, HEADS * DK)
    scale3 = pool_scale.reshape(n_odd, 1, D_POOL)
    gm_g3 = gm_ln_g.reshape(n_odd, 1, D_GMLP)
    gm_b3 = gm_ln_b.reshape(n_odd, 1, D_GMLP)
    bs_t = jnp.swapaxes(b_spatial, 1, 2)

    conv_p, conv_s, c_p, n_p, n_s, m_p, m_s = [], [], [], [], [], [], []
    pool_p, gv_p, gv_s = [], [], []
    c_s = None
    pool_s = None

    for layer in range(DEPTH):
        j = layer // 2
        x = _ffn(x, w_ffn_in, w_ffn_out, ln_g3, ln_b3, layer, 0, 3 * layer)
        if layer % 2 == 0:
            proj = _proj(x, w_even_t, j, EVEN_MAIN, TN, "proj_even", w_transposed=True)
            gates = _proj(x, w_gate_t, j, LANES, LANES, "proj_gates", w_transposed=True)
            mixin, cv, cc, nn, mm = _even_prompt(proj, gates, w_conv, b_gates_even, mh_g3, j)
            conv_p.append(cv)
            c_p.append(cc)
            n_p.append(nn)
            m_p.append(mm.reshape(BATCH, HEADS))
            q_s = proj[MP:, 3 * D_CONV:3 * D_CONV + HEADS * DK]
            k_s = proj[MP:, 3 * D_CONV + HEADS * DK:3 * D_CONV + 2 * HEADS * DK]
            v_s = proj[MP:, 4 * D_CONV:4 * D_CONV + HEADS * DV]
            gates_s = gates[MP:]
            qcol = q_s.reshape(DEC_BATCH, HEADS, DK, 1)
            kcol = k_s.reshape(DEC_BATCH, HEADS, DK, 1)
            c_s, num = _even_sample_state(state_mlstm_C, qcol, kcol, v_s, gates_s, state_mlstm_m,
                                          b_gates_even, j, c_s)
            mixin, cvs, nns, mms = _even_sample(proj, gates, num, conv_st, n_st, state_mlstm_m,
                                                w_conv, b_gates_even, mh_g3, mixin, j)
            conv_s.append(cvs.reshape(DEC_BATCH, CONV_W - 1, D_CONV))
            n_s.append(nns.reshape(DEC_BATCH, HEADS, DK))
            m_s.append(mms)
            x = _outproj(x, mixin, w_out_even, ln_g3, ln_b3, j, 3 * layer + 1)
        else:
            proj = _proj(x, w_in_odd, j, ODD_IN, TN, "proj_odd")
            mixin, pp, gv = _odd_prompt(proj, w_pool, scale3, gm_g3, gm_b3, w_spatial, bs_t, j)
            pool_p.append(pp)
            gv_p.append(gv)
            mixin, pool_s, gvs = _odd_sample(proj, pool_st, w_pool, scale3, gm_g3, gm_b3, w_spatial, bs_t,
                                             mixin, pool_s, j)
            gv_s.append(gvs.reshape(DEC_BATCH, 1, D_GMLP))
            x = _outproj(x, mixin, w_out_odd, ln_g3, ln_b3, j, 3 * layer + 1)
        x = _ffn(x, w_ffn_in, w_ffn_out, ln_g3, ln_b3, layer, 1, 3 * layer + 2)

    y_prompt = x[:MP].reshape(BATCH, SEQ, D_MODEL)
    y_sample = x[MP:].reshape(DEC_BATCH, 1, D_MODEL)
    return (y_prompt, y_sample,
            jnp.stack(conv_p), jnp.stack(conv_s),
            jnp.stack(c_p), c_s,
            jnp.stack(n_p), jnp.stack(n_s),
            jnp.stack(m_p), jnp.stack(m_s),
            jnp.stack(pool_p), jnp.swapaxes(pool_s, 1, 2),
            jnp.stack(gv_p), jnp.stack(gv_s))
```

```python
import functools

import jax
import jax.numpy as jnp
from jax import lax
from jax.experimental import pallas as pl
from jax.experimental.pallas import tpu as pltpu

F32 = jnp.float32

D_MODEL = 2048
BATCH = 4
SEQ = 2048
DEPTH = 4
DEC_BATCH = 128
PAST_LEN = 16384
D_FF = 5632
D_CONV = 1024
CONV_W = 3
HEADS = 4
DK = 128
DV = 256
D_POOL = 1024
POOL_WINDOWS = (2, 4, 8, 16)
POOL_GW = 256
POOL_BUF = 15
D_GMLP = 1024
GMLP_GW = 256
GMLP_CHUNK = 128
ALPHA = (2 * DEPTH) ** 0.25
LN_EPS = 1e-5
EVEN_MAIN = 3 * D_CONV + 2 * HEADS * DK + 2 * HEADS * DV
ODD_IN = D_POOL + 2 * D_GMLP

MP = BATCH * SEQ
M = MP + DEC_BATCH

LANES = 128
SUBLANES = 8
VMEM_BYTES_V7X = 64 * 1024 * 1024

TM = 1040
TF = 256
TN = 512
TMP = 1664
TMO = 416
LC = 256
RO = 256
SB = 8
HIST = 16


def _vmem_limit(nbytes):
    return int(min(VMEM_BYTES_V7X - 4 * 1024 * 1024, nbytes))


def _layer_norm(y, g, b):
    mu = jnp.mean(y, axis=-1, keepdims=True)
    yc = y - mu
    var = jnp.mean(yc * yc, axis=-1, keepdims=True)
    out = yc * lax.rsqrt(var + LN_EPS) * g
    if b is not None:
        out = out + b
    return out


def _log_sigmoid(x):
    return -(jnp.maximum(-x, 0.0) + jnp.log1p(jnp.exp(-jnp.abs(x))))


def _dot(a, b):
    return jnp.dot(a, b, preferred_element_type=F32)


def _dot_nt(a, b):
    return lax.dot_general(a, b, (((1,), (1,)), ((), ())), preferred_element_type=F32)


def _ffn_kernel(x_ref, wg_ref, wu_ref, wo_ref, g_ref, b_ref, o_ref, *, nf):
    f = pl.program_id(1)

    @pl.when(f == 0)
    def _():
        o_ref[...] = jnp.zeros_like(o_ref)

    x = x_ref[...]
    gate = _dot(x, wg_ref[...])
    up = _dot(x, wu_ref[...])
    h = gate * jax.nn.sigmoid(gate) * up
    for n in range(D_MODEL // TN):
        o_ref[:, n * TN:(n + 1) * TN] += _dot(h, wo_ref[:, n * TN:(n + 1) * TN])

    @pl.when(f == nf - 1)
    def _():
        y = ALPHA * x_ref[...] + 0.5 * o_ref[...]
        o_ref[...] = _layer_norm(y, g_ref[...], b_ref[...])


def _ffn(x, w_ffn_in, w_ffn_out, ln_g, ln_b, layer, which, ln_idx):
    nf = D_FF // TF
    est = 4 * (4 * TM * D_MODEL + 2 * 3 * D_MODEL * TF + TM * D_MODEL + 4 * TM * TF) + (4 << 20)
    return pl.pallas_call(
        functools.partial(_ffn_kernel, nf=nf),
        grid=(M // TM, nf),
        in_specs=[
            pl.BlockSpec((TM, D_MODEL), lambda i, f: (i, 0)),
            pl.BlockSpec((None, None, D_MODEL, TF), lambda i, f: (layer, which, 0, f)),
            pl.BlockSpec((None, None, D_MODEL, TF), lambda i, f: (layer, which, 0, nf + f)),
            pl.BlockSpec((None, None, TF, D_MODEL), lambda i, f: (layer, which, f, 0)),
            pl.BlockSpec((None, 1, D_MODEL), lambda i, f: (ln_idx, 0, 0)),
            pl.BlockSpec((None, 1, D_MODEL), lambda i, f: (ln_idx, 0, 0)),
        ],
        out_specs=pl.BlockSpec((TM, D_MODEL), lambda i, f: (i, 0)),
        out_shape=jax.ShapeDtypeStruct((M, D_MODEL), F32),
        compiler_params=pltpu.CompilerParams(
            dimension_semantics=("parallel", "arbitrary"), vmem_limit_bytes=_vmem_limit(est)),
        name="ffn_ln",
    )(x, w_ffn_in, w_ffn_in, w_ffn_out, ln_g, ln_b)


def _proj_kernel(x_ref, w_ref, o_ref, *, w_transposed):
    if w_transposed:
        o_ref[...] = _dot_nt(x_ref[...], w_ref[...])
    else:
        o_ref[...] = _dot(x_ref[...], w_ref[...])


def _proj(x, w, layer_idx, n_cols, tn, name, w_transposed=False):
    est = 4 * (2 * TMP * D_MODEL + 2 * D_MODEL * tn + 3 * TMP * tn) + (4 << 20)
    if w_transposed:
        w_spec = pl.BlockSpec((None, tn, D_MODEL), lambda i, n: (layer_idx, n, 0))
    else:
        w_spec = pl.BlockSpec((None, D_MODEL, tn), lambda i, n: (layer_idx, 0, n))
    return pl.pallas_call(
        functools.partial(_proj_kernel, w_transposed=w_transposed),
        grid=(M // TMP, n_cols // tn),
        in_specs=[
            pl.BlockSpec((TMP, D_MODEL), lambda i, n: (i, 0)),
            w_spec,
        ],
        out_specs=pl.BlockSpec((TMP, tn), lambda i, n: (i, n)),
        out_shape=jax.ShapeDtypeStruct((M, n_cols), F32),
        compiler_params=pltpu.CompilerParams(
            dimension_semantics=("parallel", "arbitrary"), vmem_limit_bytes=_vmem_limit(est)),
        name=name,
    )(x, w)


def _outproj_kernel(x_ref, a_ref, w_ref, g_ref, b_ref, o_ref):
    y = ALPHA * x_ref[...] + _dot(a_ref[...], w_ref[...])
    o_ref[...] = _layer_norm(y, g_ref[...], b_ref[...])


def _outproj(x, mix, w_out, ln_g, ln_b, layer_idx, ln_idx):
    est = 4 * (D_MODEL * D_MODEL + 9 * TMO * D_MODEL) + (4 << 20)
    return pl.pallas_call(
        _outproj_kernel,
        grid=(M // TMO,),
        in_specs=[
            pl.BlockSpec((TMO, D_MODEL), lambda i: (i, 0)),
            pl.BlockSpec((TMO, D_MODEL), lambda i: (i, 0)),
            pl.BlockSpec((None, D_MODEL, D_MODEL), lambda i: (layer_idx, 0, 0),
                         pipeline_mode=pl.Buffered(1)),
            pl.BlockSpec((None, 1, D_MODEL), lambda i: (ln_idx, 0, 0)),
            pl.BlockSpec((None, 1, D_MODEL), lambda i: (ln_idx, 0, 0)),
        ],
        out_specs=pl.BlockSpec((TMO, D_MODEL), lambda i: (i, 0)),
        out_shape=jax.ShapeDtypeStruct((M, D_MODEL), F32),
        compiler_params=pltpu.CompilerParams(
            dimension_semantics=("parallel",), vmem_limit_bytes=_vmem_limit(est)),
        name="outproj_ln",
    )(x, mix, w_out, ln_g, ln_b)


def _split3(x):
    h1 = x.astype(jnp.bfloat16).astype(F32)
    r = x - h1
    h2 = r.astype(jnp.bfloat16).astype(F32)
    return h1, h2, r - h2


def _even_prompt_kernel(bg_ref, cg_ref, xin_ref, q_ref, k_ref, v_ref, og_ref, gt_ref,
                        wc_ref, bgate_ref, mhg_ref,
                        mix_ref, conv_ref, c_out_ref, n_out_ref, m_out_ref,
                        cbuf, c_sc, n_sc, m_sc):
    c = pl.program_id(1)
    L = LC

    @pl.when(c == 0)
    def _():
        cbuf[0:SUBLANES, :] = jnp.zeros((SUBLANES, D_CONV), F32)
        c_sc[...] = jnp.zeros_like(c_sc)
        n_sc[...] = jnp.zeros_like(n_sc)
        m_sc[...] = jnp.zeros_like(m_sc)

    cx = cg_ref[...] * xin_ref[...]
    cbuf[SUBLANES:SUBLANES + L, :] = cx
    c1 = cbuf[SUBLANES - 1:SUBLANES - 1 + L, :]
    c2 = cbuf[SUBLANES - 2:SUBLANES - 2 + L, :]
    conv = c2 * wc_ref[0:1, :] + c1 * wc_ref[1:2, :] + cx * wc_ref[2:3, :]
    mix_ref[:, 0:D_CONV] = bg_ref[...] * conv
    conv_ref[...] = cbuf[SUBLANES + L - 2:SUBLANES + L, :]
    cbuf[0:SUBLANES, :] = cbuf[L:L + SUBLANES, :]

    gt = gt_ref[...]
    li_all = gt[:, 0:HEADS] + bgate_ref[0:1, :]
    lf_all = _log_sigmoid(gt[:, HEADS:2 * HEADS] + bgate_ref[1:2, :])
    row = lax.broadcasted_iota(jnp.int32, (L, L), 0)
    col = lax.broadcasted_iota(jnp.int32, (L, L), 1)
    causal = row >= col
    tril = causal.astype(F32)
    f1, f2, f3 = _split3(lf_all)
    b_all = _dot(tril, f1) + _dot(tril, f2) + _dot(tril, f3)
    z = jnp.concatenate([li_all, b_all, jnp.zeros((L, LANES - 2 * HEADS), F32)], axis=1)
    zt = z.T

    for h in range(HEADS):
        li_c = li_all[:, h:h + 1]
        b_c = b_all[:, h:h + 1]
        li_r = zt[h:h + 1, :]
        b_r = zt[HEADS + h:HEADS + h + 1, :]
        m0 = m_sc[h, 0:1, 0:1]
        dm = jnp.where(causal, b_c - b_r + li_r, -jnp.inf)
        inter = b_c + m0
        m_t = jnp.maximum(inter, jnp.max(dm, axis=-1, keepdims=True))
        a_int = jnp.exp(inter - m_t)
        qh = q_ref[:, h * DK:(h + 1) * DK]
        kh = k_ref[:, h * DK:(h + 1) * DK] * (DK ** -0.5)
        vh = v_ref[:, h * DV:(h + 1) * DV]
        w = jnp.exp(dm - m_t) * _dot_nt(qh, kh)
        ch = c_sc[h]
        n_row = n_sc[h:h + 1, :]
        num = a_int * _dot(qh, ch) + _dot(w, vh)
        den = a_int * jnp.sum(qh * n_row, axis=-1, keepdims=True) + jnp.sum(w, axis=-1, keepdims=True)
        hh = num / jnp.maximum(jnp.abs(den), jnp.exp(-m_t))
        m_new = m_t[L - 1:L, :]
        b_last = b_c[L - 1:L, :]
        w_end = jnp.exp(b_last - b_c + li_c - m_new)
        decay = jnp.exp(b_last + m0 - m_new)
        wk = w_end * kh
        c_new = decay * ch + _dot(wk.T, vh)
        n_new = decay * n_row + jnp.sum(wk, axis=0, keepdims=True)
        c_sc[h] = c_new
        n_sc[h:h + 1, :] = n_new
        m_sc[h] = jnp.broadcast_to(m_new, (SUBLANES, LANES))
        c_out_ref[h] = c_new
        n_out_ref[h:h + 1, :] = n_new
        m_out_ref[0:1, h:h + 1] = m_new
        hn = _layer_norm(hh, mhg_ref[0:1, h * DV:(h + 1) * DV], None)
        og = og_ref[:, h * DV:(h + 1) * DV]
        mix_ref[:, D_CONV + h * DV:D_CONV + (h + 1) * DV] = hn * jax.nn.sigmoid(og)


def _even_prompt(proj, gates, w_conv, b_gates, mh_g, j):
    nck = SEQ // LC
    r = lambda b, c: b * nck + c
    est = 4 * (2 * (5 * LC * 1024 + 2 * LC * 512 + LC * 128) + 2 * LC * 2048
               + 3 * HEADS * DK * DV + 16 * LC * LC) + (8 << 20)
    return pl.pallas_call(
        _even_prompt_kernel,
        grid=(BATCH, nck),
        in_specs=[
            pl.BlockSpec((LC, D_CONV), lambda b, c: (r(b, c), 0)),
            pl.BlockSpec((LC, D_CONV), lambda b, c: (r(b, c), 1)),
            pl.BlockSpec((LC, D_CONV), lambda b, c: (r(b, c), 2)),
            pl.BlockSpec((LC, HEADS * DK), lambda b, c: (r(b, c), 6)),
            pl.BlockSpec((LC, HEADS * DK), lambda b, c: (r(b, c), 7)),
            pl.BlockSpec((LC, HEADS * DV), lambda b, c: (r(b, c), 4)),
            pl.BlockSpec((LC, HEADS * DV), lambda b, c: (r(b, c), 5)),
            pl.BlockSpec((LC, LANES), lambda b, c: (r(b, c), 0)),
            pl.BlockSpec((None, CONV_W, D_CONV), lambda b, c: (j, 0, 0)),
            pl.BlockSpec((None, 2, HEADS), lambda b, c: (j, 0, 0)),
            pl.BlockSpec((None, 1, HEADS * DV), lambda b, c: (j, 0, 0)),
        ],
        out_specs=[
            pl.BlockSpec((LC, D_MODEL), lambda b, c: (r(b, c), 0)),
            pl.BlockSpec((None, CONV_W - 1, D_CONV), lambda b, c: (b, 0, 0)),
            pl.BlockSpec((None, HEADS, DK, DV), lambda b, c: (b, 0, 0, 0)),
            pl.BlockSpec((None, HEADS, DK), lambda b, c: (b, 0, 0)),
            pl.BlockSpec((None, 1, HEADS), lambda b, c: (b, 0, 0)),
        ],
        out_shape=[
            jax.ShapeDtypeStruct((M, D_MODEL), F32),
            jax.ShapeDtypeStruct((BATCH, CONV_W - 1, D_CONV), F32),
            jax.ShapeDtypeStruct((BATCH, HEADS, DK, DV), F32),
            jax.ShapeDtypeStruct((BATCH, HEADS, DK), F32),
            jax.ShapeDtypeStruct((BATCH, 1, HEADS), F32),
        ],
        scratch_shapes=[
            pltpu.VMEM((LC + SUBLANES, D_CONV), F32),
            pltpu.VMEM((HEADS, DK, DV), F32),
            pltpu.VMEM((HEADS, DK), F32),
            pltpu.VMEM((HEADS, SUBLANES, LANES), F32),
        ],
        compiler_params=pltpu.CompilerParams(
            dimension_semantics=("arbitrary", "arbitrary"), vmem_limit_bytes=_vmem_limit(est)),
        name="even_prompt",
    )(proj, proj, proj, proj, proj, proj, proj, gates, w_conv, b_gates, mh_g)


def _sample_gates(gt, bgate_ref, m):
    li = gt[:, 0:HEADS] + bgate_ref[0:1, :]
    lf = _log_sigmoid(gt[:, HEADS:2 * HEADS] + bgate_ref[1:2, :])
    inter = lf + m
    m_t = jnp.maximum(inter, li)
    return jnp.exp(inter - m_t), jnp.exp(li - m_t), m_t


def _qk_cols_kernel(x_ref, w_ref, o_ref):
    t = _dot_nt(w_ref[...], x_ref[...])
    for s in range(DEC_BATCH // SB):
        o_ref[s] = t[:, s * SB:(s + 1) * SB]


def _qk_cols(x, w_even_t, j):
    nqk = 2 * HEADS * DK
    est = 4 * (2 * DEC_BATCH * D_MODEL + 2 * nqk * D_MODEL + 3 * nqk * LANES
               + 2 * (DEC_BATCH // SB) * nqk * LANES) + (4 << 20)
    return pl.pallas_call(
        _qk_cols_kernel,
        grid=(1,),
        in_specs=[
            pl.BlockSpec((DEC_BATCH, D_MODEL), lambda i: (MP // DEC_BATCH, 0)),
            pl.BlockSpec((None, nqk, D_MODEL), lambda i: (j, 3 * D_CONV // nqk, 0)),
        ],
        out_specs=pl.BlockSpec((DEC_BATCH // SB, nqk, SB), lambda i: (0, 0, 0)),
        out_shape=jax.ShapeDtypeStruct((DEC_BATCH // SB, nqk, SB), F32),
        compiler_params=pltpu.CompilerParams(
            dimension_semantics=("arbitrary",), vmem_limit_bytes=_vmem_limit(est)),
        name="qk_cols",
    )(x, w_even_t)


def _even_sample_state_kernel(c_ref, qk_ref, v_ref, gt_ref, m_ref, bgate_ref, *rest):
    c_out_ref, num_ref = rest[-2], rest[-1]
    a, wgt, _ = _sample_gates(gt_ref[...], bgate_ref, m_ref[...])
    for bi in range(SB):
        for h in range(HEADS):
            a_s = a[bi:bi + 1, h:h + 1]
            w_s = wgt[bi:bi + 1, h:h + 1]
            ch = c_ref[bi, h]
            qc = qk_ref[h * DK:(h + 1) * DK, bi:bi + 1]
            kc = qk_ref[(HEADS + h) * DK:(HEADS + h + 1) * DK, bi:bi + 1] * (DK ** -0.5)
            vr = v_ref[bi:bi + 1, h * DV:(h + 1) * DV]
            c_out_ref[bi, h] = a_s * ch + (w_s * kc) * vr
            num_ref[bi:bi + 1, h * DV:(h + 1) * DV] = jnp.sum(qc * ch, axis=0, keepdims=True)


def _even_sample_state(state_c, qk_cols, proj, gates, state_m, b_gates, j, c_prev):
    blk5 = (None, SB, HEADS, DK, DV)
    rb = MP // SB
    in_specs = [
        pl.BlockSpec(blk5, lambda i: (j, i, 0, 0, 0)),
        pl.BlockSpec((None, 2 * HEADS * DK, SB), lambda i: (i, 0, 0)),
        pl.BlockSpec((SB, HEADS * DV), lambda i: (rb + i, 4)),
        pl.BlockSpec((SB, LANES), lambda i: (rb + i, 0)),
        pl.BlockSpec((None, SB, HEADS), lambda i: (j, i, 0)),
        pl.BlockSpec((None, 2, HEADS), lambda i: (j, 0, 0)),
    ]
    args = [state_c, qk_cols, proj, gates, state_m, b_gates]
    aliases = {}
    if c_prev is not None:
        in_specs.append(pl.BlockSpec(memory_space=pl.ANY))
        args.append(c_prev)
        aliases = {len(args) - 1: 0}
    est = 4 * (4 * SB * HEADS * DK * DV + 4 * HEADS * DK * LANES) + (8 << 20)
    return pl.pallas_call(
        _even_sample_state_kernel,
        grid=(DEC_BATCH // SB,),
        in_specs=in_specs,
        out_specs=[
            pl.BlockSpec(blk5, lambda i: (j, i, 0, 0, 0)),
            pl.BlockSpec((SB, HEADS * DV), lambda i: (i, 0)),
        ],
        out_shape=[
            jax.ShapeDtypeStruct(state_c.shape, F32),
            jax.ShapeDtypeStruct((DEC_BATCH, HEADS * DV), F32),
        ],
        input_output_aliases=aliases,
        compiler_params=pltpu.CompilerParams(
            dimension_semantics=("arbitrary",), vmem_limit_bytes=_vmem_limit(est)),
        name="even_sample_state",
    )(*args)


def _even_sample_kernel(bg_ref, cg_ref, xin_ref, q_ref, k_ref, v_ref, og_ref, gt_ref,
                        num_ref, cst_ref, n_ref, m_ref, wc_ref, bgate_ref, mhg_ref, mixin_hbm,
                        mix_ref, conv_ref, n_out_ref, m_out_ref):
    del mixin_hbm
    cx = cg_ref[...] * xin_ref[...]
    st0 = cst_ref[:, 0:D_CONV]
    st1 = cst_ref[:, D_CONV:2 * D_CONV]
    conv = st0 * wc_ref[0:1, :] + st1 * wc_ref[1:2, :] + cx * wc_ref[2:3, :]
    mix_ref[:, 0:D_CONV] = bg_ref[...] * conv
    conv_ref[:, 0:D_CONV] = st1
    conv_ref[:, D_CONV:2 * D_CONV] = cx

    a, wgt, m_t = _sample_gates(gt_ref[...], bgate_ref, m_ref[...])
    m_out_ref[...] = m_t
    floor = jnp.exp(-m_t)
    for h in range(HEADS):
        a_h = a[:, h:h + 1]
        w_h = wgt[:, h:h + 1]
        qh = q_ref[:, h * DK:(h + 1) * DK]
        kh = k_ref[:, h * DK:(h + 1) * DK] * (DK ** -0.5)
        vh = v_ref[:, h * DV:(h + 1) * DV]
        nh = n_ref[:, h * DK:(h + 1) * DK]
        wt = w_h * jnp.sum(qh * kh, axis=-1, keepdims=True)
        num = a_h * num_ref[:, h * DV:(h + 1) * DV] + wt * vh
        den = a_h * jnp.sum(qh * nh, axis=-1, keepdims=True) + wt
        hh = num / jnp.maximum(jnp.abs(den), floor[:, h:h + 1])
        n_out_ref[:, h * DK:(h + 1) * DK] = a_h * nh + w_h * kh
        hn = _layer_norm(hh, mhg_ref[0:1, h * DV:(h + 1) * DV], None)
        og = og_ref[:, h * DV:(h + 1) * DV]
        mix_ref[:, D_CONV + h * DV:D_CONV + (h + 1) * DV] = hn * jax.nn.sigmoid(og)


def _even_sample(proj, gates, num, conv_st, n_st, m_st, w_conv, b_gates, mh_g, mixin, j):
    nb = DEC_BATCH
    rb = MP // nb
    est = 4 * 2 * (5 * nb * 1024 + 2 * nb * 512 + nb * 128 + nb * 1024 + nb * 2048 + nb * 512
                   + nb * 2048 + nb * 2048 + nb * 512) + (8 << 20)
    return pl.pallas_call(
        _even_sample_kernel,
        grid=(1,),
        in_specs=[
            pl.BlockSpec((nb, D_CONV), lambda i: (rb, 0)),
            pl.BlockSpec((nb, D_CONV), lambda i: (rb, 1)),
            pl.BlockSpec((nb, D_CONV), lambda i: (rb, 2)),
            pl.BlockSpec((nb, HEADS * DK), lambda i: (rb, 6)),
            pl.BlockSpec((nb, HEADS * DK), lambda i: (rb, 7)),
            pl.BlockSpec((nb, HEADS * DV), lambda i: (rb, 4)),
            pl.BlockSpec((nb, HEADS * DV), lambda i: (rb, 5)),
            pl.BlockSpec((nb, LANES), lambda i: (rb, 0)),
            pl.BlockSpec((nb, HEADS * DV), lambda i: (0, 0)),
            pl.BlockSpec((None, nb, 2 * D_CONV), lambda i: (j, 0, 0)),
            pl.BlockSpec((None, nb, HEADS * DK), lambda i: (j, 0, 0)),
            pl.BlockSpec((None, nb, HEADS), lambda i: (j, 0, 0)),
            pl.BlockSpec((None, CONV_W, D_CONV), lambda i: (j, 0, 0)),
            pl.BlockSpec((None, 2, HEADS), lambda i: (j, 0, 0)),
            pl.BlockSpec((None, 1, HEADS * DV), lambda i: (j, 0, 0)),
            pl.BlockSpec(memory_space=pl.ANY),
        ],
        out_specs=[
            pl.BlockSpec((nb, D_MODEL), lambda i: (rb, 0)),
            pl.BlockSpec((nb, 2 * D_CONV), lambda i: (0, 0)),
            pl.BlockSpec((nb, HEADS * DK), lambda i: (0, 0)),
            pl.BlockSpec((nb, HEADS), lambda i: (0, 0)),
        ],
        out_shape=[
            jax.ShapeDtypeStruct((M, D_MODEL), F32),
            jax.ShapeDtypeStruct((nb, 2 * D_CONV), F32),
            jax.ShapeDtypeStruct((nb, HEADS * DK), F32),
            jax.ShapeDtypeStruct((nb, HEADS), F32),
        ],
        input_output_aliases={15: 0},
        compiler_params=pltpu.CompilerParams(
            dimension_semantics=("arbitrary",), vmem_limit_bytes=_vmem_limit(est)),
        name="even_sample",
    )(proj, proj, proj, proj, proj, proj, proj, gates, num, conv_st, n_st, m_st,
      w_conv, b_gates, mh_g, mixin)


def _gmlp_norm(u_raw, v_raw, g, b):
    u = jax.nn.gelu(u_raw)
    vn = _layer_norm(jax.nn.gelu(v_raw), g, b)
    return u, vn


def _odd_prompt_kernel(p_ref, u_ref, v_ref, wp_ref, sc_ref, gmg_ref, gmb_ref, ws_ref, bst_ref,
                       mix_ref, pool_ref, gv_ref, pbuf):
    s = pl.program_id(1)
    R = RO

    @pl.when(s == 0)
    def _():
        pbuf[0:HIST, :] = jnp.zeros((HIST, D_POOL), F32)

    p = p_ref[...]
    pbuf[HIST:HIST + R, :] = p
    pos = s * R + lax.broadcasted_iota(jnp.int32, (R, 1), 0)
    for g, w in enumerate(POOL_WINDOWS):
        lo, hi = g * POOL_GW, (g + 1) * POOL_GW
        win = p[:, lo:hi]
        for jj in range(1, w):
            win = win + pbuf[HIST - jj:HIST - jj + R, lo:hi]
        cnt = jnp.minimum(w, pos + 1).astype(F32)
        diff = win / cnt - p[:, lo:hi]
        mix_ref[:, lo:hi] = _dot(diff, wp_ref[g]) * sc_ref[0:1, lo:hi]
    pool_ref[...] = pbuf[HIST + R - POOL_BUF:HIST + R, :]
    pbuf[0:HIST, :] = pbuf[R:R + HIST, :]

    u, vn = _gmlp_norm(u_ref[...], v_ref[...], gmg_ref[...], gmb_ref[...])
    L = GMLP_CHUNK
    tril = lax.broadcasted_iota(jnp.int32, (L, L), 0) >= lax.broadcasted_iota(jnp.int32, (L, L), 1)
    for g in range(D_GMLP // GMLP_GW):
        lo, hi = g * GMLP_GW, (g + 1) * GMLP_GW
        ws = jnp.where(tril, ws_ref[g], 0.0)
        bcol = bst_ref[:, g:g + 1]
        for ck in range(R // L):
            r0, r1 = ck * L, (ck + 1) * L
            sv = _dot(ws, vn[r0:r1, lo:hi]) + bcol
            mix_ref[r0:r1, D_POOL + lo:D_POOL + hi] = u[r0:r1, lo:hi] * sv
    gv_ref[...] = vn[R - L:R, :]


def _odd_prompt(proj, w_pool, pool_scale, gm_g, gm_b, w_spatial, bs_t, j):
    nrb = SEQ // RO
    r = lambda b, s: b * nrb + s
    est = 4 * (2 * 3 * RO * 1024 + 2 * RO * 2048 + 2 * 4 * 256 * 256 + 2 * 4 * 128 * 128
               + (RO + HIST) * 1024 + 8 * RO * 1024) + (8 << 20)
    return pl.pallas_call(
        _odd_prompt_kernel,
        grid=(BATCH, nrb),
        in_specs=[
            pl.BlockSpec((RO, D_POOL), lambda b, s: (r(b, s), 0)),
            pl.BlockSpec((RO, D_GMLP), lambda b, s: (r(b, s), 1)),
            pl.BlockSpec((RO, D_GMLP), lambda b, s: (r(b, s), 2)),
            pl.BlockSpec((None, 4, POOL_GW, POOL_GW), lambda b, s: (j, 0, 0, 0)),
            pl.BlockSpec((None, 1, D_POOL), lambda b, s: (j, 0, 0)),
            pl.BlockSpec((None, 1, D_GMLP), lambda b, s: (j, 0, 0)),
            pl.BlockSpec((None, 1, D_GMLP), lambda b, s: (j, 0, 0)),
            pl.BlockSpec((None, 4, GMLP_CHUNK, GMLP_CHUNK), lambda b, s: (j, 0, 0, 0)),
            pl.BlockSpec((None, GMLP_CHUNK, 4), lambda b, s: (j, 0, 0)),
        ],
        out_specs=[
            pl.BlockSpec((RO, D_MODEL), lambda b, s: (r(b, s), 0)),
            pl.BlockSpec((None, POOL_BUF, D_POOL), lambda b, s: (b, 0, 0)),
            pl.BlockSpec((None, GMLP_CHUNK, D_GMLP), lambda b, s: (b, 0, 0)),
        ],
        out_shape=[
            jax.ShapeDtypeStruct((M, D_MODEL), F32),
            jax.ShapeDtypeStruct((BATCH, POOL_BUF, D_POOL), F32),
            jax.ShapeDtypeStruct((BATCH, GMLP_CHUNK, D_GMLP), F32),
        ],
        scratch_shapes=[pltpu.VMEM((RO + HIST, D_POOL), F32)],
        compiler_params=pltpu.CompilerParams(
            dimension_semantics=("arbitrary", "arbitrary"), vmem_limit_bytes=_vmem_limit(est)),
        name="odd_prompt",
    )(proj, proj, proj, w_pool, pool_scale, gm_g, gm_b, w_spatial, bs_t)


def _odd_sample_kernel(p_ref, u_ref, v_ref, st_ref, wp_ref, sc_ref, gmg_ref, gmb_ref, ws_ref, bst_ref,
                       *rest):
    mix_ref, pool_ref, gv_ref = rest[-3:]
    p = p_ref[...]
    for r in range(POOL_BUF - 1):
        pool_ref[r] = st_ref[r + 1]
    pool_ref[POOL_BUF - 1] = p
    for g, w in enumerate(POOL_WINDOWS):
        lo, hi = g * POOL_GW, (g + 1) * POOL_GW
        win = p[:, lo:hi]
        for jj in range(1, w):
            win = win + st_ref[POOL_BUF - jj, :, lo:hi]
        cnt = float(min(w, PAST_LEN + 1))
        diff = win / cnt - p[:, lo:hi]
        mix_ref[:, lo:hi] = _dot(diff, wp_ref[g]) * sc_ref[0:1, lo:hi]
    u, vn = _gmlp_norm(u_ref[...], v_ref[...], gmg_ref[...], gmb_ref[...])
    gv_ref[...] = vn
    for g in range(D_GMLP // GMLP_GW):
        lo, hi = g * GMLP_GW, (g + 1) * GMLP_GW
        sv = ws_ref[g, 0:1, 0:1] * vn[:, lo:hi] + bst_ref[0:1, g:g + 1]
        mix_ref[:, D_POOL + lo:D_POOL + hi] = u[:, lo:hi] * sv


def _odd_sample(proj, pool_st, w_pool, pool_scale, gm_g, gm_b, w_spatial, bs_t, mixin, pool_prev, j):
    nb = DEC_BATCH
    rb = MP // nb
    st_blk = (None, POOL_BUF, nb, D_POOL)
    in_specs = [
        pl.BlockSpec((nb, D_POOL), lambda i: (rb, 0)),
        pl.BlockSpec((nb, D_GMLP), lambda i: (rb, 1)),
        pl.BlockSpec((nb, D_GMLP), lambda i: (rb, 2)),
        pl.BlockSpec(st_blk, lambda i: (j, 0, 0, 0)),
        pl.BlockSpec((None, 4, POOL_GW, POOL_GW), lambda i: (j, 0, 0, 0)),
        pl.BlockSpec((None, 1, D_POOL), lambda i: (j, 0, 0)),
        pl.BlockSpec((None, 1, D_GMLP), lambda i: (j, 0, 0)),
        pl.BlockSpec((None, 1, D_GMLP), lambda i: (j, 0, 0)),
        pl.BlockSpec((None, 4, GMLP_CHUNK, GMLP_CHUNK), lambda i: (j, 0, 0, 0)),
        pl.BlockSpec((None, GMLP_CHUNK, 4), lambda i: (j, 0, 0)),
        pl.BlockSpec(memory_space=pl.ANY),
    ]
    args = [proj, proj, proj, pool_st, w_pool, pool_scale, gm_g, gm_b, w_spatial, bs_t, mixin]
    aliases = {len(args) - 1: 0}
    if pool_prev is not None:
        in_specs.append(pl.BlockSpec(memory_space=pl.ANY))
        args.append(pool_prev)
        aliases[len(args) - 1] = 1
    est = 4 * (4 * POOL_BUF * nb * D_POOL + 2 * (3 * nb * 1024 + 4 * 256 * 256 + 4 * 128 * 128
                                                + nb * 2048 + nb * 1024)) + (8 << 20)
    return pl.pallas_call(
        _odd_sample_kernel,
        grid=(1,),
        in_specs=in_specs,
        out_specs=[
            pl.BlockSpec((nb, D_MODEL), lambda i: (rb, 0)),
            pl.BlockSpec(st_blk, lambda i: (j, 0, 0, 0)),
            pl.BlockSpec((nb, D_GMLP), lambda i: (0, 0)),
        ],
        out_shape=[
            jax.ShapeDtypeStruct((M, D_MODEL), F32),
            jax.ShapeDtypeStruct(pool_st.shape, F32),
            jax.ShapeDtypeStruct((nb, D_GMLP), F32),
        ],
        input_output_aliases=aliases,
        compiler_params=pltpu.CompilerParams(
            dimension_semantics=("arbitrary",), vmem_limit_bytes=_vmem_limit(est)),
        name="odd_sample",
    )(*args)


def kernel(x_prompt, x_sample, state_conv, state_mlstm_C, state_mlstm_n, state_mlstm_m, state_pool,
           ln_g, ln_b, w_ffn_in, w_ffn_out, w_in_even, b_gates_even, w_conv, mh_norm_g, w_out_even,
           w_in_odd, w_pool, pool_scale, gm_ln_g, gm_ln_b, w_spatial, b_spatial, w_out_odd):
    n_even, n_odd = w_in_even.shape[0], w_in_odd.shape[0]
    x = jnp.concatenate([x_prompt.reshape(MP, D_MODEL), x_sample.reshape(DEC_BATCH, D_MODEL)], axis=0)

    ln_g3 = ln_g.reshape(DEPTH * 3, 1, D_MODEL)
    ln_b3 = ln_b.reshape(DEPTH * 3, 1, D_MODEL)
    w_even_t = jnp.swapaxes(w_in_even, 1, 2)
    w_gate_t = jnp.pad(w_even_t[:, EVEN_MAIN:, :], ((0, 0), (0, LANES - 2 * HEADS), (0, 0)))
    pool_st = jnp.swapaxes(state_pool, 1, 2)
    mh_g3 = mh_norm_g.reshape(n_even, 1, HEADS * DV)
    conv_st = state_conv.reshape(n_even, DEC_BATCH, (CONV_W - 1) * D_CONV)
    n_st = state_mlstm_n.reshape(n_even, DEC_BATCH, HEADS * DK)
    scale3 = pool_scale.reshape(n_odd, 1, D_POOL)
    gm_g3 = gm_ln_g.reshape(n_odd, 1, D_GMLP)
    gm_b3 = gm_ln_b.reshape(n_odd, 1, D_GMLP)
    bs_t = jnp.swapaxes(b_spatial, 1, 2)

    conv_p, conv_s, c_p, n_p, n_s, m_p, m_s = [], [], [], [], [], [], []
    pool_p, gv_p, gv_s = [], [], []
    c_s = None
    pool_s = None

    for layer in range(DEPTH):
        j = layer // 2
        x = _ffn(x, w_ffn_in, w_ffn_out, ln_g3, ln_b3, layer, 0, 3 * layer)
        if layer % 2 == 0:
            proj = _proj(x, w_even_t, j, EVEN_MAIN, TN, "proj_even", w_transposed=True)
            gates = _proj(x, w_gate_t, j, LANES, LANES, "proj_gates", w_transposed=True)
            mixin, cv, cc, nn, mm = _even_prompt(proj, gates, w_conv, b_gates_even, mh_g3, j)
            conv_p.append(cv)
            c_p.append(cc)
            n_p.append(nn)
            m_p.append(mm.reshape(BATCH, HEADS))
            qk_cols = _qk_cols(x, w_even_t, j)
            c_s, num = _even_sample_state(state_mlstm_C, qk_cols, proj, gates, state_mlstm_m,
                                          b_gates_even, j, c_s)
            mixin, cvs, nns, mms = _even_sample(proj, gates, num, conv_st, n_st, state_mlstm_m,
                                                w_conv, b_gates_even, mh_g3, mixin, j)
            conv_s.append(cvs.reshape(DEC_BATCH, CONV_W - 1, D_CONV))
            n_s.append(nns.reshape(DEC_BATCH, HEADS, DK))
            m_s.append(mms)
            x = _outproj(x, mixin, w_out_even, ln_g3, ln_b3, j, 3 * layer + 1)
        else:
            proj = _proj(x, w_in_odd, j, ODD_IN, TN, "proj_odd")
            mixin, pp, gv = _odd_prompt(proj, w_pool, scale3, gm_g3, gm_b3, w_spatial, bs_t, j)
            pool_p.append(pp)
            gv_p.append(gv)
            mixin, pool_s, gvs = _odd_sample(proj, pool_st, w_pool, scale3, gm_g3, gm_b3, w_spatial, bs_t,
                                             mixin, pool_s, j)
            gv_s.append(gvs.reshape(DEC_BATCH, 1, D_GMLP))
            x = _outproj(x, mixin, w_out_odd, ln_g3, ln_b3, j, 3 * layer + 1)
        x = _ffn(x, w_ffn_in, w_ffn_out, ln_g3, ln_b3, layer, 1, 3 * layer + 2)

    y_prompt = x[:MP].reshape(BATCH, SEQ, D_MODEL)
    y_sample = x[MP:].reshape(DEC_BATCH, 1, D_MODEL)
    return (y_prompt, y_sample,
            jnp.stack(conv_p), jnp.stack(conv_s),
            jnp.stack(c_p), c_s,
            jnp.stack(n_p), jnp.stack(n_s),
            jnp.stack(m_p), jnp.stack(m_s),
            jnp.stack(pool_p), jnp.swapaxes(pool_s, 1, 2),
            jnp.stack(gv_p), jnp.stack(gv_s))
```

```python
import functools

import jax
import jax.numpy as jnp
from jax import lax
from jax.experimental import pallas as pl
from jax.experimental.pallas import tpu as pltpu

F32 = jnp.float32

D_MODEL = 2048
BATCH = 4
SEQ = 2048
DEPTH = 4
DEC_BATCH = 128
PAST_LEN = 16384
D_FF = 5632
D_CONV = 1024
CONV_W = 3
HEADS = 4
DK = 128
DV = 256
D_POOL = 1024
POOL_WINDOWS = (2, 4, 8, 16)
POOL_GW = 256
POOL_BUF = 15
D_GMLP = 1024
GMLP_GW = 256
GMLP_CHUNK = 128
ALPHA = (2 * DEPTH) ** 0.25
LN_EPS = 1e-5
EVEN_MAIN = 3 * D_CONV + 2 * HEADS * DK + 2 * HEADS * DV
ODD_IN = D_POOL + 2 * D_GMLP

MP = BATCH * SEQ
M = MP + DEC_BATCH

LANES = 128
SUBLANES = 8
VMEM_BYTES_V7X = 64 * 1024 * 1024

TM = 1040
TF = 256
TN = 512
TMP = 1664
TMO = 416
LC = 256
RE = 512
RO = 256
SB = 16
HIST = 16


def _vmem_limit(nbytes):
    return int(min(VMEM_BYTES_V7X - 4 * 1024 * 1024, nbytes))


def _layer_norm(y, g, b):
    mu = jnp.mean(y, axis=-1, keepdims=True)
    yc = y - mu
    var = jnp.mean(yc * yc, axis=-1, keepdims=True)
    out = yc * lax.rsqrt(var + LN_EPS) * g
    if b is not None:
        out = out + b
    return out


def _log_sigmoid(x):
    return -(jnp.maximum(-x, 0.0) + jnp.log1p(jnp.exp(-jnp.abs(x))))


def _dot(a, b):
    return jnp.dot(a, b, preferred_element_type=F32)


def _dot_nt(a, b):
    return lax.dot_general(a, b, (((1,), (1,)), ((), ())), preferred_element_type=F32)


def _ffn_kernel(x_ref, wg_ref, wu_ref, wo_ref, g_ref, b_ref, o_ref, *maybe_sample_ref, nf):
    f = pl.program_id(1)

    @pl.when(f == 0)
    def _():
        o_ref[...] = jnp.zeros_like(o_ref)

    x = x_ref[...]
    gate = _dot(x, wg_ref[...])
    up = _dot(x, wu_ref[...])
    h = gate * jax.nn.sigmoid(gate) * up
    for n in range(D_MODEL // TN):
        o_ref[:, n * TN:(n + 1) * TN] += _dot(h, wo_ref[:, n * TN:(n + 1) * TN])

    @pl.when(f == nf - 1)
    def _():
        y = ALPHA * x_ref[...] + 0.5 * o_ref[...]
        o_ref[...] = _layer_norm(y, g_ref[...], b_ref[...])

    if maybe_sample_ref:
        @pl.when((f == nf - 1) & (pl.program_id(0) == M // TM - 1))
        def _():
            maybe_sample_ref[0][...] = o_ref[TM - DEC_BATCH:, :]


def _ffn(x, w_ffn_in, w_ffn_out, ln_g, ln_b, layer, which, ln_idx, split_out=False):
    nf = D_FF // TF
    est = 4 * (4 * TM * D_MODEL + 2 * 3 * D_MODEL * TF + TM * D_MODEL + 4 * TM * TF) + (4 << 20)
    out_specs = pl.BlockSpec((TM, D_MODEL), lambda i, f: (i, 0))
    out_shape = jax.ShapeDtypeStruct((M, D_MODEL), F32)
    if split_out:
        out_specs = [out_specs, pl.BlockSpec((DEC_BATCH, D_MODEL), lambda i, f: (0, 0))]
        out_shape = [jax.ShapeDtypeStruct((MP, D_MODEL), F32),
                     jax.ShapeDtypeStruct((DEC_BATCH, D_MODEL), F32)]
    return pl.pallas_call(
        functools.partial(_ffn_kernel, nf=nf),
        grid=(M // TM, nf),
        in_specs=[
            pl.BlockSpec((TM, D_MODEL), lambda i, f: (i, 0)),
            pl.BlockSpec((None, None, D_MODEL, TF), lambda i, f: (layer, which, 0, f)),
            pl.BlockSpec((None, None, D_MODEL, TF), lambda i, f: (layer, which, 0, nf + f)),
            pl.BlockSpec((None, None, TF, D_MODEL), lambda i, f: (layer, which, f, 0)),
            pl.BlockSpec((None, 1, D_MODEL), lambda i, f: (ln_idx, 0, 0)),
            pl.BlockSpec((None, 1, D_MODEL), lambda i, f: (ln_idx, 0, 0)),
        ],
        out_specs=out_specs,
        out_shape=out_shape,
        compiler_params=pltpu.CompilerParams(
            dimension_semantics=("arbitrary", "arbitrary"), vmem_limit_bytes=_vmem_limit(est)),
        name="ffn_ln",
    )(x, w_ffn_in, w_ffn_in, w_ffn_out, ln_g, ln_b)


def _proj_kernel(x_ref, w_ref, *rest, w_transposed, with_gates):
    if with_gates:
        wgate_ref, o_ref, gate_ref = rest

        @pl.when(pl.program_id(1) == 0)
        def _():
            gate_ref[...] = _dot_nt(x_ref[...], wgate_ref[...])
    else:
        (o_ref,) = rest
    if w_transposed:
        o_ref[...] = _dot_nt(x_ref[...], w_ref[...])
    else:
        o_ref[...] = _dot(x_ref[...], w_ref[...])


def _proj(x, w, layer_idx, n_cols, tn, name, w_transposed=False, w_gate_t=None):
    est = 4 * (2 * TMP * D_MODEL + 2 * D_MODEL * tn + 3 * TMP * tn) + (4 << 20)
    if w_transposed:
        w_spec = pl.BlockSpec((None, tn, D_MODEL), lambda i, n: (layer_idx, n, 0))
    else:
        w_spec = pl.BlockSpec((None, D_MODEL, tn), lambda i, n: (layer_idx, 0, n))
    in_specs = [pl.BlockSpec((TMP, D_MODEL), lambda i, n: (i, 0)), w_spec]
    args = [x, w]
    out_specs = pl.BlockSpec((TMP, tn), lambda i, n: (i, n))
    out_shape = jax.ShapeDtypeStruct((M, n_cols), F32)
    if w_gate_t is not None:
        in_specs.append(pl.BlockSpec((None, LANES, D_MODEL), lambda i, n: (layer_idx, 0, 0)))
        args.append(w_gate_t)
        out_specs = [out_specs, pl.BlockSpec((TMP, LANES), lambda i, n: (i, 0))]
        out_shape = [out_shape, jax.ShapeDtypeStruct((M, LANES), F32)]
        est += 4 * (2 * LANES * D_MODEL + 3 * TMP * LANES)
    return pl.pallas_call(
        functools.partial(_proj_kernel, w_transposed=w_transposed, with_gates=w_gate_t is not None),
        grid=(M // TMP, n_cols // tn),
        in_specs=in_specs,
        out_specs=out_specs,
        out_shape=out_shape,
        compiler_params=pltpu.CompilerParams(
            dimension_semantics=("parallel", "arbitrary"), vmem_limit_bytes=_vmem_limit(est)),
        name=name,
    )(*args)


def _outproj_kernel(x_ref, a_ref, w_ref, g_ref, b_ref, o_ref):
    y = ALPHA * x_ref[...] + _dot(a_ref[...], w_ref[...])
    o_ref[...] = _layer_norm(y, g_ref[...], b_ref[...])


def _outproj(x, mix, w_out, ln_g, ln_b, layer_idx, ln_idx):
    est = 4 * (D_MODEL * D_MODEL + 9 * TMO * D_MODEL) + (4 << 20)
    return pl.pallas_call(
        _outproj_kernel,
        grid=(M // TMO,),
        in_specs=[
            pl.BlockSpec((TMO, D_MODEL), lambda i: (i, 0)),
            pl.BlockSpec((TMO, D_MODEL), lambda i: (i, 0)),
            pl.BlockSpec((None, D_MODEL, D_MODEL), lambda i: (layer_idx, 0, 0),
                         pipeline_mode=pl.Buffered(1)),
            pl.BlockSpec((None, 1, D_MODEL), lambda i: (ln_idx, 0, 0)),
            pl.BlockSpec((None, 1, D_MODEL), lambda i: (ln_idx, 0, 0)),
        ],
        out_specs=pl.BlockSpec((TMO, D_MODEL), lambda i: (i, 0)),
        out_shape=jax.ShapeDtypeStruct((M, D_MODEL), F32),
        compiler_params=pltpu.CompilerParams(
            dimension_semantics=("parallel",), vmem_limit_bytes=_vmem_limit(est)),
        name="outproj_ln",
    )(x, mix, w_out, ln_g, ln_b)


def _split3(x):
    h1 = x.astype(jnp.bfloat16).astype(F32)
    r = x - h1
    h2 = r.astype(jnp.bfloat16).astype(F32)
    return h1, h2, r - h2


def _even_prompt_kernel(bg_ref, cg_ref, xin_ref, q_ref, k_ref, v_ref, og_ref, gt_ref,
                        wc_ref, bgate_ref, mhg_ref,
                        mix_ref, conv_ref, c_out_ref, n_out_ref, m_out_ref,
                        cbuf, c_sc, n_sc, m_sc):
    c = pl.program_id(1)
    L = LC
    R = RE

    @pl.when(c == 0)
    def _():
        cbuf[0:SUBLANES, :] = jnp.zeros((SUBLANES, D_CONV), F32)
        c_sc[...] = jnp.zeros_like(c_sc)
        n_sc[...] = jnp.zeros_like(n_sc)
        m_sc[...] = jnp.zeros_like(m_sc)

    cx = cg_ref[...] * xin_ref[...]
    cbuf[SUBLANES:SUBLANES + R, :] = cx
    c1 = cbuf[SUBLANES - 1:SUBLANES - 1 + R, :]
    c2 = cbuf[SUBLANES - 2:SUBLANES - 2 + R, :]
    conv = c2 * wc_ref[0:1, :] + c1 * wc_ref[1:2, :] + cx * wc_ref[2:3, :]
    mix_ref[:, 0:D_CONV] = bg_ref[...] * conv
    conv_ref[...] = cbuf[SUBLANES + R - 2:SUBLANES + R, :]
    cbuf[0:SUBLANES, :] = cbuf[R:R + SUBLANES, :]

    row = lax.broadcasted_iota(jnp.int32, (L, L), 0)
    col = lax.broadcasted_iota(jnp.int32, (L, L), 1)
    causal = row >= col
    tril = causal.astype(F32)
    c_state = [c_sc[h] for h in range(HEADS)]
    n_state = [n_sc[h:h + 1, :] for h in range(HEADS)]
    m_state = [m_sc[h, 0:1, 0:1] for h in range(HEADS)]

    for s in range(R // L):
        r0, r1 = s * L, (s + 1) * L
        gt = gt_ref[r0:r1, :]
        li_all = gt[:, 0:HEADS] + bgate_ref[0:1, :]
        lf_all = _log_sigmoid(gt[:, HEADS:2 * HEADS] + bgate_ref[1:2, :])
        f1, f2, f3 = _split3(lf_all)
        b_all = _dot(tril, f1) + _dot(tril, f2) + _dot(tril, f3)
        z = jnp.concatenate([li_all, b_all, jnp.zeros((L, LANES - 2 * HEADS), F32)], axis=1)
        zt = z.T

        for h in range(HEADS):
            li_c = li_all[:, h:h + 1]
            b_c = b_all[:, h:h + 1]
            li_r = zt[h:h + 1, :]
            b_r = zt[HEADS + h:HEADS + h + 1, :]
            m0 = m_state[h]
            dm = jnp.where(causal, b_c - b_r + li_r, -jnp.inf)
            inter = b_c + m0
            m_t = jnp.maximum(inter, jnp.max(dm, axis=-1, keepdims=True))
            a_int = jnp.exp(inter - m_t)
            qh = q_ref[r0:r1, h * DK:(h + 1) * DK]
            kh = k_ref[r0:r1, h * DK:(h + 1) * DK] * (DK ** -0.5)
            vh = v_ref[r0:r1, h * DV:(h + 1) * DV]
            w = jnp.exp(dm - m_t) * _dot_nt(qh, kh)
            ch = c_state[h]
            n_row = n_state[h]
            num = a_int * _dot(qh, ch) + _dot(w, vh)
            den = (a_int * jnp.sum(qh * n_row, axis=-1, keepdims=True)
                   + jnp.sum(w, axis=-1, keepdims=True))
            hh = num / jnp.maximum(jnp.abs(den), jnp.exp(-m_t))
            m_new = m_t[L - 1:L, :]
            b_last = b_c[L - 1:L, :]
            w_end = jnp.exp(b_last - b_c + li_c - m_new)
            decay = jnp.exp(b_last + m0 - m_new)
            wk = w_end * kh
            c_state[h] = decay * ch + _dot(wk.T, vh)
            n_state[h] = decay * n_row + jnp.sum(wk, axis=0, keepdims=True)
            m_state[h] = m_new
            hn = _layer_norm(hh, mhg_ref[0:1, h * DV:(h + 1) * DV], None)
            og = og_ref[r0:r1, h * DV:(h + 1) * DV]
            mix_ref[r0:r1, D_CONV + h * DV:D_CONV + (h + 1) * DV] = hn * jax.nn.sigmoid(og)

    for h in range(HEADS):
        c_sc[h] = c_state[h]
        n_sc[h:h + 1, :] = n_state[h]
        m_sc[h] = jnp.broadcast_to(m_state[h], (SUBLANES, LANES))
        c_out_ref[h] = c_state[h]
        n_out_ref[h:h + 1, :] = n_state[h]
        m_out_ref[0:1, h:h + 1] = m_state[h]


def _even_prompt(proj, gates, w_conv, b_gates, mh_g, j):
    nck = SEQ // RE
    r = lambda b, c: b * nck + c
    est = 4 * (2 * (5 * RE * 1024 + 2 * RE * 512 + RE * 128) + 2 * RE * 2048 + 4 * RE * 1024
               + 3 * HEADS * DK * DV + 24 * LC * LC) + (8 << 20)
    return pl.pallas_call(
        _even_prompt_kernel,
        grid=(BATCH, nck),
        in_specs=[
            pl.BlockSpec((RE, D_CONV), lambda b, c: (r(b, c), 0)),
            pl.BlockSpec((RE, D_CONV), lambda b, c: (r(b, c), 1)),
            pl.BlockSpec((RE, D_CONV), lambda b, c: (r(b, c), 2)),
            pl.BlockSpec((RE, HEADS * DK), lambda b, c: (r(b, c), 6)),
            pl.BlockSpec((RE, HEADS * DK), lambda b, c: (r(b, c), 7)),
            pl.BlockSpec((RE, HEADS * DV), lambda b, c: (r(b, c), 4)),
            pl.BlockSpec((RE, HEADS * DV), lambda b, c: (r(b, c), 5)),
            pl.BlockSpec((RE, LANES), lambda b, c: (r(b, c), 0)),
            pl.BlockSpec((None, CONV_W, D_CONV), lambda b, c: (j, 0, 0)),
            pl.BlockSpec((None, 2, HEADS), lambda b, c: (j, 0, 0)),
            pl.BlockSpec((None, 1, HEADS * DV), lambda b, c: (j, 0, 0)),
        ],
        out_specs=[
            pl.BlockSpec((RE, D_MODEL), lambda b, c: (r(b, c), 0)),
            pl.BlockSpec((None, CONV_W - 1, D_CONV), lambda b, c: (b, 0, 0)),
            pl.BlockSpec((None, HEADS, DK, DV), lambda b, c: (b, 0, 0, 0)),
            pl.BlockSpec((None, HEADS, DK), lambda b, c: (b, 0, 0)),
            pl.BlockSpec((None, 1, HEADS), lambda b, c: (b, 0, 0)),
        ],
        out_shape=[
            jax.ShapeDtypeStruct((M, D_MODEL), F32),
            jax.ShapeDtypeStruct((BATCH, CONV_W - 1, D_CONV), F32),
            jax.ShapeDtypeStruct((BATCH, HEADS, DK, DV), F32),
            jax.ShapeDtypeStruct((BATCH, HEADS, DK), F32),
            jax.ShapeDtypeStruct((BATCH, 1, HEADS), F32),
        ],
        scratch_shapes=[
            pltpu.VMEM((RE + SUBLANES, D_CONV), F32),
            pltpu.VMEM((HEADS, DK, DV), F32),
            pltpu.VMEM((HEADS, DK), F32),
            pltpu.VMEM((HEADS, SUBLANES, LANES), F32),
        ],
        compiler_params=pltpu.CompilerParams(
            dimension_semantics=("arbitrary", "arbitrary"), vmem_limit_bytes=_vmem_limit(est)),
        name="even_prompt",
    )(proj, proj, proj, proj, proj, proj, proj, gates, w_conv, b_gates, mh_g)


def _sample_gates(gt, bgate_ref, m):
    li = gt[:, 0:HEADS] + bgate_ref[0:1, :]
    lf = _log_sigmoid(gt[:, HEADS:2 * HEADS] + bgate_ref[1:2, :])
    inter = lf + m
    m_t = jnp.maximum(inter, li)
    return jnp.exp(inter - m_t), jnp.exp(li - m_t), m_t


def _qk_cols_kernel(x_ref, w_ref, o_ref):
    t = _dot_nt(w_ref[...], x_ref[...])
    for s in range(DEC_BATCH // SB):
        o_ref[s] = t[:, s * SB:(s + 1) * SB]


def _qk_cols(x, w_even_t, j):
    nqk = 2 * HEADS * DK
    est = 4 * (2 * DEC_BATCH * D_MODEL + 2 * nqk * D_MODEL + 3 * nqk * LANES
               + 2 * (DEC_BATCH // SB) * nqk * LANES) + (4 << 20)
    return pl.pallas_call(
        _qk_cols_kernel,
        grid=(1,),
        in_specs=[
            pl.BlockSpec((DEC_BATCH, D_MODEL), lambda i: (MP // DEC_BATCH, 0)),
            pl.BlockSpec((None, nqk, D_MODEL), lambda i: (j, 3 * D_CONV // nqk, 0)),
        ],
        out_specs=pl.BlockSpec((DEC_BATCH // SB, nqk, SB), lambda i: (0, 0, 0)),
        out_shape=jax.ShapeDtypeStruct((DEC_BATCH // SB, nqk, SB), F32),
        compiler_params=pltpu.CompilerParams(
            dimension_semantics=("arbitrary",), vmem_limit_bytes=_vmem_limit(est)),
        name="qk_cols",
    )(x, w_even_t)


def _even_sample_state_kernel(c_ref, qk_ref, v_ref, gt_ref, m_ref, bgate_ref, *rest):
    c_out_ref, num_ref = rest[-2], rest[-1]
    a, wgt, _ = _sample_gates(gt_ref[...], bgate_ref, m_ref[...])
    for bi in range(SB):
        for h in range(HEADS):
            a_s = a[bi:bi + 1, h:h + 1]
            w_s = wgt[bi:bi + 1, h:h + 1]
            ch = c_ref[bi, h]
            qc = qk_ref[h * DK:(h + 1) * DK, bi:bi + 1]
            kc = qk_ref[(HEADS + h) * DK:(HEADS + h + 1) * DK, bi:bi + 1] * (DK ** -0.5)
            vr = v_ref[bi:bi + 1, h * DV:(h + 1) * DV]
            c_out_ref[bi, h] = a_s * ch + (w_s * kc) * vr
            num_ref[bi:bi + 1, h * DV:(h + 1) * DV] = jnp.sum(qc * ch, axis=0, keepdims=True)


def _even_sample_state(state_c, qk_cols, proj, gates, state_m, b_gates, j, c_prev):
    blk5 = (None, SB, HEADS, DK, DV)
    rb = MP // SB
    in_specs = [
        pl.BlockSpec(blk5, lambda i: (j, i, 0, 0, 0)),
        pl.BlockSpec((None, 2 * HEADS * DK, SB), lambda i: (i, 0, 0)),
        pl.BlockSpec((SB, HEADS * DV), lambda i: (rb + i, 4)),
        pl.BlockSpec((SB, LANES), lambda i: (rb + i, 0)),
        pl.BlockSpec((None, SB, HEADS), lambda i: (j, i, 0)),
        pl.BlockSpec((None, 2, HEADS), lambda i: (j, 0, 0)),
    ]
    args = [state_c, qk_cols, proj, gates, state_m, b_gates]
    aliases = {}
    if c_prev is not None:
        in_specs.append(pl.BlockSpec(memory_space=pl.ANY))
        args.append(c_prev)
        aliases = {len(args) - 1: 0}
    est = 4 * (4 * SB * HEADS * DK * DV + 4 * HEADS * DK * LANES) + (8 << 20)
    return pl.pallas_call(
        _even_sample_state_kernel,
        grid=(DEC_BATCH // SB,),
        in_specs=in_specs,
        out_specs=[
            pl.BlockSpec(blk5, lambda i: (j, i, 0, 0, 0)),
            pl.BlockSpec((SB, HEADS * DV), lambda i: (i, 0)),
        ],
        out_shape=[
            jax.ShapeDtypeStruct(state_c.shape, F32),
            jax.ShapeDtypeStruct((DEC_BATCH, HEADS * DV), F32),
        ],
        input_output_aliases=aliases,
        compiler_params=pltpu.CompilerParams(
            dimension_semantics=("arbitrary",), vmem_limit_bytes=_vmem_limit(est)),
        name="even_sample_state",
    )(*args)


def _even_sample_kernel(bg_ref, cg_ref, xin_ref, q_ref, k_ref, v_ref, og_ref, gt_ref,
                        num_ref, cst_ref, n_ref, m_ref, wc_ref, bgate_ref, mhg_ref, mixin_hbm,
                        mix_ref, conv_ref, n_out_ref, m_out_ref):
    del mixin_hbm
    cx = cg_ref[...] * xin_ref[...]
    st0 = cst_ref[:, 0:D_CONV]
    st1 = cst_ref[:, D_CONV:2 * D_CONV]
    conv = st0 * wc_ref[0:1, :] + st1 * wc_ref[1:2, :] + cx * wc_ref[2:3, :]
    mix_ref[:, 0:D_CONV] = bg_ref[...] * conv
    conv_ref[:, 0:D_CONV] = st1
    conv_ref[:, D_CONV:2 * D_CONV] = cx

    a, wgt, m_t = _sample_gates(gt_ref[...], bgate_ref, m_ref[...])
    m_out_ref[...] = m_t
    floor = jnp.exp(-m_t)
    for h in range(HEADS):
        a_h = a[:, h:h + 1]
        w_h = wgt[:, h:h + 1]
        qh = q_ref[:, h * DK:(h + 1) * DK]
        kh = k_ref[:, h * DK:(h + 1) * DK] * (DK ** -0.5)
        vh = v_ref[:, h * DV:(h + 1) * DV]
        nh = n_ref[:, h * DK:(h + 1) * DK]
        wt = w_h * jnp.sum(qh * kh, axis=-1, keepdims=True)
        num = a_h * num_ref[:, h * DV:(h + 1) * DV] + wt * vh
        den = a_h * jnp.sum(qh * nh, axis=-1, keepdims=True) + wt
        hh = num / jnp.maximum(jnp.abs(den), floor[:, h:h + 1])
        n_out_ref[:, h * DK:(h + 1) * DK] = a_h * nh + w_h * kh
        hn = _layer_norm(hh, mhg_ref[0:1, h * DV:(h + 1) * DV], None)
        og = og_ref[:, h * DV:(h + 1) * DV]
        mix_ref[:, D_CONV + h * DV:D_CONV + (h + 1) * DV] = hn * jax.nn.sigmoid(og)


def _even_sample(proj, gates, num, conv_st, n_st, m_st, w_conv, b_gates, mh_g, mixin, j):
    nb = DEC_BATCH
    rb = MP // nb
    est = 4 * 2 * (5 * nb * 1024 + 2 * nb * 512 + nb * 128 + nb * 1024 + nb * 2048 + nb * 512
                   + nb * 2048 + nb * 2048 + nb * 512) + (8 << 20)
    return pl.pallas_call(
        _even_sample_kernel,
        grid=(1,),
        in_specs=[
            pl.BlockSpec((nb, D_CONV), lambda i: (rb, 0)),
            pl.BlockSpec((nb, D_CONV), lambda i: (rb, 1)),
            pl.BlockSpec((nb, D_CONV), lambda i: (rb, 2)),
            pl.BlockSpec((nb, HEADS * DK), lambda i: (rb, 6)),
            pl.BlockSpec((nb, HEADS * DK), lambda i: (rb, 7)),
            pl.BlockSpec((nb, HEADS * DV), lambda i: (rb, 4)),
            pl.BlockSpec((nb, HEADS * DV), lambda i: (rb, 5)),
            pl.BlockSpec((nb, LANES), lambda i: (rb, 0)),
            pl.BlockSpec((nb, HEADS * DV), lambda i: (0, 0)),
            pl.BlockSpec((None, nb, 2 * D_CONV), lambda i: (j, 0, 0)),
            pl.BlockSpec((None, nb, HEADS * DK), lambda i: (j, 0, 0)),
            pl.BlockSpec((None, nb, HEADS), lambda i: (j, 0, 0)),
            pl.BlockSpec((None, CONV_W, D_CONV), lambda i: (j, 0, 0)),
            pl.BlockSpec((None, 2, HEADS), lambda i: (j, 0, 0)),
            pl.BlockSpec((None, 1, HEADS * DV), lambda i: (j, 0, 0)),
            pl.BlockSpec(memory_space=pl.ANY),
        ],
        out_specs=[
            pl.BlockSpec((nb, D_MODEL), lambda i: (rb, 0)),
            pl.BlockSpec((nb, 2 * D_CONV), lambda i: (0, 0)),
            pl.BlockSpec((nb, HEADS * DK), lambda i: (0, 0)),
            pl.BlockSpec((nb, HEADS), lambda i: (0, 0)),
        ],
        out_shape=[
            jax.ShapeDtypeStruct((M, D_MODEL), F32),
            jax.ShapeDtypeStruct((nb, 2 * D_CONV), F32),
            jax.ShapeDtypeStruct((nb, HEADS * DK), F32),
            jax.ShapeDtypeStruct((nb, HEADS), F32),
        ],
        input_output_aliases={15: 0},
        compiler_params=pltpu.CompilerParams(
            dimension_semantics=("arbitrary",), vmem_limit_bytes=_vmem_limit(est)),
        name="even_sample",
    )(proj, proj, proj, proj, proj, proj, proj, gates, num, conv_st, n_st, m_st,
      w_conv, b_gates, mh_g, mixin)


def _gmlp_norm(u_raw, v_raw, g, b):
    u = jax.nn.gelu(u_raw)
    vn = _layer_norm(jax.nn.gelu(v_raw), g, b)
    return u, vn


def _odd_prompt_kernel(p_ref, u_ref, v_ref, wp_ref, sc_ref, gmg_ref, gmb_ref, ws_ref, bst_ref,
                       mix_ref, pool_ref, gv_ref, pbuf):
    s = pl.program_id(1)
    R = RO

    @pl.when(s == 0)
    def _():
        pbuf[0:HIST, :] = jnp.zeros((HIST, D_POOL), F32)

    p = p_ref[...]
    pbuf[HIST:HIST + R, :] = p
    pos = s * R + lax.broadcasted_iota(jnp.int32, (R, 1), 0)
    for g, w in enumerate(POOL_WINDOWS):
        lo, hi = g * POOL_GW, (g + 1) * POOL_GW
        win = p[:, lo:hi]
        for jj in range(1, w):
            win = win + pbuf[HIST - jj:HIST - jj + R, lo:hi]
        cnt = jnp.minimum(w, pos + 1).astype(F32)
        diff = win / cnt - p[:, lo:hi]
        mix_ref[:, lo:hi] = _dot(diff, wp_ref[g]) * sc_ref[0:1, lo:hi]
    pool_ref[...] = pbuf[HIST + R - POOL_BUF:HIST + R, :]
    pbuf[0:HIST, :] = pbuf[R:R + HIST, :]

    u, vn = _gmlp_norm(u_ref[...], v_ref[...], gmg_ref[...], gmb_ref[...])
    L = GMLP_CHUNK
    tril = lax.broadcasted_iota(jnp.int32, (L, L), 0) >= lax.broadcasted_iota(jnp.int32, (L, L), 1)
    for g in range(D_GMLP // GMLP_GW):
        lo, hi = g * GMLP_GW, (g + 1) * GMLP_GW
        ws = jnp.where(tril, ws_ref[g], 0.0)
        bcol = bst_ref[:, g:g + 1]
        for ck in range(R // L):
            r0, r1 = ck * L, (ck + 1) * L
            sv = _dot(ws, vn[r0:r1, lo:hi]) + bcol
            mix_ref[r0:r1, D_POOL + lo:D_POOL + hi] = u[r0:r1, lo:hi] * sv
    gv_ref[...] = vn[R - L:R, :]


def _odd_prompt(proj, w_pool, pool_scale, gm_g, gm_b, w_spatial, bs_t, j):
    nrb = SEQ // RO
    r = lambda b, s: b * nrb + s
    est = 4 * (2 * 3 * RO * 1024 + 2 * RO * 2048 + 2 * 4 * 256 * 256 + 2 * 4 * 128 * 128
               + (RO + HIST) * 1024 + 8 * RO * 1024) + (8 << 20)
    return pl.pallas_call(
        _odd_prompt_kernel,
        grid=(BATCH, nrb),
        in_specs=[
            pl.BlockSpec((RO, D_POOL), lambda b, s: (r(b, s), 0)),
            pl.BlockSpec((RO, D_GMLP), lambda b, s: (r(b, s), 1)),
            pl.BlockSpec((RO, D_GMLP), lambda b, s: (r(b, s), 2)),
            pl.BlockSpec((None, 4, POOL_GW, POOL_GW), lambda b, s: (j, 0, 0, 0)),
            pl.BlockSpec((None, 1, D_POOL), lambda b, s: (j, 0, 0)),
            pl.BlockSpec((None, 1, D_GMLP), lambda b, s: (j, 0, 0)),
            pl.BlockSpec((None, 1, D_GMLP), lambda b, s: (j, 0, 0)),
            pl.BlockSpec((None, 4, GMLP_CHUNK, GMLP_CHUNK), lambda b, s: (j, 0, 0, 0)),
            pl.BlockSpec((None, GMLP_CHUNK, 4), lambda b, s: (j, 0, 0)),
        ],
        out_specs=[
            pl.BlockSpec((RO, D_MODEL), lambda b, s: (r(b, s), 0)),
            pl.BlockSpec((None, POOL_BUF, D_POOL), lambda b, s: (b, 0, 0)),
            pl.BlockSpec((None, GMLP_CHUNK, D_GMLP), lambda b, s: (b, 0, 0)),
        ],
        out_shape=[
            jax.ShapeDtypeStruct((M, D_MODEL), F32),
            jax.ShapeDtypeStruct((BATCH, POOL_BUF, D_POOL), F32),
            jax.ShapeDtypeStruct((BATCH, GMLP_CHUNK, D_GMLP), F32),
        ],
        scratch_shapes=[pltpu.VMEM((RO + HIST, D_POOL), F32)],
        compiler_params=pltpu.CompilerParams(
            dimension_semantics=("arbitrary", "arbitrary"), vmem_limit_bytes=_vmem_limit(est)),
        name="odd_prompt",
    )(proj, proj, proj, w_pool, pool_scale, gm_g, gm_b, w_spatial, bs_t)


def _odd_sample_kernel(p_ref, u_ref, v_ref, st_ref, wp_ref, sc_ref, gmg_ref, gmb_ref, ws_ref, bst_ref,
                       *rest):
    mix_ref, pool_ref, gv_ref = rest[-3:]
    p = p_ref[...]
    for r in range(POOL_BUF - 1):
        pool_ref[r] = st_ref[r + 1]
    pool_ref[POOL_BUF - 1] = p
    for g, w in enumerate(POOL_WINDOWS):
        lo, hi = g * POOL_GW, (g + 1) * POOL_GW
        win = p[:, lo:hi]
        for jj in range(1, w):
            win = win + st_ref[POOL_BUF - jj, :, lo:hi]
        cnt = float(min(w, PAST_LEN + 1))
        diff = win / cnt - p[:, lo:hi]
        mix_ref[:, lo:hi] = _dot(diff, wp_ref[g]) * sc_ref[0:1, lo:hi]
    u, vn = _gmlp_norm(u_ref[...], v_ref[...], gmg_ref[...], gmb_ref[...])
    gv_ref[...] = vn
    for g in range(D_GMLP // GMLP_GW):
        lo, hi = g * GMLP_GW, (g + 1) * GMLP_GW
        sv = ws_ref[g, 0:1, 0:1] * vn[:, lo:hi] + bst_ref[0:1, g:g + 1]
        mix_ref[:, D_POOL + lo:D_POOL + hi] = u[:, lo:hi] * sv


def _odd_sample(proj, pool_st, w_pool, pool_scale, gm_g, gm_b, w_spatial, bs_t, mixin, pool_prev, j):
    nb = DEC_BATCH
    rb = MP // nb
    st_blk = (None, POOL_BUF, nb, D_POOL)
    in_specs = [
        pl.BlockSpec((nb, D_POOL), lambda i: (rb, 0)),
        pl.BlockSpec((nb, D_GMLP), lambda i: (rb, 1)),
        pl.BlockSpec((nb, D_GMLP), lambda i: (rb, 2)),
        pl.BlockSpec(st_blk, lambda i: (j, 0, 0, 0)),
        pl.BlockSpec((None, 4, POOL_GW, POOL_GW), lambda i: (j, 0, 0, 0)),
        pl.BlockSpec((None, 1, D_POOL), lambda i: (j, 0, 0)),
        pl.BlockSpec((None, 1, D_GMLP), lambda i: (j, 0, 0)),
        pl.BlockSpec((None, 1, D_GMLP), lambda i: (j, 0, 0)),
        pl.BlockSpec((None, 4, GMLP_CHUNK, GMLP_CHUNK), lambda i: (j, 0, 0, 0)),
        pl.BlockSpec((None, GMLP_CHUNK, 4), lambda i: (j, 0, 0)),
        pl.BlockSpec(memory_space=pl.ANY),
    ]
    args = [proj, proj, proj, pool_st, w_pool, pool_scale, gm_g, gm_b, w_spatial, bs_t, mixin]
    aliases = {len(args) - 1: 0}
    if pool_prev is not None:
        in_specs.append(pl.BlockSpec(memory_space=pl.ANY))
        args.append(pool_prev)
        aliases[len(args) - 1] = 1
    est = 4 * (4 * POOL_BUF * nb * D_POOL + 2 * (3 * nb * 1024 + 4 * 256 * 256 + 4 * 128 * 128
                                                + nb * 2048 + nb * 1024)) + (8 << 20)
    return pl.pallas_call(
        _odd_sample_kernel,
        grid=(1,),
        in_specs=in_specs,
        out_specs=[
            pl.BlockSpec((nb, D_MODEL), lambda i: (rb, 0)),
            pl.BlockSpec(st_blk, lambda i: (j, 0, 0, 0)),
            pl.BlockSpec((nb, D_GMLP), lambda i: (0, 0)),
        ],
        out_shape=[
            jax.ShapeDtypeStruct((M, D_MODEL), F32),
            jax.ShapeDtypeStruct(pool_st.shape, F32),
            jax.ShapeDtypeStruct((nb, D_GMLP), F32),
        ],
        input_output_aliases=aliases,
        compiler_params=pltpu.CompilerParams(
            dimension_semantics=("arbitrary",), vmem_limit_bytes=_vmem_limit(est)),
        name="odd_sample",
    )(*args)


def kernel(x_prompt, x_sample, state_conv, state_mlstm_C, state_mlstm_n, state_mlstm_m, state_pool,
           ln_g, ln_b, w_ffn_in, w_ffn_out, w_in_even, b_gates_even, w_conv, mh_norm_g, w_out_even,
           w_in_odd, w_pool, pool_scale, gm_ln_g, gm_ln_b, w_spatial, b_spatial, w_out_odd):
    n_even, n_odd = w_in_even.shape[0], w_in_odd.shape[0]
    x = jnp.concatenate([x_prompt.reshape(MP, D_MODEL), x_sample.reshape(DEC_BATCH, D_MODEL)], axis=0)

    ln_g3 = ln_g.reshape(DEPTH * 3, 1, D_MODEL)
    ln_b3 = ln_b.reshape(DEPTH * 3, 1, D_MODEL)
    w_even_t = jnp.swapaxes(w_in_even, 1, 2)
    w_gate_t = jnp.pad(w_even_t[:, EVEN_MAIN:, :], ((0, 0), (0, LANES - 2 * HEADS), (0, 0)))
    pool_st = jnp.swapaxes(state_pool, 1, 2)
    mh_g3 = mh_norm_g.reshape(n_even, 1, HEADS * DV)
    conv_st = state_conv.reshape(n_even, DEC_BATCH, (CONV_W - 1) * D_CONV)
    n_st = state_mlstm_n.reshape(n_even, DEC_BATCH, HEADS * DK)
    scale3 = pool_scale.reshape(n_odd, 1, D_POOL)
    gm_g3 = gm_ln_g.reshape(n_odd, 1, D_GMLP)
    gm_b3 = gm_ln_b.reshape(n_odd, 1, D_GMLP)
    bs_t = jnp.swapaxes(b_spatial, 1, 2)

    conv_p, conv_s, c_p, n_p, n_s, m_p, m_s = [], [], [], [], [], [], []
    pool_p, gv_p, gv_s = [], [], []
    c_s = None
    pool_s = None

    for layer in range(DEPTH):
        j = layer // 2
        x = _ffn(x, w_ffn_in, w_ffn_out, ln_g3, ln_b3, layer, 0, 3 * layer)
        if layer % 2 == 0:
            proj, gates = _proj(x, w_even_t, j, EVEN_MAIN, TN, "proj_even", w_transposed=True,
                                w_gate_t=w_gate_t)
            mixin, cv, cc, nn, mm = _even_prompt(proj, gates, w_conv, b_gates_even, mh_g3, j)
            conv_p.append(cv)
            c_p.append(cc)
            n_p.append(nn)
            m_p.append(mm.reshape(BATCH, HEADS))
            qk_cols = _qk_cols(x, w_even_t, j)
            c_s, num = _even_sample_state(state_mlstm_C, qk_cols, proj, gates, state_mlstm_m,
                                          b_gates_even, j, c_s)
            mixin, cvs, nns, mms = _even_sample(proj, gates, num, conv_st, n_st, state_mlstm_m,
                                                w_conv, b_gates_even, mh_g3, mixin, j)
            conv_s.append(cvs.reshape(DEC_BATCH, CONV_W - 1, D_CONV))
            n_s.append(nns.reshape(DEC_BATCH, HEADS, DK))
            m_s.append(mms)
            x = _outproj(x, mixin, w_out_even, ln_g3, ln_b3, j, 3 * layer + 1)
        else:
            proj = _proj(x, w_in_odd, j, ODD_IN, TN, "proj_odd")
            mixin, pp, gv = _odd_prompt(proj, w_pool, scale3, gm_g3, gm_b3, w_spatial, bs_t, j)
            pool_p.append(pp)
            gv_p.append(gv)
            mixin, pool_s, gvs = _odd_sample(proj, pool_st, w_pool, scale3, gm_g3, gm_b3, w_spatial, bs_t,
                                             mixin, pool_s, j)
            gv_s.append(gvs.reshape(DEC_BATCH, 1, D_GMLP))
            x = _outproj(x, mixin, w_out_odd, ln_g3, ln_b3, j, 3 * layer + 1)
        x = _ffn(x, w_ffn_in, w_ffn_out, ln_g3, ln_b3, layer, 1, 3 * layer + 2,
                 split_out=layer == DEPTH - 1)

    y_prompt = x[0].reshape(BATCH, SEQ, D_MODEL)
    y_sample = x[1].reshape(DEC_BATCH, 1, D_MODEL)
    return (y_prompt, y_sample,
            jnp.stack(conv_p), jnp.stack(conv_s),
            jnp.stack(c_p), c_s,
            jnp.stack(n_p), jnp.stack(n_s),
            jnp.stack(m_p), jnp.stack(m_s),
            jnp.stack(pool_p), jnp.swapaxes(pool_s, 1, 2),
            jnp.stack(gv_p), jnp.stack(gv_s))
```

```python
import functools

import jax
import jax.numpy as jnp
from jax import lax
from jax.experimental import pallas as pl
from jax.experimental.pallas import tpu as pltpu

F32 = jnp.float32

D_MODEL = 2048
BATCH = 4
SEQ = 2048
DEPTH = 4
DEC_BATCH = 128
PAST_LEN = 16384
D_FF = 5632
D_CONV = 1024
CONV_W = 3
HEADS = 4
DK = 128
DV = 256
D_POOL = 1024
POOL_WINDOWS = (2, 4, 8, 16)
POOL_GW = 256
POOL_BUF = 15
D_GMLP = 1024
GMLP_GW = 256
GMLP_CHUNK = 128
ALPHA = (2 * DEPTH) ** 0.25
LN_EPS = 1e-5
EVEN_MAIN = 3 * D_CONV + 2 * HEADS * DK + 2 * HEADS * DV
ODD_IN = D_POOL + 2 * D_GMLP

MP = BATCH * SEQ
M = MP + DEC_BATCH

LANES = 128
SUBLANES = 8
VMEM_BYTES_V7X = 64 * 1024 * 1024

TM = 1040
TF = 256
TN = 512
TMP = 520
PROJ_COLS = 3072
TMO = 416
LC = 256
RE = 256
RO = 256
SB = 16
HIST = 16


def _vmem_limit(nbytes):
    return int(min(VMEM_BYTES_V7X - 4 * 1024 * 1024, nbytes))


def _layer_norm(y, g, b):
    mu = jnp.mean(y, axis=-1, keepdims=True)
    yc = y - mu
    var = jnp.mean(yc * yc, axis=-1, keepdims=True)
    out = yc * lax.rsqrt(var + LN_EPS) * g
    if b is not None:
        out = out + b
    return out


def _log_sigmoid(x):
    return -(jnp.maximum(-x, 0.0) + jnp.log1p(jnp.exp(-jnp.abs(x))))


def _dot(a, b):
    return jnp.dot(a, b, preferred_element_type=F32)


def _dot_nt(a, b):
    return lax.dot_general(a, b, (((1,), (1,)), ((), ())), preferred_element_type=F32)


def _ffn_kernel(x_ref, wg_ref, wu_ref, wo_ref, g_ref, b_ref, o_ref, *maybe_sample_ref, nf):
    f = pl.program_id(1)

    @pl.when(f == 0)
    def _():
        o_ref[...] = jnp.zeros_like(o_ref)

    x = x_ref[...]
    gate = _dot(x, wg_ref[...])
    up = _dot(x, wu_ref[...])
    h = gate * jax.nn.sigmoid(gate) * up
    for n in range(D_MODEL // TN):
        o_ref[:, n * TN:(n + 1) * TN] += _dot(h, wo_ref[:, n * TN:(n + 1) * TN])

    @pl.when(f == nf - 1)
    def _():
        y = ALPHA * x_ref[...] + 0.5 * o_ref[...]
        o_ref[...] = _layer_norm(y, g_ref[...], b_ref[...])

    if maybe_sample_ref:
        @pl.when((f == nf - 1) & (pl.program_id(0) == M // TM - 1))
        def _():
            maybe_sample_ref[0][...] = o_ref[TM - DEC_BATCH:, :]


def _ffn(x, w_ffn_in, w_ffn_out, ln_g, ln_b, layer, which, ln_idx, split_out=False):
    nf = D_FF // TF
    est = 4 * (4 * TM * D_MODEL + 2 * 3 * D_MODEL * TF + TM * D_MODEL + 4 * TM * TF) + (4 << 20)
    out_specs = pl.BlockSpec((TM, D_MODEL), lambda i, f: (i, 0))
    out_shape = jax.ShapeDtypeStruct((M, D_MODEL), F32)
    if split_out:
        out_specs = [out_specs, pl.BlockSpec((DEC_BATCH, D_MODEL), lambda i, f: (0, 0))]
        out_shape = [jax.ShapeDtypeStruct((MP, D_MODEL), F32),
                     jax.ShapeDtypeStruct((DEC_BATCH, D_MODEL), F32)]
    return pl.pallas_call(
        functools.partial(_ffn_kernel, nf=nf),
        grid=(M // TM, nf),
        in_specs=[
            pl.BlockSpec((TM, D_MODEL), lambda i, f: (i, 0)),
            pl.BlockSpec((None, None, D_MODEL, TF), lambda i, f: (layer, which, 0, f)),
            pl.BlockSpec((None, None, D_MODEL, TF), lambda i, f: (layer, which, 0, nf + f)),
            pl.BlockSpec((None, None, TF, D_MODEL), lambda i, f: (layer, which, f, 0)),
            pl.BlockSpec((None, 1, D_MODEL), lambda i, f: (ln_idx, 0, 0)),
            pl.BlockSpec((None, 1, D_MODEL), lambda i, f: (ln_idx, 0, 0)),
        ],
        out_specs=out_specs,
        out_shape=out_shape,
        compiler_params=pltpu.CompilerParams(
            dimension_semantics=("arbitrary", "arbitrary"), vmem_limit_bytes=_vmem_limit(est)),
        name="ffn_ln",
    )(x, w_ffn_in, w_ffn_in, w_ffn_out, ln_g, ln_b)


def _proj_kernel(x_ref, w_ref, *rest, w_transposed, with_gates, aliased):
    rest = rest[1:] if aliased else rest
    if with_gates:
        wgate_ref, o_ref, gate_ref = rest
        gate_ref[...] = _dot_nt(x_ref[...], wgate_ref[...])
    else:
        (o_ref,) = rest
    for n in range(PROJ_COLS // TN):
        lo, hi = n * TN, (n + 1) * TN
        if w_transposed:
            o_ref[:, lo:hi] = _dot_nt(x_ref[...], w_ref[lo:hi, :])
        else:
            o_ref[:, lo:hi] = _dot(x_ref[...], w_ref[:, lo:hi])


def _proj(x, w, layer_idx, col_block, n_total, name, w_transposed=False, w_gate_t=None, prev=None):
    est = 4 * (PROJ_COLS * D_MODEL + 2 * TMP * D_MODEL + 2 * TMP * PROJ_COLS + 2 * TMP * TN) + (4 << 20)
    if w_transposed:
        w_spec = pl.BlockSpec((None, PROJ_COLS, D_MODEL), lambda i: (layer_idx, col_block, 0),
                              pipeline_mode=pl.Buffered(1))
    else:
        w_spec = pl.BlockSpec((None, D_MODEL, PROJ_COLS), lambda i: (layer_idx, 0, col_block),
                              pipeline_mode=pl.Buffered(1))
    in_specs = [pl.BlockSpec((TMP, D_MODEL), lambda i: (i, 0)), w_spec]
    args = [x, w]
    aliases = {}
    if prev is not None:
        in_specs.append(pl.BlockSpec(memory_space=pl.ANY))
        args.append(prev)
        aliases = {2: 0}
    out_specs = pl.BlockSpec((TMP, PROJ_COLS), lambda i: (i, col_block))
    out_shape = jax.ShapeDtypeStruct((M, n_total), F32)
    if w_gate_t is not None:
        in_specs.append(pl.BlockSpec((None, LANES, D_MODEL), lambda i: (layer_idx, 0, 0)))
        args.append(w_gate_t)
        out_specs = [out_specs, pl.BlockSpec((TMP, LANES), lambda i: (i, 0))]
        out_shape = [out_shape, jax.ShapeDtypeStruct((M, LANES), F32)]
        est += 4 * (2 * LANES * D_MODEL + 3 * TMP * LANES)
    return pl.pallas_call(
        functools.partial(_proj_kernel, w_transposed=w_transposed, with_gates=w_gate_t is not None,
                          aliased=prev is not None),
        grid=(M // TMP,),
        in_specs=in_specs,
        out_specs=out_specs,
        out_shape=out_shape,
        input_output_aliases=aliases,
        compiler_params=pltpu.CompilerParams(
            dimension_semantics=("parallel",), vmem_limit_bytes=_vmem_limit(est)),
        name=name,
    )(*args)


def _outproj_kernel(x_ref, a_ref, w_ref, g_ref, b_ref, o_ref):
    y = ALPHA * x_ref[...] + _dot(a_ref[...], w_ref[...])
    o_ref[...] = _layer_norm(y, g_ref[...], b_ref[...])


def _outproj(x, mix, w_out, ln_g, ln_b, layer_idx, ln_idx):
    est = 4 * (D_MODEL * D_MODEL + 9 * TMO * D_MODEL) + (4 << 20)
    return pl.pallas_call(
        _outproj_kernel,
        grid=(M // TMO,),
        in_specs=[
            pl.BlockSpec((TMO, D_MODEL), lambda i: (i, 0)),
            pl.BlockSpec((TMO, D_MODEL), lambda i: (i, 0)),
            pl.BlockSpec((None, D_MODEL, D_MODEL), lambda i: (layer_idx, 0, 0),
                         pipeline_mode=pl.Buffered(1)),
            pl.BlockSpec((None, 1, D_MODEL), lambda i: (ln_idx, 0, 0)),
            pl.BlockSpec((None, 1, D_MODEL), lambda i: (ln_idx, 0, 0)),
        ],
        out_specs=pl.BlockSpec((TMO, D_MODEL), lambda i: (i, 0)),
        out_shape=jax.ShapeDtypeStruct((M, D_MODEL), F32),
        compiler_params=pltpu.CompilerParams(
            dimension_semantics=("parallel",), vmem_limit_bytes=_vmem_limit(est)),
        name="outproj_ln",
    )(x, mix, w_out, ln_g, ln_b)


def _split3(x):
    h1 = x.astype(jnp.bfloat16).astype(F32)
    r = x - h1
    h2 = r.astype(jnp.bfloat16).astype(F32)
    return h1, h2, r - h2


def _even_prompt_kernel(bg_ref, cg_ref, xin_ref, q_ref, k_ref, v_ref, og_ref, gt_ref,
                        wc_ref, bgate_ref, mhg_ref,
                        mix_ref, conv_ref, c_out_ref, n_out_ref, m_out_ref,
                        cbuf, c_sc, n_sc, m_sc):
    c = pl.program_id(1)
    L = LC
    R = RE

    @pl.when(c == 0)
    def _():
        cbuf[0:SUBLANES, :] = jnp.zeros((SUBLANES, D_CONV), F32)
        c_sc[...] = jnp.zeros_like(c_sc)
        n_sc[...] = jnp.zeros_like(n_sc)
        m_sc[...] = jnp.zeros_like(m_sc)

    cx = cg_ref[...] * xin_ref[...]
    cbuf[SUBLANES:SUBLANES + R, :] = cx
    c1 = cbuf[SUBLANES - 1:SUBLANES - 1 + R, :]
    c2 = cbuf[SUBLANES - 2:SUBLANES - 2 + R, :]
    conv = c2 * wc_ref[0:1, :] + c1 * wc_ref[1:2, :] + cx * wc_ref[2:3, :]
    mix_ref[:, 0:D_CONV] = bg_ref[...] * conv
    conv_ref[...] = cbuf[SUBLANES + R - 2:SUBLANES + R, :]
    cbuf[0:SUBLANES, :] = cbuf[R:R + SUBLANES, :]

    row = lax.broadcasted_iota(jnp.int32, (L, L), 0)
    col = lax.broadcasted_iota(jnp.int32, (L, L), 1)
    causal = row >= col
    tril = causal.astype(F32)
    c_state = [c_sc[h] for h in range(HEADS)]
    n_state = [n_sc[h:h + 1, :] for h in range(HEADS)]
    m_state = [m_sc[h, 0:1, 0:1] for h in range(HEADS)]

    for s in range(R // L):
        r0, r1 = s * L, (s + 1) * L
        gt = gt_ref[r0:r1, :]
        li_all = gt[:, 0:HEADS] + bgate_ref[0:1, :]
        lf_all = _log_sigmoid(gt[:, HEADS:2 * HEADS] + bgate_ref[1:2, :])
        f1, f2, f3 = _split3(lf_all)
        b_all = _dot(tril, f1) + _dot(tril, f2) + _dot(tril, f3)
        z = jnp.concatenate([li_all, b_all, jnp.zeros((L, LANES - 2 * HEADS), F32)], axis=1)
        zt = z.T

        for h in range(HEADS):
            li_c = li_all[:, h:h + 1]
            b_c = b_all[:, h:h + 1]
            li_r = zt[h:h + 1, :]
            b_r = zt[HEADS + h:HEADS + h + 1, :]
            m0 = m_state[h]
            dm = jnp.where(causal, b_c - b_r + li_r, -jnp.inf)
            inter = b_c + m0
            m_t = jnp.maximum(inter, jnp.max(dm, axis=-1, keepdims=True))
            a_int = jnp.exp(inter - m_t)
            qh = q_ref[r0:r1, h * DK:(h + 1) * DK]
            kh = k_ref[r0:r1, h * DK:(h + 1) * DK] * (DK ** -0.5)
            vh = v_ref[r0:r1, h * DV:(h + 1) * DV]
            w = jnp.exp(dm - m_t) * _dot_nt(qh, kh)
            ch = c_state[h]
            n_row = n_state[h]
            num = a_int * _dot(qh, ch) + _dot(w, vh)
            den = (a_int * jnp.sum(qh * n_row, axis=-1, keepdims=True)
                   + jnp.sum(w, axis=-1, keepdims=True))
            hh = num / jnp.maximum(jnp.abs(den), jnp.exp(-m_t))
            m_new = m_t[L - 1:L, :]
            b_last = b_c[L - 1:L, :]
            w_end = jnp.exp(b_last - b_c + li_c - m_new)
            decay = jnp.exp(b_last + m0 - m_new)
            wk = w_end * kh
            c_state[h] = decay * ch + _dot(wk.T, vh)
            n_state[h] = decay * n_row + jnp.sum(wk, axis=0, keepdims=True)
            m_state[h] = m_new
            hn = _layer_norm(hh, mhg_ref[0:1, h * DV:(h + 1) * DV], None)
            og = og_ref[r0:r1, h * DV:(h + 1) * DV]
            mix_ref[r0:r1, D_CONV + h * DV:D_CONV + (h + 1) * DV] = hn * jax.nn.sigmoid(og)

    for h in range(HEADS):
        c_sc[h] = c_state[h]
        n_sc[h:h + 1, :] = n_state[h]
        m_sc[h] = jnp.broadcast_to(m_state[h], (SUBLANES, LANES))
        c_out_ref[h] = c_state[h]
        n_out_ref[h:h + 1, :] = n_state[h]
        m_out_ref[0:1, h:h + 1] = m_state[h]


def _even_prompt(proj, gates, w_conv, b_gates, mh_g, j):
    nck = SEQ // RE
    r = lambda b, c: b * nck + c
    est = 4 * (2 * (5 * RE * 1024 + 2 * RE * 512 + RE * 128) + 2 * RE * 2048 + 4 * RE * 1024
               + 3 * HEADS * DK * DV + 24 * LC * LC) + (8 << 20)
    return pl.pallas_call(
        _even_prompt_kernel,
        grid=(BATCH, nck),
        in_specs=[
            pl.BlockSpec((RE, D_CONV), lambda b, c: (r(b, c), 0)),
            pl.BlockSpec((RE, D_CONV), lambda b, c: (r(b, c), 1)),
            pl.BlockSpec((RE, D_CONV), lambda b, c: (r(b, c), 2)),
            pl.BlockSpec((RE, HEADS * DK), lambda b, c: (r(b, c), 6)),
            pl.BlockSpec((RE, HEADS * DK), lambda b, c: (r(b, c), 7)),
            pl.BlockSpec((RE, HEADS * DV), lambda b, c: (r(b, c), 4)),
            pl.BlockSpec((RE, HEADS * DV), lambda b, c: (r(b, c), 5)),
            pl.BlockSpec((RE, LANES), lambda b, c: (r(b, c), 0)),
            pl.BlockSpec((None, CONV_W, D_CONV), lambda b, c: (j, 0, 0)),
            pl.BlockSpec((None, 2, HEADS), lambda b, c: (j, 0, 0)),
            pl.BlockSpec((None, 1, HEADS * DV), lambda b, c: (j, 0, 0)),
        ],
        out_specs=[
            pl.BlockSpec((RE, D_MODEL), lambda b, c: (r(b, c), 0)),
            pl.BlockSpec((None, CONV_W - 1, D_CONV), lambda b, c: (b, 0, 0)),
            pl.BlockSpec((None, HEADS, DK, DV), lambda b, c: (b, 0, 0, 0)),
            pl.BlockSpec((None, HEADS, DK), lambda b, c: (b, 0, 0)),
            pl.BlockSpec((None, 1, HEADS), lambda b, c: (b, 0, 0)),
        ],
        out_shape=[
            jax.ShapeDtypeStruct((M, D_MODEL), F32),
            jax.ShapeDtypeStruct((BATCH, CONV_W - 1, D_CONV), F32),
            jax.ShapeDtypeStruct((BATCH, HEADS, DK, DV), F32),
            jax.ShapeDtypeStruct((BATCH, HEADS, DK), F32),
            jax.ShapeDtypeStruct((BATCH, 1, HEADS), F32),
        ],
        scratch_shapes=[
            pltpu.VMEM((RE + SUBLANES, D_CONV), F32),
            pltpu.VMEM((HEADS, DK, DV), F32),
            pltpu.VMEM((HEADS, DK), F32),
            pltpu.VMEM((HEADS, SUBLANES, LANES), F32),
        ],
        compiler_params=pltpu.CompilerParams(
            dimension_semantics=("arbitrary", "arbitrary"), vmem_limit_bytes=_vmem_limit(est)),
        name="even_prompt",
    )(proj, proj, proj, proj, proj, proj, proj, gates, w_conv, b_gates, mh_g)


def _sample_gates(gt, bgate_ref, m):
    li = gt[:, 0:HEADS] + bgate_ref[0:1, :]
    lf = _log_sigmoid(gt[:, HEADS:2 * HEADS] + bgate_ref[1:2, :])
    inter = lf + m
    m_t = jnp.maximum(inter, li)
    return jnp.exp(inter - m_t), jnp.exp(li - m_t), m_t


def _qk_cols_kernel(x_ref, w_ref, o_ref):
    t = _dot_nt(w_ref[...], x_ref[...])
    for s in range(DEC_BATCH // SB):
        o_ref[s] = t[:, s * SB:(s + 1) * SB]


def _qk_cols(x, w_even_t, j):
    nqk = 2 * HEADS * DK
    est = 4 * (2 * DEC_BATCH * D_MODEL + 2 * nqk * D_MODEL + 3 * nqk * LANES
               + 2 * (DEC_BATCH // SB) * nqk * LANES) + (4 << 20)
    return pl.pallas_call(
        _qk_cols_kernel,
        grid=(1,),
        in_specs=[
            pl.BlockSpec((DEC_BATCH, D_MODEL), lambda i: (MP // DEC_BATCH, 0)),
            pl.BlockSpec((None, nqk, D_MODEL), lambda i: (j, 3 * D_CONV // nqk, 0)),
        ],
        out_specs=pl.BlockSpec((DEC_BATCH // SB, nqk, SB), lambda i: (0, 0, 0)),
        out_shape=jax.ShapeDtypeStruct((DEC_BATCH // SB, nqk, SB), F32),
        compiler_params=pltpu.CompilerParams(
            dimension_semantics=("arbitrary",), vmem_limit_bytes=_vmem_limit(est)),
        name="qk_cols",
    )(x, w_even_t)


def _even_sample_state_kernel(c_ref, qk_ref, v_ref, gt_ref, m_ref, bgate_ref, *rest):
    c_out_ref, num_ref = rest[-2], rest[-1]
    a, wgt, _ = _sample_gates(gt_ref[...], bgate_ref, m_ref[...])
    for bi in range(SB):
        for h in range(HEADS):
            a_s = a[bi:bi + 1, h:h + 1]
            w_s = wgt[bi:bi + 1, h:h + 1]
            ch = c_ref[bi, h]
            qc = qk_ref[h * DK:(h + 1) * DK, bi:bi + 1]
            kc = qk_ref[(HEADS + h) * DK:(HEADS + h + 1) * DK, bi:bi + 1] * (DK ** -0.5)
            vr = v_ref[bi:bi + 1, h * DV:(h + 1) * DV]
            c_out_ref[bi, h] = a_s * ch + (w_s * kc) * vr
            num_ref[bi:bi + 1, h * DV:(h + 1) * DV] = jnp.sum(qc * ch, axis=0, keepdims=True)


def _even_sample_state(state_c, qk_cols, proj, gates, state_m, b_gates, j, c_prev):
    blk5 = (None, SB, HEADS, DK, DV)
    rb = MP // SB
    in_specs = [
        pl.BlockSpec(blk5, lambda i: (j, i, 0, 0, 0)),
        pl.BlockSpec((None, 2 * HEADS * DK, SB), lambda i: (i, 0, 0)),
        pl.BlockSpec((SB, HEADS * DV), lambda i: (rb + i, 4)),
        pl.BlockSpec((SB, LANES), lambda i: (rb + i, 0)),
        pl.BlockSpec((None, SB, HEADS), lambda i: (j, i, 0)),
        pl.BlockSpec((None, 2, HEADS), lambda i: (j, 0, 0)),
    ]
    args = [state_c, qk_cols, proj, gates, state_m, b_gates]
    aliases = {}
    if c_prev is not None:
        in_specs.append(pl.BlockSpec(memory_space=pl.ANY))
        args.append(c_prev)
        aliases = {len(args) - 1: 0}
    est = 4 * (4 * SB * HEADS * DK * DV + 4 * HEADS * DK * LANES) + (8 << 20)
    return pl.pallas_call(
        _even_sample_state_kernel,
        grid=(DEC_BATCH // SB,),
        in_specs=in_specs,
        out_specs=[
            pl.BlockSpec(blk5, lambda i: (j, i, 0, 0, 0)),
            pl.BlockSpec((SB, HEADS * DV), lambda i: (i, 0)),
        ],
        out_shape=[
            jax.ShapeDtypeStruct(state_c.shape, F32),
            jax.ShapeDtypeStruct((DEC_BATCH, HEADS * DV), F32),
        ],
        input_output_aliases=aliases,
        compiler_params=pltpu.CompilerParams(
            dimension_semantics=("arbitrary",), vmem_limit_bytes=_vmem_limit(est)),
        name="even_sample_state",
    )(*args)


def _even_sample_kernel(bg_ref, cg_ref, xin_ref, q_ref, k_ref, v_ref, og_ref, gt_ref,
                        num_ref, cst_ref, n_ref, m_ref, wc_ref, bgate_ref, mhg_ref, mixin_hbm,
                        mix_ref, conv_ref, n_out_ref, m_out_ref):
    del mixin_hbm
    cx = cg_ref[...] * xin_ref[...]
    st0 = cst_ref[:, 0:D_CONV]
    st1 = cst_ref[:, D_CONV:2 * D_CONV]
    conv = st0 * wc_ref[0:1, :] + st1 * wc_ref[1:2, :] + cx * wc_ref[2:3, :]
    mix_ref[:, 0:D_CONV] = bg_ref[...] * conv
    conv_ref[:, 0:D_CONV] = st1
    conv_ref[:, D_CONV:2 * D_CONV] = cx

    a, wgt, m_t = _sample_gates(gt_ref[...], bgate_ref, m_ref[...])
    m_out_ref[...] = m_t
    floor = jnp.exp(-m_t)
    for h in range(HEADS):
        a_h = a[:, h:h + 1]
        w_h = wgt[:, h:h + 1]
        qh = q_ref[:, h * DK:(h + 1) * DK]
        kh = k_ref[:, h * DK:(h + 1) * DK] * (DK ** -0.5)
        vh = v_ref[:, h * DV:(h + 1) * DV]
        nh = n_ref[:, h * DK:(h + 1) * DK]
        wt = w_h * jnp.sum(qh * kh, axis=-1, keepdims=True)
        num = a_h * num_ref[:, h * DV:(h + 1) * DV] + wt * vh
        den = a_h * jnp.sum(qh * nh, axis=-1, keepdims=True) + wt
        hh = num / jnp.maximum(jnp.abs(den), floor[:, h:h + 1])
        n_out_ref[:, h * DK:(h + 1) * DK] = a_h * nh + w_h * kh
        hn = _layer_norm(hh, mhg_ref[0:1, h * DV:(h + 1) * DV], None)
        og = og_ref[:, h * DV:(h + 1) * DV]
        mix_ref[:, D_CONV + h * DV:D_CONV + (h + 1) * DV] = hn * jax.nn.sigmoid(og)


def _even_sample(proj, gates, num, conv_st, n_st, m_st, w_conv, b_gates, mh_g, mixin, j):
    nb = DEC_BATCH
    rb = MP // nb
    est = 4 * 2 * (5 * nb * 1024 + 2 * nb * 512 + nb * 128 + nb * 1024 + nb * 2048 + nb * 512
                   + nb * 2048 + nb * 2048 + nb * 512) + (8 << 20)
    return pl.pallas_call(
        _even_sample_kernel,
        grid=(1,),
        in_specs=[
            pl.BlockSpec((nb, D_CONV), lambda i: (rb, 0)),
            pl.BlockSpec((nb, D_CONV), lambda i: (rb, 1)),
            pl.BlockSpec((nb, D_CONV), lambda i: (rb, 2)),
            pl.BlockSpec((nb, HEADS * DK), lambda i: (rb, 6)),
            pl.BlockSpec((nb, HEADS * DK), lambda i: (rb, 7)),
            pl.BlockSpec((nb, HEADS * DV), lambda i: (rb, 4)),
            pl.BlockSpec((nb, HEADS * DV), lambda i: (rb, 5)),
            pl.BlockSpec((nb, LANES), lambda i: (rb, 0)),
            pl.BlockSpec((nb, HEADS * DV), lambda i: (0, 0)),
            pl.BlockSpec((None, nb, 2 * D_CONV), lambda i: (j, 0, 0)),
            pl.BlockSpec((None, nb, HEADS * DK), lambda i: (j, 0, 0)),
            pl.BlockSpec((None, nb, HEADS), lambda i: (j, 0, 0)),
            pl.BlockSpec((None, CONV_W, D_CONV), lambda i: (j, 0, 0)),
            pl.BlockSpec((None, 2, HEADS), lambda i: (j, 0, 0)),
            pl.BlockSpec((None, 1, HEADS * DV), lambda i: (j, 0, 0)),
            pl.BlockSpec(memory_space=pl.ANY),
        ],
        out_specs=[
            pl.BlockSpec((nb, D_MODEL), lambda i: (rb, 0)),
            pl.BlockSpec((nb, 2 * D_CONV), lambda i: (0, 0)),
            pl.BlockSpec((nb, HEADS * DK), lambda i: (0, 0)),
            pl.BlockSpec((nb, HEADS), lambda i: (0, 0)),
        ],
        out_shape=[
            jax.ShapeDtypeStruct((M, D_MODEL), F32),
            jax.ShapeDtypeStruct((nb, 2 * D_CONV), F32),
            jax.ShapeDtypeStruct((nb, HEADS * DK), F32),
            jax.ShapeDtypeStruct((nb, HEADS), F32),
        ],
        input_output_aliases={15: 0},
        compiler_params=pltpu.CompilerParams(
            dimension_semantics=("arbitrary",), vmem_limit_bytes=_vmem_limit(est)),
        name="even_sample",
    )(proj, proj, proj, proj, proj, proj, proj, gates, num, conv_st, n_st, m_st,
      w_conv, b_gates, mh_g, mixin)


def _gmlp_norm(u_raw, v_raw, g, b):
    u = jax.nn.gelu(u_raw)
    vn = _layer_norm(jax.nn.gelu(v_raw), g, b)
    return u, vn


def _odd_prompt_kernel(p_ref, u_ref, v_ref, wp_ref, sc_ref, gmg_ref, gmb_ref, ws_ref, bst_ref,
                       mix_ref, pool_ref, gv_ref, pbuf):
    s = pl.program_id(1)
    R = RO

    @pl.when(s == 0)
    def _():
        pbuf[0:HIST, :] = jnp.zeros((HIST, D_POOL), F32)

    p = p_ref[...]
    pbuf[HIST:HIST + R, :] = p
    pos = s * R + lax.broadcasted_iota(jnp.int32, (R, 1), 0)
    for g, w in enumerate(POOL_WINDOWS):
        lo, hi = g * POOL_GW, (g + 1) * POOL_GW
        win = p[:, lo:hi]
        for jj in range(1, w):
            win = win + pbuf[HIST - jj:HIST - jj + R, lo:hi]
        cnt = jnp.minimum(w, pos + 1).astype(F32)
        diff = win / cnt - p[:, lo:hi]
        mix_ref[:, lo:hi] = _dot(diff, wp_ref[g]) * sc_ref[0:1, lo:hi]
    pool_ref[...] = pbuf[HIST + R - POOL_BUF:HIST + R, :]
    pbuf[0:HIST, :] = pbuf[R:R + HIST, :]

    u, vn = _gmlp_norm(u_ref[...], v_ref[...], gmg_ref[...], gmb_ref[...])
    L = GMLP_CHUNK
    tril = lax.broadcasted_iota(jnp.int32, (L, L), 0) >= lax.broadcasted_iota(jnp.int32, (L, L), 1)
    for g in range(D_GMLP // GMLP_GW):
        lo, hi = g * GMLP_GW, (g + 1) * GMLP_GW
        ws = jnp.where(tril, ws_ref[g], 0.0)
        bcol = bst_ref[:, g:g + 1]
        for ck in range(R // L):
            r0, r1 = ck * L, (ck + 1) * L
            sv = _dot(ws, vn[r0:r1, lo:hi]) + bcol
            mix_ref[r0:r1, D_POOL + lo:D_POOL + hi] = u[r0:r1, lo:hi] * sv
    gv_ref[...] = vn[R - L:R, :]


def _odd_prompt(proj, w_pool, pool_scale, gm_g, gm_b, w_spatial, bs_t, j):
    nrb = SEQ // RO
    r = lambda b, s: b * nrb + s
    est = 4 * (2 * 3 * RO * 1024 + 2 * RO * 2048 + 2 * 4 * 256 * 256 + 2 * 4 * 128 * 128
               + (RO + HIST) * 1024 + 8 * RO * 1024) + (8 << 20)
    return pl.pallas_call(
        _odd_prompt_kernel,
        grid=(BATCH, nrb),
        in_specs=[
            pl.BlockSpec((RO, D_POOL), lambda b, s: (r(b, s), 0)),
            pl.BlockSpec((RO, D_GMLP), lambda b, s: (r(b, s), 1)),
            pl.BlockSpec((RO, D_GMLP), lambda b, s: (r(b, s), 2)),
            pl.BlockSpec((None, 4, POOL_GW, POOL_GW), lambda b, s: (j, 0, 0, 0)),
            pl.BlockSpec((None, 1, D_POOL), lambda b, s: (j, 0, 0)),
            pl.BlockSpec((None, 1, D_GMLP), lambda b, s: (j, 0, 0)),
            pl.BlockSpec((None, 1, D_GMLP), lambda b, s: (j, 0, 0)),
            pl.BlockSpec((None, 4, GMLP_CHUNK, GMLP_CHUNK), lambda b, s: (j, 0, 0, 0)),
            pl.BlockSpec((None, GMLP_CHUNK, 4), lambda b, s: (j, 0, 0)),
        ],
        out_specs=[
            pl.BlockSpec((RO, D_MODEL), lambda b, s: (r(b, s), 0)),
            pl.BlockSpec((None, POOL_BUF, D_POOL), lambda b, s: (b, 0, 0)),
            pl.BlockSpec((None, GMLP_CHUNK, D_GMLP), lambda b, s: (b, 0, 0)),
        ],
        out_shape=[
            jax.ShapeDtypeStruct((M, D_MODEL), F32),
            jax.ShapeDtypeStruct((BATCH, POOL_BUF, D_POOL), F32),
            jax.ShapeDtypeStruct((BATCH, GMLP_CHUNK, D_GMLP), F32),
        ],
        scratch_shapes=[pltpu.VMEM((RO + HIST, D_POOL), F32)],
        compiler_params=pltpu.CompilerParams(
            dimension_semantics=("arbitrary", "arbitrary"), vmem_limit_bytes=_vmem_limit(est)),
        name="odd_prompt",
    )(proj, proj, proj, w_pool, pool_scale, gm_g, gm_b, w_spatial, bs_t)


def _odd_sample_kernel(p_ref, u_ref, v_ref, st_ref, wp_ref, sc_ref, gmg_ref, gmb_ref, ws_ref, bst_ref,
                       *rest):
    mix_ref, pool_ref, gv_ref = rest[-3:]
    p = p_ref[...]
    for r in range(POOL_BUF - 1):
        pool_ref[r] = st_ref[r + 1]
    pool_ref[POOL_BUF - 1] = p
    for g, w in enumerate(POOL_WINDOWS):
        lo, hi = g * POOL_GW, (g + 1) * POOL_GW
        win = p[:, lo:hi]
        for jj in range(1, w):
            win = win + st_ref[POOL_BUF - jj, :, lo:hi]
        cnt = float(min(w, PAST_LEN + 1))
        diff = win / cnt - p[:, lo:hi]
        mix_ref[:, lo:hi] = _dot(diff, wp_ref[g]) * sc_ref[0:1, lo:hi]
    u, vn = _gmlp_norm(u_ref[...], v_ref[...], gmg_ref[...], gmb_ref[...])
    gv_ref[...] = vn
    for g in range(D_GMLP // GMLP_GW):
        lo, hi = g * GMLP_GW, (g + 1) * GMLP_GW
        sv = ws_ref[g, 0:1, 0:1] * vn[:, lo:hi] + bst_ref[0:1, g:g + 1]
        mix_ref[:, D_POOL + lo:D_POOL + hi] = u[:, lo:hi] * sv


def _odd_sample(proj, pool_st, w_pool, pool_scale, gm_g, gm_b, w_spatial, bs_t, mixin, pool_prev, j):
    nb = DEC_BATCH
    rb = MP // nb
    st_blk = (None, POOL_BUF, nb, D_POOL)
    in_specs = [
        pl.BlockSpec((nb, D_POOL), lambda i: (rb, 0)),
        pl.BlockSpec((nb, D_GMLP), lambda i: (rb, 1)),
        pl.BlockSpec((nb, D_GMLP), lambda i: (rb, 2)),
        pl.BlockSpec(st_blk, lambda i: (j, 0, 0, 0)),
        pl.BlockSpec((None, 4, POOL_GW, POOL_GW), lambda i: (j, 0, 0, 0)),
        pl.BlockSpec((None, 1, D_POOL), lambda i: (j, 0, 0)),
        pl.BlockSpec((None, 1, D_GMLP), lambda i: (j, 0, 0)),
        pl.BlockSpec((None, 1, D_GMLP), lambda i: (j, 0, 0)),
        pl.BlockSpec((None, 4, GMLP_CHUNK, GMLP_CHUNK), lambda i: (j, 0, 0, 0)),
        pl.BlockSpec((None, GMLP_CHUNK, 4), lambda i: (j, 0, 0)),
        pl.BlockSpec(memory_space=pl.ANY),
    ]
    args = [proj, proj, proj, pool_st, w_pool, pool_scale, gm_g, gm_b, w_spatial, bs_t, mixin]
    aliases = {len(args) - 1: 0}
    if pool_prev is not None:
        in_specs.append(pl.BlockSpec(memory_space=pl.ANY))
        args.append(pool_prev)
        aliases[len(args) - 1] = 1
    est = 4 * (4 * POOL_BUF * nb * D_POOL + 2 * (3 * nb * 1024 + 4 * 256 * 256 + 4 * 128 * 128
                                                + nb * 2048 + nb * 1024)) + (8 << 20)
    return pl.pallas_call(
        _odd_sample_kernel,
        grid=(1,),
        in_specs=in_specs,
        out_specs=[
            pl.BlockSpec((nb, D_MODEL), lambda i: (rb, 0)),
            pl.BlockSpec(st_blk, lambda i: (j, 0, 0, 0)),
            pl.BlockSpec((nb, D_GMLP), lambda i: (0, 0)),
        ],
        out_shape=[
            jax.ShapeDtypeStruct((M, D_MODEL), F32),
            jax.ShapeDtypeStruct(pool_st.shape, F32),
            jax.ShapeDtypeStruct((nb, D_GMLP), F32),
        ],
        input_output_aliases=aliases,
        compiler_params=pltpu.CompilerParams(
            dimension_semantics=("arbitrary",), vmem_limit_bytes=_vmem_limit(est)),
        name="odd_sample",
    )(*args)


def kernel(x_prompt, x_sample, state_conv, state_mlstm_C, state_mlstm_n, state_mlstm_m, state_pool,
           ln_g, ln_b, w_ffn_in, w_ffn_out, w_in_even, b_gates_even, w_conv, mh_norm_g, w_out_even,
           w_in_odd, w_pool, pool_scale, gm_ln_g, gm_ln_b, w_spatial, b_spatial, w_out_odd):
    n_even, n_odd = w_in_even.shape[0], w_in_odd.shape[0]
    x = jnp.concatenate([x_prompt.reshape(MP, D_MODEL), x_sample.reshape(DEC_BATCH, D_MODEL)], axis=0)

    ln_g3 = ln_g.reshape(DEPTH * 3, 1, D_MODEL)
    ln_b3 = ln_b.reshape(DEPTH * 3, 1, D_MODEL)
    w_even_t = jnp.swapaxes(w_in_even, 1, 2)
    w_gate_t = jnp.pad(w_even_t[:, EVEN_MAIN:, :], ((0, 0), (0, LANES - 2 * HEADS), (0, 0)))
    pool_st = jnp.swapaxes(state_pool, 1, 2)
    mh_g3 = mh_norm_g.reshape(n_even, 1, HEADS * DV)
    conv_st = state_conv.reshape(n_even, DEC_BATCH, (CONV_W - 1) * D_CONV)
    n_st = state_mlstm_n.reshape(n_even, DEC_BATCH, HEADS * DK)
    scale3 = pool_scale.reshape(n_odd, 1, D_POOL)
    gm_g3 = gm_ln_g.reshape(n_odd, 1, D_GMLP)
    gm_b3 = gm_ln_b.reshape(n_odd, 1, D_GMLP)
    bs_t = jnp.swapaxes(b_spatial, 1, 2)

    conv_p, conv_s, c_p, n_p, n_s, m_p, m_s = [], [], [], [], [], [], []
    pool_p, gv_p, gv_s = [], [], []
    c_s = None
    pool_s = None

    for layer in range(DEPTH):
        j = layer // 2
        x = _ffn(x, w_ffn_in, w_ffn_out, ln_g3, ln_b3, layer, 0, 3 * layer)
        if layer % 2 == 0:
            proj, gates = _proj(x, w_even_t, j, 0, EVEN_MAIN, "proj_even", w_transposed=True,
                                w_gate_t=w_gate_t)
            proj = _proj(x, w_even_t, j, 1, EVEN_MAIN, "proj_even", w_transposed=True, prev=proj)
            mixin, cv, cc, nn, mm = _even_prompt(proj, gates, w_conv, b_gates_even, mh_g3, j)
            conv_p.append(cv)
            c_p.append(cc)
            n_p.append(nn)
            m_p.append(mm.reshape(BATCH, HEADS))
            qk_cols = _qk_cols(x, w_even_t, j)
            c_s, num = _even_sample_state(state_mlstm_C, qk_cols, proj, gates, state_mlstm_m,
                                          b_gates_even, j, c_s)
            mixin, cvs, nns, mms = _even_sample(proj, gates, num, conv_st, n_st, state_mlstm_m,
                                                w_conv, b_gates_even, mh_g3, mixin, j)
            conv_s.append(cvs.reshape(DEC_BATCH, CONV_W - 1, D_CONV))
            n_s.append(nns.reshape(DEC_BATCH, HEADS, DK))
            m_s.append(mms)
            x = _outproj(x, mixin, w_out_even, ln_g3, ln_b3, j, 3 * layer + 1)
        else:
            proj = _proj(x, w_in_odd, j, 0, ODD_IN, "proj_odd")
            mixin, pp, gv = _odd_prompt(proj, w_pool, scale3, gm_g3, gm_b3, w_spatial, bs_t, j)
            pool_p.append(pp)
            gv_p.append(gv)
            mixin, pool_s, gvs = _odd_sample(proj, pool_st, w_pool, scale3, gm_g3, gm_b3, w_spatial, bs_t,
                                             mixin, pool_s, j)
            gv_s.append(gvs.reshape(DEC_BATCH, 1, D_GMLP))
            x = _outproj(x, mixin, w_out_odd, ln_g3, ln_b3, j, 3 * layer + 1)
        x = _ffn(x, w_ffn_in, w_ffn_out, ln_g3, ln_b3, layer, 1, 3 * layer + 2,
                 split_out=layer == DEPTH - 1)

    y_prompt = x[0].reshape(BATCH, SEQ, D_MODEL)
    y_sample = x[1].reshape(DEC_BATCH, 1, D_MODEL)
    return (y_prompt, y_sample,
            jnp.stack(conv_p), jnp.stack(conv_s),
            jnp.stack(c_p), c_s,
            jnp.stack(n_p), jnp.stack(n_s),
            jnp.stack(m_p), jnp.stack(m_s),
            jnp.stack(pool_p), jnp.swapaxes(pool_s, 1, 2),
            jnp.stack(gv_p), jnp.stack(gv_s))
```

```python
import functools

import jax
import jax.numpy as jnp
from jax import lax
from jax.experimental import pallas as pl
from jax.experimental.pallas import tpu as pltpu

F32 = jnp.float32

D_MODEL = 2048
BATCH = 4
SEQ = 2048
DEPTH = 4
DEC_BATCH = 128
PAST_LEN = 16384
D_FF = 5632
D_CONV = 1024
CONV_W = 3
HEADS = 4
DK = 128
DV = 256
D_POOL = 1024
POOL_WINDOWS = (2, 4, 8, 16)
POOL_GW = 256
POOL_BUF = 15
D_GMLP = 1024
GMLP_GW = 256
GMLP_CHUNK = 128
ALPHA = (2 * DEPTH) ** 0.25
LN_EPS = 1e-5
EVEN_MAIN = 3 * D_CONV + 2 * HEADS * DK + 2 * HEADS * DV
ODD_IN = D_POOL + 2 * D_GMLP

MP = BATCH * SEQ
M = MP + DEC_BATCH

LANES = 128
SUBLANES = 8
VMEM_BYTES_V7X = 64 * 1024 * 1024

TM = 1040
TF = 256
TN = 512
TMP = 520
PROJ_COLS = 3072
TMO = 416
LC = 256
RE = 256
RO = 256
SB = 16
HIST = 16


def _vmem_limit(nbytes):
    return int(min(VMEM_BYTES_V7X - 4 * 1024 * 1024, nbytes))


def _layer_norm(y, g, b, eps=LN_EPS):
    mu = jnp.mean(y, axis=-1, keepdims=True)
    yc = y - mu
    var = jnp.mean(yc * yc, axis=-1, keepdims=True)
    out = yc * lax.rsqrt(var + eps) * g
    if b is not None:
        out = out + b
    return out


def _log_sigmoid(x):
    return -(jnp.maximum(-x, 0.0) + jnp.log1p(jnp.exp(-jnp.abs(x))))


def _dot(a, b):
    return jnp.dot(a, b, preferred_element_type=F32)


def _dot_nt(a, b):
    return lax.dot_general(a, b, (((1,), (1,)), ((), ())), preferred_element_type=F32)


def _ffn_kernel(x_ref, wg_ref, wu_ref, wo_ref, g_ref, b_ref, o_ref, *maybe_sample_ref, nf):
    f = pl.program_id(1)

    @pl.when(f == 0)
    def _():
        o_ref[...] = (2.0 * ALPHA) * x_ref[...]

    x = x_ref[...]
    gate = _dot(x, wg_ref[...])
    up = _dot(x, wu_ref[...])
    h = gate * jax.nn.sigmoid(gate) * up
    for n in range(D_MODEL // TN):
        o_ref[:, n * TN:(n + 1) * TN] += _dot(h, wo_ref[:, n * TN:(n + 1) * TN])

    @pl.when(f == nf - 1)
    def _():
        o_ref[...] = _layer_norm(o_ref[...], g_ref[...], b_ref[...], eps=4.0 * LN_EPS)

    if maybe_sample_ref:
        @pl.when((f == nf - 1) & (pl.program_id(0) == M // TM - 1))
        def _():
            maybe_sample_ref[0][...] = o_ref[TM - DEC_BATCH:, :]


def _ffn(x, w_ffn_in, w_ffn_out, ln_g, ln_b, layer, which, ln_idx, split_out=False):
    nf = D_FF // TF
    est = 4 * (4 * TM * D_MODEL + 2 * 3 * D_MODEL * TF + TM * D_MODEL + 4 * TM * TF) + (4 << 20)
    out_specs = pl.BlockSpec((TM, D_MODEL), lambda i, f: (i, 0))
    out_shape = jax.ShapeDtypeStruct((M, D_MODEL), F32)
    if split_out:
        out_specs = [out_specs, pl.BlockSpec((DEC_BATCH, D_MODEL), lambda i, f: (0, 0))]
        out_shape = [jax.ShapeDtypeStruct((MP, D_MODEL), F32),
                     jax.ShapeDtypeStruct((DEC_BATCH, D_MODEL), F32)]
    return pl.pallas_call(
        functools.partial(_ffn_kernel, nf=nf),
        grid=(M // TM, nf),
        in_specs=[
            pl.BlockSpec((TM, D_MODEL), lambda i, f: (i, 0)),
            pl.BlockSpec((None, None, D_MODEL, TF), lambda i, f: (layer, which, 0, f)),
            pl.BlockSpec((None, None, D_MODEL, TF), lambda i, f: (layer, which, 0, nf + f)),
            pl.BlockSpec((None, None, TF, D_MODEL), lambda i, f: (layer, which, f, 0)),
            pl.BlockSpec((None, 1, D_MODEL), lambda i, f: (ln_idx, 0, 0)),
            pl.BlockSpec((None, 1, D_MODEL), lambda i, f: (ln_idx, 0, 0)),
        ],
        out_specs=out_specs,
        out_shape=out_shape,
        compiler_params=pltpu.CompilerParams(
            dimension_semantics=("arbitrary", "arbitrary"), vmem_limit_bytes=_vmem_limit(est)),
        name="ffn_ln",
    )(x, w_ffn_in, w_ffn_in, w_ffn_out, ln_g, ln_b)


def _proj_kernel(x_ref, w_ref, *rest, w_transposed, with_gates, aliased):
    rest = rest[1:] if aliased else rest
    if with_gates:
        wgate_ref, o_ref, gate_ref = rest
        gate_ref[...] = _dot_nt(x_ref[...], wgate_ref[...])
    else:
        (o_ref,) = rest
    for n in range(PROJ_COLS // TN):
        lo, hi = n * TN, (n + 1) * TN
        if w_transposed:
            o_ref[:, lo:hi] = _dot_nt(x_ref[...], w_ref[lo:hi, :])
        else:
            o_ref[:, lo:hi] = _dot(x_ref[...], w_ref[:, lo:hi])


def _proj(x, w, layer_idx, col_block, n_total, name, w_transposed=False, w_gate_t=None, prev=None):
    est = 4 * (PROJ_COLS * D_MODEL + 2 * TMP * D_MODEL + 2 * TMP * PROJ_COLS + 2 * TMP * TN) + (4 << 20)
    if w_transposed:
        w_spec = pl.BlockSpec((None, PROJ_COLS, D_MODEL), lambda i: (layer_idx, col_block, 0),
                              pipeline_mode=pl.Buffered(1))
    else:
        w_spec = pl.BlockSpec((None, D_MODEL, PROJ_COLS), lambda i: (layer_idx, 0, col_block),
                              pipeline_mode=pl.Buffered(1))
    in_specs = [pl.BlockSpec((TMP, D_MODEL), lambda i: (i, 0)), w_spec]
    args = [x, w]
    aliases = {}
    if prev is not None:
        in_specs.append(pl.BlockSpec(memory_space=pl.ANY))
        args.append(prev)
        aliases = {2: 0}
    out_specs = pl.BlockSpec((TMP, PROJ_COLS), lambda i: (i, col_block))
    out_shape = jax.ShapeDtypeStruct((M, n_total), F32)
    if w_gate_t is not None:
        in_specs.append(pl.BlockSpec((None, LANES, D_MODEL), lambda i: (layer_idx, 0, 0)))
        args.append(w_gate_t)
        out_specs = [out_specs, pl.BlockSpec((TMP, LANES), lambda i: (i, 0))]
        out_shape = [out_shape, jax.ShapeDtypeStruct((M, LANES), F32)]
        est += 4 * (2 * LANES * D_MODEL + 3 * TMP * LANES)
    return pl.pallas_call(
        functools.partial(_proj_kernel, w_transposed=w_transposed, with_gates=w_gate_t is not None,
                          aliased=prev is not None),
        grid=(M // TMP,),
        in_specs=in_specs,
        out_specs=out_specs,
        out_shape=out_shape,
        input_output_aliases=aliases,
        compiler_params=pltpu.CompilerParams(
            dimension_semantics=("parallel",), vmem_limit_bytes=_vmem_limit(est)),
        name=name,
    )(*args)


def _outproj_kernel(x_ref, a_ref, w_ref, g_ref, b_ref, o_ref):
    y = ALPHA * x_ref[...] + _dot(a_ref[...], w_ref[...])
    o_ref[...] = _layer_norm(y, g_ref[...], b_ref[...])


def _outproj(x, mix, w_out, ln_g, ln_b, layer_idx, ln_idx):
    est = 4 * (D_MODEL * D_MODEL + 9 * TMO * D_MODEL) + (4 << 20)
    return pl.pallas_call(
        _outproj_kernel,
        grid=(M // TMO,),
        in_specs=[
            pl.BlockSpec((TMO, D_MODEL), lambda i: (i, 0)),
            pl.BlockSpec((TMO, D_MODEL), lambda i: (i, 0)),
            pl.BlockSpec((None, D_MODEL, D_MODEL), lambda i: (layer_idx, 0, 0),
                         pipeline_mode=pl.Buffered(1)),
            pl.BlockSpec((None, 1, D_MODEL), lambda i: (ln_idx, 0, 0)),
            pl.BlockSpec((None, 1, D_MODEL), lambda i: (ln_idx, 0, 0)),
        ],
        out_specs=pl.BlockSpec((TMO, D_MODEL), lambda i: (i, 0)),
        out_shape=jax.ShapeDtypeStruct((M, D_MODEL), F32),
        compiler_params=pltpu.CompilerParams(
            dimension_semantics=("parallel",), vmem_limit_bytes=_vmem_limit(est)),
        name="outproj_ln",
    )(x, mix, w_out, ln_g, ln_b)


def _split3(x):
    h1 = x.astype(jnp.bfloat16).astype(F32)
    r = x - h1
    h2 = r.astype(jnp.bfloat16).astype(F32)
    return h1, h2, r - h2


def _even_prompt_kernel(bg_ref, cg_ref, xin_ref, q_ref, k_ref, v_ref, og_ref, gt_ref,
                        wc_ref, bgate_ref, mhg_ref,
                        mix_ref, conv_ref, c_out_ref, n_out_ref, m_out_ref,
                        cbuf, c_sc, n_sc, m_sc):
    c = pl.program_id(1)
    L = LC
    R = RE

    @pl.when(c == 0)
    def _():
        cbuf[0:SUBLANES, :] = jnp.zeros((SUBLANES, D_CONV), F32)
        c_sc[...] = jnp.zeros_like(c_sc)
        n_sc[...] = jnp.zeros_like(n_sc)
        m_sc[...] = jnp.zeros_like(m_sc)

    cx = cg_ref[...] * xin_ref[...]
    cbuf[SUBLANES:SUBLANES + R, :] = cx
    c1 = cbuf[SUBLANES - 1:SUBLANES - 1 + R, :]
    c2 = cbuf[SUBLANES - 2:SUBLANES - 2 + R, :]
    conv = c2 * wc_ref[0:1, :] + c1 * wc_ref[1:2, :] + cx * wc_ref[2:3, :]
    mix_ref[:, 0:D_CONV] = bg_ref[...] * conv
    conv_ref[...] = cbuf[SUBLANES + R - 2:SUBLANES + R, :]
    cbuf[0:SUBLANES, :] = cbuf[R:R + SUBLANES, :]

    row = lax.broadcasted_iota(jnp.int32, (L, L), 0)
    col = lax.broadcasted_iota(jnp.int32, (L, L), 1)
    causal = row >= col
    tril = causal.astype(F32)
    c_state = [c_sc[h] for h in range(HEADS)]
    n_state = [n_sc[h:h + 1, :] for h in range(HEADS)]
    m_state = [m_sc[h, 0:1, 0:1] for h in range(HEADS)]

    for s in range(R // L):
        r0, r1 = s * L, (s + 1) * L
        gt = gt_ref[r0:r1, :]
        li_all = gt[:, 0:HEADS] + bgate_ref[0:1, :]
        lf_all = _log_sigmoid(gt[:, HEADS:2 * HEADS] + bgate_ref[1:2, :])
        f1, f2, f3 = _split3(lf_all)
        b_all = _dot(tril, f1) + _dot(tril, f2) + _dot(tril, f3)
        z = jnp.concatenate([li_all, b_all, jnp.zeros((L, LANES - 2 * HEADS), F32)], axis=1)
        zt = z.T

        for h in range(HEADS):
            li_c = li_all[:, h:h + 1]
            b_c = b_all[:, h:h + 1]
            li_r = zt[h:h + 1, :]
            b_r = zt[HEADS + h:HEADS + h + 1, :]
            m0 = m_state[h]
            dm = jnp.where(causal, b_c - b_r + li_r, -jnp.inf)
            inter = b_c + m0
            m_t = jnp.maximum(inter, jnp.max(dm, axis=-1, keepdims=True))
            a_int = jnp.exp(inter - m_t)
            qh = q_ref[r0:r1, h * DK:(h + 1) * DK]
            kh = k_ref[r0:r1, h * DK:(h + 1) * DK] * (DK ** -0.5)
            vh = v_ref[r0:r1, h * DV:(h + 1) * DV]
            w = jnp.exp(dm - m_t) * _dot_nt(qh, kh)
            ch = c_state[h]
            n_row = n_state[h]
            num = a_int * _dot(qh, ch) + _dot(w, vh)
            den = (a_int * jnp.sum(qh * n_row, axis=-1, keepdims=True)
                   + jnp.sum(w, axis=-1, keepdims=True))
            hh = num / jnp.maximum(jnp.abs(den), jnp.exp(-m_t))
            m_new = m_t[L - 1:L, :]
            b_last = b_c[L - 1:L, :]
            w_end = jnp.exp(b_last - b_c + li_c - m_new)
            decay = jnp.exp(b_last + m0 - m_new)
            wk = w_end * kh
            c_state[h] = decay * ch + _dot(wk.T, vh)
            n_state[h] = decay * n_row + jnp.sum(wk, axis=0, keepdims=True)
            m_state[h] = m_new
            hn = _layer_norm(hh, mhg_ref[0:1, h * DV:(h + 1) * DV], None)
            og = og_ref[r0:r1, h * DV:(h + 1) * DV]
            mix_ref[r0:r1, D_CONV + h * DV:D_CONV + (h + 1) * DV] = hn * jax.nn.sigmoid(og)

    for h in range(HEADS):
        c_sc[h] = c_state[h]
        n_sc[h:h + 1, :] = n_state[h]
        m_sc[h] = jnp.broadcast_to(m_state[h], (SUBLANES, LANES))
        c_out_ref[h] = c_state[h]
        n_out_ref[h:h + 1, :] = n_state[h]
        m_out_ref[0:1, h:h + 1] = m_state[h]


def _even_prompt(proj, gates, w_conv, b_gates, mh_g, j):
    nck = SEQ // RE
    r = lambda b, c: b * nck + c
    est = 4 * (2 * (5 * RE * 1024 + 2 * RE * 512 + RE * 128) + 2 * RE * 2048 + 4 * RE * 1024
               + 3 * HEADS * DK * DV + 24 * LC * LC) + (8 << 20)
    return pl.pallas_call(
        _even_prompt_kernel,
        grid=(BATCH, nck),
        in_specs=[
            pl.BlockSpec((RE, D_CONV), lambda b, c: (r(b, c), 0)),
            pl.BlockSpec((RE, D_CONV), lambda b, c: (r(b, c), 1)),
            pl.BlockSpec((RE, D_CONV), lambda b, c: (r(b, c), 2)),
            pl.BlockSpec((RE, HEADS * DK), lambda b, c: (r(b, c), 6)),
            pl.BlockSpec((RE, HEADS * DK), lambda b, c: (r(b, c), 7)),
            pl.BlockSpec((RE, HEADS * DV), lambda b, c: (r(b, c), 4)),
            pl.BlockSpec((RE, HEADS * DV), lambda b, c: (r(b, c), 5)),
            pl.BlockSpec((RE, LANES), lambda b, c: (r(b, c), 0)),
            pl.BlockSpec((None, CONV_W, D_CONV), lambda b, c: (j, 0, 0)),
            pl.BlockSpec((None, 2, HEADS), lambda b, c: (j, 0, 0)),
            pl.BlockSpec((None, 1, HEADS * DV), lambda b, c: (j, 0, 0)),
        ],
        out_specs=[
            pl.BlockSpec((RE, D_MODEL), lambda b, c: (r(b, c), 0)),
            pl.BlockSpec((None, CONV_W - 1, D_CONV), lambda b, c: (b, 0, 0)),
            pl.BlockSpec((None, HEADS, DK, DV), lambda b, c: (b, 0, 0, 0)),
            pl.BlockSpec((None, HEADS, DK), lambda b, c: (b, 0, 0)),
            pl.BlockSpec((None, 1, HEADS), lambda b, c: (b, 0, 0)),
        ],
        out_shape=[
            jax.ShapeDtypeStruct((M, D_MODEL), F32),
            jax.ShapeDtypeStruct((BATCH, CONV_W - 1, D_CONV), F32),
            jax.ShapeDtypeStruct((BATCH, HEADS, DK, DV), F32),
            jax.ShapeDtypeStruct((BATCH, HEADS, DK), F32),
            jax.ShapeDtypeStruct((BATCH, 1, HEADS), F32),
        ],
        scratch_shapes=[
            pltpu.VMEM((RE + SUBLANES, D_CONV), F32),
            pltpu.VMEM((HEADS, DK, DV), F32),
            pltpu.VMEM((HEADS, DK), F32),
            pltpu.VMEM((HEADS, SUBLANES, LANES), F32),
        ],
        compiler_params=pltpu.CompilerParams(
            dimension_semantics=("arbitrary", "arbitrary"), vmem_limit_bytes=_vmem_limit(est)),
        name="even_prompt",
    )(proj, proj, proj, proj, proj, proj, proj, gates, w_conv, b_gates, mh_g)


def _sample_gates(gt, bgate_ref, m):
    li = gt[:, 0:HEADS] + bgate_ref[0:1, :]
    lf = _log_sigmoid(gt[:, HEADS:2 * HEADS] + bgate_ref[1:2, :])
    inter = lf + m
    m_t = jnp.maximum(inter, li)
    return jnp.exp(inter - m_t), jnp.exp(li - m_t), m_t


def _qk_cols_kernel(x_ref, w_ref, o_ref):
    t = _dot_nt(w_ref[...], x_ref[...])
    for s in range(DEC_BATCH // SB):
        o_ref[s] = t[:, s * SB:(s + 1) * SB]


def _qk_cols(x, w_even_t, j):
    nqk = 2 * HEADS * DK
    est = 4 * (2 * DEC_BATCH * D_MODEL + 2 * nqk * D_MODEL + 3 * nqk * LANES
               + 2 * (DEC_BATCH // SB) * nqk * LANES) + (4 << 20)
    return pl.pallas_call(
        _qk_cols_kernel,
        grid=(1,),
        in_specs=[
            pl.BlockSpec((DEC_BATCH, D_MODEL), lambda i: (MP // DEC_BATCH, 0)),
            pl.BlockSpec((None, nqk, D_MODEL), lambda i: (j, 3 * D_CONV // nqk, 0)),
        ],
        out_specs=pl.BlockSpec((DEC_BATCH // SB, nqk, SB), lambda i: (0, 0, 0)),
        out_shape=jax.ShapeDtypeStruct((DEC_BATCH // SB, nqk, SB), F32),
        compiler_params=pltpu.CompilerParams(
            dimension_semantics=("arbitrary",), vmem_limit_bytes=_vmem_limit(est)),
        name="qk_cols",
    )(x, w_even_t)


def _even_sample_state_kernel(c_ref, qk_ref, v_ref, gt_ref, m_ref, bgate_ref, *rest):
    c_out_ref, num_ref = rest[-2], rest[-1]
    a, wgt, _ = _sample_gates(gt_ref[...], bgate_ref, m_ref[...])
    for bi in range(SB):
        for h in range(HEADS):
            a_s = a[bi:bi + 1, h:h + 1]
            w_s = wgt[bi:bi + 1, h:h + 1]
            ch = c_ref[bi, h]
            qc = qk_ref[h * DK:(h + 1) * DK, bi:bi + 1]
            kc = qk_ref[(HEADS + h) * DK:(HEADS + h + 1) * DK, bi:bi + 1] * (DK ** -0.5)
            vr = v_ref[bi:bi + 1, h * DV:(h + 1) * DV]
            c_out_ref[bi, h] = a_s * ch + (w_s * kc) * vr
            num_ref[bi:bi + 1, h * DV:(h + 1) * DV] = jnp.sum(qc * ch, axis=0, keepdims=True)


def _even_sample_state(state_c, qk_cols, proj, gates, state_m, b_gates, j, c_prev):
    blk5 = (None, SB, HEADS, DK, DV)
    rb = MP // SB
    in_specs = [
        pl.BlockSpec(blk5, lambda i: (j, i, 0, 0, 0)),
        pl.BlockSpec((None, 2 * HEADS * DK, SB), lambda i: (i, 0, 0)),
        pl.BlockSpec((SB, HEADS * DV), lambda i: (rb + i, 4)),
        pl.BlockSpec((SB, LANES), lambda i: (rb + i, 0)),
        pl.BlockSpec((None, SB, HEADS), lambda i: (j, i, 0)),
        pl.BlockSpec((None, 2, HEADS), lambda i: (j, 0, 0)),
    ]
    args = [state_c, qk_cols, proj, gates, state_m, b_gates]
    aliases = {}
    if c_prev is not None:
        in_specs.append(pl.BlockSpec(memory_space=pl.ANY))
        args.append(c_prev)
        aliases = {len(args) - 1: 0}
    est = 4 * (4 * SB * HEADS * DK * DV + 4 * HEADS * DK * LANES) + (8 << 20)
    return pl.pallas_call(
        _even_sample_state_kernel,
        grid=(DEC_BATCH // SB,),
        in_specs=in_specs,
        out_specs=[
            pl.BlockSpec(blk5, lambda i: (j, i, 0, 0, 0)),
            pl.BlockSpec((SB, HEADS * DV), lambda i: (i, 0)),
        ],
        out_shape=[
            jax.ShapeDtypeStruct(state_c.shape, F32),
            jax.ShapeDtypeStruct((DEC_BATCH, HEADS * DV), F32),
        ],
        input_output_aliases=aliases,
        compiler_params=pltpu.CompilerParams(
            dimension_semantics=("arbitrary",), vmem_limit_bytes=_vmem_limit(est)),
        name="even_sample_state",
    )(*args)


def _even_sample_kernel(bg_ref, cg_ref, xin_ref, q_ref, k_ref, v_ref, og_ref, gt_ref,
                        num_ref, cst_ref, n_ref, m_ref, wc_ref, bgate_ref, mhg_ref, mixin_hbm,
                        mix_ref, conv_ref, n_out_ref, m_out_ref):
    del mixin_hbm
    cx = cg_ref[...] * xin_ref[...]
    st0 = cst_ref[:, 0:D_CONV]
    st1 = cst_ref[:, D_CONV:2 * D_CONV]
    conv = st0 * wc_ref[0:1, :] + st1 * wc_ref[1:2, :] + cx * wc_ref[2:3, :]
    mix_ref[:, 0:D_CONV] = bg_ref[...] * conv
    conv_ref[:, 0:D_CONV] = st1
    conv_ref[:, D_CONV:2 * D_CONV] = cx

    a, wgt, m_t = _sample_gates(gt_ref[...], bgate_ref, m_ref[...])
    m_out_ref[...] = m_t
    floor = jnp.exp(-m_t)
    for h in range(HEADS):
        a_h = a[:, h:h + 1]
        w_h = wgt[:, h:h + 1]
        qh = q_ref[:, h * DK:(h + 1) * DK]
        kh = k_ref[:, h * DK:(h + 1) * DK] * (DK ** -0.5)
        vh = v_ref[:, h * DV:(h + 1) * DV]
        nh = n_ref[:, h * DK:(h + 1) * DK]
        wt = w_h * jnp.sum(qh * kh, axis=-1, keepdims=True)
        num = a_h * num_ref[:, h * DV:(h + 1) * DV] + wt * vh
        den = a_h * jnp.sum(qh * nh, axis=-1, keepdims=True) + wt
        hh = num / jnp.maximum(jnp.abs(den), floor[:, h:h + 1])
        n_out_ref[:, h * DK:(h + 1) * DK] = a_h * nh + w_h * kh
        hn = _layer_norm(hh, mhg_ref[0:1, h * DV:(h + 1) * DV], None)
        og = og_ref[:, h * DV:(h + 1) * DV]
        mix_ref[:, D_CONV + h * DV:D_CONV + (h + 1) * DV] = hn * jax.nn.sigmoid(og)


def _even_sample(proj, gates, num, conv_st, n_st, m_st, w_conv, b_gates, mh_g, mixin, j):
    nb = DEC_BATCH
    rb = MP // nb
    est = 4 * 2 * (5 * nb * 1024 + 2 * nb * 512 + nb * 128 + nb * 1024 + nb * 2048 + nb * 512
                   + nb * 2048 + nb * 2048 + nb * 512) + (8 << 20)
    return pl.pallas_call(
        _even_sample_kernel,
        grid=(1,),
        in_specs=[
            pl.BlockSpec((nb, D_CONV), lambda i: (rb, 0)),
            pl.BlockSpec((nb, D_CONV), lambda i: (rb, 1)),
            pl.BlockSpec((nb, D_CONV), lambda i: (rb, 2)),
            pl.BlockSpec((nb, HEADS * DK), lambda i: (rb, 6)),
            pl.BlockSpec((nb, HEADS * DK), lambda i: (rb, 7)),
            pl.BlockSpec((nb, HEADS * DV), lambda i: (rb, 4)),
            pl.BlockSpec((nb, HEADS * DV), lambda i: (rb, 5)),
            pl.BlockSpec((nb, LANES), lambda i: (rb, 0)),
            pl.BlockSpec((nb, HEADS * DV), lambda i: (0, 0)),
            pl.BlockSpec((None, nb, 2 * D_CONV), lambda i: (j, 0, 0)),
            pl.BlockSpec((None, nb, HEADS * DK), lambda i: (j, 0, 0)),
            pl.BlockSpec((None, nb, HEADS), lambda i: (j, 0, 0)),
            pl.BlockSpec((None, CONV_W, D_CONV), lambda i: (j, 0, 0)),
            pl.BlockSpec((None, 2, HEADS), lambda i: (j, 0, 0)),
            pl.BlockSpec((None, 1, HEADS * DV), lambda i: (j, 0, 0)),
            pl.BlockSpec(memory_space=pl.ANY),
        ],
        out_specs=[
            pl.BlockSpec((nb, D_MODEL), lambda i: (rb, 0)),
            pl.BlockSpec((nb, 2 * D_CONV), lambda i: (0, 0)),
            pl.BlockSpec((nb, HEADS * DK), lambda i: (0, 0)),
            pl.BlockSpec((nb, HEADS), lambda i: (0, 0)),
        ],
        out_shape=[
            jax.ShapeDtypeStruct((M, D_MODEL), F32),
            jax.ShapeDtypeStruct((nb, 2 * D_CONV), F32),
            jax.ShapeDtypeStruct((nb, HEADS * DK), F32),
            jax.ShapeDtypeStruct((nb, HEADS), F32),
        ],
        input_output_aliases={15: 0},
        compiler_params=pltpu.CompilerParams(
            dimension_semantics=("arbitrary",), vmem_limit_bytes=_vmem_limit(est)),
        name="even_sample",
    )(proj, proj, proj, proj, proj, proj, proj, gates, num, conv_st, n_st, m_st,
      w_conv, b_gates, mh_g, mixin)


def _gmlp_norm(u_raw, v_raw, g, b):
    u = jax.nn.gelu(u_raw)
    vn = _layer_norm(jax.nn.gelu(v_raw), g, b)
    return u, vn


def _odd_prompt_kernel(p_ref, u_ref, v_ref, wp_ref, sc_ref, gmg_ref, gmb_ref, ws_ref, bst_ref,
                       mix_ref, pool_ref, gv_ref, pbuf, sbuf):
    s = pl.program_id(1)
    R = RO

    @pl.when(s == 0)
    def _():
        pbuf[0:HIST, :] = jnp.zeros((HIST, D_POOL), F32)
        sbuf[0:SUBLANES, :] = jnp.zeros((SUBLANES, D_POOL), F32)

    p = p_ref[...]
    pbuf[HIST:HIST + R, :] = p
    pos = s * R + lax.broadcasted_iota(jnp.int32, (R, 1), 0)
    for g, w in enumerate(POOL_WINDOWS):
        lo, hi = g * POOL_GW, (g + 1) * POOL_GW
        cur = pbuf[0:HIST + R, lo:hi]
        d = 1
        while d < w:
            sbuf[SUBLANES:SUBLANES + HIST + R, lo:hi] = cur
            cur = cur + sbuf[SUBLANES - d:SUBLANES - d + HIST + R, lo:hi]
            d *= 2
        win = cur[HIST:, :]
        cnt = jnp.minimum(w, pos + 1).astype(F32)
        diff = win / cnt - p[:, lo:hi]
        mix_ref[:, lo:hi] = _dot(diff, wp_ref[g]) * sc_ref[0:1, lo:hi]
    pool_ref[...] = pbuf[HIST + R - POOL_BUF:HIST + R, :]
    pbuf[0:HIST, :] = pbuf[R:R + HIST, :]

    u, vn = _gmlp_norm(u_ref[...], v_ref[...], gmg_ref[...], gmb_ref[...])
    L = GMLP_CHUNK
    tril = lax.broadcasted_iota(jnp.int32, (L, L), 0) >= lax.broadcasted_iota(jnp.int32, (L, L), 1)
    for g in range(D_GMLP // GMLP_GW):
        lo, hi = g * GMLP_GW, (g + 1) * GMLP_GW
        ws = jnp.where(tril, ws_ref[g], 0.0)
        bcol = bst_ref[:, g:g + 1]
        for ck in range(R // L):
            r0, r1 = ck * L, (ck + 1) * L
            sv = _dot(ws, vn[r0:r1, lo:hi]) + bcol
            mix_ref[r0:r1, D_POOL + lo:D_POOL + hi] = u[r0:r1, lo:hi] * sv
    gv_ref[...] = vn[R - L:R, :]


def _odd_prompt(proj, w_pool, pool_scale, gm_g, gm_b, w_spatial, bs_t, j):
    nrb = SEQ // RO
    r = lambda b, s: b * nrb + s
    est = 4 * (2 * 3 * RO * 1024 + 2 * RO * 2048 + 2 * 4 * 256 * 256 + 2 * 4 * 128 * 128
               + (RO + HIST) * 1024 + 8 * RO * 1024) + (8 << 20)
    return pl.pallas_call(
        _odd_prompt_kernel,
        grid=(BATCH, nrb),
        in_specs=[
            pl.BlockSpec((RO, D_POOL), lambda b, s: (r(b, s), 0)),
            pl.BlockSpec((RO, D_GMLP), lambda b, s: (r(b, s), 1)),
            pl.BlockSpec((RO, D_GMLP), lambda b, s: (r(b, s), 2)),
            pl.BlockSpec((None, 4, POOL_GW, POOL_GW), lambda b, s: (j, 0, 0, 0)),
            pl.BlockSpec((None, 1, D_POOL), lambda b, s: (j, 0, 0)),
            pl.BlockSpec((None, 1, D_GMLP), lambda b, s: (j, 0, 0)),
            pl.BlockSpec((None, 1, D_GMLP), lambda b, s: (j, 0, 0)),
            pl.BlockSpec((None, 4, GMLP_CHUNK, GMLP_CHUNK), lambda b, s: (j, 0, 0, 0)),
            pl.BlockSpec((None, GMLP_CHUNK, 4), lambda b, s: (j, 0, 0)),
        ],
        out_specs=[
            pl.BlockSpec((RO, D_MODEL), lambda b, s: (r(b, s), 0)),
            pl.BlockSpec((None, POOL_BUF, D_POOL), lambda b, s: (b, 0, 0)),
            pl.BlockSpec((None, GMLP_CHUNK, D_GMLP), lambda b, s: (b, 0, 0)),
        ],
        out_shape=[
            jax.ShapeDtypeStruct((M, D_MODEL), F32),
            jax.ShapeDtypeStruct((BATCH, POOL_BUF, D_POOL), F32),
            jax.ShapeDtypeStruct((BATCH, GMLP_CHUNK, D_GMLP), F32),
        ],
        scratch_shapes=[pltpu.VMEM((RO + HIST, D_POOL), F32),
                        pltpu.VMEM((SUBLANES + RO + HIST, D_POOL), F32)],
        compiler_params=pltpu.CompilerParams(
            dimension_semantics=("arbitrary", "arbitrary"), vmem_limit_bytes=_vmem_limit(est)),
        name="odd_prompt",
    )(proj, proj, proj, w_pool, pool_scale, gm_g, gm_b, w_spatial, bs_t)


def _odd_sample_kernel(p_ref, u_ref, v_ref, st_ref, wp_ref, sc_ref, gmg_ref, gmb_ref, ws_ref, bst_ref,
                       *rest):
    mix_ref, pool_ref, gv_ref = rest[-3:]
    p = p_ref[...]
    for r in range(POOL_BUF - 1):
        pool_ref[r] = st_ref[r + 1]
    pool_ref[POOL_BUF - 1] = p
    for g, w in enumerate(POOL_WINDOWS):
        lo, hi = g * POOL_GW, (g + 1) * POOL_GW
        win = p[:, lo:hi]
        for jj in range(1, w):
            win = win + st_ref[POOL_BUF - jj, :, lo:hi]
        cnt = float(min(w, PAST_LEN + 1))
        diff = win / cnt - p[:, lo:hi]
        mix_ref[:, lo:hi] = _dot(diff, wp_ref[g]) * sc_ref[0:1, lo:hi]
    u, vn = _gmlp_norm(u_ref[...], v_ref[...], gmg_ref[...], gmb_ref[...])
    gv_ref[...] = vn
    for g in range(D_GMLP // GMLP_GW):
        lo, hi = g * GMLP_GW, (g + 1) * GMLP_GW
        sv = ws_ref[g, 0:1, 0:1] * vn[:, lo:hi] + bst_ref[0:1, g:g + 1]
        mix_ref[:, D_POOL + lo:D_POOL + hi] = u[:, lo:hi] * sv


def _odd_sample(proj, pool_st, w_pool, pool_scale, gm_g, gm_b, w_spatial, bs_t, mixin, pool_prev, j):
    nb = DEC_BATCH
    rb = MP // nb
    st_blk = (None, POOL_BUF, nb, D_POOL)
    in_specs = [
        pl.BlockSpec((nb, D_POOL), lambda i: (rb, 0)),
        pl.BlockSpec((nb, D_GMLP), lambda i: (rb, 1)),
        pl.BlockSpec((nb, D_GMLP), lambda i: (rb, 2)),
        pl.BlockSpec(st_blk, lambda i: (j, 0, 0, 0)),
        pl.BlockSpec((None, 4, POOL_GW, POOL_GW), lambda i: (j, 0, 0, 0)),
        pl.BlockSpec((None, 1, D_POOL), lambda i: (j, 0, 0)),
        pl.BlockSpec((None, 1, D_GMLP), lambda i: (j, 0, 0)),
        pl.BlockSpec((None, 1, D_GMLP), lambda i: (j, 0, 0)),
        pl.BlockSpec((None, 4, GMLP_CHUNK, GMLP_CHUNK), lambda i: (j, 0, 0, 0)),
        pl.BlockSpec((None, GMLP_CHUNK, 4), lambda i: (j, 0, 0)),
        pl.BlockSpec(memory_space=pl.ANY),
    ]
    args = [proj, proj, proj, pool_st, w_pool, pool_scale, gm_g, gm_b, w_spatial, bs_t, mixin]
    aliases = {len(args) - 1: 0}
    if pool_prev is not None:
        in_specs.append(pl.BlockSpec(memory_space=pl.ANY))
        args.append(pool_prev)
        aliases[len(args) - 1] = 1
    est = 4 * (4 * POOL_BUF * nb * D_POOL + 2 * (3 * nb * 1024 + 4 * 256 * 256 + 4 * 128 * 128
                                                + nb * 2048 + nb * 1024)) + (8 << 20)
    return pl.pallas_call(
        _odd_sample_kernel,
        grid=(1,),
        in_specs=in_specs,
        out_specs=[
            pl.BlockSpec((nb, D_MODEL), lambda i: (rb, 0)),
            pl.BlockSpec(st_blk, lambda i: (j, 0, 0, 0)),
            pl.BlockSpec((nb, D_GMLP), lambda i: (0, 0)),
        ],
        out_shape=[
            jax.ShapeDtypeStruct((M, D_MODEL), F32),
            jax.ShapeDtypeStruct(pool_st.shape, F32),
            jax.ShapeDtypeStruct((nb, D_GMLP), F32),
        ],
        input_output_aliases=aliases,
        compiler_params=pltpu.CompilerParams(
            dimension_semantics=("arbitrary",), vmem_limit_bytes=_vmem_limit(est)),
        name="odd_sample",
    )(*args)


def kernel(x_prompt, x_sample, state_conv, state_mlstm_C, state_mlstm_n, state_mlstm_m, state_pool,
           ln_g, ln_b, w_ffn_in, w_ffn_out, w_in_even, b_gates_even, w_conv, mh_norm_g, w_out_even,
           w_in_odd, w_pool, pool_scale, gm_ln_g, gm_ln_b, w_spatial, b_spatial, w_out_odd):
    n_even, n_odd = w_in_even.shape[0], w_in_odd.shape[0]
    x = jnp.concatenate([x_prompt.reshape(MP, D_MODEL), x_sample.reshape(DEC_BATCH, D_MODEL)], axis=0)

    ln_g3 = ln_g.reshape(DEPTH * 3, 1, D_MODEL)
    ln_b3 = ln_b.reshape(DEPTH * 3, 1, D_MODEL)
    w_even_t = jnp.swapaxes(w_in_even, 1, 2)
    w_gate_t = jnp.pad(w_even_t[:, EVEN_MAIN:, :], ((0, 0), (0, LANES - 2 * HEADS), (0, 0)))
    pool_st = jnp.swapaxes(state_pool, 1, 2)
    mh_g3 = mh_norm_g.reshape(n_even, 1, HEADS * DV)
    conv_st = state_conv.reshape(n_even, DEC_BATCH, (CONV_W - 1) * D_CONV)
    n_st = state_mlstm_n.reshape(n_even, DEC_BATCH, HEADS * DK)
    scale3 = pool_scale.reshape(n_odd, 1, D_POOL)
    gm_g3 = gm_ln_g.reshape(n_odd, 1, D_GMLP)
    gm_b3 = gm_ln_b.reshape(n_odd, 1, D_GMLP)
    bs_t = jnp.swapaxes(b_spatial, 1, 2)

    conv_p, conv_s, c_p, n_p, n_s, m_p, m_s = [], [], [], [], [], [], []
    pool_p, gv_p, gv_s = [], [], []
    c_s = None
    pool_s = None

    for layer in range(DEPTH):
        j = layer // 2
        x = _ffn(x, w_ffn_in, w_ffn_out, ln_g3, ln_b3, layer, 0, 3 * layer)
        if layer % 2 == 0:
            proj, gates = _proj(x, w_even_t, j, 0, EVEN_MAIN, "proj_even", w_transposed=True,
                                w_gate_t=w_gate_t)
            proj = _proj(x, w_even_t, j, 1, EVEN_MAIN, "proj_even", w_transposed=True, prev=proj)
            mixin, cv, cc, nn, mm = _even_prompt(proj, gates, w_conv, b_gates_even, mh_g3, j)
            conv_p.append(cv)
            c_p.append(cc)
            n_p.append(nn)
            m_p.append(mm.reshape(BATCH, HEADS))
            qk_cols = _qk_cols(x, w_even_t, j)
            c_s, num = _even_sample_state(state_mlstm_C, qk_cols, proj, gates, state_mlstm_m,
                                          b_gates_even, j, c_s)
            mixin, cvs, nns, mms = _even_sample(proj, gates, num, conv_st, n_st, state_mlstm_m,
                                                w_conv, b_gates_even, mh_g3, mixin, j)
            conv_s.append(cvs.reshape(DEC_BATCH, CONV_W - 1, D_CONV))
            n_s.append(nns.reshape(DEC_BATCH, HEADS, DK))
            m_s.append(mms)
            x = _outproj(x, mixin, w_out_even, ln_g3, ln_b3, j, 3 * layer + 1)
        else:
            proj = _proj(x, w_in_odd, j, 0, ODD_IN, "proj_odd")
            mixin, pp, gv = _odd_prompt(proj, w_pool, scale3, gm_g3, gm_b3, w_spatial, bs_t, j)
            pool_p.append(pp)
            gv_p.append(gv)
            mixin, pool_s, gvs = _odd_sample(proj, pool_st, w_pool, scale3, gm_g3, gm_b3, w_spatial, bs_t,
                                             mixin, pool_s, j)
            gv_s.append(gvs.reshape(DEC_BATCH, 1, D_GMLP))
            x = _outproj(x, mixin, w_out_odd, ln_g3, ln_b3, j, 3 * layer + 1)
        x = _ffn(x, w_ffn_in, w_ffn_out, ln_g3, ln_b3, layer, 1, 3 * layer + 2,
                 split_out=layer == DEPTH - 1)

    y_prompt = x[0].reshape(BATCH, SEQ, D_MODEL)
    y_sample = x[1].reshape(DEC_BATCH, 1, D_MODEL)
    return (y_prompt, y_sample,
            jnp.stack(conv_p), jnp.stack(conv_s),
            jnp.stack(c_p), c_s,
            jnp.stack(n_p), jnp.stack(n_s),
            jnp.stack(m_p), jnp.stack(m_s),
            jnp.stack(pool_p), jnp.swapaxes(pool_s, 1, 2),
            jnp.stack(gv_p), jnp.stack(gv_s))
```

```python
import functools

import jax
import jax.numpy as jnp
from jax import lax
from jax.experimental import pallas as pl
from jax.experimental.pallas import tpu as pltpu

F32 = jnp.float32

D_MODEL = 2048
BATCH = 4
SEQ = 2048
DEPTH = 4
DEC_BATCH = 128
PAST_LEN = 16384
D_FF = 5632
D_CONV = 1024
CONV_W = 3
HEADS = 4
DK = 128
DV = 256
D_POOL = 1024
POOL_WINDOWS = (2, 4, 8, 16)
POOL_GW = 256
POOL_BUF = 15
D_GMLP = 1024
GMLP_GW = 256
GMLP_CHUNK = 128
ALPHA = (2 * DEPTH) ** 0.25
LN_EPS = 1e-5
EVEN_MAIN = 3 * D_CONV + 2 * HEADS * DK + 2 * HEADS * DV
ODD_IN = D_POOL + 2 * D_GMLP

MP = BATCH * SEQ
M = MP + DEC_BATCH

LANES = 128
SUBLANES = 8
VMEM_BYTES_V7X = 64 * 1024 * 1024

TM = 1040
TF = 512
TFS = 256
TN = 512
TMP = 520
PROJ_COLS = 3072
TMO = 416
LC = 256
RE = 256
RO = 256
SB = 16
HIST = 16


def _vmem_limit(nbytes):
    return int(min(VMEM_BYTES_V7X - 4 * 1024 * 1024, nbytes))


def _layer_norm(y, g, b, eps=LN_EPS):
    mu = jnp.mean(y, axis=-1, keepdims=True)
    yc = y - mu
    var = jnp.mean(yc * yc, axis=-1, keepdims=True)
    out = yc * lax.rsqrt(var + eps) * g
    if b is not None:
        out = out + b
    return out


def _log_sigmoid(x):
    return -(jnp.maximum(-x, 0.0) + jnp.log1p(jnp.exp(-jnp.abs(x))))


def _dot(a, b):
    return jnp.dot(a, b, preferred_element_type=F32)


def _dot_nt(a, b):
    return lax.dot_general(a, b, (((1,), (1,)), ((), ())), preferred_element_type=F32)


def _ffn_kernel(x_ref, wg_ref, wu_ref, wo_ref, g_ref, b_ref, o_ref, *maybe_sample_ref, nf):
    f = pl.program_id(1)

    @pl.when(f == 0)
    def _():
        o_ref[...] = (2.0 * ALPHA) * x_ref[...]

    x = x_ref[...]
    for c in range(TF // TFS):
        c0, c1 = c * TFS, (c + 1) * TFS
        gate = _dot(x, wg_ref[:, c0:c1])
        up = _dot(x, wu_ref[:, c0:c1])
        h = gate * jax.nn.sigmoid(gate) * up
        for n in range(D_MODEL // TN):
            o_ref[:, n * TN:(n + 1) * TN] += _dot(h, wo_ref[c0:c1, n * TN:(n + 1) * TN])

    @pl.when(f == nf - 1)
    def _():
        o_ref[...] = _layer_norm(o_ref[...], g_ref[...], b_ref[...], eps=4.0 * LN_EPS)

    if maybe_sample_ref:
        @pl.when((f == nf - 1) & (pl.program_id(0) == M // TM - 1))
        def _():
            maybe_sample_ref[0][...] = o_ref[TM - DEC_BATCH:, :]


def _ffn(x, w_ffn_in, w_ffn_out, ln_g, ln_b, layer, which, ln_idx, split_out=False):
    nf = D_FF // TF
    est = 4 * (3 * TM * D_MODEL + 2 * 3 * D_MODEL * TF + 4 * TM * TFS + TM * TN) + (4 << 20)
    out_specs = pl.BlockSpec((TM, D_MODEL), lambda i, f: (i, 0))
    out_shape = jax.ShapeDtypeStruct((M, D_MODEL), F32)
    if split_out:
        out_specs = [out_specs, pl.BlockSpec((DEC_BATCH, D_MODEL), lambda i, f: (0, 0))]
        out_shape = [jax.ShapeDtypeStruct((MP, D_MODEL), F32),
                     jax.ShapeDtypeStruct((DEC_BATCH, D_MODEL), F32)]
    return pl.pallas_call(
        functools.partial(_ffn_kernel, nf=nf),
        grid=(M // TM, nf),
        in_specs=[
            pl.BlockSpec((TM, D_MODEL), lambda i, f: (i, 0), pipeline_mode=pl.Buffered(1)),
            pl.BlockSpec((None, None, D_MODEL, TF), lambda i, f: (layer, which, 0, f)),
            pl.BlockSpec((None, None, D_MODEL, TF), lambda i, f: (layer, which, 0, nf + f)),
            pl.BlockSpec((None, None, TF, D_MODEL), lambda i, f: (layer, which, f, 0)),
            pl.BlockSpec((None, 1, D_MODEL), lambda i, f: (ln_idx, 0, 0)),
            pl.BlockSpec((None, 1, D_MODEL), lambda i, f: (ln_idx, 0, 0)),
        ],
        out_specs=out_specs,
        out_shape=out_shape,
        compiler_params=pltpu.CompilerParams(
            dimension_semantics=("arbitrary", "arbitrary"), vmem_limit_bytes=_vmem_limit(est)),
        name="ffn_ln",
    )(x, w_ffn_in, w_ffn_in, w_ffn_out, ln_g, ln_b)


def _proj_kernel(x_ref, w_ref, *rest, w_transposed, with_gates, aliased):
    rest = rest[1:] if aliased else rest
    if with_gates:
        wgate_ref, o_ref, gate_ref = rest
        gate_ref[...] = _dot_nt(x_ref[...], wgate_ref[...])
    else:
        (o_ref,) = rest
    for n in range(PROJ_COLS // TN):
        lo, hi = n * TN, (n + 1) * TN
        if w_transposed:
            o_ref[:, lo:hi] = _dot_nt(x_ref[...], w_ref[lo:hi, :])
        else:
            o_ref[:, lo:hi] = _dot(x_ref[...], w_ref[:, lo:hi])


def _proj(x, w, layer_idx, col_block, n_total, name, w_transposed=False, w_gate_t=None, prev=None):
    est = 4 * (PROJ_COLS * D_MODEL + 2 * TMP * D_MODEL + 2 * TMP * PROJ_COLS + 2 * TMP * TN) + (4 << 20)
    if w_transposed:
        w_spec = pl.BlockSpec((None, PROJ_COLS, D_MODEL), lambda i: (layer_idx, col_block, 0),
                              pipeline_mode=pl.Buffered(1))
    else:
        w_spec = pl.BlockSpec((None, D_MODEL, PROJ_COLS), lambda i: (layer_idx, 0, col_block),
                              pipeline_mode=pl.Buffered(1))
    in_specs = [pl.BlockSpec((TMP, D_MODEL), lambda i: (i, 0)), w_spec]
    args = [x, w]
    aliases = {}
    if prev is not None:
        in_specs.append(pl.BlockSpec(memory_space=pl.ANY))
        args.append(prev)
        aliases = {2: 0}
    out_specs = pl.BlockSpec((TMP, PROJ_COLS), lambda i: (i, col_block))
    out_shape = jax.ShapeDtypeStruct((M, n_total), F32)
    if w_gate_t is not None:
        in_specs.append(pl.BlockSpec((None, LANES, D_MODEL), lambda i: (layer_idx, 0, 0)))
        args.append(w_gate_t)
        out_specs = [out_specs, pl.BlockSpec((TMP, LANES), lambda i: (i, 0))]
        out_shape = [out_shape, jax.ShapeDtypeStruct((M, LANES), F32)]
        est += 4 * (2 * LANES * D_MODEL + 3 * TMP * LANES)
    return pl.pallas_call(
        functools.partial(_proj_kernel, w_transposed=w_transposed, with_gates=w_gate_t is not None,
                          aliased=prev is not None),
        grid=(M // TMP,),
        in_specs=in_specs,
        out_specs=out_specs,
        out_shape=out_shape,
        input_output_aliases=aliases,
        compiler_params=pltpu.CompilerParams(
            dimension_semantics=("parallel",), vmem_limit_bytes=_vmem_limit(est)),
        name=name,
    )(*args)


def _outproj_kernel(x_ref, a_ref, w_ref, g_ref, b_ref, o_ref):
    y = ALPHA * x_ref[...] + _dot(a_ref[...], w_ref[...])
    o_ref[...] = _layer_norm(y, g_ref[...], b_ref[...])


def _outproj(x, mix, w_out, ln_g, ln_b, layer_idx, ln_idx):
    est = 4 * (D_MODEL * D_MODEL + 9 * TMO * D_MODEL) + (4 << 20)
    return pl.pallas_call(
        _outproj_kernel,
        grid=(M // TMO,),
        in_specs=[
            pl.BlockSpec((TMO, D_MODEL), lambda i: (i, 0)),
            pl.BlockSpec((TMO, D_MODEL), lambda i: (i, 0)),
            pl.BlockSpec((None, D_MODEL, D_MODEL), lambda i: (layer_idx, 0, 0),
                         pipeline_mode=pl.Buffered(1)),
            pl.BlockSpec((None, 1, D_MODEL), lambda i: (ln_idx, 0, 0)),
            pl.BlockSpec((None, 1, D_MODEL), lambda i: (ln_idx, 0, 0)),
        ],
        out_specs=pl.BlockSpec((TMO, D_MODEL), lambda i: (i, 0)),
        out_shape=jax.ShapeDtypeStruct((M, D_MODEL), F32),
        compiler_params=pltpu.CompilerParams(
            dimension_semantics=("parallel",), vmem_limit_bytes=_vmem_limit(est)),
        name="outproj_ln",
    )(x, mix, w_out, ln_g, ln_b)


def _split3(x):
    h1 = x.astype(jnp.bfloat16).astype(F32)
    r = x - h1
    h2 = r.astype(jnp.bfloat16).astype(F32)
    return h1, h2, r - h2


def _even_prompt_kernel(bg_ref, cg_ref, xin_ref, q_ref, k_ref, v_ref, og_ref, gt_ref,
                        wc_ref, bgate_ref, mhg_ref,
                        mix_ref, conv_ref, c_out_ref, n_out_ref, m_out_ref,
                        cbuf, c_sc, n_sc, m_sc):
    c = pl.program_id(1)
    L = LC
    R = RE

    @pl.when(c == 0)
    def _():
        cbuf[0:SUBLANES, :] = jnp.zeros((SUBLANES, D_CONV), F32)
        c_sc[...] = jnp.zeros_like(c_sc)
        n_sc[...] = jnp.zeros_like(n_sc)
        m_sc[...] = jnp.zeros_like(m_sc)

    cx = cg_ref[...] * xin_ref[...]
    cbuf[SUBLANES:SUBLANES + R, :] = cx
    c1 = cbuf[SUBLANES - 1:SUBLANES - 1 + R, :]
    c2 = cbuf[SUBLANES - 2:SUBLANES - 2 + R, :]
    conv = c2 * wc_ref[0:1, :] + c1 * wc_ref[1:2, :] + cx * wc_ref[2:3, :]
    mix_ref[:, 0:D_CONV] = bg_ref[...] * conv
    conv_ref[...] = cbuf[SUBLANES + R - 2:SUBLANES + R, :]
    cbuf[0:SUBLANES, :] = cbuf[R:R + SUBLANES, :]

    row = lax.broadcasted_iota(jnp.int32, (L, L), 0)
    col = lax.broadcasted_iota(jnp.int32, (L, L), 1)
    causal = row >= col
    tril = causal.astype(F32)
    c_state = [c_sc[h] for h in range(HEADS)]
    n_state = [n_sc[h:h + 1, :] for h in range(HEADS)]
    m_state = [m_sc[h, 0:1, 0:1] for h in range(HEADS)]

    for s in range(R // L):
        r0, r1 = s * L, (s + 1) * L
        gt = gt_ref[r0:r1, :]
        li_all = gt[:, 0:HEADS] + bgate_ref[0:1, :]
        lf_all = _log_sigmoid(gt[:, HEADS:2 * HEADS] + bgate_ref[1:2, :])
        f1, f2, f3 = _split3(lf_all)
        b_all = _dot(tril, f1) + _dot(tril, f2) + _dot(tril, f3)
        z = jnp.concatenate([li_all, b_all, jnp.zeros((L, LANES - 2 * HEADS), F32)], axis=1)
        zt = z.T

        for h in range(HEADS):
            li_c = li_all[:, h:h + 1]
            b_c = b_all[:, h:h + 1]
            li_r = zt[h:h + 1, :]
            b_r = zt[HEADS + h:HEADS + h + 1, :]
            m0 = m_state[h]
            dm = jnp.where(causal, b_c - b_r + li_r, -jnp.inf)
            inter = b_c + m0
            m_t = jnp.maximum(inter, jnp.max(dm, axis=-1, keepdims=True))
            a_int = jnp.exp(inter - m_t)
            qh = q_ref[r0:r1, h * DK:(h + 1) * DK]
            kh = k_ref[r0:r1, h * DK:(h + 1) * DK] * (DK ** -0.5)
            vh = v_ref[r0:r1, h * DV:(h + 1) * DV]
            w = jnp.exp(dm - m_t) * _dot_nt(qh, kh)
            ch = c_state[h]
            n_row = n_state[h]
            num = a_int * _dot(qh, ch) + _dot(w, vh)
            den = (a_int * jnp.sum(qh * n_row, axis=-1, keepdims=True)
                   + jnp.sum(w, axis=-1, keepdims=True))
            hh = num / jnp.maximum(jnp.abs(den), jnp.exp(-m_t))
            m_new = m_t[L - 1:L, :]
            b_last = b_c[L - 1:L, :]
            w_end = jnp.exp(b_last - b_c + li_c - m_new)
            decay = jnp.exp(b_last + m0 - m_new)
            wk = w_end * kh
            c_state[h] = decay * ch + _dot(wk.T, vh)
            n_state[h] = decay * n_row + jnp.sum(wk, axis=0, keepdims=True)
            m_state[h] = m_new
            hn = _layer_norm(hh, mhg_ref[0:1, h * DV:(h + 1) * DV], None)
            og = og_ref[r0:r1, h * DV:(h + 1) * DV]
            mix_ref[r0:r1, D_CONV + h * DV:D_CONV + (h + 1) * DV] = hn * jax.nn.sigmoid(og)

    for h in range(HEADS):
        c_sc[h] = c_state[h]
        n_sc[h:h + 1, :] = n_state[h]
        m_sc[h] = jnp.broadcast_to(m_state[h], (SUBLANES, LANES))
        c_out_ref[h] = c_state[h]
        n_out_ref[h:h + 1, :] = n_state[h]
        m_out_ref[0:1, h:h + 1] = m_state[h]


def _even_prompt(proj, gates, w_conv, b_gates, mh_g, j):
    nck = SEQ // RE
    r = lambda b, c: b * nck + c
    est = 4 * (2 * (5 * RE * 1024 + 2 * RE * 512 + RE * 128) + 2 * RE * 2048 + 4 * RE * 1024
               + 3 * HEADS * DK * DV + 24 * LC * LC) + (8 << 20)
    return pl.pallas_call(
        _even_prompt_kernel,
        grid=(BATCH, nck),
        in_specs=[
            pl.BlockSpec((RE, D_CONV), lambda b, c: (r(b, c), 0)),
            pl.BlockSpec((RE, D_CONV), lambda b, c: (r(b, c), 1)),
            pl.BlockSpec((RE, D_CONV), lambda b, c: (r(b, c), 2)),
            pl.BlockSpec((RE, HEADS * DK), lambda b, c: (r(b, c), 6)),
            pl.BlockSpec((RE, HEADS * DK), lambda b, c: (r(b, c), 7)),
            pl.BlockSpec((RE, HEADS * DV), lambda b, c: (r(b, c), 4)),
            pl.BlockSpec((RE, HEADS * DV), lambda b, c: (r(b, c), 5)),
            pl.BlockSpec((RE, LANES), lambda b, c: (r(b, c), 0)),
            pl.BlockSpec((None, CONV_W, D_CONV), lambda b, c: (j, 0, 0)),
            pl.BlockSpec((None, 2, HEADS), lambda b, c: (j, 0, 0)),
            pl.BlockSpec((None, 1, HEADS * DV), lambda b, c: (j, 0, 0)),
        ],
        out_specs=[
            pl.BlockSpec((RE, D_MODEL), lambda b, c: (r(b, c), 0)),
            pl.BlockSpec((None, CONV_W - 1, D_CONV), lambda b, c: (b, 0, 0)),
            pl.BlockSpec((None, HEADS, DK, DV), lambda b, c: (b, 0, 0, 0)),
            pl.BlockSpec((None, HEADS, DK), lambda b, c: (b, 0, 0)),
            pl.BlockSpec((None, 1, HEADS), lambda b, c: (b, 0, 0)),
        ],
        out_shape=[
            jax.ShapeDtypeStruct((M, D_MODEL), F32),
            jax.ShapeDtypeStruct((BATCH, CONV_W - 1, D_CONV), F32),
            jax.ShapeDtypeStruct((BATCH, HEADS, DK, DV), F32),
            jax.ShapeDtypeStruct((BATCH, HEADS, DK), F32),
            jax.ShapeDtypeStruct((BATCH, 1, HEADS), F32),
        ],
        scratch_shapes=[
            pltpu.VMEM((RE + SUBLANES, D_CONV), F32),
            pltpu.VMEM((HEADS, DK, DV), F32),
            pltpu.VMEM((HEADS, DK), F32),
            pltpu.VMEM((HEADS, SUBLANES, LANES), F32),
        ],
        compiler_params=pltpu.CompilerParams(
            dimension_semantics=("arbitrary", "arbitrary"), vmem_limit_bytes=_vmem_limit(est)),
        name="even_prompt",
    )(proj, proj, proj, proj, proj, proj, proj, gates, w_conv, b_gates, mh_g)


def _sample_gates(gt, bgate_ref, m):
    li = gt[:, 0:HEADS] + bgate_ref[0:1, :]
    lf = _log_sigmoid(gt[:, HEADS:2 * HEADS] + bgate_ref[1:2, :])
    inter = lf + m
    m_t = jnp.maximum(inter, li)
    return jnp.exp(inter - m_t), jnp.exp(li - m_t), m_t


def _qk_cols_kernel(x_ref, w_ref, o_ref):
    t = _dot_nt(w_ref[...], x_ref[...])
    for s in range(DEC_BATCH // SB):
        o_ref[s] = t[:, s * SB:(s + 1) * SB]


def _qk_cols(x, w_even_t, j):
    nqk = 2 * HEADS * DK
    est = 4 * (2 * DEC_BATCH * D_MODEL + 2 * nqk * D_MODEL + 3 * nqk * LANES
               + 2 * (DEC_BATCH // SB) * nqk * LANES) + (4 << 20)
    return pl.pallas_call(
        _qk_cols_kernel,
        grid=(1,),
        in_specs=[
            pl.BlockSpec((DEC_BATCH, D_MODEL), lambda i: (MP // DEC_BATCH, 0)),
            pl.BlockSpec((None, nqk, D_MODEL), lambda i: (j, 3 * D_CONV // nqk, 0)),
        ],
        out_specs=pl.BlockSpec((DEC_BATCH // SB, nqk, SB), lambda i: (0, 0, 0)),
        out_shape=jax.ShapeDtypeStruct((DEC_BATCH // SB, nqk, SB), F32),
        compiler_params=pltpu.CompilerParams(
            dimension_semantics=("arbitrary",), vmem_limit_bytes=_vmem_limit(est)),
        name="qk_cols",
    )(x, w_even_t)


def _even_sample_state_kernel(c_ref, qk_ref, v_ref, gt_ref, m_ref, bgate_ref, *rest):
    c_out_ref, num_ref = rest[-2], rest[-1]
    a, wgt, _ = _sample_gates(gt_ref[...], bgate_ref, m_ref[...])
    for bi in range(SB):
        for h in range(HEADS):
            a_s = a[bi:bi + 1, h:h + 1]
            w_s = wgt[bi:bi + 1, h:h + 1]
            ch = c_ref[bi, h]
            qc = qk_ref[h * DK:(h + 1) * DK, bi:bi + 1]
            kc = qk_ref[(HEADS + h) * DK:(HEADS + h + 1) * DK, bi:bi + 1] * (DK ** -0.5)
            vr = v_ref[bi:bi + 1, h * DV:(h + 1) * DV]
            c_out_ref[bi, h] = a_s * ch + (w_s * kc) * vr
            num_ref[bi:bi + 1, h * DV:(h + 1) * DV] = jnp.sum(qc * ch, axis=0, keepdims=True)


def _even_sample_state(state_c, qk_cols, proj, gates, state_m, b_gates, j, c_prev):
    blk5 = (None, SB, HEADS, DK, DV)
    rb = MP // SB
    in_specs = [
        pl.BlockSpec(blk5, lambda i: (j, i, 0, 0, 0)),
        pl.BlockSpec((None, 2 * HEADS * DK, SB), lambda i: (i, 0, 0)),
        pl.BlockSpec((SB, HEADS * DV), lambda i: (rb + i, 4)),
        pl.BlockSpec((SB, LANES), lambda i: (rb + i, 0)),
        pl.BlockSpec((None, SB, HEADS), lambda i: (j, i, 0)),
        pl.BlockSpec((None, 2, HEADS), lambda i: (j, 0, 0)),
    ]
    args = [state_c, qk_cols, proj, gates, state_m, b_gates]
    aliases = {}
    if c_prev is not None:
        in_specs.append(pl.BlockSpec(memory_space=pl.ANY))
        args.append(c_prev)
        aliases = {len(args) - 1: 0}
    est = 4 * (4 * SB * HEADS * DK * DV + 4 * HEADS * DK * LANES) + (8 << 20)
    return pl.pallas_call(
        _even_sample_state_kernel,
        grid=(DEC_BATCH // SB,),
        in_specs=in_specs,
        out_specs=[
            pl.BlockSpec(blk5, lambda i: (j, i, 0, 0, 0)),
            pl.BlockSpec((SB, HEADS * DV), lambda i: (i, 0)),
        ],
        out_shape=[
            jax.ShapeDtypeStruct(state_c.shape, F32),
            jax.ShapeDtypeStruct((DEC_BATCH, HEADS * DV), F32),
        ],
        input_output_aliases=aliases,
        compiler_params=pltpu.CompilerParams(
            dimension_semantics=("arbitrary",), vmem_limit_bytes=_vmem_limit(est)),
        name="even_sample_state",
    )(*args)


def _even_sample_kernel(bg_ref, cg_ref, xin_ref, q_ref, k_ref, v_ref, og_ref, gt_ref,
                        num_ref, cst_ref, n_ref, m_ref, wc_ref, bgate_ref, mhg_ref, mixin_hbm,
                        mix_ref, conv_ref, n_out_ref, m_out_ref):
    del mixin_hbm
    cx = cg_ref[...] * xin_ref[...]
    st0 = cst_ref[:, 0:D_CONV]
    st1 = cst_ref[:, D_CONV:2 * D_CONV]
    conv = st0 * wc_ref[0:1, :] + st1 * wc_ref[1:2, :] + cx * wc_ref[2:3, :]
    mix_ref[:, 0:D_CONV] = bg_ref[...] * conv
    conv_ref[:, 0:D_CONV] = st1
    conv_ref[:, D_CONV:2 * D_CONV] = cx

    a, wgt, m_t = _sample_gates(gt_ref[...], bgate_ref, m_ref[...])
    m_out_ref[...] = m_t
    floor = jnp.exp(-m_t)
    for h in range(HEADS):
        a_h = a[:, h:h + 1]
        w_h = wgt[:, h:h + 1]
        qh = q_ref[:, h * DK:(h + 1) * DK]
        kh = k_ref[:, h * DK:(h + 1) * DK] * (DK ** -0.5)
        vh = v_ref[:, h * DV:(h + 1) * DV]
        nh = n_ref[:, h * DK:(h + 1) * DK]
        wt = w_h * jnp.sum(qh * kh, axis=-1, keepdims=True)
        num = a_h * num_ref[:, h * DV:(h + 1) * DV] + wt * vh
        den = a_h * jnp.sum(qh * nh, axis=-1, keepdims=True) + wt
        hh = num / jnp.maximum(jnp.abs(den), floor[:, h:h + 1])
        n_out_ref[:, h * DK:(h + 1) * DK] = a_h * nh + w_h * kh
        hn = _layer_norm(hh, mhg_ref[0:1, h * DV:(h + 1) * DV], None)
        og = og_ref[:, h * DV:(h + 1) * DV]
        mix_ref[:, D_CONV + h * DV:D_CONV + (h + 1) * DV] = hn * jax.nn.sigmoid(og)


def _even_sample(proj, gates, num, conv_st, n_st, m_st, w_conv, b_gates, mh_g, mixin, j):
    nb = DEC_BATCH
    rb = MP // nb
    est = 4 * 2 * (5 * nb * 1024 + 2 * nb * 512 + nb * 128 + nb * 1024 + nb * 2048 + nb * 512
                   + nb * 2048 + nb * 2048 + nb * 512) + (8 << 20)
    return pl.pallas_call(
        _even_sample_kernel,
        grid=(1,),
        in_specs=[
            pl.BlockSpec((nb, D_CONV), lambda i: (rb, 0)),
            pl.BlockSpec((nb, D_CONV), lambda i: (rb, 1)),
            pl.BlockSpec((nb, D_CONV), lambda i: (rb, 2)),
            pl.BlockSpec((nb, HEADS * DK), lambda i: (rb, 6)),
            pl.BlockSpec((nb, HEADS * DK), lambda i: (rb, 7)),
            pl.BlockSpec((nb, HEADS * DV), lambda i: (rb, 4)),
            pl.BlockSpec((nb, HEADS * DV), lambda i: (rb, 5)),
            pl.BlockSpec((nb, LANES), lambda i: (rb, 0)),
            pl.BlockSpec((nb, HEADS * DV), lambda i: (0, 0)),
            pl.BlockSpec((None, nb, 2 * D_CONV), lambda i: (j, 0, 0)),
            pl.BlockSpec((None, nb, HEADS * DK), lambda i: (j, 0, 0)),
            pl.BlockSpec((None, nb, HEADS), lambda i: (j, 0, 0)),
            pl.BlockSpec((None, CONV_W, D_CONV), lambda i: (j, 0, 0)),
            pl.BlockSpec((None, 2, HEADS), lambda i: (j, 0, 0)),
            pl.BlockSpec((None, 1, HEADS * DV), lambda i: (j, 0, 0)),
            pl.BlockSpec(memory_space=pl.ANY),
        ],
        out_specs=[
            pl.BlockSpec((nb, D_MODEL), lambda i: (rb, 0)),
            pl.BlockSpec((nb, 2 * D_CONV), lambda i: (0, 0)),
            pl.BlockSpec((nb, HEADS * DK), lambda i: (0, 0)),
            pl.BlockSpec((nb, HEADS), lambda i: (0, 0)),
        ],
        out_shape=[
            jax.ShapeDtypeStruct((M, D_MODEL), F32),
            jax.ShapeDtypeStruct((nb, 2 * D_CONV), F32),
            jax.ShapeDtypeStruct((nb, HEADS * DK), F32),
            jax.ShapeDtypeStruct((nb, HEADS), F32),
        ],
        input_output_aliases={15: 0},
        compiler_params=pltpu.CompilerParams(
            dimension_semantics=("arbitrary",), vmem_limit_bytes=_vmem_limit(est)),
        name="even_sample",
    )(proj, proj, proj, proj, proj, proj, proj, gates, num, conv_st, n_st, m_st,
      w_conv, b_gates, mh_g, mixin)


def _gmlp_norm(u_raw, v_raw, g, b):
    u = jax.nn.gelu(u_raw)
    vn = _layer_norm(jax.nn.gelu(v_raw), g, b)
    return u, vn


def _odd_prompt_kernel(p_ref, u_ref, v_ref, wp_ref, sc_ref, gmg_ref, gmb_ref, ws_ref, bst_ref,
                       mix_ref, pool_ref, gv_ref, pbuf, sbuf):
    s = pl.program_id(1)
    R = RO

    @pl.when(s == 0)
    def _():
        pbuf[0:HIST, :] = jnp.zeros((HIST, D_POOL), F32)
        sbuf[0:SUBLANES, :] = jnp.zeros((SUBLANES, D_POOL), F32)

    p = p_ref[...]
    pbuf[HIST:HIST + R, :] = p
    pos = s * R + lax.broadcasted_iota(jnp.int32, (R, 1), 0)
    for g, w in enumerate(POOL_WINDOWS):
        lo, hi = g * POOL_GW, (g + 1) * POOL_GW
        cur = pbuf[0:HIST + R, lo:hi]
        d = 1
        while d < w:
            sbuf[SUBLANES:SUBLANES + HIST + R, lo:hi] = cur
            cur = cur + sbuf[SUBLANES - d:SUBLANES - d + HIST + R, lo:hi]
            d *= 2
        win = cur[HIST:, :]
        cnt = jnp.minimum(w, pos + 1).astype(F32)
        diff = win / cnt - p[:, lo:hi]
        mix_ref[:, lo:hi] = _dot(diff, wp_ref[g]) * sc_ref[0:1, lo:hi]
    pool_ref[...] = pbuf[HIST + R - POOL_BUF:HIST + R, :]
    pbuf[0:HIST, :] = pbuf[R:R + HIST, :]

    u, vn = _gmlp_norm(u_ref[...], v_ref[...], gmg_ref[...], gmb_ref[...])
    L = GMLP_CHUNK
    tril = lax.broadcasted_iota(jnp.int32, (L, L), 0) >= lax.broadcasted_iota(jnp.int32, (L, L), 1)
    for g in range(D_GMLP // GMLP_GW):
        lo, hi = g * GMLP_GW, (g + 1) * GMLP_GW
        ws = jnp.where(tril, ws_ref[g], 0.0)
        bcol = bst_ref[:, g:g + 1]
        for ck in range(R // L):
            r0, r1 = ck * L, (ck + 1) * L
            sv = _dot(ws, vn[r0:r1, lo:hi]) + bcol
            mix_ref[r0:r1, D_POOL + lo:D_POOL + hi] = u[r0:r1, lo:hi] * sv
    gv_ref[...] = vn[R - L:R, :]


def _odd_prompt(proj, w_pool, pool_scale, gm_g, gm_b, w_spatial, bs_t, j):
    nrb = SEQ // RO
    r = lambda b, s: b * nrb + s
    est = 4 * (2 * 3 * RO * 1024 + 2 * RO * 2048 + 2 * 4 * 256 * 256 + 2 * 4 * 128 * 128
               + (RO + HIST) * 1024 + 8 * RO * 1024) + (8 << 20)
    return pl.pallas_call(
        _odd_prompt_kernel,
        grid=(BATCH, nrb),
        in_specs=[
            pl.BlockSpec((RO, D_POOL), lambda b, s: (r(b, s), 0)),
            pl.BlockSpec((RO, D_GMLP), lambda b, s: (r(b, s), 1)),
            pl.BlockSpec((RO, D_GMLP), lambda b, s: (r(b, s), 2)),
            pl.BlockSpec((None, 4, POOL_GW, POOL_GW), lambda b, s: (j, 0, 0, 0)),
            pl.BlockSpec((None, 1, D_POOL), lambda b, s: (j, 0, 0)),
            pl.BlockSpec((None, 1, D_GMLP), lambda b, s: (j, 0, 0)),
            pl.BlockSpec((None, 1, D_GMLP), lambda b, s: (j, 0, 0)),
            pl.BlockSpec((None, 4, GMLP_CHUNK, GMLP_CHUNK), lambda b, s: (j, 0, 0, 0)),
            pl.BlockSpec((None, GMLP_CHUNK, 4), lambda b, s: (j, 0, 0)),
        ],
        out_specs=[
            pl.BlockSpec((RO, D_MODEL), lambda b, s: (r(b, s), 0)),
            pl.BlockSpec((None, POOL_BUF, D_POOL), lambda b, s: (b, 0, 0)),
            pl.BlockSpec((None, GMLP_CHUNK, D_GMLP), lambda b, s: (b, 0, 0)),
        ],
        out_shape=[
            jax.ShapeDtypeStruct((M, D_MODEL), F32),
            jax.ShapeDtypeStruct((BATCH, POOL_BUF, D_POOL), F32),
            jax.ShapeDtypeStruct((BATCH, GMLP_CHUNK, D_GMLP), F32),
        ],
        scratch_shapes=[pltpu.VMEM((RO + HIST, D_POOL), F32),
                        pltpu.VMEM((SUBLANES + RO + HIST, D_POOL), F32)],
        compiler_params=pltpu.CompilerParams(
            dimension_semantics=("arbitrary", "arbitrary"), vmem_limit_bytes=_vmem_limit(est)),
        name="odd_prompt",
    )(proj, proj, proj, w_pool, pool_scale, gm_g, gm_b, w_spatial, bs_t)


def _odd_sample_kernel(p_ref, u_ref, v_ref, st_ref, wp_ref, sc_ref, gmg_ref, gmb_ref, ws_ref, bst_ref,
                       *rest):
    mix_ref, pool_ref, gv_ref = rest[-3:]
    p = p_ref[...]
    for r in range(POOL_BUF - 1):
        pool_ref[r] = st_ref[r + 1]
    pool_ref[POOL_BUF - 1] = p
    for g, w in enumerate(POOL_WINDOWS):
        lo, hi = g * POOL_GW, (g + 1) * POOL_GW
        win = p[:, lo:hi]
        for jj in range(1, w):
            win = win + st_ref[POOL_BUF - jj, :, lo:hi]
        cnt = float(min(w, PAST_LEN + 1))
        diff = win / cnt - p[:, lo:hi]
        mix_ref[:, lo:hi] = _dot(diff, wp_ref[g]) * sc_ref[0:1, lo:hi]
    u, vn = _gmlp_norm(u_ref[...], v_ref[...], gmg_ref[...], gmb_ref[...])
    gv_ref[...] = vn
    for g in range(D_GMLP // GMLP_GW):
        lo, hi = g * GMLP_GW, (g + 1) * GMLP_GW
        sv = ws_ref[g, 0:1, 0:1] * vn[:, lo:hi] + bst_ref[0:1, g:g + 1]
        mix_ref[:, D_POOL + lo:D_POOL + hi] = u[:, lo:hi] * sv


def _odd_sample(proj, pool_st, w_pool, pool_scale, gm_g, gm_b, w_spatial, bs_t, mixin, pool_prev, j):
    nb = DEC_BATCH
    rb = MP // nb
    st_blk = (None, POOL_BUF, nb, D_POOL)
    in_specs = [
        pl.BlockSpec((nb, D_POOL), lambda i: (rb, 0)),
        pl.BlockSpec((nb, D_GMLP), lambda i: (rb, 1)),
        pl.BlockSpec((nb, D_GMLP), lambda i: (rb, 2)),
        pl.BlockSpec(st_blk, lambda i: (j, 0, 0, 0)),
        pl.BlockSpec((None, 4, POOL_GW, POOL_GW), lambda i: (j, 0, 0, 0)),
        pl.BlockSpec((None, 1, D_POOL), lambda i: (j, 0, 0)),
        pl.BlockSpec((None, 1, D_GMLP), lambda i: (j, 0, 0)),
        pl.BlockSpec((None, 1, D_GMLP), lambda i: (j, 0, 0)),
        pl.BlockSpec((None, 4, GMLP_CHUNK, GMLP_CHUNK), lambda i: (j, 0, 0, 0)),
        pl.BlockSpec((None, GMLP_CHUNK, 4), lambda i: (j, 0, 0)),
        pl.BlockSpec(memory_space=pl.ANY),
    ]
    args = [proj, proj, proj, pool_st, w_pool, pool_scale, gm_g, gm_b, w_spatial, bs_t, mixin]
    aliases = {len(args) - 1: 0}
    if pool_prev is not None:
        in_specs.append(pl.BlockSpec(memory_space=pl.ANY))
        args.append(pool_prev)
        aliases[len(args) - 1] = 1
    est = 4 * (4 * POOL_BUF * nb * D_POOL + 2 * (3 * nb * 1024 + 4 * 256 * 256 + 4 * 128 * 128
                                                + nb * 2048 + nb * 1024)) + (8 << 20)
    return pl.pallas_call(
        _odd_sample_kernel,
        grid=(1,),
        in_specs=in_specs,
        out_specs=[
            pl.BlockSpec((nb, D_MODEL), lambda i: (rb, 0)),
            pl.BlockSpec(st_blk, lambda i: (j, 0, 0, 0)),
            pl.BlockSpec((nb, D_GMLP), lambda i: (0, 0)),
        ],
        out_shape=[
            jax.ShapeDtypeStruct((M, D_MODEL), F32),
            jax.ShapeDtypeStruct(pool_st.shape, F32),
            jax.ShapeDtypeStruct((nb, D_GMLP), F32),
        ],
        input_output_aliases=aliases,
        compiler_params=pltpu.CompilerParams(
            dimension_semantics=("arbitrary",), vmem_limit_bytes=_vmem_limit(est)),
        name="odd_sample",
    )(*args)


def kernel(x_prompt, x_sample, state_conv, state_mlstm_C, state_mlstm_n, state_mlstm_m, state_pool,
           ln_g, ln_b, w_ffn_in, w_ffn_out, w_in_even, b_gates_even, w_conv, mh_norm_g, w_out_even,
           w_in_odd, w_pool, pool_scale, gm_ln_g, gm_ln_b, w_spatial, b_spatial, w_out_odd):
    n_even, n_odd = w_in_even.shape[0], w_in_odd.shape[0]
    x = jnp.concatenate([x_prompt.reshape(MP, D_MODEL), x_sample.reshape(DEC_BATCH, D_MODEL)], axis=0)

    ln_g3 = ln_g.reshape(DEPTH * 3, 1, D_MODEL)
    ln_b3 = ln_b.reshape(DEPTH * 3, 1, D_MODEL)
    w_even_t = jnp.swapaxes(w_in_even, 1, 2)
    w_gate_t = jnp.pad(w_even_t[:, EVEN_MAIN:, :], ((0, 0), (0, LANES - 2 * HEADS), (0, 0)))
    pool_st = jnp.swapaxes(state_pool, 1, 2)
    mh_g3 = mh_norm_g.reshape(n_even, 1, HEADS * DV)
    conv_st = state_conv.reshape(n_even, DEC_BATCH, (CONV_W - 1) * D_CONV)
    n_st = state_mlstm_n.reshape(n_even, DEC_BATCH, HEADS * DK)
    scale3 = pool_scale.reshape(n_odd, 1, D_POOL)
    gm_g3 = gm_ln_g.reshape(n_odd, 1, D_GMLP)
    gm_b3 = gm_ln_b.reshape(n_odd, 1, D_GMLP)
    bs_t = jnp.swapaxes(b_spatial, 1, 2)

    conv_p, conv_s, c_p, n_p, n_s, m_p, m_s = [], [], [], [], [], [], []
    pool_p, gv_p, gv_s = [], [], []
    c_s = None
    pool_s = None

    for layer in range(DEPTH):
        j = layer // 2
        x = _ffn(x, w_ffn_in, w_ffn_out, ln_g3, ln_b3, layer, 0, 3 * layer)
        if layer % 2 == 0:
            proj, gates = _proj(x, w_even_t, j, 0, EVEN_MAIN, "proj_even", w_transposed=True,
                                w_gate_t=w_gate_t)
            proj = _proj(x, w_even_t, j, 1, EVEN_MAIN, "proj_even", w_transposed=True, prev=proj)
            mixin, cv, cc, nn, mm = _even_prompt(proj, gates, w_conv, b_gates_even, mh_g3, j)
            conv_p.append(cv)
            c_p.append(cc)
            n_p.append(nn)
            m_p.append(mm.reshape(BATCH, HEADS))
            qk_cols = _qk_cols(x, w_even_t, j)
            c_s, num = _even_sample_state(state_mlstm_C, qk_cols, proj, gates, state_mlstm_m,
                                          b_gates_even, j, c_s)
            mixin, cvs, nns, mms = _even_sample(proj, gates, num, conv_st, n_st, state_mlstm_m,
                                                w_conv, b_gates_even, mh_g3, mixin, j)
            conv_s.append(cvs.reshape(DEC_BATCH, CONV_W - 1, D_CONV))
            n_s.append(nns.reshape(DEC_BATCH, HEADS, DK))
            m_s.append(mms)
            x = _outproj(x, mixin, w_out_even, ln_g3, ln_b3, j, 3 * layer + 1)
        else:
            proj = _proj(x, w_in_odd, j, 0, ODD_IN, "proj_odd")
            mixin, pp, gv = _odd_prompt(proj, w_pool, scale3, gm_g3, gm_b3, w_spatial, bs_t, j)
            pool_p.append(pp)
            gv_p.append(gv)
            mixin, pool_s, gvs = _odd_sample(proj, pool_st, w_pool, scale3, gm_g3, gm_b3, w_spatial, bs_t,
                                             mixin, pool_s, j)
            gv_s.append(gvs.reshape(DEC_BATCH, 1, D_GMLP))
            x = _outproj(x, mixin, w_out_odd, ln_g3, ln_b3, j, 3 * layer + 1)
        x = _ffn(x, w_ffn_in, w_ffn_out, ln_g3, ln_b3, layer, 1, 3 * layer + 2,
                 split_out=layer == DEPTH - 1)

    y_prompt = x[0].reshape(BATCH, SEQ, D_MODEL)
    y_sample = x[1].reshape(DEC_BATCH, 1, D_MODEL)
    return (y_prompt, y_sample,
            jnp.stack(conv_p), jnp.stack(conv_s),
            jnp.stack(c_p), c_s,
            jnp.stack(n_p), jnp.stack(n_s),
            jnp.stack(m_p), jnp.stack(m_s),
            jnp.stack(pool_p), jnp.swapaxes(pool_s, 1, 2),
            jnp.stack(gv_p), jnp.stack(gv_s))
```

```python
import functools

import jax
import jax.numpy as jnp
from jax import lax
from jax.experimental import pallas as pl
from jax.experimental.pallas import tpu as pltpu

F32 = jnp.float32

D_MODEL = 2048
BATCH = 4
SEQ = 2048
DEPTH = 4
DEC_BATCH = 128
PAST_LEN = 16384
D_FF = 5632
D_CONV = 1024
CONV_W = 3
HEADS = 4
DK = 128
DV = 256
D_POOL = 1024
POOL_WINDOWS = (2, 4, 8, 16)
POOL_GW = 256
POOL_BUF = 15
D_GMLP = 1024
GMLP_GW = 256
GMLP_CHUNK = 128
ALPHA = (2 * DEPTH) ** 0.25
LN_EPS = 1e-5
EVEN_MAIN = 3 * D_CONV + 2 * HEADS * DK + 2 * HEADS * DV
ODD_IN = D_POOL + 2 * D_GMLP

MP = BATCH * SEQ
M = MP + DEC_BATCH

LANES = 128
SUBLANES = 8
VMEM_BYTES_V7X = 64 * 1024 * 1024

TM = 832
TF = 512
TFS = 256
TN = 512
TMP = 520
PROJ_COLS = 3072
TMO = 416
LC = 256
RE = 256
RO = 256
SB = 16
HIST = 16


def _vmem_limit(nbytes):
    return int(min(VMEM_BYTES_V7X - 4 * 1024 * 1024, nbytes))


def _layer_norm(y, g, b, eps=LN_EPS):
    mu = jnp.mean(y, axis=-1, keepdims=True)
    yc = y - mu
    var = jnp.mean(yc * yc, axis=-1, keepdims=True)
    out = yc * lax.rsqrt(var + eps) * g
    if b is not None:
        out = out + b
    return out


def _log_sigmoid(x):
    return -(jnp.maximum(-x, 0.0) + jnp.log1p(jnp.exp(-jnp.abs(x))))


def _dot(a, b):
    return jnp.dot(a, b, preferred_element_type=F32)


def _dot_nt(a, b):
    return lax.dot_general(a, b, (((1,), (1,)), ((), ())), preferred_element_type=F32)


def _ffn_kernel(x_ref, wg_ref, wu_ref, wo_ref, g_ref, b_ref, o_ref, *maybe_sample_ref, nf):
    f = pl.program_id(1)

    @pl.when(f == 0)
    def _():
        o_ref[...] = (2.0 * ALPHA) * x_ref[...]

    x = x_ref[...]
    for c in range(TF // TFS):
        c0, c1 = c * TFS, (c + 1) * TFS
        gate = _dot(x, wg_ref[:, c0:c1])
        up = _dot(x, wu_ref[:, c0:c1])
        h = gate * jax.nn.sigmoid(gate) * up
        for n in range(D_MODEL // TN):
            o_ref[:, n * TN:(n + 1) * TN] += _dot(h, wo_ref[c0:c1, n * TN:(n + 1) * TN])

    @pl.when(f == nf - 1)
    def _():
        o_ref[...] = _layer_norm(o_ref[...], g_ref[...], b_ref[...], eps=4.0 * LN_EPS)

    if maybe_sample_ref:
        @pl.when((f == nf - 1) & (pl.program_id(0) == M // TM - 1))
        def _():
            maybe_sample_ref[0][...] = o_ref[TM - DEC_BATCH:, :]


def _ffn(x, w_ffn_in, w_ffn_out, ln_g, ln_b, layer, which, ln_idx, split_out=False):
    nf = D_FF // TF
    est = 4 * (4 * TM * D_MODEL + 2 * 3 * D_MODEL * TF + 4 * TM * TFS + TM * TN) + (4 << 20)
    out_specs = pl.BlockSpec((TM, D_MODEL), lambda i, f: (i, 0))
    out_shape = jax.ShapeDtypeStruct((M, D_MODEL), F32)
    if split_out:
        out_specs = [out_specs, pl.BlockSpec((DEC_BATCH, D_MODEL), lambda i, f: (0, 0))]
        out_shape = [jax.ShapeDtypeStruct((MP, D_MODEL), F32),
                     jax.ShapeDtypeStruct((DEC_BATCH, D_MODEL), F32)]
    return pl.pallas_call(
        functools.partial(_ffn_kernel, nf=nf),
        grid=(M // TM, nf),
        in_specs=[
            pl.BlockSpec((TM, D_MODEL), lambda i, f: (i, 0)),
            pl.BlockSpec((None, None, D_MODEL, TF), lambda i, f: (layer, which, 0, f)),
            pl.BlockSpec((None, None, D_MODEL, TF), lambda i, f: (layer, which, 0, nf + f)),
            pl.BlockSpec((None, None, TF, D_MODEL), lambda i, f: (layer, which, f, 0)),
            pl.BlockSpec((None, 1, D_MODEL), lambda i, f: (ln_idx, 0, 0)),
            pl.BlockSpec((None, 1, D_MODEL), lambda i, f: (ln_idx, 0, 0)),
        ],
        out_specs=out_specs,
        out_shape=out_shape,
        compiler_params=pltpu.CompilerParams(
            dimension_semantics=("arbitrary", "arbitrary"), vmem_limit_bytes=_vmem_limit(est)),
        name="ffn_ln",
    )(x, w_ffn_in, w_ffn_in, w_ffn_out, ln_g, ln_b)


def _proj_kernel(x_ref, w_ref, *rest, w_transposed, with_gates, aliased):
    rest = rest[1:] if aliased else rest
    if with_gates:
        wgate_ref, o_ref, gate_ref = rest
        gate_ref[...] = _dot_nt(x_ref[...], wgate_ref[...])
    else:
        (o_ref,) = rest
    for n in range(PROJ_COLS // TN):
        lo, hi = n * TN, (n + 1) * TN
        if w_transposed:
            o_ref[:, lo:hi] = _dot_nt(x_ref[...], w_ref[lo:hi, :])
        else:
            o_ref[:, lo:hi] = _dot(x_ref[...], w_ref[:, lo:hi])


def _proj(x, w, layer_idx, col_block, n_total, name, w_transposed=False, w_gate_t=None, prev=None):
    est = 4 * (PROJ_COLS * D_MODEL + 2 * TMP * D_MODEL + 2 * TMP * PROJ_COLS + 2 * TMP * TN) + (4 << 20)
    if w_transposed:
        w_spec = pl.BlockSpec((None, PROJ_COLS, D_MODEL), lambda i: (layer_idx, col_block, 0),
                              pipeline_mode=pl.Buffered(1))
    else:
        w_spec = pl.BlockSpec((None, D_MODEL, PROJ_COLS), lambda i: (layer_idx, 0, col_block),
                              pipeline_mode=pl.Buffered(1))
    in_specs = [pl.BlockSpec((TMP, D_MODEL), lambda i: (i, 0)), w_spec]
    args = [x, w]
    aliases = {}
    if prev is not None:
        in_specs.append(pl.BlockSpec(memory_space=pl.ANY))
        args.append(prev)
        aliases = {2: 0}
    out_specs = pl.BlockSpec((TMP, PROJ_COLS), lambda i: (i, col_block))
    out_shape = jax.ShapeDtypeStruct((M, n_total), F32)
    if w_gate_t is not None:
        in_specs.append(pl.BlockSpec((None, LANES, D_MODEL), lambda i: (layer_idx, 0, 0)))
        args.append(w_gate_t)
        out_specs = [out_specs, pl.BlockSpec((TMP, LANES), lambda i: (i, 0))]
        out_shape = [out_shape, jax.ShapeDtypeStruct((M, LANES), F32)]
        est += 4 * (2 * LANES * D_MODEL + 3 * TMP * LANES)
    return pl.pallas_call(
        functools.partial(_proj_kernel, w_transposed=w_transposed, with_gates=w_gate_t is not None,
                          aliased=prev is not None),
        grid=(M // TMP,),
        in_specs=in_specs,
        out_specs=out_specs,
        out_shape=out_shape,
        input_output_aliases=aliases,
        compiler_params=pltpu.CompilerParams(
            dimension_semantics=("parallel",), vmem_limit_bytes=_vmem_limit(est)),
        name=name,
    )(*args)


def _outproj_kernel(x_ref, a_ref, w_ref, g_ref, b_ref, o_ref):
    y = ALPHA * x_ref[...] + _dot(a_ref[...], w_ref[...])
    o_ref[...] = _layer_norm(y, g_ref[...], b_ref[...])


def _outproj(x, mix, w_out, ln_g, ln_b, layer_idx, ln_idx):
    est = 4 * (D_MODEL * D_MODEL + 9 * TMO * D_MODEL) + (4 << 20)
    return pl.pallas_call(
        _outproj_kernel,
        grid=(M // TMO,),
        in_specs=[
            pl.BlockSpec((TMO, D_MODEL), lambda i: (i, 0)),
            pl.BlockSpec((TMO, D_MODEL), lambda i: (i, 0)),
            pl.BlockSpec((None, D_MODEL, D_MODEL), lambda i: (layer_idx, 0, 0),
                         pipeline_mode=pl.Buffered(1)),
            pl.BlockSpec((None, 1, D_MODEL), lambda i: (ln_idx, 0, 0)),
            pl.BlockSpec((None, 1, D_MODEL), lambda i: (ln_idx, 0, 0)),
        ],
        out_specs=pl.BlockSpec((TMO, D_MODEL), lambda i: (i, 0)),
        out_shape=jax.ShapeDtypeStruct((M, D_MODEL), F32),
        compiler_params=pltpu.CompilerParams(
            dimension_semantics=("parallel",), vmem_limit_bytes=_vmem_limit(est)),
        name="outproj_ln",
    )(x, mix, w_out, ln_g, ln_b)


def _split3(x):
    h1 = x.astype(jnp.bfloat16).astype(F32)
    r = x - h1
    h2 = r.astype(jnp.bfloat16).astype(F32)
    return h1, h2, r - h2


def _even_prompt_kernel(bg_ref, cg_ref, xin_ref, q_ref, k_ref, v_ref, og_ref, gt_ref,
                        wc_ref, bgate_ref, mhg_ref,
                        mix_ref, conv_ref, c_out_ref, n_out_ref, m_out_ref,
                        cbuf, c_sc, n_sc, m_sc):
    c = pl.program_id(1)
    L = LC
    R = RE

    @pl.when(c == 0)
    def _():
        cbuf[0:SUBLANES, :] = jnp.zeros((SUBLANES, D_CONV), F32)
        c_sc[...] = jnp.zeros_like(c_sc)
        n_sc[...] = jnp.zeros_like(n_sc)
        m_sc[...] = jnp.zeros_like(m_sc)

    cx = cg_ref[...] * xin_ref[...]
    cbuf[SUBLANES:SUBLANES + R, :] = cx
    c1 = cbuf[SUBLANES - 1:SUBLANES - 1 + R, :]
    c2 = cbuf[SUBLANES - 2:SUBLANES - 2 + R, :]
    conv = c2 * wc_ref[0:1, :] + c1 * wc_ref[1:2, :] + cx * wc_ref[2:3, :]
    mix_ref[:, 0:D_CONV] = bg_ref[...] * conv
    conv_ref[...] = cbuf[SUBLANES + R - 2:SUBLANES + R, :]
    cbuf[0:SUBLANES, :] = cbuf[R:R + SUBLANES, :]

    row = lax.broadcasted_iota(jnp.int32, (L, L), 0)
    col = lax.broadcasted_iota(jnp.int32, (L, L), 1)
    causal = row >= col
    tril = causal.astype(F32)
    c_state = [c_sc[h] for h in range(HEADS)]
    n_state = [n_sc[h:h + 1, :] for h in range(HEADS)]
    m_state = [m_sc[h, 0:1, 0:1] for h in range(HEADS)]

    for s in range(R // L):
        r0, r1 = s * L, (s + 1) * L
        gt = gt_ref[r0:r1, :]
        li_all = gt[:, 0:HEADS] + bgate_ref[0:1, :]
        lf_all = _log_sigmoid(gt[:, HEADS:2 * HEADS] + bgate_ref[1:2, :])
        f1, f2, f3 = _split3(lf_all)
        b_all = _dot(tril, f1) + _dot(tril, f2) + _dot(tril, f3)
        z = jnp.concatenate([li_all, b_all, jnp.zeros((L, LANES - 2 * HEADS), F32)], axis=1)
        zt = z.T

        for h in range(HEADS):
            li_c = li_all[:, h:h + 1]
            b_c = b_all[:, h:h + 1]
            li_r = zt[h:h + 1, :]
            b_r = zt[HEADS + h:HEADS + h + 1, :]
            m0 = m_state[h]
            dm = jnp.where(causal, b_c - b_r + li_r, -jnp.inf)
            inter = b_c + m0
            m_t = jnp.maximum(inter, jnp.max(dm, axis=-1, keepdims=True))
            a_int = jnp.exp(inter - m_t)
            qh = q_ref[r0:r1, h * DK:(h + 1) * DK]
            kh = k_ref[r0:r1, h * DK:(h + 1) * DK] * (DK ** -0.5)
            vh = v_ref[r0:r1, h * DV:(h + 1) * DV]
            w = jnp.exp(dm - m_t) * _dot_nt(qh, kh)
            ch = c_state[h]
            n_row = n_state[h]
            num = a_int * _dot(qh, ch) + _dot(w, vh)
            den = (a_int * jnp.sum(qh * n_row, axis=-1, keepdims=True)
                   + jnp.sum(w, axis=-1, keepdims=True))
            hh = num / jnp.maximum(jnp.abs(den), jnp.exp(-m_t))
            m_new = m_t[L - 1:L, :]
            b_last = b_c[L - 1:L, :]
            w_end = jnp.exp(b_last - b_c + li_c - m_new)
            decay = jnp.exp(b_last + m0 - m_new)
            wk = w_end * kh
            c_state[h] = decay * ch + _dot(wk.T, vh)
            n_state[h] = decay * n_row + jnp.sum(wk, axis=0, keepdims=True)
            m_state[h] = m_new
            hn = _layer_norm(hh, mhg_ref[0:1, h * DV:(h + 1) * DV], None)
            og = og_ref[r0:r1, h * DV:(h + 1) * DV]
            mix_ref[r0:r1, D_CONV + h * DV:D_CONV + (h + 1) * DV] = hn * jax.nn.sigmoid(og)

    for h in range(HEADS):
        c_sc[h] = c_state[h]
        n_sc[h:h + 1, :] = n_state[h]
        m_sc[h] = jnp.broadcast_to(m_state[h], (SUBLANES, LANES))
        c_out_ref[h] = c_state[h]
        n_out_ref[h:h + 1, :] = n_state[h]
        m_out_ref[0:1, h:h + 1] = m_state[h]


def _even_prompt(proj, gates, w_conv, b_gates, mh_g, j):
    nck = SEQ // RE
    r = lambda b, c: b * nck + c
    est = 4 * (2 * (5 * RE * 1024 + 2 * RE * 512 + RE * 128) + 2 * RE * 2048 + 4 * RE * 1024
               + 3 * HEADS * DK * DV + 24 * LC * LC) + (8 << 20)
    return pl.pallas_call(
        _even_prompt_kernel,
        grid=(BATCH, nck),
        in_specs=[
            pl.BlockSpec((RE, D_CONV), lambda b, c: (r(b, c), 0)),
            pl.BlockSpec((RE, D_CONV), lambda b, c: (r(b, c), 1)),
            pl.BlockSpec((RE, D_CONV), lambda b, c: (r(b, c), 2)),
            pl.BlockSpec((RE, HEADS * DK), lambda b, c: (r(b, c), 6)),
            pl.BlockSpec((RE, HEADS * DK), lambda b, c: (r(b, c), 7)),
            pl.BlockSpec((RE, HEADS * DV), lambda b, c: (r(b, c), 4)),
            pl.BlockSpec((RE, HEADS * DV), lambda b, c: (r(b, c), 5)),
            pl.BlockSpec((RE, LANES), lambda b, c: (r(b, c), 0)),
            pl.BlockSpec((None, CONV_W, D_CONV), lambda b, c: (j, 0, 0)),
            pl.BlockSpec((None, 2, HEADS), lambda b, c: (j, 0, 0)),
            pl.BlockSpec((None, 1, HEADS * DV), lambda b, c: (j, 0, 0)),
        ],
        out_specs=[
            pl.BlockSpec((RE, D_MODEL), lambda b, c: (r(b, c), 0)),
            pl.BlockSpec((None, CONV_W - 1, D_CONV), lambda b, c: (b, 0, 0)),
            pl.BlockSpec((None, HEADS, DK, DV), lambda b, c: (b, 0, 0, 0)),
            pl.BlockSpec((None, HEADS, DK), lambda b, c: (b, 0, 0)),
            pl.BlockSpec((None, 1, HEADS), lambda b, c: (b, 0, 0)),
        ],
        out_shape=[
            jax.ShapeDtypeStruct((M, D_MODEL), F32),
            jax.ShapeDtypeStruct((BATCH, CONV_W - 1, D_CONV), F32),
            jax.ShapeDtypeStruct((BATCH, HEADS, DK, DV), F32),
            jax.ShapeDtypeStruct((BATCH, HEADS, DK), F32),
            jax.ShapeDtypeStruct((BATCH, 1, HEADS), F32),
        ],
        scratch_shapes=[
            pltpu.VMEM((RE + SUBLANES, D_CONV), F32),
            pltpu.VMEM((HEADS, DK, DV), F32),
            pltpu.VMEM((HEADS, DK), F32),
            pltpu.VMEM((HEADS, SUBLANES, LANES), F32),
        ],
        compiler_params=pltpu.CompilerParams(
            dimension_semantics=("arbitrary", "arbitrary"), vmem_limit_bytes=_vmem_limit(est)),
        name="even_prompt",
    )(proj, proj, proj, proj, proj, proj, proj, gates, w_conv, b_gates, mh_g)


def _sample_gates(gt, bgate_ref, m):
    li = gt[:, 0:HEADS] + bgate_ref[0:1, :]
    lf = _log_sigmoid(gt[:, HEADS:2 * HEADS] + bgate_ref[1:2, :])
    inter = lf + m
    m_t = jnp.maximum(inter, li)
    return jnp.exp(inter - m_t), jnp.exp(li - m_t), m_t


def _qk_cols_kernel(x_ref, w_ref, o_ref):
    t = _dot_nt(w_ref[...], x_ref[...])
    for s in range(DEC_BATCH // SB):
        o_ref[s] = t[:, s * SB:(s + 1) * SB]


def _qk_cols(x, w_even_t, j):
    nqk = 2 * HEADS * DK
    est = 4 * (2 * DEC_BATCH * D_MODEL + 2 * nqk * D_MODEL + 3 * nqk * LANES
               + 2 * (DEC_BATCH // SB) * nqk * LANES) + (4 << 20)
    return pl.pallas_call(
        _qk_cols_kernel,
        grid=(1,),
        in_specs=[
            pl.BlockSpec((DEC_BATCH, D_MODEL), lambda i: (MP // DEC_BATCH, 0)),
            pl.BlockSpec((None, nqk, D_MODEL), lambda i: (j, 3 * D_CONV // nqk, 0)),
        ],
        out_specs=pl.BlockSpec((DEC_BATCH // SB, nqk, SB), lambda i: (0, 0, 0)),
        out_shape=jax.ShapeDtypeStruct((DEC_BATCH // SB, nqk, SB), F32),
        compiler_params=pltpu.CompilerParams(
            dimension_semantics=("arbitrary",), vmem_limit_bytes=_vmem_limit(est)),
        name="qk_cols",
    )(x, w_even_t)


def _even_sample_state_kernel(c_ref, qk_ref, v_ref, gt_ref, m_ref, bgate_ref, *rest):
    c_out_ref, num_ref = rest[-2], rest[-1]
    a, wgt, _ = _sample_gates(gt_ref[...], bgate_ref, m_ref[...])
    for bi in range(SB):
        for h in range(HEADS):
            a_s = a[bi:bi + 1, h:h + 1]
            w_s = wgt[bi:bi + 1, h:h + 1]
            ch = c_ref[bi, h]
            qc = qk_ref[h * DK:(h + 1) * DK, bi:bi + 1]
            kc = qk_ref[(HEADS + h) * DK:(HEADS + h + 1) * DK, bi:bi + 1] * (DK ** -0.5)
            vr = v_ref[bi:bi + 1, h * DV:(h + 1) * DV]
            c_out_ref[bi, h] = a_s * ch + (w_s * kc) * vr
            num_ref[bi:bi + 1, h * DV:(h + 1) * DV] = jnp.sum(qc * ch, axis=0, keepdims=True)


def _even_sample_state(state_c, qk_cols, proj, gates, state_m, b_gates, j, c_prev):
    blk5 = (None, SB, HEADS, DK, DV)
    rb = MP // SB
    in_specs = [
        pl.BlockSpec(blk5, lambda i: (j, i, 0, 0, 0)),
        pl.BlockSpec((None, 2 * HEADS * DK, SB), lambda i: (i, 0, 0)),
        pl.BlockSpec((SB, HEADS * DV), lambda i: (rb + i, 4)),
        pl.BlockSpec((SB, LANES), lambda i: (rb + i, 0)),
        pl.BlockSpec((None, SB, HEADS), lambda i: (j, i, 0)),
        pl.BlockSpec((None, 2, HEADS), lambda i: (j, 0, 0)),
    ]
    args = [state_c, qk_cols, proj, gates, state_m, b_gates]
    aliases = {}
    if c_prev is not None:
        in_specs.append(pl.BlockSpec(memory_space=pl.ANY))
        args.append(c_prev)
        aliases = {len(args) - 1: 0}
    est = 4 * (4 * SB * HEADS * DK * DV + 4 * HEADS * DK * LANES) + (8 << 20)
    return pl.pallas_call(
        _even_sample_state_kernel,
        grid=(DEC_BATCH // SB,),
        in_specs=in_specs,
        out_specs=[
            pl.BlockSpec(blk5, lambda i: (j, i, 0, 0, 0)),
            pl.BlockSpec((SB, HEADS * DV), lambda i: (i, 0)),
        ],
        out_shape=[
            jax.ShapeDtypeStruct(state_c.shape, F32),
            jax.ShapeDtypeStruct((DEC_BATCH, HEADS * DV), F32),
        ],
        input_output_aliases=aliases,
        compiler_params=pltpu.CompilerParams(
            dimension_semantics=("arbitrary",), vmem_limit_bytes=_vmem_limit(est)),
        name="even_sample_state",
    )(*args)


def _even_sample_kernel(bg_ref, cg_ref, xin_ref, q_ref, k_ref, v_ref, og_ref, gt_ref,
                        num_ref, cst_ref, n_ref, m_ref, wc_ref, bgate_ref, mhg_ref, mixin_hbm,
                        mix_ref, conv_ref, n_out_ref, m_out_ref):
    del mixin_hbm
    cx = cg_ref[...] * xin_ref[...]
    st0 = cst_ref[:, 0:D_CONV]
    st1 = cst_ref[:, D_CONV:2 * D_CONV]
    conv = st0 * wc_ref[0:1, :] + st1 * wc_ref[1:2, :] + cx * wc_ref[2:3, :]
    mix_ref[:, 0:D_CONV] = bg_ref[...] * conv
    conv_ref[:, 0:D_CONV] = st1
    conv_ref[:, D_CONV:2 * D_CONV] = cx

    a, wgt, m_t = _sample_gates(gt_ref[...], bgate_ref, m_ref[...])
    m_out_ref[...] = m_t
    floor = jnp.exp(-m_t)
    for h in range(HEADS):
        a_h = a[:, h:h + 1]
        w_h = wgt[:, h:h + 1]
        qh = q_ref[:, h * DK:(h + 1) * DK]
        kh = k_ref[:, h * DK:(h + 1) * DK] * (DK ** -0.5)
        vh = v_ref[:, h * DV:(h + 1) * DV]
        nh = n_ref[:, h * DK:(h + 1) * DK]
        wt = w_h * jnp.sum(qh * kh, axis=-1, keepdims=True)
        num = a_h * num_ref[:, h * DV:(h + 1) * DV] + wt * vh
        den = a_h * jnp.sum(qh * nh, axis=-1, keepdims=True) + wt
        hh = num / jnp.maximum(jnp.abs(den), floor[:, h:h + 1])
        n_out_ref[:, h * DK:(h + 1) * DK] = a_h * nh + w_h * kh
        hn = _layer_norm(hh, mhg_ref[0:1, h * DV:(h + 1) * DV], None)
        og = og_ref[:, h * DV:(h + 1) * DV]
        mix_ref[:, D_CONV + h * DV:D_CONV + (h + 1) * DV] = hn * jax.nn.sigmoid(og)


def _even_sample(proj, gates, num, conv_st, n_st, m_st, w_conv, b_gates, mh_g, mixin, j):
    nb = DEC_BATCH
    rb = MP // nb
    est = 4 * 2 * (5 * nb * 1024 + 2 * nb * 512 + nb * 128 + nb * 1024 + nb * 2048 + nb * 512
                   + nb * 2048 + nb * 2048 + nb * 512) + (8 << 20)
    return pl.pallas_call(
        _even_sample_kernel,
        grid=(1,),
        in_specs=[
            pl.BlockSpec((nb, D_CONV), lambda i: (rb, 0)),
            pl.BlockSpec((nb, D_CONV), lambda i: (rb, 1)),
            pl.BlockSpec((nb, D_CONV), lambda i: (rb, 2)),
            pl.BlockSpec((nb, HEADS * DK), lambda i: (rb, 6)),
            pl.BlockSpec((nb, HEADS * DK), lambda i: (rb, 7)),
            pl.BlockSpec((nb, HEADS * DV), lambda i: (rb, 4)),
            pl.BlockSpec((nb, HEADS * DV), lambda i: (rb, 5)),
            pl.BlockSpec((nb, LANES), lambda i: (rb, 0)),
            pl.BlockSpec((nb, HEADS * DV), lambda i: (0, 0)),
            pl.BlockSpec((None, nb, 2 * D_CONV), lambda i: (j, 0, 0)),
            pl.BlockSpec((None, nb, HEADS * DK), lambda i: (j, 0, 0)),
            pl.BlockSpec((None, nb, HEADS), lambda i: (j, 0, 0)),
            pl.BlockSpec((None, CONV_W, D_CONV), lambda i: (j, 0, 0)),
            pl.BlockSpec((None, 2, HEADS), lambda i: (j, 0, 0)),
            pl.BlockSpec((None, 1, HEADS * DV), lambda i: (j, 0, 0)),
            pl.BlockSpec(memory_space=pl.ANY),
        ],
        out_specs=[
            pl.BlockSpec((nb, D_MODEL), lambda i: (rb, 0)),
            pl.BlockSpec((nb, 2 * D_CONV), lambda i: (0, 0)),
            pl.BlockSpec((nb, HEADS * DK), lambda i: (0, 0)),
            pl.BlockSpec((nb, HEADS), lambda i: (0, 0)),
        ],
        out_shape=[
            jax.ShapeDtypeStruct((M, D_MODEL), F32),
            jax.ShapeDtypeStruct((nb, 2 * D_CONV), F32),
            jax.ShapeDtypeStruct((nb, HEADS * DK), F32),
            jax.ShapeDtypeStruct((nb, HEADS), F32),
        ],
        input_output_aliases={15: 0},
        compiler_params=pltpu.CompilerParams(
            dimension_semantics=("arbitrary",), vmem_limit_bytes=_vmem_limit(est)),
        name="even_sample",
    )(proj, proj, proj, proj, proj, proj, proj, gates, num, conv_st, n_st, m_st,
      w_conv, b_gates, mh_g, mixin)


def _gmlp_norm(u_raw, v_raw, g, b):
    u = jax.nn.gelu(u_raw)
    vn = _layer_norm(jax.nn.gelu(v_raw), g, b)
    return u, vn


def _odd_prompt_kernel(p_ref, u_ref, v_ref, wp_ref, sc_ref, gmg_ref, gmb_ref, ws_ref, bst_ref,
                       mix_ref, pool_ref, gv_ref, pbuf, sbuf):
    s = pl.program_id(1)
    R = RO

    @pl.when(s == 0)
    def _():
        pbuf[0:HIST, :] = jnp.zeros((HIST, D_POOL), F32)
        sbuf[0:SUBLANES, :] = jnp.zeros((SUBLANES, D_POOL), F32)

    p = p_ref[...]
    pbuf[HIST:HIST + R, :] = p
    pos = s * R + lax.broadcasted_iota(jnp.int32, (R, 1), 0)
    for g, w in enumerate(POOL_WINDOWS):
        lo, hi = g * POOL_GW, (g + 1) * POOL_GW
        cur = pbuf[0:HIST + R, lo:hi]
        d = 1
        while d < w:
            sbuf[SUBLANES:SUBLANES + HIST + R, lo:hi] = cur
            cur = cur + sbuf[SUBLANES - d:SUBLANES - d + HIST + R, lo:hi]
            d *= 2
        win = cur[HIST:, :]
        cnt = jnp.minimum(w, pos + 1).astype(F32)
        diff = win / cnt - p[:, lo:hi]
        mix_ref[:, lo:hi] = _dot(diff, wp_ref[g]) * sc_ref[0:1, lo:hi]
    pool_ref[...] = pbuf[HIST + R - POOL_BUF:HIST + R, :]
    pbuf[0:HIST, :] = pbuf[R:R + HIST, :]

    u, vn = _gmlp_norm(u_ref[...], v_ref[...], gmg_ref[...], gmb_ref[...])
    L = GMLP_CHUNK
    tril = lax.broadcasted_iota(jnp.int32, (L, L), 0) >= lax.broadcasted_iota(jnp.int32, (L, L), 1)
    for g in range(D_GMLP // GMLP_GW):
        lo, hi = g * GMLP_GW, (g + 1) * GMLP_GW
        ws = jnp.where(tril, ws_ref[g], 0.0)
        bcol = bst_ref[:, g:g + 1]
        for ck in range(R // L):
            r0, r1 = ck * L, (ck + 1) * L
            sv = _dot(ws, vn[r0:r1, lo:hi]) + bcol
            mix_ref[r0:r1, D_POOL + lo:D_POOL + hi] = u[r0:r1, lo:hi] * sv
    gv_ref[...] = vn[R - L:R, :]


def _odd_prompt(proj, w_pool, pool_scale, gm_g, gm_b, w_spatial, bs_t, j):
    nrb = SEQ // RO
    r = lambda b, s: b * nrb + s
    est = 4 * (2 * 3 * RO * 1024 + 2 * RO * 2048 + 2 * 4 * 256 * 256 + 2 * 4 * 128 * 128
               + (RO + HIST) * 1024 + 8 * RO * 1024) + (8 << 20)
    return pl.pallas_call(
        _odd_prompt_kernel,
        grid=(BATCH, nrb),
        in_specs=[
            pl.BlockSpec((RO, D_POOL), lambda b, s: (r(b, s), 0)),
            pl.BlockSpec((RO, D_GMLP), lambda b, s: (r(b, s), 1)),
            pl.BlockSpec((RO, D_GMLP), lambda b, s: (r(b, s), 2)),
            pl.BlockSpec((None, 4, POOL_GW, POOL_GW), lambda b, s: (j, 0, 0, 0)),
            pl.BlockSpec((None, 1, D_POOL), lambda b, s: (j, 0, 0)),
            pl.BlockSpec((None, 1, D_GMLP), lambda b, s: (j, 0, 0)),
            pl.BlockSpec((None, 1, D_GMLP), lambda b, s: (j, 0, 0)),
            pl.BlockSpec((None, 4, GMLP_CHUNK, GMLP_CHUNK), lambda b, s: (j, 0, 0, 0)),
            pl.BlockSpec((None, GMLP_CHUNK, 4), lambda b, s: (j, 0, 0)),
        ],
        out_specs=[
            pl.BlockSpec((RO, D_MODEL), lambda b, s: (r(b, s), 0)),
            pl.BlockSpec((None, POOL_BUF, D_POOL), lambda b, s: (b, 0, 0)),
            pl.BlockSpec((None, GMLP_CHUNK, D_GMLP), lambda b, s: (b, 0, 0)),
        ],
        out_shape=[
            jax.ShapeDtypeStruct((M, D_MODEL), F32),
            jax.ShapeDtypeStruct((BATCH, POOL_BUF, D_POOL), F32),
            jax.ShapeDtypeStruct((BATCH, GMLP_CHUNK, D_GMLP), F32),
        ],
        scratch_shapes=[pltpu.VMEM((RO + HIST, D_POOL), F32),
                        pltpu.VMEM((SUBLANES + RO + HIST, D_POOL), F32)],
        compiler_params=pltpu.CompilerParams(
            dimension_semantics=("arbitrary", "arbitrary"), vmem_limit_bytes=_vmem_limit(est)),
        name="odd_prompt",
    )(proj, proj, proj, w_pool, pool_scale, gm_g, gm_b, w_spatial, bs_t)


def _odd_sample_kernel(p_ref, u_ref, v_ref, st_ref, wp_ref, sc_ref, gmg_ref, gmb_ref, ws_ref, bst_ref,
                       *rest):
    mix_ref, pool_ref, gv_ref = rest[-3:]
    p = p_ref[...]
    for r in range(POOL_BUF - 1):
        pool_ref[r] = st_ref[r + 1]
    pool_ref[POOL_BUF - 1] = p
    for g, w in enumerate(POOL_WINDOWS):
        lo, hi = g * POOL_GW, (g + 1) * POOL_GW
        win = p[:, lo:hi]
        for jj in range(1, w):
            win = win + st_ref[POOL_BUF - jj, :, lo:hi]
        cnt = float(min(w, PAST_LEN + 1))
        diff = win / cnt - p[:, lo:hi]
        mix_ref[:, lo:hi] = _dot(diff, wp_ref[g]) * sc_ref[0:1, lo:hi]
    u, vn = _gmlp_norm(u_ref[...], v_ref[...], gmg_ref[...], gmb_ref[...])
    gv_ref[...] = vn
    for g in range(D_GMLP // GMLP_GW):
        lo, hi = g * GMLP_GW, (g + 1) * GMLP_GW
        sv = ws_ref[g, 0:1, 0:1] * vn[:, lo:hi] + bst_ref[0:1, g:g + 1]
        mix_ref[:, D_POOL + lo:D_POOL + hi] = u[:, lo:hi] * sv


def _odd_sample(proj, pool_st, w_pool, pool_scale, gm_g, gm_b, w_spatial, bs_t, mixin, pool_prev, j):
    nb = DEC_BATCH
    rb = MP // nb
    st_blk = (None, POOL_BUF, nb, D_POOL)
    in_specs = [
        pl.BlockSpec((nb, D_POOL), lambda i: (rb, 0)),
        pl.BlockSpec((nb, D_GMLP), lambda i: (rb, 1)),
        pl.BlockSpec((nb, D_GMLP), lambda i: (rb, 2)),
        pl.BlockSpec(st_blk, lambda i: (j, 0, 0, 0)),
        pl.BlockSpec((None, 4, POOL_GW, POOL_GW), lambda i: (j, 0, 0, 0)),
        pl.BlockSpec((None, 1, D_POOL), lambda i: (j, 0, 0)),
        pl.BlockSpec((None, 1, D_GMLP), lambda i: (j, 0, 0)),
        pl.BlockSpec((None, 1, D_GMLP), lambda i: (j, 0, 0)),
        pl.BlockSpec((None, 4, GMLP_CHUNK, GMLP_CHUNK), lambda i: (j, 0, 0, 0)),
        pl.BlockSpec((None, GMLP_CHUNK, 4), lambda i: (j, 0, 0)),
        pl.BlockSpec(memory_space=pl.ANY),
    ]
    args = [proj, proj, proj, pool_st, w_pool, pool_scale, gm_g, gm_b, w_spatial, bs_t, mixin]
    aliases = {len(args) - 1: 0}
    if pool_prev is not None:
        in_specs.append(pl.BlockSpec(memory_space=pl.ANY))
        args.append(pool_prev)
        aliases[len(args) - 1] = 1
    est = 4 * (4 * POOL_BUF * nb * D_POOL + 2 * (3 * nb * 1024 + 4 * 256 * 256 + 4 * 128 * 128
                                                + nb * 2048 + nb * 1024)) + (8 << 20)
    return pl.pallas_call(
        _odd_sample_kernel,
        grid=(1,),
        in_specs=in_specs,
        out_specs=[
            pl.BlockSpec((nb, D_MODEL), lambda i: (rb, 0)),
            pl.BlockSpec(st_blk, lambda i: (j, 0, 0, 0)),
            pl.BlockSpec((nb, D_GMLP), lambda i: (0, 0)),
        ],
        out_shape=[
            jax.ShapeDtypeStruct((M, D_MODEL), F32),
            jax.ShapeDtypeStruct(pool_st.shape, F32),
            jax.ShapeDtypeStruct((nb, D_GMLP), F32),
        ],
        input_output_aliases=aliases,
        compiler_params=pltpu.CompilerParams(
            dimension_semantics=("arbitrary",), vmem_limit_bytes=_vmem_limit(est)),
        name="odd_sample",
    )(*args)


def kernel(x_prompt, x_sample, state_conv, state_mlstm_C, state_mlstm_n, state_mlstm_m, state_pool,
           ln_g, ln_b, w_ffn_in, w_ffn_out, w_in_even, b_gates_even, w_conv, mh_norm_g, w_out_even,
           w_in_odd, w_pool, pool_scale, gm_ln_g, gm_ln_b, w_spatial, b_spatial, w_out_odd):
    n_even, n_odd = w_in_even.shape[0], w_in_odd.shape[0]
    x = jnp.concatenate([x_prompt.reshape(MP, D_MODEL), x_sample.reshape(DEC_BATCH, D_MODEL)], axis=0)

    ln_g3 = ln_g.reshape(DEPTH * 3, 1, D_MODEL)
    ln_b3 = ln_b.reshape(DEPTH * 3, 1, D_MODEL)
    w_even_t = jnp.swapaxes(w_in_even, 1, 2)
    w_gate_t = jnp.pad(w_even_t[:, EVEN_MAIN:, :], ((0, 0), (0, LANES - 2 * HEADS), (0, 0)))
    pool_st = jnp.swapaxes(state_pool, 1, 2)
    mh_g3 = mh_norm_g.reshape(n_even, 1, HEADS * DV)
    conv_st = state_conv.reshape(n_even, DEC_BATCH, (CONV_W - 1) * D_CONV)
    n_st = state_mlstm_n.reshape(n_even, DEC_BATCH, HEADS * DK)
    scale3 = pool_scale.reshape(n_odd, 1, D_POOL)
    gm_g3 = gm_ln_g.reshape(n_odd, 1, D_GMLP)
    gm_b3 = gm_ln_b.reshape(n_odd, 1, D_GMLP)
    bs_t = jnp.swapaxes(b_spatial, 1, 2)

    conv_p, conv_s, c_p, n_p, n_s, m_p, m_s = [], [], [], [], [], [], []
    pool_p, gv_p, gv_s = [], [], []
    c_s = None
    pool_s = None

    for layer in range(DEPTH):
        j = layer // 2
        x = _ffn(x, w_ffn_in, w_ffn_out, ln_g3, ln_b3, layer, 0, 3 * layer)
        if layer % 2 == 0:
            proj, gates = _proj(x, w_even_t, j, 0, EVEN_MAIN, "proj_even", w_transposed=True,
                                w_gate_t=w_gate_t)
            proj = _proj(x, w_even_t, j, 1, EVEN_MAIN, "proj_even", w_transposed=True, prev=proj)
            mixin, cv, cc, nn, mm = _even_prompt(proj, gates, w_conv, b_gates_even, mh_g3, j)
            conv_p.append(cv)
            c_p.append(cc)
            n_p.append(nn)
            m_p.append(mm.reshape(BATCH, HEADS))
            qk_cols = _qk_cols(x, w_even_t, j)
            c_s, num = _even_sample_state(state_mlstm_C, qk_cols, proj, gates, state_mlstm_m,
                                          b_gates_even, j, c_s)
            mixin, cvs, nns, mms = _even_sample(proj, gates, num, conv_st, n_st, state_mlstm_m,
                                                w_conv, b_gates_even, mh_g3, mixin, j)
            conv_s.append(cvs.reshape(DEC_BATCH, CONV_W - 1, D_CONV))
            n_s.append(nns.reshape(DEC_BATCH, HEADS, DK))
            m_s.append(mms)
            x = _outproj(x, mixin, w_out_even, ln_g3, ln_b3, j, 3 * layer + 1)
        else:
            proj = _proj(x, w_in_odd, j, 0, ODD_IN, "proj_odd")
            mixin, pp, gv = _odd_prompt(proj, w_pool, scale3, gm_g3, gm_b3, w_spatial, bs_t, j)
            pool_p.append(pp)
            gv_p.append(gv)
            mixin, pool_s, gvs = _odd_sample(proj, pool_st, w_pool, scale3, gm_g3, gm_b3, w_spatial, bs_t,
                                             mixin, pool_s, j)
            gv_s.append(gvs.reshape(DEC_BATCH, 1, D_GMLP))
            x = _outproj(x, mixin, w_out_odd, ln_g3, ln_b3, j, 3 * layer + 1)
        x = _ffn(x, w_ffn_in, w_ffn_out, ln_g3, ln_b3, layer, 1, 3 * layer + 2,
                 split_out=layer == DEPTH - 1)

    y_prompt = x[0].reshape(BATCH, SEQ, D_MODEL)
    y_sample = x[1].reshape(DEC_BATCH, 1, D_MODEL)
    return (y_prompt, y_sample,
            jnp.stack(conv_p), jnp.stack(conv_s),
            jnp.stack(c_p), c_s,
            jnp.stack(n_p), jnp.stack(n_s),
            jnp.stack(m_p), jnp.stack(m_s),
            jnp.stack(pool_p), jnp.swapaxes(pool_s, 1, 2),
            jnp.stack(gv_p), jnp.stack(gv_s))
```

```python
import functools

import jax
import jax.numpy as jnp
from jax import lax
from jax.experimental import pallas as pl
from jax.experimental.pallas import tpu as pltpu

F32 = jnp.float32

D_MODEL = 2048
BATCH = 4
SEQ = 2048
DEPTH = 4
DEC_BATCH = 128
PAST_LEN = 16384
D_FF = 5632
D_CONV = 1024
CONV_W = 3
HEADS = 4
DK = 128
DV = 256
D_POOL = 1024
POOL_WINDOWS = (2, 4, 8, 16)
POOL_GW = 256
POOL_BUF = 15
D_GMLP = 1024
GMLP_GW = 256
GMLP_CHUNK = 128
ALPHA = (2 * DEPTH) ** 0.25
LN_EPS = 1e-5
EVEN_MAIN = 3 * D_CONV + 2 * HEADS * DK + 2 * HEADS * DV
ODD_IN = D_POOL + 2 * D_GMLP

MP = BATCH * SEQ
M = MP + DEC_BATCH

LANES = 128
SUBLANES = 8
VMEM_BYTES_V7X = 64 * 1024 * 1024

TM = 832
TF = 512
TFS = 256
TN = 512
TMP = 520
PROJ_COLS = 3072
TMO = 416
LC = 256
RE = 256
RO = 256
SB = 16
HIST = 16


def _vmem_limit(nbytes):
    return int(min(VMEM_BYTES_V7X - 4 * 1024 * 1024, nbytes))


def _layer_norm(y, g, b, eps=LN_EPS):
    mu = jnp.mean(y, axis=-1, keepdims=True)
    yc = y - mu
    var = jnp.mean(yc * yc, axis=-1, keepdims=True)
    out = yc * lax.rsqrt(var + eps) * g
    if b is not None:
        out = out + b
    return out


def _log_sigmoid(x):
    return -(jnp.maximum(-x, 0.0) + jnp.log1p(jnp.exp(-jnp.abs(x))))


def _dot(a, b):
    return jnp.dot(a, b, preferred_element_type=F32)


def _dot_nt(a, b):
    return lax.dot_general(a, b, (((1,), (1,)), ((), ())), preferred_element_type=F32)


def _ffn_kernel(x_ref, wg_ref, wu_ref, wo_ref, g_ref, b_ref, o_ref, *maybe_sample_ref, nf):
    f = pl.program_id(1)

    def accumulate(first_chunk):
        x = x_ref[...]
        for c in range(TF // TFS):
            c0, c1 = c * TFS, (c + 1) * TFS
            gate = _dot(x, wg_ref[:, c0:c1])
            up = _dot(x, wu_ref[:, c0:c1])
            h = gate * jax.nn.sigmoid(gate) * up
            for n in range(D_MODEL // TN):
                n0, n1 = n * TN, (n + 1) * TN
                part = _dot(h, wo_ref[c0:c1, n0:n1])
                if first_chunk and c == 0:
                    o_ref[:, n0:n1] = (2.0 * ALPHA) * x_ref[:, n0:n1] + part
                else:
                    o_ref[:, n0:n1] += part

    @pl.when(f == 0)
    def _():
        accumulate(True)

    @pl.when(f > 0)
    def _():
        accumulate(False)

    @pl.when(f == nf - 1)
    def _():
        o_ref[...] = _layer_norm(o_ref[...], g_ref[...], b_ref[...], eps=4.0 * LN_EPS)

    if maybe_sample_ref:
        @pl.when((f == nf - 1) & (pl.program_id(0) == M // TM - 1))
        def _():
            maybe_sample_ref[0][...] = o_ref[TM - DEC_BATCH:, :]


def _ffn(x, w_ffn_in, w_ffn_out, ln_g, ln_b, layer, which, ln_idx, split_out=False):
    nf = D_FF // TF
    est = 4 * (4 * TM * D_MODEL + 2 * 3 * D_MODEL * TF + 4 * TM * TFS + TM * TN) + (4 << 20)
    out_specs = pl.BlockSpec((TM, D_MODEL), lambda i, f: (i, 0))
    out_shape = jax.ShapeDtypeStruct((M, D_MODEL), F32)
    if split_out:
        out_specs = [out_specs, pl.BlockSpec((DEC_BATCH, D_MODEL), lambda i, f: (0, 0))]
        out_shape = [jax.ShapeDtypeStruct((MP, D_MODEL), F32),
                     jax.ShapeDtypeStruct((DEC_BATCH, D_MODEL), F32)]
    return pl.pallas_call(
        functools.partial(_ffn_kernel, nf=nf),
        grid=(M // TM, nf),
        in_specs=[
            pl.BlockSpec((TM, D_MODEL), lambda i, f: (i, 0)),
            pl.BlockSpec((None, None, D_MODEL, TF), lambda i, f: (layer, which, 0, f)),
            pl.BlockSpec((None, None, D_MODEL, TF), lambda i, f: (layer, which, 0, nf + f)),
            pl.BlockSpec((None, None, TF, D_MODEL), lambda i, f: (layer, which, f, 0)),
            pl.BlockSpec((None, 1, D_MODEL), lambda i, f: (ln_idx, 0, 0)),
            pl.BlockSpec((None, 1, D_MODEL), lambda i, f: (ln_idx, 0, 0)),
        ],
        out_specs=out_specs,
        out_shape=out_shape,
        compiler_params=pltpu.CompilerParams(
            dimension_semantics=("arbitrary", "arbitrary"), vmem_limit_bytes=_vmem_limit(est)),
        name="ffn_ln",
    )(x, w_ffn_in, w_ffn_in, w_ffn_out, ln_g, ln_b)


def _proj_kernel(x_ref, w_ref, *rest, w_transposed, with_gates, aliased):
    rest = rest[1:] if aliased else rest
    if with_gates:
        wgate_ref, o_ref, gate_ref = rest
        gate_ref[...] = _dot_nt(x_ref[...], wgate_ref[...])
    else:
        (o_ref,) = rest
    for n in range(PROJ_COLS // TN):
        lo, hi = n * TN, (n + 1) * TN
        if w_transposed:
            o_ref[:, lo:hi] = _dot_nt(x_ref[...], w_ref[lo:hi, :])
        else:
            o_ref[:, lo:hi] = _dot(x_ref[...], w_ref[:, lo:hi])


def _proj(x, w, layer_idx, col_block, n_total, name, w_transposed=False, w_gate_t=None, prev=None):
    est = 4 * (PROJ_COLS * D_MODEL + 2 * TMP * D_MODEL + 2 * TMP * PROJ_COLS + 2 * TMP * TN) + (4 << 20)
    if w_transposed:
        w_spec = pl.BlockSpec((None, PROJ_COLS, D_MODEL), lambda i: (layer_idx, col_block, 0),
                              pipeline_mode=pl.Buffered(1))
    else:
        w_spec = pl.BlockSpec((None, D_MODEL, PROJ_COLS), lambda i: (layer_idx, 0, col_block),
                              pipeline_mode=pl.Buffered(1))
    in_specs = [pl.BlockSpec((TMP, D_MODEL), lambda i: (i, 0)), w_spec]
    args = [x, w]
    aliases = {}
    if prev is not None:
        in_specs.append(pl.BlockSpec(memory_space=pl.ANY))
        args.append(prev)
        aliases = {2: 0}
    out_specs = pl.BlockSpec((TMP, PROJ_COLS), lambda i: (i, col_block))
    out_shape = jax.ShapeDtypeStruct((M, n_total), F32)
    if w_gate_t is not None:
        in_specs.append(pl.BlockSpec((None, LANES, D_MODEL), lambda i: (layer_idx, 0, 0)))
        args.append(w_gate_t)
        out_specs = [out_specs, pl.BlockSpec((TMP, LANES), lambda i: (i, 0))]
        out_shape = [out_shape, jax.ShapeDtypeStruct((M, LANES), F32)]
        est += 4 * (2 * LANES * D_MODEL + 3 * TMP * LANES)
    return pl.pallas_call(
        functools.partial(_proj_kernel, w_transposed=w_transposed, with_gates=w_gate_t is not None,
                          aliased=prev is not None),
        grid=(M // TMP,),
        in_specs=in_specs,
        out_specs=out_specs,
        out_shape=out_shape,
        input_output_aliases=aliases,
        compiler_params=pltpu.CompilerParams(
            dimension_semantics=("parallel",), vmem_limit_bytes=_vmem_limit(est)),
        name=name,
    )(*args)


def _outproj_kernel(x_ref, a_ref, w_ref, g_ref, b_ref, o_ref):
    y = ALPHA * x_ref[...] + _dot(a_ref[...], w_ref[...])
    o_ref[...] = _layer_norm(y, g_ref[...], b_ref[...])


def _outproj(x, mix, w_out, ln_g, ln_b, layer_idx, ln_idx):
    est = 4 * (D_MODEL * D_MODEL + 9 * TMO * D_MODEL) + (4 << 20)
    return pl.pallas_call(
        _outproj_kernel,
        grid=(M // TMO,),
        in_specs=[
            pl.BlockSpec((TMO, D_MODEL), lambda i: (i, 0)),
            pl.BlockSpec((TMO, D_MODEL), lambda i: (i, 0)),
            pl.BlockSpec((None, D_MODEL, D_MODEL), lambda i: (layer_idx, 0, 0),
                         pipeline_mode=pl.Buffered(1)),
            pl.BlockSpec((None, 1, D_MODEL), lambda i: (ln_idx, 0, 0)),
            pl.BlockSpec((None, 1, D_MODEL), lambda i: (ln_idx, 0, 0)),
        ],
        out_specs=pl.BlockSpec((TMO, D_MODEL), lambda i: (i, 0)),
        out_shape=jax.ShapeDtypeStruct((M, D_MODEL), F32),
        compiler_params=pltpu.CompilerParams(
            dimension_semantics=("parallel",), vmem_limit_bytes=_vmem_limit(est)),
        name="outproj_ln",
    )(x, mix, w_out, ln_g, ln_b)


def _split3(x):
    h1 = x.astype(jnp.bfloat16).astype(F32)
    r = x - h1
    h2 = r.astype(jnp.bfloat16).astype(F32)
    return h1, h2, r - h2


def _even_prompt_kernel(bg_ref, cg_ref, xin_ref, q_ref, k_ref, v_ref, og_ref, gt_ref,
                        wc_ref, bgate_ref, mhg_ref,
                        mix_ref, conv_ref, c_out_ref, n_out_ref, m_out_ref,
                        cbuf, c_sc, n_sc, m_sc):
    c = pl.program_id(1)
    L = LC
    R = RE

    @pl.when(c == 0)
    def _():
        cbuf[0:SUBLANES, :] = jnp.zeros((SUBLANES, D_CONV), F32)
        c_sc[...] = jnp.zeros_like(c_sc)
        n_sc[...] = jnp.zeros_like(n_sc)
        m_sc[...] = jnp.zeros_like(m_sc)

    cx = cg_ref[...] * xin_ref[...]
    cbuf[SUBLANES:SUBLANES + R, :] = cx
    c1 = cbuf[SUBLANES - 1:SUBLANES - 1 + R, :]
    c2 = cbuf[SUBLANES - 2:SUBLANES - 2 + R, :]
    conv = c2 * wc_ref[0:1, :] + c1 * wc_ref[1:2, :] + cx * wc_ref[2:3, :]
    mix_ref[:, 0:D_CONV] = bg_ref[...] * conv
    conv_ref[...] = cbuf[SUBLANES + R - 2:SUBLANES + R, :]
    cbuf[0:SUBLANES, :] = cbuf[R:R + SUBLANES, :]

    row = lax.broadcasted_iota(jnp.int32, (L, L), 0)
    col = lax.broadcasted_iota(jnp.int32, (L, L), 1)
    causal = row >= col
    tril = causal.astype(F32)
    c_state = [c_sc[h] for h in range(HEADS)]
    n_state = [n_sc[h:h + 1, :] for h in range(HEADS)]
    m_state = [m_sc[h, 0:1, 0:1] for h in range(HEADS)]

    for s in range(R // L):
        r0, r1 = s * L, (s + 1) * L
        gt = gt_ref[r0:r1, :]
        li_all = gt[:, 0:HEADS] + bgate_ref[0:1, :]
        lf_all = _log_sigmoid(gt[:, HEADS:2 * HEADS] + bgate_ref[1:2, :])
        f1, f2, f3 = _split3(lf_all)
        b_all = _dot(tril, f1) + _dot(tril, f2) + _dot(tril, f3)
        z = jnp.concatenate([li_all, b_all, jnp.zeros((L, LANES - 2 * HEADS), F32)], axis=1)
        zt = z.T

        for h in range(HEADS):
            li_c = li_all[:, h:h + 1]
            b_c = b_all[:, h:h + 1]
            li_r = zt[h:h + 1, :]
            b_r = zt[HEADS + h:HEADS + h + 1, :]
            m0 = m_state[h]
            e = jnp.where(causal, li_r - b_r, -jnp.inf)
            inter = b_c + m0
            m_t = jnp.maximum(inter, b_c + jnp.max(e, axis=-1, keepdims=True))
            a_int = jnp.exp(inter - m_t)
            qh = q_ref[r0:r1, h * DK:(h + 1) * DK]
            kh = k_ref[r0:r1, h * DK:(h + 1) * DK] * (DK ** -0.5)
            vh = v_ref[r0:r1, h * DV:(h + 1) * DV]
            w = jnp.exp(e + (b_c - m_t)) * _dot_nt(qh, kh)
            ch = c_state[h]
            n_row = n_state[h]
            num = a_int * _dot(qh, ch) + _dot(w, vh)
            w_tiles = sum(w[:, t * LANES:(t + 1) * LANES] for t in range(L // LANES))
            w_rows = jnp.sum(w_tiles, axis=-1, keepdims=True)
            den = a_int * jnp.sum(qh * n_row, axis=-1, keepdims=True) + w_rows
            hh = num / jnp.maximum(jnp.abs(den), jnp.exp(-m_t))
            m_new = m_t[L - 1:L, :]
            b_last = b_c[L - 1:L, :]
            w_end = jnp.exp(b_last - b_c + li_c - m_new)
            decay = jnp.exp(b_last + m0 - m_new)
            wk = w_end * kh
            c_state[h] = decay * ch + _dot(wk.T, vh)
            n_state[h] = decay * n_row + jnp.sum(wk, axis=0, keepdims=True)
            m_state[h] = m_new
            hn = _layer_norm(hh, mhg_ref[0:1, h * DV:(h + 1) * DV], None)
            og = og_ref[r0:r1, h * DV:(h + 1) * DV]
            mix_ref[r0:r1, D_CONV + h * DV:D_CONV + (h + 1) * DV] = hn * jax.nn.sigmoid(og)

    for h in range(HEADS):
        c_sc[h] = c_state[h]
        n_sc[h:h + 1, :] = n_state[h]
        m_sc[h] = jnp.broadcast_to(m_state[h], (SUBLANES, LANES))
        c_out_ref[h] = c_state[h]
        n_out_ref[h:h + 1, :] = n_state[h]
        m_out_ref[0:1, h:h + 1] = m_state[h]


def _even_prompt(proj, gates, w_conv, b_gates, mh_g, j):
    nck = SEQ // RE
    r = lambda b, c: b * nck + c
    est = 4 * (2 * (5 * RE * 1024 + 2 * RE * 512 + RE * 128) + 2 * RE * 2048 + 4 * RE * 1024
               + 3 * HEADS * DK * DV + 24 * LC * LC) + (8 << 20)
    return pl.pallas_call(
        _even_prompt_kernel,
        grid=(BATCH, nck),
        in_specs=[
            pl.BlockSpec((RE, D_CONV), lambda b, c: (r(b, c), 0)),
            pl.BlockSpec((RE, D_CONV), lambda b, c: (r(b, c), 1)),
            pl.BlockSpec((RE, D_CONV), lambda b, c: (r(b, c), 2)),
            pl.BlockSpec((RE, HEADS * DK), lambda b, c: (r(b, c), 6)),
            pl.BlockSpec((RE, HEADS * DK), lambda b, c: (r(b, c), 7)),
            pl.BlockSpec((RE, HEADS * DV), lambda b, c: (r(b, c), 4)),
            pl.BlockSpec((RE, HEADS * DV), lambda b, c: (r(b, c), 5)),
            pl.BlockSpec((RE, LANES), lambda b, c: (r(b, c), 0)),
            pl.BlockSpec((None, CONV_W, D_CONV), lambda b, c: (j, 0, 0)),
            pl.BlockSpec((None, 2, HEADS), lambda b, c: (j, 0, 0)),
            pl.BlockSpec((None, 1, HEADS * DV), lambda b, c: (j, 0, 0)),
        ],
        out_specs=[
            pl.BlockSpec((RE, D_MODEL), lambda b, c: (r(b, c), 0)),
            pl.BlockSpec((None, CONV_W - 1, D_CONV), lambda b, c: (b, 0, 0)),
            pl.BlockSpec((None, HEADS, DK, DV), lambda b, c: (b, 0, 0, 0)),
            pl.BlockSpec((None, HEADS, DK), lambda b, c: (b, 0, 0)),
            pl.BlockSpec((None, 1, HEADS), lambda b, c: (b, 0, 0)),
        ],
        out_shape=[
            jax.ShapeDtypeStruct((M, D_MODEL), F32),
            jax.ShapeDtypeStruct((BATCH, CONV_W - 1, D_CONV), F32),
            jax.ShapeDtypeStruct((BATCH, HEADS, DK, DV), F32),
            jax.ShapeDtypeStruct((BATCH, HEADS, DK), F32),
            jax.ShapeDtypeStruct((BATCH, 1, HEADS), F32),
        ],
        scratch_shapes=[
            pltpu.VMEM((RE + SUBLANES, D_CONV), F32),
            pltpu.VMEM((HEADS, DK, DV), F32),
            pltpu.VMEM((HEADS, DK), F32),
            pltpu.VMEM((HEADS, SUBLANES, LANES), F32),
        ],
        compiler_params=pltpu.CompilerParams(
            dimension_semantics=("arbitrary", "arbitrary"), vmem_limit_bytes=_vmem_limit(est)),
        name="even_prompt",
    )(proj, proj, proj, proj, proj, proj, proj, gates, w_conv, b_gates, mh_g)


def _sample_gates(gt, bgate_ref, m):
    li = gt[:, 0:HEADS] + bgate_ref[0:1, :]
    lf = _log_sigmoid(gt[:, HEADS:2 * HEADS] + bgate_ref[1:2, :])
    inter = lf + m
    m_t = jnp.maximum(inter, li)
    return jnp.exp(inter - m_t), jnp.exp(li - m_t), m_t


def _qk_cols_kernel(x_ref, w_ref, o_ref):
    t = _dot_nt(w_ref[...], x_ref[...])
    for s in range(DEC_BATCH // SB):
        o_ref[s] = t[:, s * SB:(s + 1) * SB]


def _qk_cols(x, w_even_t, j):
    nqk = 2 * HEADS * DK
    est = 4 * (2 * DEC_BATCH * D_MODEL + 2 * nqk * D_MODEL + 3 * nqk * LANES
               + 2 * (DEC_BATCH // SB) * nqk * LANES) + (4 << 20)
    return pl.pallas_call(
        _qk_cols_kernel,
        grid=(1,),
        in_specs=[
            pl.BlockSpec((DEC_BATCH, D_MODEL), lambda i: (MP // DEC_BATCH, 0)),
            pl.BlockSpec((None, nqk, D_MODEL), lambda i: (j, 3 * D_CONV // nqk, 0)),
        ],
        out_specs=pl.BlockSpec((DEC_BATCH // SB, nqk, SB), lambda i: (0, 0, 0)),
        out_shape=jax.ShapeDtypeStruct((DEC_BATCH // SB, nqk, SB), F32),
        compiler_params=pltpu.CompilerParams(
            dimension_semantics=("arbitrary",), vmem_limit_bytes=_vmem_limit(est)),
        name="qk_cols",
    )(x, w_even_t)


def _even_sample_state_kernel(c_ref, qk_ref, v_ref, gt_ref, m_ref, bgate_ref, *rest):
    c_out_ref, num_ref = rest[-2], rest[-1]
    a, wgt, _ = _sample_gates(gt_ref[...], bgate_ref, m_ref[...])
    for bi in range(SB):
        for h in range(HEADS):
            a_s = a[bi:bi + 1, h:h + 1]
            w_s = wgt[bi:bi + 1, h:h + 1]
            ch = c_ref[bi, h]
            qc = qk_ref[h * DK:(h + 1) * DK, bi:bi + 1]
            kc = qk_ref[(HEADS + h) * DK:(HEADS + h + 1) * DK, bi:bi + 1] * (DK ** -0.5)
            vr = v_ref[bi:bi + 1, h * DV:(h + 1) * DV]
            c_out_ref[bi, h] = a_s * ch + (w_s * kc) * vr
            num_ref[bi:bi + 1, h * DV:(h + 1) * DV] = jnp.sum(qc * ch, axis=0, keepdims=True)


def _even_sample_state(state_c, qk_cols, proj, gates, state_m, b_gates, j, c_prev):
    blk5 = (None, SB, HEADS, DK, DV)
    rb = MP // SB
    in_specs = [
        pl.BlockSpec(blk5, lambda i: (j, i, 0, 0, 0)),
        pl.BlockSpec((None, 2 * HEADS * DK, SB), lambda i: (i, 0, 0)),
        pl.BlockSpec((SB, HEADS * DV), lambda i: (rb + i, 4)),
        pl.BlockSpec((SB, LANES), lambda i: (rb + i, 0)),
        pl.BlockSpec((None, SB, HEADS), lambda i: (j, i, 0)),
        pl.BlockSpec((None, 2, HEADS), lambda i: (j, 0, 0)),
    ]
    args = [state_c, qk_cols, proj, gates, state_m, b_gates]
    aliases = {}
    if c_prev is not None:
        in_specs.append(pl.BlockSpec(memory_space=pl.ANY))
        args.append(c_prev)
        aliases = {len(args) - 1: 0}
    est = 4 * (4 * SB * HEADS * DK * DV + 4 * HEADS * DK * LANES) + (8 << 20)
    return pl.pallas_call(
        _even_sample_state_kernel,
        grid=(DEC_BATCH // SB,),
        in_specs=in_specs,
        out_specs=[
            pl.BlockSpec(blk5, lambda i: (j, i, 0, 0, 0)),
            pl.BlockSpec((SB, HEADS * DV), lambda i: (i, 0)),
        ],
        out_shape=[
            jax.ShapeDtypeStruct(state_c.shape, F32),
            jax.ShapeDtypeStruct((DEC_BATCH, HEADS * DV), F32),
        ],
        input_output_aliases=aliases,
        compiler_params=pltpu.CompilerParams(
            dimension_semantics=("arbitrary",), vmem_limit_bytes=_vmem_limit(est)),
        name="even_sample_state",
    )(*args)


def _even_sample_kernel(bg_ref, cg_ref, xin_ref, q_ref, k_ref, v_ref, og_ref, gt_ref,
                        num_ref, cst_ref, n_ref, m_ref, wc_ref, bgate_ref, mhg_ref, mixin_hbm,
                        mix_ref, conv_ref, n_out_ref, m_out_ref):
    del mixin_hbm
    cx = cg_ref[...] * xin_ref[...]
    st0 = cst_ref[:, 0:D_CONV]
    st1 = cst_ref[:, D_CONV:2 * D_CONV]
    conv = st0 * wc_ref[0:1, :] + st1 * wc_ref[1:2, :] + cx * wc_ref[2:3, :]
    mix_ref[:, 0:D_CONV] = bg_ref[...] * conv
    conv_ref[:, 0:D_CONV] = st1
    conv_ref[:, D_CONV:2 * D_CONV] = cx

    a, wgt, m_t = _sample_gates(gt_ref[...], bgate_ref, m_ref[...])
    m_out_ref[...] = m_t
    floor = jnp.exp(-m_t)
    for h in range(HEADS):
        a_h = a[:, h:h + 1]
        w_h = wgt[:, h:h + 1]
        qh = q_ref[:, h * DK:(h + 1) * DK]
        kh = k_ref[:, h * DK:(h + 1) * DK] * (DK ** -0.5)
        vh = v_ref[:, h * DV:(h + 1) * DV]
        nh = n_ref[:, h * DK:(h + 1) * DK]
        wt = w_h * jnp.sum(qh * kh, axis=-1, keepdims=True)
        num = a_h * num_ref[:, h * DV:(h + 1) * DV] + wt * vh
        den = a_h * jnp.sum(qh * nh, axis=-1, keepdims=True) + wt
        hh = num / jnp.maximum(jnp.abs(den), floor[:, h:h + 1])
        n_out_ref[:, h * DK:(h + 1) * DK] = a_h * nh + w_h * kh
        hn = _layer_norm(hh, mhg_ref[0:1, h * DV:(h + 1) * DV], None)
        og = og_ref[:, h * DV:(h + 1) * DV]
        mix_ref[:, D_CONV + h * DV:D_CONV + (h + 1) * DV] = hn * jax.nn.sigmoid(og)


def _even_sample(proj, gates, num, conv_st, n_st, m_st, w_conv, b_gates, mh_g, mixin, j):
    nb = DEC_BATCH
    rb = MP // nb
    est = 4 * 2 * (5 * nb * 1024 + 2 * nb * 512 + nb * 128 + nb * 1024 + nb * 2048 + nb * 512
                   + nb * 2048 + nb * 2048 + nb * 512) + (8 << 20)
    return pl.pallas_call(
        _even_sample_kernel,
        grid=(1,),
        in_specs=[
            pl.BlockSpec((nb, D_CONV), lambda i: (rb, 0)),
            pl.BlockSpec((nb, D_CONV), lambda i: (rb, 1)),
            pl.BlockSpec((nb, D_CONV), lambda i: (rb, 2)),
            pl.BlockSpec((nb, HEADS * DK), lambda i: (rb, 6)),
            pl.BlockSpec((nb, HEADS * DK), lambda i: (rb, 7)),
            pl.BlockSpec((nb, HEADS * DV), lambda i: (rb, 4)),
            pl.BlockSpec((nb, HEADS * DV), lambda i: (rb, 5)),
            pl.BlockSpec((nb, LANES), lambda i: (rb, 0)),
            pl.BlockSpec((nb, HEADS * DV), lambda i: (0, 0)),
            pl.BlockSpec((None, nb, 2 * D_CONV), lambda i: (j, 0, 0)),
            pl.BlockSpec((None, nb, HEADS * DK), lambda i: (j, 0, 0)),
            pl.BlockSpec((None, nb, HEADS), lambda i: (j, 0, 0)),
            pl.BlockSpec((None, CONV_W, D_CONV), lambda i: (j, 0, 0)),
            pl.BlockSpec((None, 2, HEADS), lambda i: (j, 0, 0)),
            pl.BlockSpec((None, 1, HEADS * DV), lambda i: (j, 0, 0)),
            pl.BlockSpec(memory_space=pl.ANY),
        ],
        out_specs=[
            pl.BlockSpec((nb, D_MODEL), lambda i: (rb, 0)),
            pl.BlockSpec((nb, 2 * D_CONV), lambda i: (0, 0)),
            pl.BlockSpec((nb, HEADS * DK), lambda i: (0, 0)),
            pl.BlockSpec((nb, HEADS), lambda i: (0, 0)),
        ],
        out_shape=[
            jax.ShapeDtypeStruct((M, D_MODEL), F32),
            jax.ShapeDtypeStruct((nb, 2 * D_CONV), F32),
            jax.ShapeDtypeStruct((nb, HEADS * DK), F32),
            jax.ShapeDtypeStruct((nb, HEADS), F32),
        ],
        input_output_aliases={15: 0},
        compiler_params=pltpu.CompilerParams(
            dimension_semantics=("arbitrary",), vmem_limit_bytes=_vmem_limit(est)),
        name="even_sample",
    )(proj, proj, proj, proj, proj, proj, proj, gates, num, conv_st, n_st, m_st,
      w_conv, b_gates, mh_g, mixin)


def _gmlp_norm(u_raw, v_raw, g, b):
    u = jax.nn.gelu(u_raw)
    vn = _layer_norm(jax.nn.gelu(v_raw), g, b)
    return u, vn


def _odd_prompt_kernel(p_ref, u_ref, v_ref, wp_ref, sc_ref, gmg_ref, gmb_ref, ws_ref, bst_ref,
                       mix_ref, pool_ref, gv_ref, pbuf, sbuf):
    s = pl.program_id(1)
    R = RO

    @pl.when(s == 0)
    def _():
        pbuf[0:HIST, :] = jnp.zeros((HIST, D_POOL), F32)
        sbuf[0:SUBLANES, :] = jnp.zeros((SUBLANES, D_POOL), F32)

    p = p_ref[...]
    pbuf[HIST:HIST + R, :] = p
    pos = s * R + lax.broadcasted_iota(jnp.int32, (R, 1), 0)
    for g, w in enumerate(POOL_WINDOWS):
        lo, hi = g * POOL_GW, (g + 1) * POOL_GW
        cur = pbuf[0:HIST + R, lo:hi]
        d = 1
        while d < w:
            sbuf[SUBLANES:SUBLANES + HIST + R, lo:hi] = cur
            cur = cur + sbuf[SUBLANES - d:SUBLANES - d + HIST + R, lo:hi]
            d *= 2
        win = cur[HIST:, :]
        cnt = jnp.minimum(w, pos + 1).astype(F32)
        diff = win / cnt - p[:, lo:hi]
        mix_ref[:, lo:hi] = _dot(diff, wp_ref[g]) * sc_ref[0:1, lo:hi]
    pool_ref[...] = pbuf[HIST + R - POOL_BUF:HIST + R, :]
    pbuf[0:HIST, :] = pbuf[R:R + HIST, :]

    u, vn = _gmlp_norm(u_ref[...], v_ref[...], gmg_ref[...], gmb_ref[...])
    L = GMLP_CHUNK
    tril = lax.broadcasted_iota(jnp.int32, (L, L), 0) >= lax.broadcasted_iota(jnp.int32, (L, L), 1)
    for g in range(D_GMLP // GMLP_GW):
        lo, hi = g * GMLP_GW, (g + 1) * GMLP_GW
        ws = jnp.where(tril, ws_ref[g], 0.0)
        bcol = bst_ref[:, g:g + 1]
        for ck in range(R // L):
            r0, r1 = ck * L, (ck + 1) * L
            sv = _dot(ws, vn[r0:r1, lo:hi]) + bcol
            mix_ref[r0:r1, D_POOL + lo:D_POOL + hi] = u[r0:r1, lo:hi] * sv
    gv_ref[...] = vn[R - L:R, :]


def _odd_prompt(proj, w_pool, pool_scale, gm_g, gm_b, w_spatial, bs_t, j):
    nrb = SEQ // RO
    r = lambda b, s: b * nrb + s
    est = 4 * (2 * 3 * RO * 1024 + 2 * RO * 2048 + 2 * 4 * 256 * 256 + 2 * 4 * 128 * 128
               + (RO + HIST) * 1024 + 8 * RO * 1024) + (8 << 20)
    return pl.pallas_call(
        _odd_prompt_kernel,
        grid=(BATCH, nrb),
        in_specs=[
            pl.BlockSpec((RO, D_POOL), lambda b, s: (r(b, s), 0)),
            pl.BlockSpec((RO, D_GMLP), lambda b, s: (r(b, s), 1)),
            pl.BlockSpec((RO, D_GMLP), lambda b, s: (r(b, s), 2)),
            pl.BlockSpec((None, 4, POOL_GW, POOL_GW), lambda b, s: (j, 0, 0, 0)),
            pl.BlockSpec((None, 1, D_POOL), lambda b, s: (j, 0, 0)),
            pl.BlockSpec((None, 1, D_GMLP), lambda b, s: (j, 0, 0)),
            pl.BlockSpec((None, 1, D_GMLP), lambda b, s: (j, 0, 0)),
            pl.BlockSpec((None, 4, GMLP_CHUNK, GMLP_CHUNK), lambda b, s: (j, 0, 0, 0)),
            pl.BlockSpec((None, GMLP_CHUNK, 4), lambda b, s: (j, 0, 0)),
        ],
        out_specs=[
            pl.BlockSpec((RO, D_MODEL), lambda b, s: (r(b, s), 0)),
            pl.BlockSpec((None, POOL_BUF, D_POOL), lambda b, s: (b, 0, 0)),
            pl.BlockSpec((None, GMLP_CHUNK, D_GMLP), lambda b, s: (b, 0, 0)),
        ],
        out_shape=[
            jax.ShapeDtypeStruct((M, D_MODEL), F32),
            jax.ShapeDtypeStruct((BATCH, POOL_BUF, D_POOL), F32),
            jax.ShapeDtypeStruct((BATCH, GMLP_CHUNK, D_GMLP), F32),
        ],
        scratch_shapes=[pltpu.VMEM((RO + HIST, D_POOL), F32),
                        pltpu.VMEM((SUBLANES + RO + HIST, D_POOL), F32)],
        compiler_params=pltpu.CompilerParams(
            dimension_semantics=("arbitrary", "arbitrary"), vmem_limit_bytes=_vmem_limit(est)),
        name="odd_prompt",
    )(proj, proj, proj, w_pool, pool_scale, gm_g, gm_b, w_spatial, bs_t)


def _odd_sample_kernel(p_ref, u_ref, v_ref, st_ref, wp_ref, sc_ref, gmg_ref, gmb_ref, ws_ref, bst_ref,
                       *rest):
    mix_ref, pool_ref, gv_ref = rest[-3:]
    p = p_ref[...]
    for r in range(POOL_BUF - 1):
        pool_ref[r] = st_ref[r + 1]
    pool_ref[POOL_BUF - 1] = p
    for g, w in enumerate(POOL_WINDOWS):
        lo, hi = g * POOL_GW, (g + 1) * POOL_GW
        win = p[:, lo:hi]
        for jj in range(1, w):
            win = win + st_ref[POOL_BUF - jj, :, lo:hi]
        cnt = float(min(w, PAST_LEN + 1))
        diff = win / cnt - p[:, lo:hi]
        mix_ref[:, lo:hi] = _dot(diff, wp_ref[g]) * sc_ref[0:1, lo:hi]
    u, vn = _gmlp_norm(u_ref[...], v_ref[...], gmg_ref[...], gmb_ref[...])
    gv_ref[...] = vn
    for g in range(D_GMLP // GMLP_GW):
        lo, hi = g * GMLP_GW, (g + 1) * GMLP_GW
        sv = ws_ref[g, 0:1, 0:1] * vn[:, lo:hi] + bst_ref[0:1, g:g + 1]
        mix_ref[:, D_POOL + lo:D_POOL + hi] = u[:, lo:hi] * sv


def _odd_sample(proj, pool_st, w_pool, pool_scale, gm_g, gm_b, w_spatial, bs_t, mixin, pool_prev, j):
    nb = DEC_BATCH
    rb = MP // nb
    st_blk = (None, POOL_BUF, nb, D_POOL)
    in_specs = [
        pl.BlockSpec((nb, D_POOL), lambda i: (rb, 0)),
        pl.BlockSpec((nb, D_GMLP), lambda i: (rb, 1)),
        pl.BlockSpec((nb, D_GMLP), lambda i: (rb, 2)),
        pl.BlockSpec(st_blk, lambda i: (j, 0, 0, 0)),
        pl.BlockSpec((None, 4, POOL_GW, POOL_GW), lambda i: (j, 0, 0, 0)),
        pl.BlockSpec((None, 1, D_POOL), lambda i: (j, 0, 0)),
        pl.BlockSpec((None, 1, D_GMLP), lambda i: (j, 0, 0)),
        pl.BlockSpec((None, 1, D_GMLP), lambda i: (j, 0, 0)),
        pl.BlockSpec((None, 4, GMLP_CHUNK, GMLP_CHUNK), lambda i: (j, 0, 0, 0)),
        pl.BlockSpec((None, GMLP_CHUNK, 4), lambda i: (j, 0, 0)),
        pl.BlockSpec(memory_space=pl.ANY),
    ]
    args = [proj, proj, proj, pool_st, w_pool, pool_scale, gm_g, gm_b, w_spatial, bs_t, mixin]
    aliases = {len(args) - 1: 0}
    if pool_prev is not None:
        in_specs.append(pl.BlockSpec(memory_space=pl.ANY))
        args.append(pool_prev)
        aliases[len(args) - 1] = 1
    est = 4 * (4 * POOL_BUF * nb * D_POOL + 2 * (3 * nb * 1024 + 4 * 256 * 256 + 4 * 128 * 128
                                                + nb * 2048 + nb * 1024)) + (8 << 20)
    return pl.pallas_call(
        _odd_sample_kernel,
        grid=(1,),
        in_specs=in_specs,
        out_specs=[
            pl.BlockSpec((nb, D_MODEL), lambda i: (rb, 0)),
            pl.BlockSpec(st_blk, lambda i: (j, 0, 0, 0)),
            pl.BlockSpec((nb, D_GMLP), lambda i: (0, 0)),
        ],
        out_shape=[
            jax.ShapeDtypeStruct((M, D_MODEL), F32),
            jax.ShapeDtypeStruct(pool_st.shape, F32),
            jax.ShapeDtypeStruct((nb, D_GMLP), F32),
        ],
        input_output_aliases=aliases,
        compiler_params=pltpu.CompilerParams(
            dimension_semantics=("arbitrary",), vmem_limit_bytes=_vmem_limit(est)),
        name="odd_sample",
    )(*args)


def kernel(x_prompt, x_sample, state_conv, state_mlstm_C, state_mlstm_n, state_mlstm_m, state_pool,
           ln_g, ln_b, w_ffn_in, w_ffn_out, w_in_even, b_gates_even, w_conv, mh_norm_g, w_out_even,
           w_in_odd, w_pool, pool_scale, gm_ln_g, gm_ln_b, w_spatial, b_spatial, w_out_odd):
    n_even, n_odd = w_in_even.shape[0], w_in_odd.shape[0]
    x = jnp.concatenate([x_prompt.reshape(MP, D_MODEL), x_sample.reshape(DEC_BATCH, D_MODEL)], axis=0)

    ln_g3 = ln_g.reshape(DEPTH * 3, 1, D_MODEL)
    ln_b3 = ln_b.reshape(DEPTH * 3, 1, D_MODEL)
    w_even_t = jnp.swapaxes(w_in_even, 1, 2)
    w_gate_t = jnp.pad(w_even_t[:, EVEN_MAIN:, :], ((0, 0), (0, LANES - 2 * HEADS), (0, 0)))
    pool_st = jnp.swapaxes(state_pool, 1, 2)
    mh_g3 = mh_norm_g.reshape(n_even, 1, HEADS * DV)
    conv_st = state_conv.reshape(n_even, DEC_BATCH, (CONV_W - 1) * D_CONV)
    n_st = state_mlstm_n.reshape(n_even, DEC_BATCH, HEADS * DK)
    scale3 = pool_scale.reshape(n_odd, 1, D_POOL)
    gm_g3 = gm_ln_g.reshape(n_odd, 1, D_GMLP)
    gm_b3 = gm_ln_b.reshape(n_odd, 1, D_GMLP)
    bs_t = jnp.swapaxes(b_spatial, 1, 2)

    conv_p, conv_s, c_p, n_p, n_s, m_p, m_s = [], [], [], [], [], [], []
    pool_p, gv_p, gv_s = [], [], []
    c_s = None
    pool_s = None

    for layer in range(DEPTH):
        j = layer // 2
        x = _ffn(x, w_ffn_in, w_ffn_out, ln_g3, ln_b3, layer, 0, 3 * layer)
        if layer % 2 == 0:
            proj, gates = _proj(x, w_even_t, j, 0, EVEN_MAIN, "proj_even", w_transposed=True,
                                w_gate_t=w_gate_t)
            proj = _proj(x, w_even_t, j, 1, EVEN_MAIN, "proj_even", w_transposed=True, prev=proj)
            mixin, cv, cc, nn, mm = _even_prompt(proj, gates, w_conv, b_gates_even, mh_g3, j)
            conv_p.append(cv)
            c_p.append(cc)
            n_p.append(nn)
            m_p.append(mm.reshape(BATCH, HEADS))
            qk_cols = _qk_cols(x, w_even_t, j)
            c_s, num = _even_sample_state(state_mlstm_C, qk_cols, proj, gates, state_mlstm_m,
                                          b_gates_even, j, c_s)
            mixin, cvs, nns, mms = _even_sample(proj, gates, num, conv_st, n_st, state_mlstm_m,
                                                w_conv, b_gates_even, mh_g3, mixin, j)
            conv_s.append(cvs.reshape(DEC_BATCH, CONV_W - 1, D_CONV))
            n_s.append(nns.reshape(DEC_BATCH, HEADS, DK))
            m_s.append(mms)
            x = _outproj(x, mixin, w_out_even, ln_g3, ln_b3, j, 3 * layer + 1)
        else:
            proj = _proj(x, w_in_odd, j, 0, ODD_IN, "proj_odd")
            mixin, pp, gv = _odd_prompt(proj, w_pool, scale3, gm_g3, gm_b3, w_spatial, bs_t, j)
            pool_p.append(pp)
            gv_p.append(gv)
            mixin, pool_s, gvs = _odd_sample(proj, pool_st, w_pool, scale3, gm_g3, gm_b3, w_spatial, bs_t,
                                             mixin, pool_s, j)
            gv_s.append(gvs.reshape(DEC_BATCH, 1, D_GMLP))
            x = _outproj(x, mixin, w_out_odd, ln_g3, ln_b3, j, 3 * layer + 1)
        x = _ffn(x, w_ffn_in, w_ffn_out, ln_g3, ln_b3, layer, 1, 3 * layer + 2,
                 split_out=layer == DEPTH - 1)

    y_prompt = x[0].reshape(BATCH, SEQ, D_MODEL)
    y_sample = x[1].reshape(DEC_BATCH, 1, D_MODEL)
    return (y_prompt, y_sample,
            jnp.stack(conv_p), jnp.stack(conv_s),
            jnp.stack(c_p), c_s,
            jnp.stack(n_p), jnp.stack(n_s),
            jnp.stack(m_p), jnp.stack(m_s),
            jnp.stack(pool_p), jnp.swapaxes(pool_s, 1, 2),
            jnp.stack(gv_p), jnp.stack(gv_s))
```

```python
import functools

import jax
import jax.numpy as jnp
from jax import lax
from jax.experimental import pallas as pl
from jax.experimental.pallas import tpu as pltpu

F32 = jnp.float32

D_MODEL = 2048
BATCH = 4
SEQ = 2048
DEPTH = 4
DEC_BATCH = 128
PAST_LEN = 16384
D_FF = 5632
D_CONV = 1024
CONV_W = 3
HEADS = 4
DK = 128
DV = 256
D_POOL = 1024
POOL_WINDOWS = (2, 4, 8, 16)
POOL_GW = 256
POOL_BUF = 15
D_GMLP = 1024
GMLP_GW = 256
GMLP_CHUNK = 128
ALPHA = (2 * DEPTH) ** 0.25
LN_EPS = 1e-5
EVEN_MAIN = 3 * D_CONV + 2 * HEADS * DK + 2 * HEADS * DV
ODD_IN = D_POOL + 2 * D_GMLP

MP = BATCH * SEQ
M = MP + DEC_BATCH

LANES = 128
SUBLANES = 8
VMEM_BYTES_V7X = 64 * 1024 * 1024
VMEM_RESERVED_BYTES = 4 * 1024 * 1024
VMEM_TEMP_BYTES = 8 * 1024 * 1024

TM = 832
TF = 512
TFS = 256
TN = 512
TMP = 520
PROJ_COLS = 3072
TMO = 416
LC = 256
RE = 256
RO = 256
SB = 16
HIST = 16


def _vmem_limit(block_bytes):
    return int(min(VMEM_BYTES_V7X - VMEM_RESERVED_BYTES, block_bytes + VMEM_TEMP_BYTES))


def _layer_norm(y, g, b, eps=LN_EPS):
    mu = jnp.mean(y, axis=-1, keepdims=True)
    yc = y - mu
    var = jnp.mean(yc * yc, axis=-1, keepdims=True)
    out = yc * lax.rsqrt(var + eps) * g
    if b is not None:
        out = out + b
    return out


def _log_sigmoid(x):
    return -(jnp.maximum(-x, 0.0) + jnp.log1p(jnp.exp(-jnp.abs(x))))


def _dot(a, b):
    return jnp.dot(a, b, preferred_element_type=F32)


def _dot_nt(a, b):
    return lax.dot_general(a, b, (((1,), (1,)), ((), ())), preferred_element_type=F32)


def _ffn_kernel(x_ref, wg_ref, wu_ref, wo_ref, g_ref, b_ref, o_ref, *maybe_sample_ref, nf):
    f = pl.program_id(1)

    def accumulate(first_chunk):
        x = x_ref[...]
        for c in range(TF // TFS):
            c0, c1 = c * TFS, (c + 1) * TFS
            gate = _dot(x, wg_ref[:, c0:c1])
            up = _dot(x, wu_ref[:, c0:c1])
            h = gate * jax.nn.sigmoid(gate) * up
            for n in range(D_MODEL // TN):
                n0, n1 = n * TN, (n + 1) * TN
                part = _dot(h, wo_ref[c0:c1, n0:n1])
                if first_chunk and c == 0:
                    o_ref[:, n0:n1] = (2.0 * ALPHA) * x_ref[:, n0:n1] + part
                else:
                    o_ref[:, n0:n1] += part

    @pl.when(f == 0)
    def _():
        accumulate(True)

    @pl.when(f > 0)
    def _():
        accumulate(False)

    @pl.when(f == nf - 1)
    def _():
        o_ref[...] = _layer_norm(o_ref[...], g_ref[...], b_ref[...], eps=4.0 * LN_EPS)

    if maybe_sample_ref:
        @pl.when((f == nf - 1) & (pl.program_id(0) == M // TM - 1))
        def _():
            maybe_sample_ref[0][...] = o_ref[TM - DEC_BATCH:, :]


def _ffn(x, w_ffn_in, w_ffn_out, ln_g, ln_b, layer, which, ln_idx, split_out=False):
    nf = D_FF // TF
    est = 4 * (4 * TM * D_MODEL + 2 * 3 * D_MODEL * TF + 4 * TM * TFS + TM * TN)
    out_specs = pl.BlockSpec((TM, D_MODEL), lambda i, f: (i, 0))
    out_shape = jax.ShapeDtypeStruct((M, D_MODEL), F32)

    def chunk(i, f):
        return jnp.where(i % 2 == 0, f, nf - 1 - f)

    if split_out:
        out_specs = [out_specs, pl.BlockSpec((DEC_BATCH, D_MODEL), lambda i, f: (0, 0))]
        out_shape = [jax.ShapeDtypeStruct((MP, D_MODEL), F32),
                     jax.ShapeDtypeStruct((DEC_BATCH, D_MODEL), F32)]
    return pl.pallas_call(
        functools.partial(_ffn_kernel, nf=nf),
        grid=(M // TM, nf),
        in_specs=[
            pl.BlockSpec((TM, D_MODEL), lambda i, f: (i, 0)),
            pl.BlockSpec((None, None, D_MODEL, TF), lambda i, f: (layer, which, 0, chunk(i, f))),
            pl.BlockSpec((None, None, D_MODEL, TF), lambda i, f: (layer, which, 0, nf + chunk(i, f))),
            pl.BlockSpec((None, None, TF, D_MODEL), lambda i, f: (layer, which, chunk(i, f), 0)),
            pl.BlockSpec((None, 1, D_MODEL), lambda i, f: (ln_idx, 0, 0)),
            pl.BlockSpec((None, 1, D_MODEL), lambda i, f: (ln_idx, 0, 0)),
        ],
        out_specs=out_specs,
        out_shape=out_shape,
        compiler_params=pltpu.CompilerParams(
            dimension_semantics=("arbitrary", "arbitrary"), vmem_limit_bytes=_vmem_limit(est)),
        name="ffn_ln",
    )(x, w_ffn_in, w_ffn_in, w_ffn_out, ln_g, ln_b)


def _proj_kernel(x_ref, w_ref, *rest, w_transposed, with_gates, aliased):
    rest = rest[1:] if aliased else rest
    if with_gates:
        wgate_ref, o_ref, gate_ref = rest
        gate_ref[...] = _dot_nt(x_ref[...], wgate_ref[...])
    else:
        (o_ref,) = rest
    for n in range(PROJ_COLS // TN):
        lo, hi = n * TN, (n + 1) * TN
        if w_transposed:
            o_ref[:, lo:hi] = _dot_nt(x_ref[...], w_ref[lo:hi, :])
        else:
            o_ref[:, lo:hi] = _dot(x_ref[...], w_ref[:, lo:hi])


def _proj(x, w, layer_idx, col_block, n_total, name, w_transposed=False, w_gate_t=None, prev=None):
    est = 4 * (PROJ_COLS * D_MODEL + 2 * TMP * D_MODEL + 2 * TMP * PROJ_COLS + 2 * TMP * TN)
    if w_transposed:
        w_spec = pl.BlockSpec((None, PROJ_COLS, D_MODEL), lambda i: (layer_idx, col_block, 0),
                              pipeline_mode=pl.Buffered(1))
    else:
        w_spec = pl.BlockSpec((None, D_MODEL, PROJ_COLS), lambda i: (layer_idx, 0, col_block),
                              pipeline_mode=pl.Buffered(1))
    in_specs = [pl.BlockSpec((TMP, D_MODEL), lambda i: (i, 0)), w_spec]
    args = [x, w]
    aliases = {}
    if prev is not None:
        in_specs.append(pl.BlockSpec(memory_space=pl.ANY))
        args.append(prev)
        aliases = {2: 0}
    out_specs = pl.BlockSpec((TMP, PROJ_COLS), lambda i: (i, col_block))
    out_shape = jax.ShapeDtypeStruct((M, n_total), F32)
    if w_gate_t is not None:
        in_specs.append(pl.BlockSpec((None, LANES, D_MODEL), lambda i: (layer_idx, 0, 0)))
        args.append(w_gate_t)
        out_specs = [out_specs, pl.BlockSpec((TMP, LANES), lambda i: (i, 0))]
        out_shape = [out_shape, jax.ShapeDtypeStruct((M, LANES), F32)]
        est += 4 * (2 * LANES * D_MODEL + 3 * TMP * LANES)
    return pl.pallas_call(
        functools.partial(_proj_kernel, w_transposed=w_transposed, with_gates=w_gate_t is not None,
                          aliased=prev is not None),
        grid=(M // TMP,),
        in_specs=in_specs,
        out_specs=out_specs,
        out_shape=out_shape,
        input_output_aliases=aliases,
        compiler_params=pltpu.CompilerParams(
            dimension_semantics=("parallel",), vmem_limit_bytes=_vmem_limit(est)),
        name=name,
    )(*args)


def _outproj_kernel(x_ref, a_ref, w_ref, g_ref, b_ref, o_ref):
    y = ALPHA * x_ref[...] + _dot(a_ref[...], w_ref[...])
    o_ref[...] = _layer_norm(y, g_ref[...], b_ref[...])


def _outproj(x, mix, w_out, ln_g, ln_b, layer_idx, ln_idx):
    est = 4 * (D_MODEL * D_MODEL + 9 * TMO * D_MODEL)
    return pl.pallas_call(
        _outproj_kernel,
        grid=(M // TMO,),
        in_specs=[
            pl.BlockSpec((TMO, D_MODEL), lambda i: (i, 0)),
            pl.BlockSpec((TMO, D_MODEL), lambda i: (i, 0)),
            pl.BlockSpec((None, D_MODEL, D_MODEL), lambda i: (layer_idx, 0, 0),
                         pipeline_mode=pl.Buffered(1)),
            pl.BlockSpec((None, 1, D_MODEL), lambda i: (ln_idx, 0, 0)),
            pl.BlockSpec((None, 1, D_MODEL), lambda i: (ln_idx, 0, 0)),
        ],
        out_specs=pl.BlockSpec((TMO, D_MODEL), lambda i: (i, 0)),
        out_shape=jax.ShapeDtypeStruct((M, D_MODEL), F32),
        compiler_params=pltpu.CompilerParams(
            dimension_semantics=("parallel",), vmem_limit_bytes=_vmem_limit(est)),
        name="outproj_ln",
    )(x, mix, w_out, ln_g, ln_b)


def _split3(x):
    h1 = x.astype(jnp.bfloat16).astype(F32)
    r = x - h1
    h2 = r.astype(jnp.bfloat16).astype(F32)
    return h1, h2, r - h2


def _even_prompt_kernel(bg_ref, cg_ref, xin_ref, q_ref, k_ref, v_ref, og_ref, gt_ref,
                        wc_ref, bgate_ref, mhg_ref,
                        mix_ref, conv_ref, c_out_ref, n_out_ref, m_out_ref,
                        cbuf, c_sc, n_sc, m_sc):
    c = pl.program_id(1)
    L = LC
    R = RE

    @pl.when(c == 0)
    def _():
        cbuf[0:SUBLANES, :] = jnp.zeros((SUBLANES, D_CONV), F32)
        c_sc[...] = jnp.zeros_like(c_sc)
        n_sc[...] = jnp.zeros_like(n_sc)
        m_sc[...] = jnp.zeros_like(m_sc)

    cx = cg_ref[...] * xin_ref[...]
    cbuf[SUBLANES:SUBLANES + R, :] = cx
    c1 = cbuf[SUBLANES - 1:SUBLANES - 1 + R, :]
    c2 = cbuf[SUBLANES - 2:SUBLANES - 2 + R, :]
    conv = c2 * wc_ref[0:1, :] + c1 * wc_ref[1:2, :] + cx * wc_ref[2:3, :]
    mix_ref[:, 0:D_CONV] = bg_ref[...] * conv
    conv_ref[...] = cbuf[SUBLANES + R - 2:SUBLANES + R, :]
    cbuf[0:SUBLANES, :] = cbuf[R:R + SUBLANES, :]

    row = lax.broadcasted_iota(jnp.int32, (L, L), 0)
    col = lax.broadcasted_iota(jnp.int32, (L, L), 1)
    causal = row >= col
    tril = causal.astype(F32)
    c_state = [c_sc[h] for h in range(HEADS)]
    n_state = [n_sc[h:h + 1, :] for h in range(HEADS)]
    m_state = [m_sc[h, 0:1, 0:1] for h in range(HEADS)]

    for s in range(R // L):
        r0, r1 = s * L, (s + 1) * L
        gt = gt_ref[r0:r1, :]
        li_all = gt[:, 0:HEADS] + bgate_ref[0:1, :]
        lf_all = _log_sigmoid(gt[:, HEADS:2 * HEADS] + bgate_ref[1:2, :])
        f1, f2, f3 = _split3(lf_all)
        b_all = _dot(tril, f1) + _dot(tril, f2) + _dot(tril, f3)
        z = jnp.concatenate([li_all, b_all, jnp.zeros((L, LANES - 2 * HEADS), F32)], axis=1)
        zt = z.T

        for h in range(HEADS):
            li_c = li_all[:, h:h + 1]
            b_c = b_all[:, h:h + 1]
            li_r = zt[h:h + 1, :]
            b_r = zt[HEADS + h:HEADS + h + 1, :]
            m0 = m_state[h]
            e = jnp.where(causal, li_r - b_r, -jnp.inf)
            inter = b_c + m0
            m_t = jnp.maximum(inter, b_c + jnp.max(e, axis=-1, keepdims=True))
            a_int = jnp.exp(inter - m_t)
            qh = q_ref[r0:r1, h * DK:(h + 1) * DK]
            kh = k_ref[r0:r1, h * DK:(h + 1) * DK] * (DK ** -0.5)
            vh = v_ref[r0:r1, h * DV:(h + 1) * DV]
            w = jnp.exp(e + (b_c - m_t)) * _dot_nt(qh, kh)
            ch = c_state[h]
            n_row = n_state[h]
            num = a_int * _dot(qh, ch) + _dot(w, vh)
            w_tiles = sum(w[:, t * LANES:(t + 1) * LANES] for t in range(L // LANES))
            w_rows = jnp.sum(w_tiles, axis=-1, keepdims=True)
            den = a_int * jnp.sum(qh * n_row, axis=-1, keepdims=True) + w_rows
            hh = num / jnp.maximum(jnp.abs(den), jnp.exp(-m_t))
            m_new = m_t[L - 1:L, :]
            b_last = b_c[L - 1:L, :]
            w_end = jnp.exp(b_last - b_c + li_c - m_new)
            decay = jnp.exp(b_last + m0 - m_new)
            wk = w_end * kh
            c_state[h] = decay * ch + _dot(wk.T, vh)
            n_state[h] = decay * n_row + jnp.sum(wk, axis=0, keepdims=True)
            m_state[h] = m_new
            hn = _layer_norm(hh, mhg_ref[0:1, h * DV:(h + 1) * DV], None)
            og = og_ref[r0:r1, h * DV:(h + 1) * DV]
            mix_ref[r0:r1, D_CONV + h * DV:D_CONV + (h + 1) * DV] = hn * jax.nn.sigmoid(og)

    for h in range(HEADS):
        c_sc[h] = c_state[h]
        n_sc[h:h + 1, :] = n_state[h]
        m_sc[h] = jnp.broadcast_to(m_state[h], (SUBLANES, LANES))
        c_out_ref[h] = c_state[h]
        n_out_ref[h:h + 1, :] = n_state[h]
        m_out_ref[0:1, h:h + 1] = m_state[h]


def _even_prompt(proj, gates, w_conv, b_gates, mh_g, j):
    nck = SEQ // RE
    r = lambda b, c: b * nck + c
    est = 4 * (2 * (5 * RE * 1024 + 2 * RE * 512 + RE * 128) + 2 * RE * 2048 + 4 * RE * 1024
               + 3 * HEADS * DK * DV + 24 * LC * LC)
    return pl.pallas_call(
        _even_prompt_kernel,
        grid=(BATCH, nck),
        in_specs=[
            pl.BlockSpec((RE, D_CONV), lambda b, c: (r(b, c), 0)),
            pl.BlockSpec((RE, D_CONV), lambda b, c: (r(b, c), 1)),
            pl.BlockSpec((RE, D_CONV), lambda b, c: (r(b, c), 2)),
            pl.BlockSpec((RE, HEADS * DK), lambda b, c: (r(b, c), 6)),
            pl.BlockSpec((RE, HEADS * DK), lambda b, c: (r(b, c), 7)),
            pl.BlockSpec((RE, HEADS * DV), lambda b, c: (r(b, c), 4)),
            pl.BlockSpec((RE, HEADS * DV), lambda b, c: (r(b, c), 5)),
            pl.BlockSpec((RE, LANES), lambda b, c: (r(b, c), 0)),
            pl.BlockSpec((None, CONV_W, D_CONV), lambda b, c: (j, 0, 0)),
            pl.BlockSpec((None, 2, HEADS), lambda b, c: (j, 0, 0)),
            pl.BlockSpec((None, 1, HEADS * DV), lambda b, c: (j, 0, 0)),
        ],
        out_specs=[
            pl.BlockSpec((RE, D_MODEL), lambda b, c: (r(b, c), 0)),
            pl.BlockSpec((None, CONV_W - 1, D_CONV), lambda b, c: (b, 0, 0)),
            pl.BlockSpec((None, HEADS, DK, DV), lambda b, c: (b, 0, 0, 0)),
            pl.BlockSpec((None, HEADS, DK), lambda b, c: (b, 0, 0)),
            pl.BlockSpec((None, 1, HEADS), lambda b, c: (b, 0, 0)),
        ],
        out_shape=[
            jax.ShapeDtypeStruct((M, D_MODEL), F32),
            jax.ShapeDtypeStruct((BATCH, CONV_W - 1, D_CONV), F32),
            jax.ShapeDtypeStruct((BATCH, HEADS, DK, DV), F32),
            jax.ShapeDtypeStruct((BATCH, HEADS, DK), F32),
            jax.ShapeDtypeStruct((BATCH, 1, HEADS), F32),
        ],
        scratch_shapes=[
            pltpu.VMEM((RE + SUBLANES, D_CONV), F32),
            pltpu.VMEM((HEADS, DK, DV), F32),
            pltpu.VMEM((HEADS, DK), F32),
            pltpu.VMEM((HEADS, SUBLANES, LANES), F32),
        ],
        compiler_params=pltpu.CompilerParams(
            dimension_semantics=("arbitrary", "arbitrary"), vmem_limit_bytes=_vmem_limit(est)),
        name="even_prompt",
    )(proj, proj, proj, proj, proj, proj, proj, gates, w_conv, b_gates, mh_g)


def _sample_gates(gt, bgate_ref, m):
    li = gt[:, 0:HEADS] + bgate_ref[0:1, :]
    lf = _log_sigmoid(gt[:, HEADS:2 * HEADS] + bgate_ref[1:2, :])
    inter = lf + m
    m_t = jnp.maximum(inter, li)
    return jnp.exp(inter - m_t), jnp.exp(li - m_t), m_t


def _qk_cols_kernel(x_ref, w_ref, o_ref):
    t = _dot_nt(w_ref[...], x_ref[...])
    for s in range(DEC_BATCH // SB):
        o_ref[s] = t[:, s * SB:(s + 1) * SB]


def _qk_cols(x, w_even_t, j):
    nqk = 2 * HEADS * DK
    est = 4 * (2 * DEC_BATCH * D_MODEL + 2 * nqk * D_MODEL + 3 * nqk * LANES
               + 2 * (DEC_BATCH // SB) * nqk * LANES)
    return pl.pallas_call(
        _qk_cols_kernel,
        grid=(1,),
        in_specs=[
            pl.BlockSpec((DEC_BATCH, D_MODEL), lambda i: (MP // DEC_BATCH, 0)),
            pl.BlockSpec((None, nqk, D_MODEL), lambda i: (j, 3 * D_CONV // nqk, 0)),
        ],
        out_specs=pl.BlockSpec((DEC_BATCH // SB, nqk, SB), lambda i: (0, 0, 0)),
        out_shape=jax.ShapeDtypeStruct((DEC_BATCH // SB, nqk, SB), F32),
        compiler_params=pltpu.CompilerParams(
            dimension_semantics=("arbitrary",), vmem_limit_bytes=_vmem_limit(est)),
        name="qk_cols",
    )(x, w_even_t)


def _even_sample_state_kernel(c_ref, qk_ref, v_ref, gt_ref, m_ref, bgate_ref, *rest):
    c_out_ref, num_ref = rest[-2], rest[-1]
    a, wgt, _ = _sample_gates(gt_ref[...], bgate_ref, m_ref[...])
    for bi in range(SB):
        for h in range(HEADS):
            a_s = a[bi:bi + 1, h:h + 1]
            w_s = wgt[bi:bi + 1, h:h + 1]
            ch = c_ref[bi, h]
            qc = qk_ref[h * DK:(h + 1) * DK, bi:bi + 1]
            kc = qk_ref[(HEADS + h) * DK:(HEADS + h + 1) * DK, bi:bi + 1] * (DK ** -0.5)
            vr = v_ref[bi:bi + 1, h * DV:(h + 1) * DV]
            c_out_ref[bi, h] = a_s * ch + (w_s * kc) * vr
            num_ref[bi:bi + 1, h * DV:(h + 1) * DV] = jnp.sum(qc * ch, axis=0, keepdims=True)


def _even_sample_state(state_c, qk_cols, proj, gates, state_m, b_gates, j, c_prev):
    blk5 = (None, SB, HEADS, DK, DV)
    rb = MP // SB
    in_specs = [
        pl.BlockSpec(blk5, lambda i: (j, i, 0, 0, 0)),
        pl.BlockSpec((None, 2 * HEADS * DK, SB), lambda i: (i, 0, 0)),
        pl.BlockSpec((SB, HEADS * DV), lambda i: (rb + i, 4)),
        pl.BlockSpec((SB, LANES), lambda i: (rb + i, 0)),
        pl.BlockSpec((None, SB, HEADS), lambda i: (j, i, 0)),
        pl.BlockSpec((None, 2, HEADS), lambda i: (j, 0, 0)),
    ]
    args = [state_c, qk_cols, proj, gates, state_m, b_gates]
    aliases = {}
    if c_prev is not None:
        in_specs.append(pl.BlockSpec(memory_space=pl.ANY))
        args.append(c_prev)
        aliases = {len(args) - 1: 0}
    est = 4 * (4 * SB * HEADS * DK * DV + 4 * HEADS * DK * LANES)
    return pl.pallas_call(
        _even_sample_state_kernel,
        grid=(DEC_BATCH // SB,),
        in_specs=in_specs,
        out_specs=[
            pl.BlockSpec(blk5, lambda i: (j, i, 0, 0, 0)),
            pl.BlockSpec((SB, HEADS * DV), lambda i: (i, 0)),
        ],
        out_shape=[
            jax.ShapeDtypeStruct(state_c.shape, F32),
            jax.ShapeDtypeStruct((DEC_BATCH, HEADS * DV), F32),
        ],
        input_output_aliases=aliases,
        compiler_params=pltpu.CompilerParams(
            dimension_semantics=("arbitrary",), vmem_limit_bytes=_vmem_limit(est)),
        name="even_sample_state",
    )(*args)


def _even_sample_kernel(bg_ref, cg_ref, xin_ref, q_ref, k_ref, v_ref, og_ref, gt_ref,
                        num_ref, cst_ref, n_ref, m_ref, wc_ref, bgate_ref, mhg_ref, mixin_hbm,
                        mix_ref, conv_ref, n_out_ref, m_out_ref):
    del mixin_hbm
    cx = cg_ref[...] * xin_ref[...]
    st0 = cst_ref[:, 0:D_CONV]
    st1 = cst_ref[:, D_CONV:2 * D_CONV]
    conv = st0 * wc_ref[0:1, :] + st1 * wc_ref[1:2, :] + cx * wc_ref[2:3, :]
    mix_ref[:, 0:D_CONV] = bg_ref[...] * conv
    conv_ref[:, 0:D_CONV] = st1
    conv_ref[:, D_CONV:2 * D_CONV] = cx

    a, wgt, m_t = _sample_gates(gt_ref[...], bgate_ref, m_ref[...])
    m_out_ref[...] = m_t
    floor = jnp.exp(-m_t)
    for h in range(HEADS):
        a_h = a[:, h:h + 1]
        w_h = wgt[:, h:h + 1]
        qh = q_ref[:, h * DK:(h + 1) * DK]
        kh = k_ref[:, h * DK:(h + 1) * DK] * (DK ** -0.5)
        vh = v_ref[:, h * DV:(h + 1) * DV]
        nh = n_ref[:, h * DK:(h + 1) * DK]
        wt = w_h * jnp.sum(qh * kh, axis=-1, keepdims=True)
        num = a_h * num_ref[:, h * DV:(h + 1) * DV] + wt * vh
        den = a_h * jnp.sum(qh * nh, axis=-1, keepdims=True) + wt
        hh = num / jnp.maximum(jnp.abs(den), floor[:, h:h + 1])
        n_out_ref[:, h * DK:(h + 1) * DK] = a_h * nh + w_h * kh
        hn = _layer_norm(hh, mhg_ref[0:1, h * DV:(h + 1) * DV], None)
        og = og_ref[:, h * DV:(h + 1) * DV]
        mix_ref[:, D_CONV + h * DV:D_CONV + (h + 1) * DV] = hn * jax.nn.sigmoid(og)


def _even_sample(proj, gates, num, conv_st, n_st, m_st, w_conv, b_gates, mh_g, mixin, j):
    nb = DEC_BATCH
    rb = MP // nb
    est = 4 * 2 * (5 * nb * 1024 + 2 * nb * 512 + nb * 128 + nb * 1024 + nb * 2048 + nb * 512
                   + nb * 2048 + nb * 2048 + nb * 512)
    return pl.pallas_call(
        _even_sample_kernel,
        grid=(1,),
        in_specs=[
            pl.BlockSpec((nb, D_CONV), lambda i: (rb, 0)),
            pl.BlockSpec((nb, D_CONV), lambda i: (rb, 1)),
            pl.BlockSpec((nb, D_CONV), lambda i: (rb, 2)),
            pl.BlockSpec((nb, HEADS * DK), lambda i: (rb, 6)),
            pl.BlockSpec((nb, HEADS * DK), lambda i: (rb, 7)),
            pl.BlockSpec((nb, HEADS * DV), lambda i: (rb, 4)),
            pl.BlockSpec((nb, HEADS * DV), lambda i: (rb, 5)),
            pl.BlockSpec((nb, LANES), lambda i: (rb, 0)),
            pl.BlockSpec((nb, HEADS * DV), lambda i: (0, 0)),
            pl.BlockSpec((None, nb, 2 * D_CONV), lambda i: (j, 0, 0)),
            pl.BlockSpec((None, nb, HEADS * DK), lambda i: (j, 0, 0)),
            pl.BlockSpec((None, nb, HEADS), lambda i: (j, 0, 0)),
            pl.BlockSpec((None, CONV_W, D_CONV), lambda i: (j, 0, 0)),
            pl.BlockSpec((None, 2, HEADS), lambda i: (j, 0, 0)),
            pl.BlockSpec((None, 1, HEADS * DV), lambda i: (j, 0, 0)),
            pl.BlockSpec(memory_space=pl.ANY),
        ],
        out_specs=[
            pl.BlockSpec((nb, D_MODEL), lambda i: (rb, 0)),
            pl.BlockSpec((nb, 2 * D_CONV), lambda i: (0, 0)),
            pl.BlockSpec((nb, HEADS * DK), lambda i: (0, 0)),
            pl.BlockSpec((nb, HEADS), lambda i: (0, 0)),
        ],
        out_shape=[
            jax.ShapeDtypeStruct((M, D_MODEL), F32),
            jax.ShapeDtypeStruct((nb, 2 * D_CONV), F32),
            jax.ShapeDtypeStruct((nb, HEADS * DK), F32),
            jax.ShapeDtypeStruct((nb, HEADS), F32),
        ],
        input_output_aliases={15: 0},
        compiler_params=pltpu.CompilerParams(
            dimension_semantics=("arbitrary",), vmem_limit_bytes=_vmem_limit(est)),
        name="even_sample",
    )(proj, proj, proj, proj, proj, proj, proj, gates, num, conv_st, n_st, m_st,
      w_conv, b_gates, mh_g, mixin)


def _gmlp_norm(u_raw, v_raw, g, b):
    u = jax.nn.gelu(u_raw)
    vn = _layer_norm(jax.nn.gelu(v_raw), g, b)
    return u, vn


def _odd_prompt_kernel(p_ref, u_ref, v_ref, wp_ref, sc_ref, gmg_ref, gmb_ref, ws_ref, bst_ref,
                       mix_ref, pool_ref, gv_ref, pbuf, sbuf):
    s = pl.program_id(1)
    R = RO

    @pl.when(s == 0)
    def _():
        pbuf[0:HIST, :] = jnp.zeros((HIST, D_POOL), F32)
        sbuf[0:SUBLANES, :] = jnp.zeros((SUBLANES, D_POOL), F32)

    p = p_ref[...]
    pbuf[HIST:HIST + R, :] = p
    pos = s * R + lax.broadcasted_iota(jnp.int32, (R, 1), 0)
    for g, w in enumerate(POOL_WINDOWS):
        lo, hi = g * POOL_GW, (g + 1) * POOL_GW
        cur = pbuf[0:HIST + R, lo:hi]
        d = 1
        while d < w:
            sbuf[SUBLANES:SUBLANES + HIST + R, lo:hi] = cur
            cur = cur + sbuf[SUBLANES - d:SUBLANES - d + HIST + R, lo:hi]
            d *= 2
        win = cur[HIST:, :]
        cnt = jnp.minimum(w, pos + 1).astype(F32)
        diff = win / cnt - p[:, lo:hi]
        mix_ref[:, lo:hi] = _dot(diff, wp_ref[g]) * sc_ref[0:1, lo:hi]
    pool_ref[...] = pbuf[HIST + R - POOL_BUF:HIST + R, :]
    pbuf[0:HIST, :] = pbuf[R:R + HIST, :]

    u, vn = _gmlp_norm(u_ref[...], v_ref[...], gmg_ref[...], gmb_ref[...])
    L = GMLP_CHUNK
    tril = lax.broadcasted_iota(jnp.int32, (L, L), 0) >= lax.broadcasted_iota(jnp.int32, (L, L), 1)
    for g in range(D_GMLP // GMLP_GW):
        lo, hi = g * GMLP_GW, (g + 1) * GMLP_GW
        ws = jnp.where(tril, ws_ref[g], 0.0)
        bcol = bst_ref[:, g:g + 1]
        for ck in range(R // L):
            r0, r1 = ck * L, (ck + 1) * L
            sv = _dot(ws, vn[r0:r1, lo:hi]) + bcol
            mix_ref[r0:r1, D_POOL + lo:D_POOL + hi] = u[r0:r1, lo:hi] * sv
    gv_ref[...] = vn[R - L:R, :]


def _odd_prompt(proj, w_pool, pool_scale, gm_g, gm_b, w_spatial, bs_t, j):
    nrb = SEQ // RO
    r = lambda b, s: b * nrb + s
    est = 4 * (2 * 3 * RO * 1024 + 2 * RO * 2048 + 2 * 4 * 256 * 256 + 2 * 4 * 128 * 128
               + (RO + HIST) * 1024 + 8 * RO * 1024)
    return pl.pallas_call(
        _odd_prompt_kernel,
        grid=(BATCH, nrb),
        in_specs=[
            pl.BlockSpec((RO, D_POOL), lambda b, s: (r(b, s), 0)),
            pl.BlockSpec((RO, D_GMLP), lambda b, s: (r(b, s), 1)),
            pl.BlockSpec((RO, D_GMLP), lambda b, s: (r(b, s), 2)),
            pl.BlockSpec((None, 4, POOL_GW, POOL_GW), lambda b, s: (j, 0, 0, 0)),
            pl.BlockSpec((None, 1, D_POOL), lambda b, s: (j, 0, 0)),
            pl.BlockSpec((None, 1, D_GMLP), lambda b, s: (j, 0, 0)),
            pl.BlockSpec((None, 1, D_GMLP), lambda b, s: (j, 0, 0)),
            pl.BlockSpec((None, 4, GMLP_CHUNK, GMLP_CHUNK), lambda b, s: (j, 0, 0, 0)),
            pl.BlockSpec((None, GMLP_CHUNK, 4), lambda b, s: (j, 0, 0)),
        ],
        out_specs=[
            pl.BlockSpec((RO, D_MODEL), lambda b, s: (r(b, s), 0)),
            pl.BlockSpec((None, POOL_BUF, D_POOL), lambda b, s: (b, 0, 0)),
            pl.BlockSpec((None, GMLP_CHUNK, D_GMLP), lambda b, s: (b, 0, 0)),
        ],
        out_shape=[
            jax.ShapeDtypeStruct((M, D_MODEL), F32),
            jax.ShapeDtypeStruct((BATCH, POOL_BUF, D_POOL), F32),
            jax.ShapeDtypeStruct((BATCH, GMLP_CHUNK, D_GMLP), F32),
        ],
        scratch_shapes=[pltpu.VMEM((RO + HIST, D_POOL), F32),
                        pltpu.VMEM((SUBLANES + RO + HIST, D_POOL), F32)],
        compiler_params=pltpu.CompilerParams(
            dimension_semantics=("arbitrary", "arbitrary"), vmem_limit_bytes=_vmem_limit(est)),
        name="odd_prompt",
    )(proj, proj, proj, w_pool, pool_scale, gm_g, gm_b, w_spatial, bs_t)


def _odd_sample_kernel(p_ref, u_ref, v_ref, st_ref, wp_ref, sc_ref, gmg_ref, gmb_ref, ws_ref, bst_ref,
                       *rest):
    mix_ref, pool_ref, gv_ref = rest[-3:]
    p = p_ref[...]
    for r in range(POOL_BUF - 1):
        pool_ref[r] = st_ref[r + 1]
    pool_ref[POOL_BUF - 1] = p
    for g, w in enumerate(POOL_WINDOWS):
        lo, hi = g * POOL_GW, (g + 1) * POOL_GW
        win = p[:, lo:hi]
        for jj in range(1, w):
            win = win + st_ref[POOL_BUF - jj, :, lo:hi]
        cnt = float(min(w, PAST_LEN + 1))
        diff = win / cnt - p[:, lo:hi]
        mix_ref[:, lo:hi] = _dot(diff, wp_ref[g]) * sc_ref[0:1, lo:hi]
    u, vn = _gmlp_norm(u_ref[...], v_ref[...], gmg_ref[...], gmb_ref[...])
    gv_ref[...] = vn
    for g in range(D_GMLP // GMLP_GW):
        lo, hi = g * GMLP_GW, (g + 1) * GMLP_GW
        sv = ws_ref[g, 0:1, 0:1] * vn[:, lo:hi] + bst_ref[0:1, g:g + 1]
        mix_ref[:, D_POOL + lo:D_POOL + hi] = u[:, lo:hi] * sv


def _odd_sample(proj, pool_st, w_pool, pool_scale, gm_g, gm_b, w_spatial, bs_t, mixin, pool_prev, j):
    nb = DEC_BATCH
    rb = MP // nb
    st_blk = (None, POOL_BUF, nb, D_POOL)
    in_specs = [
        pl.BlockSpec((nb, D_POOL), lambda i: (rb, 0)),
        pl.BlockSpec((nb, D_GMLP), lambda i: (rb, 1)),
        pl.BlockSpec((nb, D_GMLP), lambda i: (rb, 2)),
        pl.BlockSpec(st_blk, lambda i: (j, 0, 0, 0)),
        pl.BlockSpec((None, 4, POOL_GW, POOL_GW), lambda i: (j, 0, 0, 0)),
        pl.BlockSpec((None, 1, D_POOL), lambda i: (j, 0, 0)),
        pl.BlockSpec((None, 1, D_GMLP), lambda i: (j, 0, 0)),
        pl.BlockSpec((None, 1, D_GMLP), lambda i: (j, 0, 0)),
        pl.BlockSpec((None, 4, GMLP_CHUNK, GMLP_CHUNK), lambda i: (j, 0, 0, 0)),
        pl.BlockSpec((None, GMLP_CHUNK, 4), lambda i: (j, 0, 0)),
        pl.BlockSpec(memory_space=pl.ANY),
    ]
    args = [proj, proj, proj, pool_st, w_pool, pool_scale, gm_g, gm_b, w_spatial, bs_t, mixin]
    aliases = {len(args) - 1: 0}
    if pool_prev is not None:
        in_specs.append(pl.BlockSpec(memory_space=pl.ANY))
        args.append(pool_prev)
        aliases[len(args) - 1] = 1
    est = 4 * (4 * POOL_BUF * nb * D_POOL + 2 * (3 * nb * 1024 + 4 * 256 * 256 + 4 * 128 * 128
                                                + nb * 2048 + nb * 1024))
    return pl.pallas_call(
        _odd_sample_kernel,
        grid=(1,),
        in_specs=in_specs,
        out_specs=[
            pl.BlockSpec((nb, D_MODEL), lambda i: (rb, 0)),
            pl.BlockSpec(st_blk, lambda i: (j, 0, 0, 0)),
            pl.BlockSpec((nb, D_GMLP), lambda i: (0, 0)),
        ],
        out_shape=[
            jax.ShapeDtypeStruct((M, D_MODEL), F32),
            jax.ShapeDtypeStruct(pool_st.shape, F32),
            jax.ShapeDtypeStruct((nb, D_GMLP), F32),
        ],
        input_output_aliases=aliases,
        compiler_params=pltpu.CompilerParams(
            dimension_semantics=("arbitrary",), vmem_limit_bytes=_vmem_limit(est)),
        name="odd_sample",
    )(*args)


def kernel(x_prompt, x_sample, state_conv, state_mlstm_C, state_mlstm_n, state_mlstm_m, state_pool,
           ln_g, ln_b, w_ffn_in, w_ffn_out, w_in_even, b_gates_even, w_conv, mh_norm_g, w_out_even,
           w_in_odd, w_pool, pool_scale, gm_ln_g, gm_ln_b, w_spatial, b_spatial, w_out_odd):
    n_even, n_odd = w_in_even.shape[0], w_in_odd.shape[0]
    x = jnp.concatenate([x_prompt.reshape(MP, D_MODEL), x_sample.reshape(DEC_BATCH, D_MODEL)], axis=0)

    ln_g3 = ln_g.reshape(DEPTH * 3, 1, D_MODEL)
    ln_b3 = ln_b.reshape(DEPTH * 3, 1, D_MODEL)
    w_even_t = jnp.swapaxes(w_in_even, 1, 2)
    w_gate_t = jnp.pad(w_even_t[:, EVEN_MAIN:, :], ((0, 0), (0, LANES - 2 * HEADS), (0, 0)))
    pool_st = jnp.swapaxes(state_pool, 1, 2)
    mh_g3 = mh_norm_g.reshape(n_even, 1, HEADS * DV)
    conv_st = state_conv.reshape(n_even, DEC_BATCH, (CONV_W - 1) * D_CONV)
    n_st = state_mlstm_n.reshape(n_even, DEC_BATCH, HEADS * DK)
    scale3 = pool_scale.reshape(n_odd, 1, D_POOL)
    gm_g3 = gm_ln_g.reshape(n_odd, 1, D_GMLP)
    gm_b3 = gm_ln_b.reshape(n_odd, 1, D_GMLP)
    bs_t = jnp.swapaxes(b_spatial, 1, 2)

    conv_p, conv_s, c_p, n_p, n_s, m_p, m_s = [], [], [], [], [], [], []
    pool_p, gv_p, gv_s = [], [], []
    c_s = None
    pool_s = None

    for layer in range(DEPTH):
        j = layer // 2
        x = _ffn(x, w_ffn_in, w_ffn_out, ln_g3, ln_b3, layer, 0, 3 * layer)
        if layer % 2 == 0:
            proj, gates = _proj(x, w_even_t, j, 0, EVEN_MAIN, "proj_even", w_transposed=True,
                                w_gate_t=w_gate_t)
            proj = _proj(x, w_even_t, j, 1, EVEN_MAIN, "proj_even", w_transposed=True, prev=proj)
            mixin, cv, cc, nn, mm = _even_prompt(proj, gates, w_conv, b_gates_even, mh_g3, j)
            conv_p.append(cv)
            c_p.append(cc)
            n_p.append(nn)
            m_p.append(mm.reshape(BATCH, HEADS))
            qk_cols = _qk_cols(x, w_even_t, j)
            c_s, num = _even_sample_state(state_mlstm_C, qk_cols, proj, gates, state_mlstm_m,
                                          b_gates_even, j, c_s)
            mixin, cvs, nns, mms = _even_sample(proj, gates, num, conv_st, n_st, state_mlstm_m,
                                                w_conv, b_gates_even, mh_g3, mixin, j)
            conv_s.append(cvs.reshape(DEC_BATCH, CONV_W - 1, D_CONV))
            n_s.append(nns.reshape(DEC_BATCH, HEADS, DK))
            m_s.append(mms)
            x = _outproj(x, mixin, w_out_even, ln_g3, ln_b3, j, 3 * layer + 1)
        else:
            proj = _proj(x, w_in_odd, j, 0, ODD_IN, "proj_odd")
            mixin, pp, gv = _odd_prompt(proj, w_pool, scale3, gm_g3, gm_b3, w_spatial, bs_t, j)
            pool_p.append(pp)
            gv_p.append(gv)
            mixin, pool_s, gvs = _odd_sample(proj, pool_st, w_pool, scale3, gm_g3, gm_b3, w_spatial, bs_t,
                                             mixin, pool_s, j)
            gv_s.append(gvs.reshape(DEC_BATCH, 1, D_GMLP))
            x = _outproj(x, mixin, w_out_odd, ln_g3, ln_b3, j, 3 * layer + 1)
        x = _ffn(x, w_ffn_in, w_ffn_out, ln_g3, ln_b3, layer, 1, 3 * layer + 2,
                 split_out=layer == DEPTH - 1)

    y_prompt = x[0].reshape(BATCH, SEQ, D_MODEL)
    y_sample = x[1].reshape(DEC_BATCH, 1, D_MODEL)
    return (y_prompt, y_sample,
            jnp.stack(conv_p), jnp.stack(conv_s),
            jnp.stack(c_p), c_s,
            jnp.stack(n_p), jnp.stack(n_s),
            jnp.stack(m_p), jnp.stack(m_s),
            jnp.stack(pool_p), jnp.swapaxes(pool_s, 1, 2),
            jnp.stack(gv_p), jnp.stack(gv_s))
```

```python
import functools

import jax
import jax.numpy as jnp
from jax import lax
from jax.experimental import pallas as pl
from jax.experimental.pallas import tpu as pltpu

F32 = jnp.float32

D_MODEL = 2048
BATCH = 4
SEQ = 2048
DEPTH = 4
DEC_BATCH = 128
PAST_LEN = 16384
D_FF = 5632
D_CONV = 1024
CONV_W = 3
HEADS = 4
DK = 128
DV = 256
D_POOL = 1024
POOL_WINDOWS = (2, 4, 8, 16)
POOL_GW = 256
POOL_BUF = 15
D_GMLP = 1024
GMLP_GW = 256
GMLP_CHUNK = 128
ALPHA = (2 * DEPTH) ** 0.25
LN_EPS = 1e-5
EVEN_MAIN = 3 * D_CONV + 2 * HEADS * DK + 2 * HEADS * DV
ODD_IN = D_POOL + 2 * D_GMLP

MP = BATCH * SEQ
M = MP + DEC_BATCH

LANES = 128
SUBLANES = 8
VMEM_BYTES_V7X = 64 * 1024 * 1024
VMEM_RESERVED_BYTES = 4 * 1024 * 1024
VMEM_TEMP_BYTES = 8 * 1024 * 1024

TM = 832
TF = 512
TFS = 256
TN = 512
TMP = 520
PROJ_COLS = 3072
TMO = 416
LC = 256
RE = 256
HEAD_GROUP = 4
RO = 256
SB = 16
HIST = 16


def _vmem_limit(block_bytes):
    return int(min(VMEM_BYTES_V7X - VMEM_RESERVED_BYTES, block_bytes + VMEM_TEMP_BYTES))


def _layer_norm(y, g, b, eps=LN_EPS):
    mu = jnp.mean(y, axis=-1, keepdims=True)
    yc = y - mu
    var = jnp.mean(yc * yc, axis=-1, keepdims=True)
    out = yc * lax.rsqrt(var + eps) * g
    if b is not None:
        out = out + b
    return out


def _log_sigmoid(x):
    return -(jnp.maximum(-x, 0.0) + jnp.log1p(jnp.exp(-jnp.abs(x))))


def _dot(a, b):
    return jnp.dot(a, b, preferred_element_type=F32)


def _dot_nt(a, b):
    return lax.dot_general(a, b, (((1,), (1,)), ((), ())), preferred_element_type=F32)


def _ffn_kernel(x_ref, wg_ref, wu_ref, wo_ref, g_ref, b_ref, o_ref, *maybe_sample_ref, nf):
    f = pl.program_id(1)

    def accumulate(first_chunk):
        x = x_ref[...]
        for c in range(TF // TFS):
            c0, c1 = c * TFS, (c + 1) * TFS
            gate = _dot(x, wg_ref[:, c0:c1])
            up = _dot(x, wu_ref[:, c0:c1])
            h = gate * jax.nn.sigmoid(gate) * up
            for n in range(D_MODEL // TN):
                n0, n1 = n * TN, (n + 1) * TN
                part = _dot(h, wo_ref[c0:c1, n0:n1])
                if first_chunk and c == 0:
                    o_ref[:, n0:n1] = (2.0 * ALPHA) * x_ref[:, n0:n1] + part
                else:
                    o_ref[:, n0:n1] += part

    @pl.when(f == 0)
    def _():
        accumulate(True)

    @pl.when(f > 0)
    def _():
        accumulate(False)

    @pl.when(f == nf - 1)
    def _():
        o_ref[...] = _layer_norm(o_ref[...], g_ref[...], b_ref[...], eps=4.0 * LN_EPS)

    if maybe_sample_ref:
        @pl.when((f == nf - 1) & (pl.program_id(0) == M // TM - 1))
        def _():
            maybe_sample_ref[0][...] = o_ref[TM - DEC_BATCH:, :]


def _ffn(x, w_ffn_in, w_ffn_out, ln_g, ln_b, layer, which, ln_idx, split_out=False):
    nf = D_FF // TF
    est = 4 * (4 * TM * D_MODEL + 2 * 3 * D_MODEL * TF + 4 * TM * TFS + TM * TN)
    out_specs = pl.BlockSpec((TM, D_MODEL), lambda i, f: (i, 0))
    out_shape = jax.ShapeDtypeStruct((M, D_MODEL), F32)
    if split_out:
        out_specs = [out_specs, pl.BlockSpec((DEC_BATCH, D_MODEL), lambda i, f: (0, 0))]
        out_shape = [jax.ShapeDtypeStruct((MP, D_MODEL), F32),
                     jax.ShapeDtypeStruct((DEC_BATCH, D_MODEL), F32)]
    return pl.pallas_call(
        functools.partial(_ffn_kernel, nf=nf),
        grid=(M // TM, nf),
        in_specs=[
            pl.BlockSpec((TM, D_MODEL), lambda i, f: (i, 0)),
            pl.BlockSpec((None, None, D_MODEL, TF), lambda i, f: (layer, which, 0, f)),
            pl.BlockSpec((None, None, D_MODEL, TF), lambda i, f: (layer, which, 0, nf + f)),
            pl.BlockSpec((None, None, TF, D_MODEL), lambda i, f: (layer, which, f, 0)),
            pl.BlockSpec((None, 1, D_MODEL), lambda i, f: (ln_idx, 0, 0)),
            pl.BlockSpec((None, 1, D_MODEL), lambda i, f: (ln_idx, 0, 0)),
        ],
        out_specs=out_specs,
        out_shape=out_shape,
        compiler_params=pltpu.CompilerParams(
            dimension_semantics=("arbitrary", "arbitrary"), vmem_limit_bytes=_vmem_limit(est)),
        name="ffn_ln",
    )(x, w_ffn_in, w_ffn_in, w_ffn_out, ln_g, ln_b)


def _proj_kernel(x_ref, w_ref, *rest, w_transposed, with_gates, aliased):
    rest = rest[1:] if aliased else rest
    if with_gates:
        wgate_ref, o_ref, gate_ref = rest
        gate_ref[...] = _dot_nt(x_ref[...], wgate_ref[...])
    else:
        (o_ref,) = rest
    for n in range(PROJ_COLS // TN):
        lo, hi = n * TN, (n + 1) * TN
        if w_transposed:
            o_ref[:, lo:hi] = _dot_nt(x_ref[...], w_ref[lo:hi, :])
        else:
            o_ref[:, lo:hi] = _dot(x_ref[...], w_ref[:, lo:hi])


def _proj(x, w, layer_idx, col_block, n_total, name, w_transposed=False, w_gate_t=None, prev=None):
    est = 4 * (PROJ_COLS * D_MODEL + 2 * TMP * D_MODEL + 2 * TMP * PROJ_COLS + 2 * TMP * TN)
    if w_transposed:
        w_spec = pl.BlockSpec((None, PROJ_COLS, D_MODEL), lambda i: (layer_idx, col_block, 0),
                              pipeline_mode=pl.Buffered(1))
    else:
        w_spec = pl.BlockSpec((None, D_MODEL, PROJ_COLS), lambda i: (layer_idx, 0, col_block),
                              pipeline_mode=pl.Buffered(1))
    in_specs = [pl.BlockSpec((TMP, D_MODEL), lambda i: (i, 0)), w_spec]
    args = [x, w]
    aliases = {}
    if prev is not None:
        in_specs.append(pl.BlockSpec(memory_space=pl.ANY))
        args.append(prev)
        aliases = {2: 0}
    out_specs = pl.BlockSpec((TMP, PROJ_COLS), lambda i: (i, col_block))
    out_shape = jax.ShapeDtypeStruct((M, n_total), F32)
    if w_gate_t is not None:
        in_specs.append(pl.BlockSpec((None, LANES, D_MODEL), lambda i: (layer_idx, 0, 0)))
        args.append(w_gate_t)
        out_specs = [out_specs, pl.BlockSpec((TMP, LANES), lambda i: (i, 0))]
        out_shape = [out_shape, jax.ShapeDtypeStruct((M, LANES), F32)]
        est += 4 * (2 * LANES * D_MODEL + 3 * TMP * LANES)
    return pl.pallas_call(
        functools.partial(_proj_kernel, w_transposed=w_transposed, with_gates=w_gate_t is not None,
                          aliased=prev is not None),
        grid=(M // TMP,),
        in_specs=in_specs,
        out_specs=out_specs,
        out_shape=out_shape,
        input_output_aliases=aliases,
        compiler_params=pltpu.CompilerParams(
            dimension_semantics=("parallel",), vmem_limit_bytes=_vmem_limit(est)),
        name=name,
    )(*args)


def _outproj_kernel(x_ref, a_ref, w_ref, g_ref, b_ref, o_ref):
    y = ALPHA * x_ref[...] + _dot(a_ref[...], w_ref[...])
    o_ref[...] = _layer_norm(y, g_ref[...], b_ref[...])


def _outproj(x, mix, w_out, ln_g, ln_b, layer_idx, ln_idx):
    est = 4 * (D_MODEL * D_MODEL + 9 * TMO * D_MODEL)
    return pl.pallas_call(
        _outproj_kernel,
        grid=(M // TMO,),
        in_specs=[
            pl.BlockSpec((TMO, D_MODEL), lambda i: (i, 0)),
            pl.BlockSpec((TMO, D_MODEL), lambda i: (i, 0)),
            pl.BlockSpec((None, D_MODEL, D_MODEL), lambda i: (layer_idx, 0, 0),
                         pipeline_mode=pl.Buffered(1)),
            pl.BlockSpec((None, 1, D_MODEL), lambda i: (ln_idx, 0, 0)),
            pl.BlockSpec((None, 1, D_MODEL), lambda i: (ln_idx, 0, 0)),
        ],
        out_specs=pl.BlockSpec((TMO, D_MODEL), lambda i: (i, 0)),
        out_shape=jax.ShapeDtypeStruct((M, D_MODEL), F32),
        compiler_params=pltpu.CompilerParams(
            dimension_semantics=("parallel",), vmem_limit_bytes=_vmem_limit(est)),
        name="outproj_ln",
    )(x, mix, w_out, ln_g, ln_b)


def _split3(x):
    h1 = x.astype(jnp.bfloat16).astype(F32)
    r = x - h1
    h2 = r.astype(jnp.bfloat16).astype(F32)
    return h1, h2, r - h2


def _even_prompt_kernel(bg_ref, cg_ref, xin_ref, q_ref, k_ref, v_ref, og_ref, gt_ref,
                        wc_ref, bgate_ref, mhg_ref,
                        mix_ref, conv_ref, c_out_ref, n_out_ref, m_out_ref,
                        cbuf, c_sc, n_sc, m_sc):
    c = pl.program_id(1)
    L = LC
    R = RE

    @pl.when(c == 0)
    def _():
        cbuf[0:SUBLANES, :] = jnp.zeros((SUBLANES, D_CONV), F32)
        c_sc[...] = jnp.zeros_like(c_sc)
        n_sc[...] = jnp.zeros_like(n_sc)
        m_sc[...] = jnp.zeros_like(m_sc)

    cx = cg_ref[...] * xin_ref[...]
    cbuf[SUBLANES:SUBLANES + R, :] = cx
    c1 = cbuf[SUBLANES - 1:SUBLANES - 1 + R, :]
    c2 = cbuf[SUBLANES - 2:SUBLANES - 2 + R, :]
    conv = c2 * wc_ref[0:1, :] + c1 * wc_ref[1:2, :] + cx * wc_ref[2:3, :]
    mix_ref[:, 0:D_CONV] = bg_ref[...] * conv
    conv_ref[...] = cbuf[SUBLANES + R - 2:SUBLANES + R, :]
    cbuf[0:SUBLANES, :] = cbuf[R:R + SUBLANES, :]

    row = lax.broadcasted_iota(jnp.int32, (L, L), 0)
    col = lax.broadcasted_iota(jnp.int32, (L, L), 1)
    causal = row >= col
    tril = causal.astype(F32)
    c_state = [c_sc[h] for h in range(HEADS)]
    n_state = [n_sc[h:h + 1, :] for h in range(HEADS)]
    m_state = [m_sc[h, 0:1, 0:1] for h in range(HEADS)]

    for s in range(R // L):
        r0, r1 = s * L, (s + 1) * L
        gt = gt_ref[r0:r1, :]
        li_all = gt[:, 0:HEADS] + bgate_ref[0:1, :]
        lf_all = _log_sigmoid(gt[:, HEADS:2 * HEADS] + bgate_ref[1:2, :])
        f1, f2, f3 = _split3(lf_all)
        b_all = _dot(tril, f1) + _dot(tril, f2) + _dot(tril, f3)
        z = jnp.concatenate([li_all, b_all, jnp.zeros((L, LANES - 2 * HEADS), F32)], axis=1)
        zt = z.T

        for g0 in range(0, HEADS, HEAD_GROUP):
            hs = range(g0, g0 + HEAD_GROUP)
            li_c = {h: li_all[:, h:h + 1] for h in hs}
            b_c = {h: b_all[:, h:h + 1] for h in hs}
            m0 = {h: m_state[h] for h in hs}
            e = {h: jnp.where(causal, zt[h:h + 1, :] - zt[HEADS + h:HEADS + h + 1, :], -jnp.inf)
                 for h in hs}
            inter = {h: b_c[h] + m0[h] for h in hs}
            m_t = {h: jnp.maximum(inter[h], b_c[h] + jnp.max(e[h], axis=-1, keepdims=True)) for h in hs}
            a_int = {h: jnp.exp(inter[h] - m_t[h]) for h in hs}
            qh = {h: q_ref[r0:r1, h * DK:(h + 1) * DK] for h in hs}
            kh = {h: k_ref[r0:r1, h * DK:(h + 1) * DK] * (DK ** -0.5) for h in hs}
            vh = {h: v_ref[r0:r1, h * DV:(h + 1) * DV] for h in hs}
            w = {h: jnp.exp(e[h] + (b_c[h] - m_t[h])) * _dot_nt(qh[h], kh[h]) for h in hs}
            num = {h: a_int[h] * _dot(qh[h], c_state[h]) + _dot(w[h], vh[h]) for h in hs}
            w_rows = {h: jnp.sum(sum(w[h][:, t * LANES:(t + 1) * LANES] for t in range(L // LANES)),
                                 axis=-1, keepdims=True) for h in hs}
            den = {h: a_int[h] * jnp.sum(qh[h] * n_state[h], axis=-1, keepdims=True) + w_rows[h]
                   for h in hs}
            hh = {h: num[h] / jnp.maximum(jnp.abs(den[h]), jnp.exp(-m_t[h])) for h in hs}
            m_new = {h: m_t[h][L - 1:L, :] for h in hs}
            b_last = {h: b_c[h][L - 1:L, :] for h in hs}
            w_end = {h: jnp.exp(b_last[h] - b_c[h] + li_c[h] - m_new[h]) for h in hs}
            decay = {h: jnp.exp(b_last[h] + m0[h] - m_new[h]) for h in hs}
            wk = {h: w_end[h] * kh[h] for h in hs}
            c_new = {h: decay[h] * c_state[h] + _dot(wk[h].T, vh[h]) for h in hs}
            n_new = {h: decay[h] * n_state[h] + jnp.sum(wk[h], axis=0, keepdims=True) for h in hs}
            hn = {h: _layer_norm(hh[h], mhg_ref[0:1, h * DV:(h + 1) * DV], None) for h in hs}
            for h in hs:
                c_state[h], n_state[h], m_state[h] = c_new[h], n_new[h], m_new[h]
                og = og_ref[r0:r1, h * DV:(h + 1) * DV]
                mix_ref[r0:r1, D_CONV + h * DV:D_CONV + (h + 1) * DV] = hn[h] * jax.nn.sigmoid(og)

    for h in range(HEADS):
        c_sc[h] = c_state[h]
        n_sc[h:h + 1, :] = n_state[h]
        m_sc[h] = jnp.broadcast_to(m_state[h], (SUBLANES, LANES))
        c_out_ref[h] = c_state[h]
        n_out_ref[h:h + 1, :] = n_state[h]
        m_out_ref[0:1, h:h + 1] = m_state[h]


def _even_prompt(proj, gates, w_conv, b_gates, mh_g, j):
    nck = SEQ // RE
    r = lambda b, c: b * nck + c
    est = 4 * (2 * (5 * RE * 1024 + 2 * RE * 512 + RE * 128) + 2 * RE * 2048 + 4 * RE * 1024
               + 3 * HEADS * DK * DV + 24 * LC * LC)
    return pl.pallas_call(
        _even_prompt_kernel,
        grid=(BATCH, nck),
        in_specs=[
            pl.BlockSpec((RE, D_CONV), lambda b, c: (r(b, c), 0)),
            pl.BlockSpec((RE, D_CONV), lambda b, c: (r(b, c), 1)),
            pl.BlockSpec((RE, D_CONV), lambda b, c: (r(b, c), 2)),
            pl.BlockSpec((RE, HEADS * DK), lambda b, c: (r(b, c), 6)),
            pl.BlockSpec((RE, HEADS * DK), lambda b, c: (r(b, c), 7)),
            pl.BlockSpec((RE, HEADS * DV), lambda b, c: (r(b, c), 4)),
            pl.BlockSpec((RE, HEADS * DV), lambda b, c: (r(b, c), 5)),
            pl.BlockSpec((RE, LANES), lambda b, c: (r(b, c), 0)),
            pl.BlockSpec((None, CONV_W, D_CONV), lambda b, c: (j, 0, 0)),
            pl.BlockSpec((None, 2, HEADS), lambda b, c: (j, 0, 0)),
            pl.BlockSpec((None, 1, HEADS * DV), lambda b, c: (j, 0, 0)),
        ],
        out_specs=[
            pl.BlockSpec((RE, D_MODEL), lambda b, c: (r(b, c), 0)),
            pl.BlockSpec((None, CONV_W - 1, D_CONV), lambda b, c: (b, 0, 0)),
            pl.BlockSpec((None, HEADS, DK, DV), lambda b, c: (b, 0, 0, 0)),
            pl.BlockSpec((None, HEADS, DK), lambda b, c: (b, 0, 0)),
            pl.BlockSpec((None, 1, HEADS), lambda b, c: (b, 0, 0)),
        ],
        out_shape=[
            jax.ShapeDtypeStruct((M, D_MODEL), F32),
            jax.ShapeDtypeStruct((BATCH, CONV_W - 1, D_CONV), F32),
            jax.ShapeDtypeStruct((BATCH, HEADS, DK, DV), F32),
            jax.ShapeDtypeStruct((BATCH, HEADS, DK), F32),
            jax.ShapeDtypeStruct((BATCH, 1, HEADS), F32),
        ],
        scratch_shapes=[
            pltpu.VMEM((RE + SUBLANES, D_CONV), F32),
            pltpu.VMEM((HEADS, DK, DV), F32),
            pltpu.VMEM((HEADS, DK), F32),
            pltpu.VMEM((HEADS, SUBLANES, LANES), F32),
        ],
        compiler_params=pltpu.CompilerParams(
            dimension_semantics=("arbitrary", "arbitrary"), vmem_limit_bytes=_vmem_limit(est)),
        name="even_prompt",
    )(proj, proj, proj, proj, proj, proj, proj, gates, w_conv, b_gates, mh_g)


def _sample_gates(gt, bgate_ref, m):
    li = gt[:, 0:HEADS] + bgate_ref[0:1, :]
    lf = _log_sigmoid(gt[:, HEADS:2 * HEADS] + bgate_ref[1:2, :])
    inter = lf + m
    m_t = jnp.maximum(inter, li)
    return jnp.exp(inter - m_t), jnp.exp(li - m_t), m_t


def _qk_cols_kernel(x_ref, w_ref, o_ref):
    t = _dot_nt(w_ref[...], x_ref[...])
    for s in range(DEC_BATCH // SB):
        o_ref[s] = t[:, s * SB:(s + 1) * SB]


def _qk_cols(x, w_even_t, j):
    nqk = 2 * HEADS * DK
    est = 4 * (2 * DEC_BATCH * D_MODEL + 2 * nqk * D_MODEL + 3 * nqk * LANES
               + 2 * (DEC_BATCH // SB) * nqk * LANES)
    return pl.pallas_call(
        _qk_cols_kernel,
        grid=(1,),
        in_specs=[
            pl.BlockSpec((DEC_BATCH, D_MODEL), lambda i: (MP // DEC_BATCH, 0)),
            pl.BlockSpec((None, nqk, D_MODEL), lambda i: (j, 3 * D_CONV // nqk, 0)),
        ],
        out_specs=pl.BlockSpec((DEC_BATCH // SB, nqk, SB), lambda i: (0, 0, 0)),
        out_shape=jax.ShapeDtypeStruct((DEC_BATCH // SB, nqk, SB), F32),
        compiler_params=pltpu.CompilerParams(
            dimension_semantics=("arbitrary",), vmem_limit_bytes=_vmem_limit(est)),
        name="qk_cols",
    )(x, w_even_t)


def _even_sample_state_kernel(c_ref, qk_ref, v_ref, gt_ref, m_ref, bgate_ref, *rest):
    c_out_ref, num_ref = rest[-2], rest[-1]
    a, wgt, _ = _sample_gates(gt_ref[...], bgate_ref, m_ref[...])
    for bi in range(SB):
        for h in range(HEADS):
            a_s = a[bi:bi + 1, h:h + 1]
            w_s = wgt[bi:bi + 1, h:h + 1]
            ch = c_ref[bi, h]
            qc = qk_ref[h * DK:(h + 1) * DK, bi:bi + 1]
            kc = qk_ref[(HEADS + h) * DK:(HEADS + h + 1) * DK, bi:bi + 1] * (DK ** -0.5)
            vr = v_ref[bi:bi + 1, h * DV:(h + 1) * DV]
            c_out_ref[bi, h] = a_s * ch + (w_s * kc) * vr
            num_ref[bi:bi + 1, h * DV:(h + 1) * DV] = jnp.sum(qc * ch, axis=0, keepdims=True)


def _even_sample_state(state_c, qk_cols, proj, gates, state_m, b_gates, j, c_prev):
    blk5 = (None, SB, HEADS, DK, DV)
    rb = MP // SB
    in_specs = [
        pl.BlockSpec(blk5, lambda i: (j, i, 0, 0, 0)),
        pl.BlockSpec((None, 2 * HEADS * DK, SB), lambda i: (i, 0, 0)),
        pl.BlockSpec((SB, HEADS * DV), lambda i: (rb + i, 4)),
        pl.BlockSpec((SB, LANES), lambda i: (rb + i, 0)),
        pl.BlockSpec((None, SB, HEADS), lambda i: (j, i, 0)),
        pl.BlockSpec((None, 2, HEADS), lambda i: (j, 0, 0)),
    ]
    args = [state_c, qk_cols, proj, gates, state_m, b_gates]
    aliases = {}
    if c_prev is not None:
        in_specs.append(pl.BlockSpec(memory_space=pl.ANY))
        args.append(c_prev)
        aliases = {len(args) - 1: 0}
    est = 4 * (4 * SB * HEADS * DK * DV + 4 * HEADS * DK * LANES)
    return pl.pallas_call(
        _even_sample_state_kernel,
        grid=(DEC_BATCH // SB,),
        in_specs=in_specs,
        out_specs=[
            pl.BlockSpec(blk5, lambda i: (j, i, 0, 0, 0)),
            pl.BlockSpec((SB, HEADS * DV), lambda i: (i, 0)),
        ],
        out_shape=[
            jax.ShapeDtypeStruct(state_c.shape, F32),
            jax.ShapeDtypeStruct((DEC_BATCH, HEADS * DV), F32),
        ],
        input_output_aliases=aliases,
        compiler_params=pltpu.CompilerParams(
            dimension_semantics=("arbitrary",), vmem_limit_bytes=_vmem_limit(est)),
        name="even_sample_state",
    )(*args)


def _even_sample_kernel(bg_ref, cg_ref, xin_ref, q_ref, k_ref, v_ref, og_ref, gt_ref,
                        num_ref, cst_ref, n_ref, m_ref, wc_ref, bgate_ref, mhg_ref, mixin_hbm,
                        mix_ref, conv_ref, n_out_ref, m_out_ref):
    del mixin_hbm
    cx = cg_ref[...] * xin_ref[...]
    st0 = cst_ref[:, 0:D_CONV]
    st1 = cst_ref[:, D_CONV:2 * D_CONV]
    conv = st0 * wc_ref[0:1, :] + st1 * wc_ref[1:2, :] + cx * wc_ref[2:3, :]
    mix_ref[:, 0:D_CONV] = bg_ref[...] * conv
    conv_ref[:, 0:D_CONV] = st1
    conv_ref[:, D_CONV:2 * D_CONV] = cx

    a, wgt, m_t = _sample_gates(gt_ref[...], bgate_ref, m_ref[...])
    m_out_ref[...] = m_t
    floor = jnp.exp(-m_t)
    for h in range(HEADS):
        a_h = a[:, h:h + 1]
        w_h = wgt[:, h:h + 1]
        qh = q_ref[:, h * DK:(h + 1) * DK]
        kh = k_ref[:, h * DK:(h + 1) * DK] * (DK ** -0.5)
        vh = v_ref[:, h * DV:(h + 1) * DV]
        nh = n_ref[:, h * DK:(h + 1) * DK]
        wt = w_h * jnp.sum(qh * kh, axis=-1, keepdims=True)
        num = a_h * num_ref[:, h * DV:(h + 1) * DV] + wt * vh
        den = a_h * jnp.sum(qh * nh, axis=-1, keepdims=True) + wt
        hh = num / jnp.maximum(jnp.abs(den), floor[:, h:h + 1])
        n_out_ref[:, h * DK:(h + 1) * DK] = a_h * nh + w_h * kh
        hn = _layer_norm(hh, mhg_ref[0:1, h * DV:(h + 1) * DV], None)
        og = og_ref[:, h * DV:(h + 1) * DV]
        mix_ref[:, D_CONV + h * DV:D_CONV + (h + 1) * DV] = hn * jax.nn.sigmoid(og)


def _even_sample(proj, gates, num, conv_st, n_st, m_st, w_conv, b_gates, mh_g, mixin, j):
    nb = DEC_BATCH
    rb = MP // nb
    est = 4 * 2 * (5 * nb * 1024 + 2 * nb * 512 + nb * 128 + nb * 1024 + nb * 2048 + nb * 512
                   + nb * 2048 + nb * 2048 + nb * 512)
    return pl.pallas_call(
        _even_sample_kernel,
        grid=(1,),
        in_specs=[
            pl.BlockSpec((nb, D_CONV), lambda i: (rb, 0)),
            pl.BlockSpec((nb, D_CONV), lambda i: (rb, 1)),
            pl.BlockSpec((nb, D_CONV), lambda i: (rb, 2)),
            pl.BlockSpec((nb, HEADS * DK), lambda i: (rb, 6)),
            pl.BlockSpec((nb, HEADS * DK), lambda i: (rb, 7)),
            pl.BlockSpec((nb, HEADS * DV), lambda i: (rb, 4)),
            pl.BlockSpec((nb, HEADS * DV), lambda i: (rb, 5)),
            pl.BlockSpec((nb, LANES), lambda i: (rb, 0)),
            pl.BlockSpec((nb, HEADS * DV), lambda i: (0, 0)),
            pl.BlockSpec((None, nb, 2 * D_CONV), lambda i: (j, 0, 0)),
            pl.BlockSpec((None, nb, HEADS * DK), lambda i: (j, 0, 0)),
            pl.BlockSpec((None, nb, HEADS), lambda i: (j, 0, 0)),
            pl.BlockSpec((None, CONV_W, D_CONV), lambda i: (j, 0, 0)),
            pl.BlockSpec((None, 2, HEADS), lambda i: (j, 0, 0)),
            pl.BlockSpec((None, 1, HEADS * DV), lambda i: (j, 0, 0)),
            pl.BlockSpec(memory_space=pl.ANY),
        ],
        out_specs=[
            pl.BlockSpec((nb, D_MODEL), lambda i: (rb, 0)),
            pl.BlockSpec((nb, 2 * D_CONV), lambda i: (0, 0)),
            pl.BlockSpec((nb, HEADS * DK), lambda i: (0, 0)),
            pl.BlockSpec((nb, HEADS), lambda i: (0, 0)),
        ],
        out_shape=[
            jax.ShapeDtypeStruct((M, D_MODEL), F32),
            jax.ShapeDtypeStruct((nb, 2 * D_CONV), F32),
            jax.ShapeDtypeStruct((nb, HEADS * DK), F32),
            jax.ShapeDtypeStruct((nb, HEADS), F32),
        ],
        input_output_aliases={15: 0},
        compiler_params=pltpu.CompilerParams(
            dimension_semantics=("arbitrary",), vmem_limit_bytes=_vmem_limit(est)),
        name="even_sample",
    )(proj, proj, proj, proj, proj, proj, proj, gates, num, conv_st, n_st, m_st,
      w_conv, b_gates, mh_g, mixin)


def _gmlp_norm(u_raw, v_raw, g, b):
    u = jax.nn.gelu(u_raw)
    vn = _layer_norm(jax.nn.gelu(v_raw), g, b)
    return u, vn


def _odd_prompt_kernel(p_ref, u_ref, v_ref, wp_ref, sc_ref, gmg_ref, gmb_ref, ws_ref, bst_ref,
                       mix_ref, pool_ref, gv_ref, pbuf, sbuf):
    s = pl.program_id(1)
    R = RO

    @pl.when(s == 0)
    def _():
        pbuf[0:HIST, :] = jnp.zeros((HIST, D_POOL), F32)
        sbuf[0:SUBLANES, :] = jnp.zeros((SUBLANES, D_POOL), F32)

    p = p_ref[...]
    pbuf[HIST:HIST + R, :] = p
    pos = s * R + lax.broadcasted_iota(jnp.int32, (R, 1), 0)
    for g, w in enumerate(POOL_WINDOWS):
        lo, hi = g * POOL_GW, (g + 1) * POOL_GW
        cur = pbuf[0:HIST + R, lo:hi]
        d = 1
        while d < w:
            sbuf[SUBLANES:SUBLANES + HIST + R, lo:hi] = cur
            cur = cur + sbuf[SUBLANES - d:SUBLANES - d + HIST + R, lo:hi]
            d *= 2
        win = cur[HIST:, :]
        cnt = jnp.minimum(w, pos + 1).astype(F32)
        diff = win / cnt - p[:, lo:hi]
        mix_ref[:, lo:hi] = _dot(diff, wp_ref[g]) * sc_ref[0:1, lo:hi]
    pool_ref[...] = pbuf[HIST + R - POOL_BUF:HIST + R, :]
    pbuf[0:HIST, :] = pbuf[R:R + HIST, :]

    u, vn = _gmlp_norm(u_ref[...], v_ref[...], gmg_ref[...], gmb_ref[...])
    L = GMLP_CHUNK
    tril = lax.broadcasted_iota(jnp.int32, (L, L), 0) >= lax.broadcasted_iota(jnp.int32, (L, L), 1)
    for g in range(D_GMLP // GMLP_GW):
        lo, hi = g * GMLP_GW, (g + 1) * GMLP_GW
        ws = jnp.where(tril, ws_ref[g], 0.0)
        bcol = bst_ref[:, g:g + 1]
        for ck in range(R // L):
            r0, r1 = ck * L, (ck + 1) * L
            sv = _dot(ws, vn[r0:r1, lo:hi]) + bcol
            mix_ref[r0:r1, D_POOL + lo:D_POOL + hi] = u[r0:r1, lo:hi] * sv
    gv_ref[...] = vn[R - L:R, :]


def _odd_prompt(proj, w_pool, pool_scale, gm_g, gm_b, w_spatial, bs_t, j):
    nrb = SEQ // RO
    r = lambda b, s: b * nrb + s
    est = 4 * (2 * 3 * RO * 1024 + 2 * RO * 2048 + 2 * 4 * 256 * 256 + 2 * 4 * 128 * 128
               + (RO + HIST) * 1024 + 8 * RO * 1024)
    return pl.pallas_call(
        _odd_prompt_kernel,
        grid=(BATCH, nrb),
        in_specs=[
            pl.BlockSpec((RO, D_POOL), lambda b, s: (r(b, s), 0)),
            pl.BlockSpec((RO, D_GMLP), lambda b, s: (r(b, s), 1)),
            pl.BlockSpec((RO, D_GMLP), lambda b, s: (r(b, s), 2)),
            pl.BlockSpec((None, 4, POOL_GW, POOL_GW), lambda b, s: (j, 0, 0, 0)),
            pl.BlockSpec((None, 1, D_POOL), lambda b, s: (j, 0, 0)),
            pl.BlockSpec((None, 1, D_GMLP), lambda b, s: (j, 0, 0)),
            pl.BlockSpec((None, 1, D_GMLP), lambda b, s: (j, 0, 0)),
            pl.BlockSpec((None, 4, GMLP_CHUNK, GMLP_CHUNK), lambda b, s: (j, 0, 0, 0)),
            pl.BlockSpec((None, GMLP_CHUNK, 4), lambda b, s: (j, 0, 0)),
        ],
        out_specs=[
            pl.BlockSpec((RO, D_MODEL), lambda b, s: (r(b, s), 0)),
            pl.BlockSpec((None, POOL_BUF, D_POOL), lambda b, s: (b, 0, 0)),
            pl.BlockSpec((None, GMLP_CHUNK, D_GMLP), lambda b, s: (b, 0, 0)),
        ],
        out_shape=[
            jax.ShapeDtypeStruct((M, D_MODEL), F32),
            jax.ShapeDtypeStruct((BATCH, POOL_BUF, D_POOL), F32),
            jax.ShapeDtypeStruct((BATCH, GMLP_CHUNK, D_GMLP), F32),
        ],
        scratch_shapes=[pltpu.VMEM((RO + HIST, D_POOL), F32),
                        pltpu.VMEM((SUBLANES + RO + HIST, D_POOL), F32)],
        compiler_params=pltpu.CompilerParams(
            dimension_semantics=("arbitrary", "arbitrary"), vmem_limit_bytes=_vmem_limit(est)),
        name="odd_prompt",
    )(proj, proj, proj, w_pool, pool_scale, gm_g, gm_b, w_spatial, bs_t)


def _odd_sample_kernel(p_ref, u_ref, v_ref, st_ref, wp_ref, sc_ref, gmg_ref, gmb_ref, ws_ref, bst_ref,
                       *rest):
    mix_ref, pool_ref, gv_ref = rest[-3:]
    p = p_ref[...]
    for r in range(POOL_BUF - 1):
        pool_ref[r] = st_ref[r + 1]
    pool_ref[POOL_BUF - 1] = p
    for g, w in enumerate(POOL_WINDOWS):
        lo, hi = g * POOL_GW, (g + 1) * POOL_GW
        win = p[:, lo:hi]
        for jj in range(1, w):
            win = win + st_ref[POOL_BUF - jj, :, lo:hi]
        cnt = float(min(w, PAST_LEN + 1))
        diff = win / cnt - p[:, lo:hi]
        mix_ref[:, lo:hi] = _dot(diff, wp_ref[g]) * sc_ref[0:1, lo:hi]
    u, vn = _gmlp_norm(u_ref[...], v_ref[...], gmg_ref[...], gmb_ref[...])
    gv_ref[...] = vn
    for g in range(D_GMLP // GMLP_GW):
        lo, hi = g * GMLP_GW, (g + 1) * GMLP_GW
        sv = ws_ref[g, 0:1, 0:1] * vn[:, lo:hi] + bst_ref[0:1, g:g + 1]
        mix_ref[:, D_POOL + lo:D_POOL + hi] = u[:, lo:hi] * sv


def _odd_sample(proj, pool_st, w_pool, pool_scale, gm_g, gm_b, w_spatial, bs_t, mixin, pool_prev, j):
    nb = DEC_BATCH
    rb = MP // nb
    st_blk = (None, POOL_BUF, nb, D_POOL)
    in_specs = [
        pl.BlockSpec((nb, D_POOL), lambda i: (rb, 0)),
        pl.BlockSpec((nb, D_GMLP), lambda i: (rb, 1)),
        pl.BlockSpec((nb, D_GMLP), lambda i: (rb, 2)),
        pl.BlockSpec(st_blk, lambda i: (j, 0, 0, 0)),
        pl.BlockSpec((None, 4, POOL_GW, POOL_GW), lambda i: (j, 0, 0, 0)),
        pl.BlockSpec((None, 1, D_POOL), lambda i: (j, 0, 0)),
        pl.BlockSpec((None, 1, D_GMLP), lambda i: (j, 0, 0)),
        pl.BlockSpec((None, 1, D_GMLP), lambda i: (j, 0, 0)),
        pl.BlockSpec((None, 4, GMLP_CHUNK, GMLP_CHUNK), lambda i: (j, 0, 0, 0)),
        pl.BlockSpec((None, GMLP_CHUNK, 4), lambda i: (j, 0, 0)),
        pl.BlockSpec(memory_space=pl.ANY),
    ]
    args = [proj, proj, proj, pool_st, w_pool, pool_scale, gm_g, gm_b, w_spatial, bs_t, mixin]
    aliases = {len(args) - 1: 0}
    if pool_prev is not None:
        in_specs.append(pl.BlockSpec(memory_space=pl.ANY))
        args.append(pool_prev)
        aliases[len(args) - 1] = 1
    est = 4 * (4 * POOL_BUF * nb * D_POOL + 2 * (3 * nb * 1024 + 4 * 256 * 256 + 4 * 128 * 128
                                                + nb * 2048 + nb * 1024))
    return pl.pallas_call(
        _odd_sample_kernel,
        grid=(1,),
        in_specs=in_specs,
        out_specs=[
            pl.BlockSpec((nb, D_MODEL), lambda i: (rb, 0)),
            pl.BlockSpec(st_blk, lambda i: (j, 0, 0, 0)),
            pl.BlockSpec((nb, D_GMLP), lambda i: (0, 0)),
        ],
        out_shape=[
            jax.ShapeDtypeStruct((M, D_MODEL), F32),
            jax.ShapeDtypeStruct(pool_st.shape, F32),
            jax.ShapeDtypeStruct((nb, D_GMLP), F32),
        ],
        input_output_aliases=aliases,
        compiler_params=pltpu.CompilerParams(
            dimension_semantics=("arbitrary",), vmem_limit_bytes=_vmem_limit(est)),
        name="odd_sample",
    )(*args)


def kernel(x_prompt, x_sample, state_conv, state_mlstm_C, state_mlstm_n, state_mlstm_m, state_pool,
           ln_g, ln_b, w_ffn_in, w_ffn_out, w_in_even, b_gates_even, w_conv, mh_norm_g, w_out_even,
           w_in_odd, w_pool, pool_scale, gm_ln_g, gm_ln_b, w_spatial, b_spatial, w_out_odd):
    n_even, n_odd = w_in_even.shape[0], w_in_odd.shape[0]
    x = jnp.concatenate([x_prompt.reshape(MP, D_MODEL), x_sample.reshape(DEC_BATCH, D_MODEL)], axis=0)

    ln_g3 = ln_g.reshape(DEPTH * 3, 1, D_MODEL)
    ln_b3 = ln_b.reshape(DEPTH * 3, 1, D_MODEL)
    w_even_t = jnp.swapaxes(w_in_even, 1, 2)
    w_gate_t = jnp.pad(w_even_t[:, EVEN_MAIN:, :], ((0, 0), (0, LANES - 2 * HEADS), (0, 0)))
    pool_st = jnp.swapaxes(state_pool, 1, 2)
    mh_g3 = mh_norm_g.reshape(n_even, 1, HEADS * DV)
    conv_st = state_conv.reshape(n_even, DEC_BATCH, (CONV_W - 1) * D_CONV)
    n_st = state_mlstm_n.reshape(n_even, DEC_BATCH, HEADS * DK)
    scale3 = pool_scale.reshape(n_odd, 1, D_POOL)
    gm_g3 = gm_ln_g.reshape(n_odd, 1, D_GMLP)
    gm_b3 = gm_ln_b.reshape(n_odd, 1, D_GMLP)
    bs_t = jnp.swapaxes(b_spatial, 1, 2)

    conv_p, conv_s, c_p, n_p, n_s, m_p, m_s = [], [], [], [], [], [], []
    pool_p, gv_p, gv_s = [], [], []
    c_s = None
    pool_s = None

    for layer in range(DEPTH):
        j = layer // 2
        x = _ffn(x, w_ffn_in, w_ffn_out, ln_g3, ln_b3, layer, 0, 3 * layer)
        if layer % 2 == 0:
            proj, gates = _proj(x, w_even_t, j, 0, EVEN_MAIN, "proj_even", w_transposed=True,
                                w_gate_t=w_gate_t)
            proj = _proj(x, w_even_t, j, 1, EVEN_MAIN, "proj_even", w_transposed=True, prev=proj)
            mixin, cv, cc, nn, mm = _even_prompt(proj, gates, w_conv, b_gates_even, mh_g3, j)
            conv_p.append(cv)
            c_p.append(cc)
            n_p.append(nn)
            m_p.append(mm.reshape(BATCH, HEADS))
            qk_cols = _qk_cols(x, w_even_t, j)
            c_s, num = _even_sample_state(state_mlstm_C, qk_cols, proj, gates, state_mlstm_m,
                                          b_gates_even, j, c_s)
            mixin, cvs, nns, mms = _even_sample(proj, gates, num, conv_st, n_st, state_mlstm_m,
                                                w_conv, b_gates_even, mh_g3, mixin, j)
            conv_s.append(cvs.reshape(DEC_BATCH, CONV_W - 1, D_CONV))
            n_s.append(nns.reshape(DEC_BATCH, HEADS, DK))
            m_s.append(mms)
            x = _outproj(x, mixin, w_out_even, ln_g3, ln_b3, j, 3 * layer + 1)
        else:
            proj = _proj(x, w_in_odd, j, 0, ODD_IN, "proj_odd")
            mixin, pp, gv = _odd_prompt(proj, w_pool, scale3, gm_g3, gm_b3, w_spatial, bs_t, j)
            pool_p.append(pp)
            gv_p.append(gv)
            mixin, pool_s, gvs = _odd_sample(proj, pool_st, w_pool, scale3, gm_g3, gm_b3, w_spatial, bs_t,
                                             mixin, pool_s, j)
            gv_s.append(gvs.reshape(DEC_BATCH, 1, D_GMLP))
            x = _outproj(x, mixin, w_out_odd, ln_g3, ln_b3, j, 3 * layer + 1)
        x = _ffn(x, w_ffn_in, w_ffn_out, ln_g3, ln_b3, layer, 1, 3 * layer + 2,
                 split_out=layer == DEPTH - 1)

    y_prompt = x[0].reshape(BATCH, SEQ, D_MODEL)
    y_sample = x[1].reshape(DEC_BATCH, 1, D_MODEL)
    return (y_prompt, y_sample,
            jnp.stack(conv_p), jnp.stack(conv_s),
            jnp.stack(c_p), c_s,
            jnp.stack(n_p), jnp.stack(n_s),
            jnp.stack(m_p), jnp.stack(m_s),
            jnp.stack(pool_p), jnp.swapaxes(pool_s, 1, 2),
            jnp.stack(gv_p), jnp.stack(gv_s))
```

```python
import functools

import jax
import jax.numpy as jnp
from jax import lax
from jax.experimental import pallas as pl
from jax.experimental.pallas import tpu as pltpu

F32 = jnp.float32

D_MODEL = 2048
BATCH = 4
SEQ = 2048
DEPTH = 4
DEC_BATCH = 128
PAST_LEN = 16384
D_FF = 5632
D_CONV = 1024
CONV_W = 3
HEADS = 4
DK = 128
DV = 256
D_POOL = 1024
POOL_WINDOWS = (2, 4, 8, 16)
POOL_GW = 256
POOL_BUF = 15
D_GMLP = 1024
GMLP_GW = 256
GMLP_CHUNK = 128
ALPHA = (2 * DEPTH) ** 0.25
LN_EPS = 1e-5
EVEN_MAIN = 3 * D_CONV + 2 * HEADS * DK + 2 * HEADS * DV
ODD_IN = D_POOL + 2 * D_GMLP

MP = BATCH * SEQ
M = MP + DEC_BATCH

LANES = 128
SUBLANES = 8
VMEM_BYTES_V7X = 64 * 1024 * 1024
VMEM_RESERVED_BYTES = 4 * 1024 * 1024
VMEM_TEMP_BYTES = 8 * 1024 * 1024

TM = 832
TF = 512
TFS = 256
TN = 512
TMP = 520
PROJ_COLS = 3072
TMO = 416
LC = 256
RE = 256
HEAD_GROUP = 4
RO = 256
SB = 16
STATE_GROUP = 4
HIST = 16


def _vmem_limit(block_bytes):
    return int(min(VMEM_BYTES_V7X - VMEM_RESERVED_BYTES, block_bytes + VMEM_TEMP_BYTES))


def _layer_norm(y, g, b, eps=LN_EPS):
    mu = jnp.mean(y, axis=-1, keepdims=True)
    yc = y - mu
    var = jnp.mean(yc * yc, axis=-1, keepdims=True)
    out = yc * lax.rsqrt(var + eps) * g
    if b is not None:
        out = out + b
    return out


def _log_sigmoid(x):
    return -(jnp.maximum(-x, 0.0) + jnp.log1p(jnp.exp(-jnp.abs(x))))


def _dot(a, b):
    return jnp.dot(a, b, preferred_element_type=F32)


def _dot_nt(a, b):
    return lax.dot_general(a, b, (((1,), (1,)), ((), ())), preferred_element_type=F32)


def _ffn_kernel(x_ref, wg_ref, wu_ref, wo_ref, g_ref, b_ref, o_ref, *maybe_sample_ref, nf):
    f = pl.program_id(1)

    def accumulate(first_chunk):
        x = x_ref[...]
        for c in range(TF // TFS):
            c0, c1 = c * TFS, (c + 1) * TFS
            gate = _dot(x, wg_ref[:, c0:c1])
            up = _dot(x, wu_ref[:, c0:c1])
            h = gate * jax.nn.sigmoid(gate) * up
            for n in range(D_MODEL // TN):
                n0, n1 = n * TN, (n + 1) * TN
                part = _dot(h, wo_ref[c0:c1, n0:n1])
                if first_chunk and c == 0:
                    o_ref[:, n0:n1] = (2.0 * ALPHA) * x_ref[:, n0:n1] + part
                else:
                    o_ref[:, n0:n1] += part

    @pl.when(f == 0)
    def _():
        accumulate(True)

    @pl.when(f > 0)
    def _():
        accumulate(False)

    @pl.when(f == nf - 1)
    def _():
        o_ref[...] = _layer_norm(o_ref[...], g_ref[...], b_ref[...], eps=4.0 * LN_EPS)

    if maybe_sample_ref:
        @pl.when((f == nf - 1) & (pl.program_id(0) == M // TM - 1))
        def _():
            maybe_sample_ref[0][...] = o_ref[TM - DEC_BATCH:, :]


def _ffn(x, w_ffn_in, w_ffn_out, ln_g, ln_b, layer, which, ln_idx, split_out=False):
    nf = D_FF // TF
    est = 4 * (4 * TM * D_MODEL + 2 * 3 * D_MODEL * TF + 4 * TM * TFS + TM * TN)
    out_specs = pl.BlockSpec((TM, D_MODEL), lambda i, f: (i, 0))
    out_shape = jax.ShapeDtypeStruct((M, D_MODEL), F32)
    if split_out:
        out_specs = [out_specs, pl.BlockSpec((DEC_BATCH, D_MODEL), lambda i, f: (0, 0))]
        out_shape = [jax.ShapeDtypeStruct((MP, D_MODEL), F32),
                     jax.ShapeDtypeStruct((DEC_BATCH, D_MODEL), F32)]
    return pl.pallas_call(
        functools.partial(_ffn_kernel, nf=nf),
        grid=(M // TM, nf),
        in_specs=[
            pl.BlockSpec((TM, D_MODEL), lambda i, f: (i, 0)),
            pl.BlockSpec((None, None, D_MODEL, TF), lambda i, f: (layer, which, 0, f)),
            pl.BlockSpec((None, None, D_MODEL, TF), lambda i, f: (layer, which, 0, nf + f)),
            pl.BlockSpec((None, None, TF, D_MODEL), lambda i, f: (layer, which, f, 0)),
            pl.BlockSpec((None, 1, D_MODEL), lambda i, f: (ln_idx, 0, 0)),
            pl.BlockSpec((None, 1, D_MODEL), lambda i, f: (ln_idx, 0, 0)),
        ],
        out_specs=out_specs,
        out_shape=out_shape,
        compiler_params=pltpu.CompilerParams(
            dimension_semantics=("arbitrary", "arbitrary"), vmem_limit_bytes=_vmem_limit(est)),
        name="ffn_ln",
    )(x, w_ffn_in, w_ffn_in, w_ffn_out, ln_g, ln_b)


def _proj_kernel(x_ref, w_ref, *rest, w_transposed, with_gates, aliased):
    rest = rest[1:] if aliased else rest
    if with_gates:
        wgate_ref, o_ref, gate_ref = rest
        gate_ref[...] = _dot_nt(x_ref[...], wgate_ref[...])
    else:
        (o_ref,) = rest
    for n in range(PROJ_COLS // TN):
        lo, hi = n * TN, (n + 1) * TN
        if w_transposed:
            o_ref[:, lo:hi] = _dot_nt(x_ref[...], w_ref[lo:hi, :])
        else:
            o_ref[:, lo:hi] = _dot(x_ref[...], w_ref[:, lo:hi])


def _proj(x, w, layer_idx, col_block, n_total, name, w_transposed=False, w_gate_t=None, prev=None):
    est = 4 * (PROJ_COLS * D_MODEL + 2 * TMP * D_MODEL + 2 * TMP * PROJ_COLS + 2 * TMP * TN)
    if w_transposed:
        w_spec = pl.BlockSpec((None, PROJ_COLS, D_MODEL), lambda i: (layer_idx, col_block, 0),
                              pipeline_mode=pl.Buffered(1))
    else:
        w_spec = pl.BlockSpec((None, D_MODEL, PROJ_COLS), lambda i: (layer_idx, 0, col_block),
                              pipeline_mode=pl.Buffered(1))
    in_specs = [pl.BlockSpec((TMP, D_MODEL), lambda i: (i, 0)), w_spec]
    args = [x, w]
    aliases = {}
    if prev is not None:
        in_specs.append(pl.BlockSpec(memory_space=pl.ANY))
        args.append(prev)
        aliases = {2: 0}
    out_specs = pl.BlockSpec((TMP, PROJ_COLS), lambda i: (i, col_block))
    out_shape = jax.ShapeDtypeStruct((M, n_total), F32)
    if w_gate_t is not None:
        in_specs.append(pl.BlockSpec((None, LANES, D_MODEL), lambda i: (layer_idx, 0, 0)))
        args.append(w_gate_t)
        out_specs = [out_specs, pl.BlockSpec((TMP, LANES), lambda i: (i, 0))]
        out_shape = [out_shape, jax.ShapeDtypeStruct((M, LANES), F32)]
        est += 4 * (2 * LANES * D_MODEL + 3 * TMP * LANES)
    return pl.pallas_call(
        functools.partial(_proj_kernel, w_transposed=w_transposed, with_gates=w_gate_t is not None,
                          aliased=prev is not None),
        grid=(M // TMP,),
        in_specs=in_specs,
        out_specs=out_specs,
        out_shape=out_shape,
        input_output_aliases=aliases,
        compiler_params=pltpu.CompilerParams(
            dimension_semantics=("parallel",), vmem_limit_bytes=_vmem_limit(est)),
        name=name,
    )(*args)


def _outproj_kernel(x_ref, a_ref, w_ref, g_ref, b_ref, o_ref):
    y = ALPHA * x_ref[...] + _dot(a_ref[...], w_ref[...])
    o_ref[...] = _layer_norm(y, g_ref[...], b_ref[...])


def _outproj(x, mix, w_out, ln_g, ln_b, layer_idx, ln_idx):
    est = 4 * (D_MODEL * D_MODEL + 9 * TMO * D_MODEL)
    return pl.pallas_call(
        _outproj_kernel,
        grid=(M // TMO,),
        in_specs=[
            pl.BlockSpec((TMO, D_MODEL), lambda i: (i, 0)),
            pl.BlockSpec((TMO, D_MODEL), lambda i: (i, 0)),
            pl.BlockSpec((None, D_MODEL, D_MODEL), lambda i: (layer_idx, 0, 0),
                         pipeline_mode=pl.Buffered(1)),
            pl.BlockSpec((None, 1, D_MODEL), lambda i: (ln_idx, 0, 0)),
            pl.BlockSpec((None, 1, D_MODEL), lambda i: (ln_idx, 0, 0)),
        ],
        out_specs=pl.BlockSpec((TMO, D_MODEL), lambda i: (i, 0)),
        out_shape=jax.ShapeDtypeStruct((M, D_MODEL), F32),
        compiler_params=pltpu.CompilerParams(
            dimension_semantics=("parallel",), vmem_limit_bytes=_vmem_limit(est)),
        name="outproj_ln",
    )(x, mix, w_out, ln_g, ln_b)


def _split3(x):
    h1 = x.astype(jnp.bfloat16).astype(F32)
    r = x - h1
    h2 = r.astype(jnp.bfloat16).astype(F32)
    return h1, h2, r - h2


def _even_prompt_kernel(bg_ref, cg_ref, xin_ref, q_ref, k_ref, v_ref, og_ref, gt_ref,
                        wc_ref, bgate_ref, mhg_ref,
                        mix_ref, conv_ref, c_out_ref, n_out_ref, m_out_ref,
                        cbuf, c_sc, n_sc, m_sc):
    c = pl.program_id(1)
    L = LC
    R = RE

    @pl.when(c == 0)
    def _():
        cbuf[0:SUBLANES, :] = jnp.zeros((SUBLANES, D_CONV), F32)
        c_sc[...] = jnp.zeros_like(c_sc)
        n_sc[...] = jnp.zeros_like(n_sc)
        m_sc[...] = jnp.zeros_like(m_sc)

    cx = cg_ref[...] * xin_ref[...]
    cbuf[SUBLANES:SUBLANES + R, :] = cx
    c1 = cbuf[SUBLANES - 1:SUBLANES - 1 + R, :]
    c2 = cbuf[SUBLANES - 2:SUBLANES - 2 + R, :]
    conv = c2 * wc_ref[0:1, :] + c1 * wc_ref[1:2, :] + cx * wc_ref[2:3, :]
    mix_ref[:, 0:D_CONV] = bg_ref[...] * conv
    conv_ref[...] = cbuf[SUBLANES + R - 2:SUBLANES + R, :]
    cbuf[0:SUBLANES, :] = cbuf[R:R + SUBLANES, :]

    row = lax.broadcasted_iota(jnp.int32, (L, L), 0)
    col = lax.broadcasted_iota(jnp.int32, (L, L), 1)
    causal = row >= col
    tril = causal.astype(F32)
    c_state = [c_sc[h] for h in range(HEADS)]
    n_state = [n_sc[h:h + 1, :] for h in range(HEADS)]
    m_state = [m_sc[h, 0:1, 0:1] for h in range(HEADS)]

    for s in range(R // L):
        r0, r1 = s * L, (s + 1) * L
        gt = gt_ref[r0:r1, :]
        li_all = gt[:, 0:HEADS] + bgate_ref[0:1, :]
        lf_all = _log_sigmoid(gt[:, HEADS:2 * HEADS] + bgate_ref[1:2, :])
        f1, f2, f3 = _split3(lf_all)
        b_all = _dot(tril, f1) + _dot(tril, f2) + _dot(tril, f3)
        z = jnp.concatenate([li_all, b_all, jnp.zeros((L, LANES - 2 * HEADS), F32)], axis=1)
        zt = z.T

        for g0 in range(0, HEADS, HEAD_GROUP):
            hs = range(g0, g0 + HEAD_GROUP)
            li_c = {h: li_all[:, h:h + 1] for h in hs}
            b_c = {h: b_all[:, h:h + 1] for h in hs}
            m0 = {h: m_state[h] for h in hs}
            e = {h: jnp.where(causal, zt[h:h + 1, :] - zt[HEADS + h:HEADS + h + 1, :], -jnp.inf)
                 for h in hs}
            inter = {h: b_c[h] + m0[h] for h in hs}
            m_t = {h: jnp.maximum(inter[h], b_c[h] + jnp.max(e[h], axis=-1, keepdims=True)) for h in hs}
            a_int = {h: jnp.exp(inter[h] - m_t[h]) for h in hs}
            qh = {h: q_ref[r0:r1, h * DK:(h + 1) * DK] for h in hs}
            kh = {h: k_ref[r0:r1, h * DK:(h + 1) * DK] * (DK ** -0.5) for h in hs}
            vh = {h: v_ref[r0:r1, h * DV:(h + 1) * DV] for h in hs}
            w = {h: jnp.exp(e[h] + (b_c[h] - m_t[h])) * _dot_nt(qh[h], kh[h]) for h in hs}
            num = {h: a_int[h] * _dot(qh[h], c_state[h]) + _dot(w[h], vh[h]) for h in hs}
            w_rows = {h: jnp.sum(sum(w[h][:, t * LANES:(t + 1) * LANES] for t in range(L // LANES)),
                                 axis=-1, keepdims=True) for h in hs}
            den = {h: a_int[h] * jnp.sum(qh[h] * n_state[h], axis=-1, keepdims=True) + w_rows[h]
                   for h in hs}
            hh = {h: num[h] / jnp.maximum(jnp.abs(den[h]), jnp.exp(-m_t[h])) for h in hs}
            m_new = {h: m_t[h][L - 1:L, :] for h in hs}
            b_last = {h: b_c[h][L - 1:L, :] for h in hs}
            w_end = {h: jnp.exp(b_last[h] - b_c[h] + li_c[h] - m_new[h]) for h in hs}
            decay = {h: jnp.exp(b_last[h] + m0[h] - m_new[h]) for h in hs}
            wk = {h: w_end[h] * kh[h] for h in hs}
            c_new = {h: decay[h] * c_state[h] + _dot(wk[h].T, vh[h]) for h in hs}
            n_new = {h: decay[h] * n_state[h] + jnp.sum(wk[h], axis=0, keepdims=True) for h in hs}
            hn = {h: _layer_norm(hh[h], mhg_ref[0:1, h * DV:(h + 1) * DV], None) for h in hs}
            for h in hs:
                c_state[h], n_state[h], m_state[h] = c_new[h], n_new[h], m_new[h]
                og = og_ref[r0:r1, h * DV:(h + 1) * DV]
                mix_ref[r0:r1, D_CONV + h * DV:D_CONV + (h + 1) * DV] = hn[h] * jax.nn.sigmoid(og)

    for h in range(HEADS):
        c_sc[h] = c_state[h]
        n_sc[h:h + 1, :] = n_state[h]
        m_sc[h] = jnp.broadcast_to(m_state[h], (SUBLANES, LANES))
        c_out_ref[h] = c_state[h]
        n_out_ref[h:h + 1, :] = n_state[h]
        m_out_ref[0:1, h:h + 1] = m_state[h]


def _even_prompt(proj, gates, w_conv, b_gates, mh_g, j):
    nck = SEQ // RE
    r = lambda b, c: b * nck + c
    est = 4 * (2 * (5 * RE * 1024 + 2 * RE * 512 + RE * 128) + 2 * RE * 2048 + 4 * RE * 1024
               + 3 * HEADS * DK * DV + 24 * LC * LC)
    return pl.pallas_call(
        _even_prompt_kernel,
        grid=(BATCH, nck),
        in_specs=[
            pl.BlockSpec((RE, D_CONV), lambda b, c: (r(b, c), 0)),
            pl.BlockSpec((RE, D_CONV), lambda b, c: (r(b, c), 1)),
            pl.BlockSpec((RE, D_CONV), lambda b, c: (r(b, c), 2)),
            pl.BlockSpec((RE, HEADS * DK), lambda b, c: (r(b, c), 6)),
            pl.BlockSpec((RE, HEADS * DK), lambda b, c: (r(b, c), 7)),
            pl.BlockSpec((RE, HEADS * DV), lambda b, c: (r(b, c), 4)),
            pl.BlockSpec((RE, HEADS * DV), lambda b, c: (r(b, c), 5)),
            pl.BlockSpec((RE, LANES), lambda b, c: (r(b, c), 0)),
            pl.BlockSpec((None, CONV_W, D_CONV), lambda b, c: (j, 0, 0)),
            pl.BlockSpec((None, 2, HEADS), lambda b, c: (j, 0, 0)),
            pl.BlockSpec((None, 1, HEADS * DV), lambda b, c: (j, 0, 0)),
        ],
        out_specs=[
            pl.BlockSpec((RE, D_MODEL), lambda b, c: (r(b, c), 0)),
            pl.BlockSpec((None, CONV_W - 1, D_CONV), lambda b, c: (b, 0, 0)),
            pl.BlockSpec((None, HEADS, DK, DV), lambda b, c: (b, 0, 0, 0)),
            pl.BlockSpec((None, HEADS, DK), lambda b, c: (b, 0, 0)),
            pl.BlockSpec((None, 1, HEADS), lambda b, c: (b, 0, 0)),
        ],
        out_shape=[
            jax.ShapeDtypeStruct((M, D_MODEL), F32),
            jax.ShapeDtypeStruct((BATCH, CONV_W - 1, D_CONV), F32),
            jax.ShapeDtypeStruct((BATCH, HEADS, DK, DV), F32),
            jax.ShapeDtypeStruct((BATCH, HEADS, DK), F32),
            jax.ShapeDtypeStruct((BATCH, 1, HEADS), F32),
        ],
        scratch_shapes=[
            pltpu.VMEM((RE + SUBLANES, D_CONV), F32),
            pltpu.VMEM((HEADS, DK, DV), F32),
            pltpu.VMEM((HEADS, DK), F32),
            pltpu.VMEM((HEADS, SUBLANES, LANES), F32),
        ],
        compiler_params=pltpu.CompilerParams(
            dimension_semantics=("arbitrary", "arbitrary"), vmem_limit_bytes=_vmem_limit(est)),
        name="even_prompt",
    )(proj, proj, proj, proj, proj, proj, proj, gates, w_conv, b_gates, mh_g)


def _sample_gates(gt, bgate_ref, m):
    li = gt[:, 0:HEADS] + bgate_ref[0:1, :]
    lf = _log_sigmoid(gt[:, HEADS:2 * HEADS] + bgate_ref[1:2, :])
    inter = lf + m
    m_t = jnp.maximum(inter, li)
    return jnp.exp(inter - m_t), jnp.exp(li - m_t), m_t


def _qk_cols_kernel(x_ref, w_ref, o_ref):
    t = _dot_nt(w_ref[...], x_ref[...])
    for s in range(DEC_BATCH // SB):
        o_ref[s] = t[:, s * SB:(s + 1) * SB]


def _qk_cols(x, w_even_t, j):
    nqk = 2 * HEADS * DK
    est = 4 * (2 * DEC_BATCH * D_MODEL + 2 * nqk * D_MODEL + 3 * nqk * LANES
               + 2 * (DEC_BATCH // SB) * nqk * LANES)
    return pl.pallas_call(
        _qk_cols_kernel,
        grid=(1,),
        in_specs=[
            pl.BlockSpec((DEC_BATCH, D_MODEL), lambda i: (MP // DEC_BATCH, 0)),
            pl.BlockSpec((None, nqk, D_MODEL), lambda i: (j, 3 * D_CONV // nqk, 0)),
        ],
        out_specs=pl.BlockSpec((DEC_BATCH // SB, nqk, SB), lambda i: (0, 0, 0)),
        out_shape=jax.ShapeDtypeStruct((DEC_BATCH // SB, nqk, SB), F32),
        compiler_params=pltpu.CompilerParams(
            dimension_semantics=("arbitrary",), vmem_limit_bytes=_vmem_limit(est)),
        name="qk_cols",
    )(x, w_even_t)


def _even_sample_state_kernel(c_ref, qk_ref, v_ref, gt_ref, m_ref, bgate_ref, *rest):
    c_out_ref, num_ref = rest[-2], rest[-1]
    a, wgt, _ = _sample_gates(gt_ref[...], bgate_ref, m_ref[...])
    for b0 in range(0, SB, STATE_GROUP):
        ids = [(bi, h) for bi in range(b0, b0 + STATE_GROUP) for h in range(HEADS)]
        a_s = {k: a[k[0]:k[0] + 1, k[1]:k[1] + 1] for k in ids}
        w_s = {k: wgt[k[0]:k[0] + 1, k[1]:k[1] + 1] for k in ids}
        qc = {k: qk_ref[k[1] * DK:(k[1] + 1) * DK, k[0]:k[0] + 1] for k in ids}
        kc = {k: qk_ref[(HEADS + k[1]) * DK:(HEADS + k[1] + 1) * DK, k[0]:k[0] + 1] * (DK ** -0.5)
              for k in ids}
        wk = {k: w_s[k] * kc[k] for k in ids}
        vr = {k: v_ref[k[0]:k[0] + 1, k[1] * DV:(k[1] + 1) * DV] for k in ids}
        for k in ids:
            ch = c_ref[k[0], k[1]]
            c_out_ref[k[0], k[1]] = a_s[k] * ch + wk[k] * vr[k]
            num_ref[k[0]:k[0] + 1, k[1] * DV:(k[1] + 1) * DV] = jnp.sum(qc[k] * ch, axis=0, keepdims=True)


def _even_sample_state(state_c, qk_cols, proj, gates, state_m, b_gates, j, c_prev):
    blk5 = (None, SB, HEADS, DK, DV)
    rb = MP // SB
    in_specs = [
        pl.BlockSpec(blk5, lambda i: (j, i, 0, 0, 0)),
        pl.BlockSpec((None, 2 * HEADS * DK, SB), lambda i: (i, 0, 0)),
        pl.BlockSpec((SB, HEADS * DV), lambda i: (rb + i, 4)),
        pl.BlockSpec((SB, LANES), lambda i: (rb + i, 0)),
        pl.BlockSpec((None, SB, HEADS), lambda i: (j, i, 0)),
        pl.BlockSpec((None, 2, HEADS), lambda i: (j, 0, 0)),
    ]
    args = [state_c, qk_cols, proj, gates, state_m, b_gates]
    aliases = {}
    if c_prev is not None:
        in_specs.append(pl.BlockSpec(memory_space=pl.ANY))
        args.append(c_prev)
        aliases = {len(args) - 1: 0}
    est = 4 * (4 * SB * HEADS * DK * DV + 4 * HEADS * DK * LANES)
    return pl.pallas_call(
        _even_sample_state_kernel,
        grid=(DEC_BATCH // SB,),
        in_specs=in_specs,
        out_specs=[
            pl.BlockSpec(blk5, lambda i: (j, i, 0, 0, 0)),
            pl.BlockSpec((SB, HEADS * DV), lambda i: (i, 0)),
        ],
        out_shape=[
            jax.ShapeDtypeStruct(state_c.shape, F32),
            jax.ShapeDtypeStruct((DEC_BATCH, HEADS * DV), F32),
        ],
        input_output_aliases=aliases,
        compiler_params=pltpu.CompilerParams(
            dimension_semantics=("arbitrary",), vmem_limit_bytes=_vmem_limit(est)),
        name="even_sample_state",
    )(*args)


def _even_sample_kernel(bg_ref, cg_ref, xin_ref, q_ref, k_ref, v_ref, og_ref, gt_ref,
                        num_ref, cst_ref, n_ref, m_ref, wc_ref, bgate_ref, mhg_ref, mixin_hbm,
                        mix_ref, conv_ref, n_out_ref, m_out_ref):
    del mixin_hbm
    cx = cg_ref[...] * xin_ref[...]
    st0 = cst_ref[:, 0:D_CONV]
    st1 = cst_ref[:, D_CONV:2 * D_CONV]
    conv = st0 * wc_ref[0:1, :] + st1 * wc_ref[1:2, :] + cx * wc_ref[2:3, :]
    mix_ref[:, 0:D_CONV] = bg_ref[...] * conv
    conv_ref[:, 0:D_CONV] = st1
    conv_ref[:, D_CONV:2 * D_CONV] = cx

    a, wgt, m_t = _sample_gates(gt_ref[...], bgate_ref, m_ref[...])
    m_out_ref[...] = m_t
    floor = jnp.exp(-m_t)
    for h in range(HEADS):
        a_h = a[:, h:h + 1]
        w_h = wgt[:, h:h + 1]
        qh = q_ref[:, h * DK:(h + 1) * DK]
        kh = k_ref[:, h * DK:(h + 1) * DK] * (DK ** -0.5)
        vh = v_ref[:, h * DV:(h + 1) * DV]
        nh = n_ref[:, h * DK:(h + 1) * DK]
        wt = w_h * jnp.sum(qh * kh, axis=-1, keepdims=True)
        num = a_h * num_ref[:, h * DV:(h + 1) * DV] + wt * vh
        den = a_h * jnp.sum(qh * nh, axis=-1, keepdims=True) + wt
        hh = num / jnp.maximum(jnp.abs(den), floor[:, h:h + 1])
        n_out_ref[:, h * DK:(h + 1) * DK] = a_h * nh + w_h * kh
        hn = _layer_norm(hh, mhg_ref[0:1, h * DV:(h + 1) * DV], None)
        og = og_ref[:, h * DV:(h + 1) * DV]
        mix_ref[:, D_CONV + h * DV:D_CONV + (h + 1) * DV] = hn * jax.nn.sigmoid(og)


def _even_sample(proj, gates, num, conv_st, n_st, m_st, w_conv, b_gates, mh_g, mixin, j):
    nb = DEC_BATCH
    rb = MP // nb
    est = 4 * 2 * (5 * nb * 1024 + 2 * nb * 512 + nb * 128 + nb * 1024 + nb * 2048 + nb * 512
                   + nb * 2048 + nb * 2048 + nb * 512)
    return pl.pallas_call(
        _even_sample_kernel,
        grid=(1,),
        in_specs=[
            pl.BlockSpec((nb, D_CONV), lambda i: (rb, 0)),
            pl.BlockSpec((nb, D_CONV), lambda i: (rb, 1)),
            pl.BlockSpec((nb, D_CONV), lambda i: (rb, 2)),
            pl.BlockSpec((nb, HEADS * DK), lambda i: (rb, 6)),
            pl.BlockSpec((nb, HEADS * DK), lambda i: (rb, 7)),
            pl.BlockSpec((nb, HEADS * DV), lambda i: (rb, 4)),
            pl.BlockSpec((nb, HEADS * DV), lambda i: (rb, 5)),
            pl.BlockSpec((nb, LANES), lambda i: (rb, 0)),
            pl.BlockSpec((nb, HEADS * DV), lambda i: (0, 0)),
            pl.BlockSpec((None, nb, 2 * D_CONV), lambda i: (j, 0, 0)),
            pl.BlockSpec((None, nb, HEADS * DK), lambda i: (j, 0, 0)),
            pl.BlockSpec((None, nb, HEADS), lambda i: (j, 0, 0)),
            pl.BlockSpec((None, CONV_W, D_CONV), lambda i: (j, 0, 0)),
            pl.BlockSpec((None, 2, HEADS), lambda i: (j, 0, 0)),
            pl.BlockSpec((None, 1, HEADS * DV), lambda i: (j, 0, 0)),
            pl.BlockSpec(memory_space=pl.ANY),
        ],
        out_specs=[
            pl.BlockSpec((nb, D_MODEL), lambda i: (rb, 0)),
            pl.BlockSpec((nb, 2 * D_CONV), lambda i: (0, 0)),
            pl.BlockSpec((nb, HEADS * DK), lambda i: (0, 0)),
            pl.BlockSpec((nb, HEADS), lambda i: (0, 0)),
        ],
        out_shape=[
            jax.ShapeDtypeStruct((M, D_MODEL), F32),
            jax.ShapeDtypeStruct((nb, 2 * D_CONV), F32),
            jax.ShapeDtypeStruct((nb, HEADS * DK), F32),
            jax.ShapeDtypeStruct((nb, HEADS), F32),
        ],
        input_output_aliases={15: 0},
        compiler_params=pltpu.CompilerParams(
            dimension_semantics=("arbitrary",), vmem_limit_bytes=_vmem_limit(est)),
        name="even_sample",
    )(proj, proj, proj, proj, proj, proj, proj, gates, num, conv_st, n_st, m_st,
      w_conv, b_gates, mh_g, mixin)


def _gmlp_norm(u_raw, v_raw, g, b):
    u = jax.nn.gelu(u_raw)
    vn = _layer_norm(jax.nn.gelu(v_raw), g, b)
    return u, vn


def _odd_prompt_kernel(p_ref, u_ref, v_ref, wp_ref, sc_ref, gmg_ref, gmb_ref, ws_ref, bst_ref,
                       mix_ref, pool_ref, gv_ref, pbuf, sbuf):
    s = pl.program_id(1)
    R = RO

    @pl.when(s == 0)
    def _():
        pbuf[0:HIST, :] = jnp.zeros((HIST, D_POOL), F32)
        sbuf[0:SUBLANES, :] = jnp.zeros((SUBLANES, D_POOL), F32)

    p = p_ref[...]
    pbuf[HIST:HIST + R, :] = p
    pos = s * R + lax.broadcasted_iota(jnp.int32, (R, 1), 0)
    for g, w in enumerate(POOL_WINDOWS):
        lo, hi = g * POOL_GW, (g + 1) * POOL_GW
        cur = pbuf[0:HIST + R, lo:hi]
        d = 1
        while d < w:
            sbuf[SUBLANES:SUBLANES + HIST + R, lo:hi] = cur
            cur = cur + sbuf[SUBLANES - d:SUBLANES - d + HIST + R, lo:hi]
            d *= 2
        win = cur[HIST:, :]
        cnt = jnp.minimum(w, pos + 1).astype(F32)
        diff = win / cnt - p[:, lo:hi]
        mix_ref[:, lo:hi] = _dot(diff, wp_ref[g]) * sc_ref[0:1, lo:hi]
    pool_ref[...] = pbuf[HIST + R - POOL_BUF:HIST + R, :]
    pbuf[0:HIST, :] = pbuf[R:R + HIST, :]

    u, vn = _gmlp_norm(u_ref[...], v_ref[...], gmg_ref[...], gmb_ref[...])
    L = GMLP_CHUNK
    tril = lax.broadcasted_iota(jnp.int32, (L, L), 0) >= lax.broadcasted_iota(jnp.int32, (L, L), 1)
    for g in range(D_GMLP // GMLP_GW):
        lo, hi = g * GMLP_GW, (g + 1) * GMLP_GW
        ws = jnp.where(tril, ws_ref[g], 0.0)
        bcol = bst_ref[:, g:g + 1]
        for ck in range(R // L):
            r0, r1 = ck * L, (ck + 1) * L
            sv = _dot(ws, vn[r0:r1, lo:hi]) + bcol
            mix_ref[r0:r1, D_POOL + lo:D_POOL + hi] = u[r0:r1, lo:hi] * sv
    gv_ref[...] = vn[R - L:R, :]


def _odd_prompt(proj, w_pool, pool_scale, gm_g, gm_b, w_spatial, bs_t, j):
    nrb = SEQ // RO
    r = lambda b, s: b * nrb + s
    est = 4 * (2 * 3 * RO * 1024 + 2 * RO * 2048 + 2 * 4 * 256 * 256 + 2 * 4 * 128 * 128
               + (RO + HIST) * 1024 + 8 * RO * 1024)
    return pl.pallas_call(
        _odd_prompt_kernel,
        grid=(BATCH, nrb),
        in_specs=[
            pl.BlockSpec((RO, D_POOL), lambda b, s: (r(b, s), 0)),
            pl.BlockSpec((RO, D_GMLP), lambda b, s: (r(b, s), 1)),
            pl.BlockSpec((RO, D_GMLP), lambda b, s: (r(b, s), 2)),
            pl.BlockSpec((None, 4, POOL_GW, POOL_GW), lambda b, s: (j, 0, 0, 0)),
            pl.BlockSpec((None, 1, D_POOL), lambda b, s: (j, 0, 0)),
            pl.BlockSpec((None, 1, D_GMLP), lambda b, s: (j, 0, 0)),
            pl.BlockSpec((None, 1, D_GMLP), lambda b, s: (j, 0, 0)),
            pl.BlockSpec((None, 4, GMLP_CHUNK, GMLP_CHUNK), lambda b, s: (j, 0, 0, 0)),
            pl.BlockSpec((None, GMLP_CHUNK, 4), lambda b, s: (j, 0, 0)),
        ],
        out_specs=[
            pl.BlockSpec((RO, D_MODEL), lambda b, s: (r(b, s), 0)),
            pl.BlockSpec((None, POOL_BUF, D_POOL), lambda b, s: (b, 0, 0)),
            pl.BlockSpec((None, GMLP_CHUNK, D_GMLP), lambda b, s: (b, 0, 0)),
        ],
        out_shape=[
            jax.ShapeDtypeStruct((M, D_MODEL), F32),
            jax.ShapeDtypeStruct((BATCH, POOL_BUF, D_POOL), F32),
            jax.ShapeDtypeStruct((BATCH, GMLP_CHUNK, D_GMLP), F32),
        ],
        scratch_shapes=[pltpu.VMEM((RO + HIST, D_POOL), F32),
                        pltpu.VMEM((SUBLANES + RO + HIST, D_POOL), F32)],
        compiler_params=pltpu.CompilerParams(
            dimension_semantics=("arbitrary", "arbitrary"), vmem_limit_bytes=_vmem_limit(est)),
        name="odd_prompt",
    )(proj, proj, proj, w_pool, pool_scale, gm_g, gm_b, w_spatial, bs_t)


def _odd_sample_kernel(p_ref, u_ref, v_ref, st_ref, wp_ref, sc_ref, gmg_ref, gmb_ref, ws_ref, bst_ref,
                       *rest):
    mix_ref, pool_ref, gv_ref = rest[-3:]
    p = p_ref[...]
    for r in range(POOL_BUF - 1):
        pool_ref[r] = st_ref[r + 1]
    pool_ref[POOL_BUF - 1] = p
    for g, w in enumerate(POOL_WINDOWS):
        lo, hi = g * POOL_GW, (g + 1) * POOL_GW
        win = p[:, lo:hi]
        for jj in range(1, w):
            win = win + st_ref[POOL_BUF - jj, :, lo:hi]
        cnt = float(min(w, PAST_LEN + 1))
        diff = win / cnt - p[:, lo:hi]
        mix_ref[:, lo:hi] = _dot(diff, wp_ref[g]) * sc_ref[0:1, lo:hi]
    u, vn = _gmlp_norm(u_ref[...], v_ref[...], gmg_ref[...], gmb_ref[...])
    gv_ref[...] = vn
    for g in range(D_GMLP // GMLP_GW):
        lo, hi = g * GMLP_GW, (g + 1) * GMLP_GW
        sv = ws_ref[g, 0:1, 0:1] * vn[:, lo:hi] + bst_ref[0:1, g:g + 1]
        mix_ref[:, D_POOL + lo:D_POOL + hi] = u[:, lo:hi] * sv


def _odd_sample(proj, pool_st, w_pool, pool_scale, gm_g, gm_b, w_spatial, bs_t, mixin, pool_prev, j):
    nb = DEC_BATCH
    rb = MP // nb
    st_blk = (None, POOL_BUF, nb, D_POOL)
    in_specs = [
        pl.BlockSpec((nb, D_POOL), lambda i: (rb, 0)),
        pl.BlockSpec((nb, D_GMLP), lambda i: (rb, 1)),
        pl.BlockSpec((nb, D_GMLP), lambda i: (rb, 2)),
        pl.BlockSpec(st_blk, lambda i: (j, 0, 0, 0)),
        pl.BlockSpec((None, 4, POOL_GW, POOL_GW), lambda i: (j, 0, 0, 0)),
        pl.BlockSpec((None, 1, D_POOL), lambda i: (j, 0, 0)),
        pl.BlockSpec((None, 1, D_GMLP), lambda i: (j, 0, 0)),
        pl.BlockSpec((None, 1, D_GMLP), lambda i: (j, 0, 0)),
        pl.BlockSpec((None, 4, GMLP_CHUNK, GMLP_CHUNK), lambda i: (j, 0, 0, 0)),
        pl.BlockSpec((None, GMLP_CHUNK, 4), lambda i: (j, 0, 0)),
        pl.BlockSpec(memory_space=pl.ANY),
    ]
    args = [proj, proj, proj, pool_st, w_pool, pool_scale, gm_g, gm_b, w_spatial, bs_t, mixin]
    aliases = {len(args) - 1: 0}
    if pool_prev is not None:
        in_specs.append(pl.BlockSpec(memory_space=pl.ANY))
        args.append(pool_prev)
        aliases[len(args) - 1] = 1
    est = 4 * (4 * POOL_BUF * nb * D_POOL + 2 * (3 * nb * 1024 + 4 * 256 * 256 + 4 * 128 * 128
                                                + nb * 2048 + nb * 1024))
    return pl.pallas_call(
        _odd_sample_kernel,
        grid=(1,),
        in_specs=in_specs,
        out_specs=[
            pl.BlockSpec((nb, D_MODEL), lambda i: (rb, 0)),
            pl.BlockSpec(st_blk, lambda i: (j, 0, 0, 0)),
            pl.BlockSpec((nb, D_GMLP), lambda i: (0, 0)),
        ],
        out_shape=[
            jax.ShapeDtypeStruct((M, D_MODEL), F32),
            jax.ShapeDtypeStruct(pool_st.shape, F32),
            jax.ShapeDtypeStruct((nb, D_GMLP), F32),
        ],
        input_output_aliases=aliases,
        compiler_params=pltpu.CompilerParams(
            dimension_semantics=("arbitrary",), vmem_limit_bytes=_vmem_limit(est)),
        name="odd_sample",
    )(*args)


def kernel(x_prompt, x_sample, state_conv, state_mlstm_C, state_mlstm_n, state_mlstm_m, state_pool,
           ln_g, ln_b, w_ffn_in, w_ffn_out, w_in_even, b_gates_even, w_conv, mh_norm_g, w_out_even,
           w_in_odd, w_pool, pool_scale, gm_ln_g, gm_ln_b, w_spatial, b_spatial, w_out_odd):
    n_even, n_odd = w_in_even.shape[0], w_in_odd.shape[0]
    x = jnp.concatenate([x_prompt.reshape(MP, D_MODEL), x_sample.reshape(DEC_BATCH, D_MODEL)], axis=0)

    ln_g3 = ln_g.reshape(DEPTH * 3, 1, D_MODEL)
    ln_b3 = ln_b.reshape(DEPTH * 3, 1, D_MODEL)
    w_even_t = jnp.swapaxes(w_in_even, 1, 2)
    w_gate_t = jnp.pad(w_even_t[:, EVEN_MAIN:, :], ((0, 0), (0, LANES - 2 * HEADS), (0, 0)))
    pool_st = jnp.swapaxes(state_pool, 1, 2)
    mh_g3 = mh_norm_g.reshape(n_even, 1, HEADS * DV)
    conv_st = state_conv.reshape(n_even, DEC_BATCH, (CONV_W - 1) * D_CONV)
    n_st = state_mlstm_n.reshape(n_even, DEC_BATCH, HEADS * DK)
    scale3 = pool_scale.reshape(n_odd, 1, D_POOL)
    gm_g3 = gm_ln_g.reshape(n_odd, 1, D_GMLP)
    gm_b3 = gm_ln_b.reshape(n_odd, 1, D_GMLP)
    bs_t = jnp.swapaxes(b_spatial, 1, 2)

    conv_p, conv_s, c_p, n_p, n_s, m_p, m_s = [], [], [], [], [], [], []
    pool_p, gv_p, gv_s = [], [], []
    c_s = None
    pool_s = None

    for layer in range(DEPTH):
        j = layer // 2
        x = _ffn(x, w_ffn_in, w_ffn_out, ln_g3, ln_b3, layer, 0, 3 * layer)
        if layer % 2 == 0:
            proj, gates = _proj(x, w_even_t, j, 0, EVEN_MAIN, "proj_even", w_transposed=True,
                                w_gate_t=w_gate_t)
            proj = _proj(x, w_even_t, j, 1, EVEN_MAIN, "proj_even", w_transposed=True, prev=proj)
            mixin, cv, cc, nn, mm = _even_prompt(proj, gates, w_conv, b_gates_even, mh_g3, j)
            conv_p.append(cv)
            c_p.append(cc)
            n_p.append(nn)
            m_p.append(mm.reshape(BATCH, HEADS))
            qk_cols = _qk_cols(x, w_even_t, j)
            c_s, num = _even_sample_state(state_mlstm_C, qk_cols, proj, gates, state_mlstm_m,
                                          b_gates_even, j, c_s)
            mixin, cvs, nns, mms = _even_sample(proj, gates, num, conv_st, n_st, state_mlstm_m,
                                                w_conv, b_gates_even, mh_g3, mixin, j)
            conv_s.append(cvs.reshape(DEC_BATCH, CONV_W - 1, D_CONV))
            n_s.append(nns.reshape(DEC_BATCH, HEADS, DK))
            m_s.append(mms)
            x = _outproj(x, mixin, w_out_even, ln_g3, ln_b3, j, 3 * layer + 1)
        else:
            proj = _proj(x, w_in_odd, j, 0, ODD_IN, "proj_odd")
            mixin, pp, gv = _odd_prompt(proj, w_pool, scale3, gm_g3, gm_b3, w_spatial, bs_t, j)
            pool_p.append(pp)
            gv_p.append(gv)
            mixin, pool_s, gvs = _odd_sample(proj, pool_st, w_pool, scale3, gm_g3, gm_b3, w_spatial, bs_t,
                                             mixin, pool_s, j)
            gv_s.append(gvs.reshape(DEC_BATCH, 1, D_GMLP))
            x = _outproj(x, mixin, w_out_odd, ln_g3, ln_b3, j, 3 * layer + 1)
        x = _ffn(x, w_ffn_in, w_ffn_out, ln_g3, ln_b3, layer, 1, 3 * layer + 2,
                 split_out=layer == DEPTH - 1)

    y_prompt = x[0].reshape(BATCH, SEQ, D_MODEL)
    y_sample = x[1].reshape(DEC_BATCH, 1, D_MODEL)
    return (y_prompt, y_sample,
            jnp.stack(conv_p), jnp.stack(conv_s),
            jnp.stack(c_p), c_s,
            jnp.stack(n_p), jnp.stack(n_s),
            jnp.stack(m_p), jnp.stack(m_s),
            jnp.stack(pool_p), jnp.swapaxes(pool_s, 1, 2),
            jnp.stack(gv_p), jnp.stack(gv_s))
```

```python
import functools

import jax
import jax.numpy as jnp
from jax import lax
from jax.experimental import pallas as pl
from jax.experimental.pallas import tpu as pltpu

F32 = jnp.float32

D_MODEL = 2048
BATCH = 4
SEQ = 2048
DEPTH = 4
DEC_BATCH = 128
PAST_LEN = 16384
D_FF = 5632
D_CONV = 1024
CONV_W = 3
HEADS = 4
DK = 128
DV = 256
D_POOL = 1024
POOL_WINDOWS = (2, 4, 8, 16)
POOL_GW = 256
POOL_BUF = 15
D_GMLP = 1024
GMLP_GW = 256
GMLP_CHUNK = 128
ALPHA = (2 * DEPTH) ** 0.25
LN_EPS = 1e-5
EVEN_MAIN = 3 * D_CONV + 2 * HEADS * DK + 2 * HEADS * DV
ODD_IN = D_POOL + 2 * D_GMLP

MP = BATCH * SEQ
M = MP + DEC_BATCH

LANES = 128
SUBLANES = 8
VMEM_BYTES_V7X = 64 * 1024 * 1024
VMEM_RESERVED_BYTES = 4 * 1024 * 1024
VMEM_TEMP_BYTES = 8 * 1024 * 1024

TM = 832
TF = 512
TFS = 256
TN = 512
TMP = 520
PROJ_COLS = 3072
TMO = 416
LC = 256
RE = 256
HEAD_GROUP = 4
CONV_PARTS = 8
RO = 256
SB = 16
STATE_GROUP = 4
HIST = 16


def _vmem_limit(block_bytes):
    return int(min(VMEM_BYTES_V7X - VMEM_RESERVED_BYTES, block_bytes + VMEM_TEMP_BYTES))


def _layer_norm(y, g, b, eps=LN_EPS):
    mu = jnp.mean(y, axis=-1, keepdims=True)
    yc = y - mu
    var = jnp.mean(yc * yc, axis=-1, keepdims=True)
    out = yc * lax.rsqrt(var + eps) * g
    if b is not None:
        out = out + b
    return out


def _log_sigmoid(x):
    return -(jnp.maximum(-x, 0.0) + jnp.log1p(jnp.exp(-jnp.abs(x))))


def _dot(a, b):
    return jnp.dot(a, b, preferred_element_type=F32)


def _dot_nt(a, b):
    return lax.dot_general(a, b, (((1,), (1,)), ((), ())), preferred_element_type=F32)


def _ffn_kernel(x_ref, wg_ref, wu_ref, wo_ref, g_ref, b_ref, o_ref, *maybe_sample_ref, nf):
    f = pl.program_id(1)

    def accumulate(first_chunk):
        x = x_ref[...]
        for c in range(TF // TFS):
            c0, c1 = c * TFS, (c + 1) * TFS
            gate = _dot(x, wg_ref[:, c0:c1])
            up = _dot(x, wu_ref[:, c0:c1])
            h = gate * jax.nn.sigmoid(gate) * up
            for n in range(D_MODEL // TN):
                n0, n1 = n * TN, (n + 1) * TN
                part = _dot(h, wo_ref[c0:c1, n0:n1])
                if first_chunk and c == 0:
                    o_ref[:, n0:n1] = (2.0 * ALPHA) * x_ref[:, n0:n1] + part
                else:
                    o_ref[:, n0:n1] += part

    @pl.when(f == 0)
    def _():
        accumulate(True)

    @pl.when(f > 0)
    def _():
        accumulate(False)

    @pl.when(f == nf - 1)
    def _():
        o_ref[...] = _layer_norm(o_ref[...], g_ref[...], b_ref[...], eps=4.0 * LN_EPS)

    if maybe_sample_ref:
        @pl.when((f == nf - 1) & (pl.program_id(0) == M // TM - 1))
        def _():
            maybe_sample_ref[0][...] = o_ref[TM - DEC_BATCH:, :]


def _ffn(x, w_ffn_in, w_ffn_out, ln_g, ln_b, layer, which, ln_idx, split_out=False):
    nf = D_FF // TF
    est = 4 * (4 * TM * D_MODEL + 2 * 3 * D_MODEL * TF + 4 * TM * TFS + TM * TN)
    out_specs = pl.BlockSpec((TM, D_MODEL), lambda i, f: (i, 0))
    out_shape = jax.ShapeDtypeStruct((M, D_MODEL), F32)
    if split_out:
        out_specs = [out_specs, pl.BlockSpec((DEC_BATCH, D_MODEL), lambda i, f: (0, 0))]
        out_shape = [jax.ShapeDtypeStruct((MP, D_MODEL), F32),
                     jax.ShapeDtypeStruct((DEC_BATCH, D_MODEL), F32)]
    return pl.pallas_call(
        functools.partial(_ffn_kernel, nf=nf),
        grid=(M // TM, nf),
        in_specs=[
            pl.BlockSpec((TM, D_MODEL), lambda i, f: (i, 0)),
            pl.BlockSpec((None, None, D_MODEL, TF), lambda i, f: (layer, which, 0, f)),
            pl.BlockSpec((None, None, D_MODEL, TF), lambda i, f: (layer, which, 0, nf + f)),
            pl.BlockSpec((None, None, TF, D_MODEL), lambda i, f: (layer, which, f, 0)),
            pl.BlockSpec((None, 1, D_MODEL), lambda i, f: (ln_idx, 0, 0)),
            pl.BlockSpec((None, 1, D_MODEL), lambda i, f: (ln_idx, 0, 0)),
        ],
        out_specs=out_specs,
        out_shape=out_shape,
        compiler_params=pltpu.CompilerParams(
            dimension_semantics=("arbitrary", "arbitrary"), vmem_limit_bytes=_vmem_limit(est)),
        name="ffn_ln",
    )(x, w_ffn_in, w_ffn_in, w_ffn_out, ln_g, ln_b)


def _proj_kernel(x_ref, w_ref, *rest, w_transposed, with_gates, aliased):
    rest = rest[1:] if aliased else rest
    if with_gates:
        wgate_ref, o_ref, gate_ref = rest
        gate_ref[...] = _dot_nt(x_ref[...], wgate_ref[...])
    else:
        (o_ref,) = rest
    for n in range(PROJ_COLS // TN):
        lo, hi = n * TN, (n + 1) * TN
        if w_transposed:
            o_ref[:, lo:hi] = _dot_nt(x_ref[...], w_ref[lo:hi, :])
        else:
            o_ref[:, lo:hi] = _dot(x_ref[...], w_ref[:, lo:hi])


def _proj(x, w, layer_idx, col_block, n_total, name, w_transposed=False, w_gate_t=None, prev=None):
    est = 4 * (PROJ_COLS * D_MODEL + 2 * TMP * D_MODEL + 2 * TMP * PROJ_COLS + 2 * TMP * TN)
    if w_transposed:
        w_spec = pl.BlockSpec((None, PROJ_COLS, D_MODEL), lambda i: (layer_idx, col_block, 0),
                              pipeline_mode=pl.Buffered(1))
    else:
        w_spec = pl.BlockSpec((None, D_MODEL, PROJ_COLS), lambda i: (layer_idx, 0, col_block),
                              pipeline_mode=pl.Buffered(1))
    in_specs = [pl.BlockSpec((TMP, D_MODEL), lambda i: (i, 0)), w_spec]
    args = [x, w]
    aliases = {}
    if prev is not None:
        in_specs.append(pl.BlockSpec(memory_space=pl.ANY))
        args.append(prev)
        aliases = {2: 0}
    out_specs = pl.BlockSpec((TMP, PROJ_COLS), lambda i: (i, col_block))
    out_shape = jax.ShapeDtypeStruct((M, n_total), F32)
    if w_gate_t is not None:
        in_specs.append(pl.BlockSpec((None, LANES, D_MODEL), lambda i: (layer_idx, 0, 0)))
        args.append(w_gate_t)
        out_specs = [out_specs, pl.BlockSpec((TMP, LANES), lambda i: (i, 0))]
        out_shape = [out_shape, jax.ShapeDtypeStruct((M, LANES), F32)]
        est += 4 * (2 * LANES * D_MODEL + 3 * TMP * LANES)
    return pl.pallas_call(
        functools.partial(_proj_kernel, w_transposed=w_transposed, with_gates=w_gate_t is not None,
                          aliased=prev is not None),
        grid=(M // TMP,),
        in_specs=in_specs,
        out_specs=out_specs,
        out_shape=out_shape,
        input_output_aliases=aliases,
        compiler_params=pltpu.CompilerParams(
            dimension_semantics=("parallel",), vmem_limit_bytes=_vmem_limit(est)),
        name=name,
    )(*args)


def _outproj_kernel(x_ref, a_ref, w_ref, g_ref, b_ref, o_ref):
    y = ALPHA * x_ref[...] + _dot(a_ref[...], w_ref[...])
    o_ref[...] = _layer_norm(y, g_ref[...], b_ref[...])


def _outproj(x, mix, w_out, ln_g, ln_b, layer_idx, ln_idx):
    est = 4 * (D_MODEL * D_MODEL + 9 * TMO * D_MODEL)
    return pl.pallas_call(
        _outproj_kernel,
        grid=(M // TMO,),
        in_specs=[
            pl.BlockSpec((TMO, D_MODEL), lambda i: (i, 0)),
            pl.BlockSpec((TMO, D_MODEL), lambda i: (i, 0)),
            pl.BlockSpec((None, D_MODEL, D_MODEL), lambda i: (layer_idx, 0, 0),
                         pipeline_mode=pl.Buffered(1)),
            pl.BlockSpec((None, 1, D_MODEL), lambda i: (ln_idx, 0, 0)),
            pl.BlockSpec((None, 1, D_MODEL), lambda i: (ln_idx, 0, 0)),
        ],
        out_specs=pl.BlockSpec((TMO, D_MODEL), lambda i: (i, 0)),
        out_shape=jax.ShapeDtypeStruct((M, D_MODEL), F32),
        compiler_params=pltpu.CompilerParams(
            dimension_semantics=("parallel",), vmem_limit_bytes=_vmem_limit(est)),
        name="outproj_ln",
    )(x, mix, w_out, ln_g, ln_b)


def _split3(x):
    h1 = x.astype(jnp.bfloat16).astype(F32)
    r = x - h1
    h2 = r.astype(jnp.bfloat16).astype(F32)
    return h1, h2, r - h2


def _even_prompt_kernel(bg_ref, cg_ref, xin_ref, q_ref, k_ref, v_ref, og_ref, gt_ref,
                        wc_ref, bgate_ref, mhg_ref,
                        mix_ref, conv_ref, c_out_ref, n_out_ref, m_out_ref,
                        cbuf, c_sc, n_sc, m_sc):
    c = pl.program_id(1)
    L = LC
    R = RE

    @pl.when(c == 0)
    def _():
        cbuf[0:SUBLANES, :] = jnp.zeros((SUBLANES, D_CONV), F32)
        c_sc[...] = jnp.zeros_like(c_sc)
        n_sc[...] = jnp.zeros_like(n_sc)
        m_sc[...] = jnp.zeros_like(m_sc)

    def conv_piece(lo, hi):
        cx = cg_ref[:, lo:hi] * xin_ref[:, lo:hi]
        cbuf[SUBLANES:SUBLANES + R, lo:hi] = cx
        c1 = cbuf[SUBLANES - 1:SUBLANES - 1 + R, lo:hi]
        c2 = cbuf[SUBLANES - 2:SUBLANES - 2 + R, lo:hi]
        conv = c2 * wc_ref[0:1, lo:hi] + c1 * wc_ref[1:2, lo:hi] + cx * wc_ref[2:3, lo:hi]
        mix_ref[:, lo:hi] = bg_ref[:, lo:hi] * conv
        conv_ref[:, lo:hi] = cbuf[SUBLANES + R - 2:SUBLANES + R, lo:hi]
        cbuf[0:SUBLANES, lo:hi] = cbuf[R:R + SUBLANES, lo:hi]

    conv_todo = [functools.partial(conv_piece, p * (D_CONV // CONV_PARTS), (p + 1) * (D_CONV // CONV_PARTS))
                 for p in range(CONV_PARTS)]

    def conv_step():
        if conv_todo:
            conv_todo.pop(0)()

    row = lax.broadcasted_iota(jnp.int32, (L, L), 0)
    col = lax.broadcasted_iota(jnp.int32, (L, L), 1)
    causal = row >= col
    tril = causal.astype(F32)
    c_state = [c_sc[h] for h in range(HEADS)]
    n_state = [n_sc[h:h + 1, :] for h in range(HEADS)]
    m_state = [m_sc[h, 0:1, 0:1] for h in range(HEADS)]

    for s in range(R // L):
        r0, r1 = s * L, (s + 1) * L
        gt = gt_ref[r0:r1, :]
        li_all = gt[:, 0:HEADS] + bgate_ref[0:1, :]
        lf_all = _log_sigmoid(gt[:, HEADS:2 * HEADS] + bgate_ref[1:2, :])
        f1, f2, f3 = _split3(lf_all)
        b_all = _dot(tril, f1) + _dot(tril, f2) + _dot(tril, f3)
        z = jnp.concatenate([li_all, b_all, jnp.zeros((L, LANES - 2 * HEADS), F32)], axis=1)
        zt = z.T

        for g0 in range(0, HEADS, HEAD_GROUP):
            hs = range(g0, g0 + HEAD_GROUP)
            li_c = {h: li_all[:, h:h + 1] for h in hs}
            b_c = {h: b_all[:, h:h + 1] for h in hs}
            m0 = {h: m_state[h] for h in hs}
            e = {h: jnp.where(causal, zt[h:h + 1, :] - zt[HEADS + h:HEADS + h + 1, :], -jnp.inf)
                 for h in hs}
            inter = {h: b_c[h] + m0[h] for h in hs}
            m_t = {h: jnp.maximum(inter[h], b_c[h] + jnp.max(e[h], axis=-1, keepdims=True)) for h in hs}
            conv_step()
            a_int = {h: jnp.exp(inter[h] - m_t[h]) for h in hs}
            qh = {h: q_ref[r0:r1, h * DK:(h + 1) * DK] for h in hs}
            kh = {h: k_ref[r0:r1, h * DK:(h + 1) * DK] * (DK ** -0.5) for h in hs}
            vh = {h: v_ref[r0:r1, h * DV:(h + 1) * DV] for h in hs}
            w = {h: jnp.exp(e[h] + (b_c[h] - m_t[h])) * _dot_nt(qh[h], kh[h]) for h in hs}
            conv_step()
            num = {h: a_int[h] * _dot(qh[h], c_state[h]) + _dot(w[h], vh[h]) for h in hs}
            w_rows = {h: jnp.sum(sum(w[h][:, t * LANES:(t + 1) * LANES] for t in range(L // LANES)),
                                 axis=-1, keepdims=True) for h in hs}
            conv_step()
            den = {h: a_int[h] * jnp.sum(qh[h] * n_state[h], axis=-1, keepdims=True) + w_rows[h]
                   for h in hs}
            hh = {h: num[h] / jnp.maximum(jnp.abs(den[h]), jnp.exp(-m_t[h])) for h in hs}
            conv_step()
            m_new = {h: m_t[h][L - 1:L, :] for h in hs}
            b_last = {h: b_c[h][L - 1:L, :] for h in hs}
            w_end = {h: jnp.exp(b_last[h] - b_c[h] + li_c[h] - m_new[h]) for h in hs}
            decay = {h: jnp.exp(b_last[h] + m0[h] - m_new[h]) for h in hs}
            wk = {h: w_end[h] * kh[h] for h in hs}
            conv_step()
            c_new = {h: decay[h] * c_state[h] + _dot(wk[h].T, vh[h]) for h in hs}
            conv_step()
            n_new = {h: decay[h] * n_state[h] + jnp.sum(wk[h], axis=0, keepdims=True) for h in hs}
            conv_step()
            hn = {h: _layer_norm(hh[h], mhg_ref[0:1, h * DV:(h + 1) * DV], None) for h in hs}
            conv_step()
            for h in hs:
                c_state[h], n_state[h], m_state[h] = c_new[h], n_new[h], m_new[h]
                og = og_ref[r0:r1, h * DV:(h + 1) * DV]
                mix_ref[r0:r1, D_CONV + h * DV:D_CONV + (h + 1) * DV] = hn[h] * jax.nn.sigmoid(og)

    while conv_todo:
        conv_step()

    for h in range(HEADS):
        c_sc[h] = c_state[h]
        n_sc[h:h + 1, :] = n_state[h]
        m_sc[h] = jnp.broadcast_to(m_state[h], (SUBLANES, LANES))
        c_out_ref[h] = c_state[h]
        n_out_ref[h:h + 1, :] = n_state[h]
        m_out_ref[0:1, h:h + 1] = m_state[h]


def _even_prompt(proj, gates, w_conv, b_gates, mh_g, j):
    nck = SEQ // RE
    r = lambda b, c: b * nck + c
    est = 4 * (2 * (5 * RE * 1024 + 2 * RE * 512 + RE * 128) + 2 * RE * 2048 + 4 * RE * 1024
               + 3 * HEADS * DK * DV + 24 * LC * LC)
    return pl.pallas_call(
        _even_prompt_kernel,
        grid=(BATCH, nck),
        in_specs=[
            pl.BlockSpec((RE, D_CONV), lambda b, c: (r(b, c), 0)),
            pl.BlockSpec((RE, D_CONV), lambda b, c: (r(b, c), 1)),
            pl.BlockSpec((RE, D_CONV), lambda b, c: (r(b, c), 2)),
            pl.BlockSpec((RE, HEADS * DK), lambda b, c: (r(b, c), 6)),
            pl.BlockSpec((RE, HEADS * DK), lambda b, c: (r(b, c), 7)),
            pl.BlockSpec((RE, HEADS * DV), lambda b, c: (r(b, c), 4)),
            pl.BlockSpec((RE, HEADS * DV), lambda b, c: (r(b, c), 5)),
            pl.BlockSpec((RE, LANES), lambda b, c: (r(b, c), 0)),
            pl.BlockSpec((None, CONV_W, D_CONV), lambda b, c: (j, 0, 0)),
            pl.BlockSpec((None, 2, HEADS), lambda b, c: (j, 0, 0)),
            pl.BlockSpec((None, 1, HEADS * DV), lambda b, c: (j, 0, 0)),
        ],
        out_specs=[
            pl.BlockSpec((RE, D_MODEL), lambda b, c: (r(b, c), 0)),
            pl.BlockSpec((None, CONV_W - 1, D_CONV), lambda b, c: (b, 0, 0)),
            pl.BlockSpec((None, HEADS, DK, DV), lambda b, c: (b, 0, 0, 0)),
            pl.BlockSpec((None, HEADS, DK), lambda b, c: (b, 0, 0)),
            pl.BlockSpec((None, 1, HEADS), lambda b, c: (b, 0, 0)),
        ],
        out_shape=[
            jax.ShapeDtypeStruct((M, D_MODEL), F32),
            jax.ShapeDtypeStruct((BATCH, CONV_W - 1, D_CONV), F32),
            jax.ShapeDtypeStruct((BATCH, HEADS, DK, DV), F32),
            jax.ShapeDtypeStruct((BATCH, HEADS, DK), F32),
            jax.ShapeDtypeStruct((BATCH, 1, HEADS), F32),
        ],
        scratch_shapes=[
            pltpu.VMEM((RE + SUBLANES, D_CONV), F32),
            pltpu.VMEM((HEADS, DK, DV), F32),
            pltpu.VMEM((HEADS, DK), F32),
            pltpu.VMEM((HEADS, SUBLANES, LANES), F32),
        ],
        compiler_params=pltpu.CompilerParams(
            dimension_semantics=("arbitrary", "arbitrary"), vmem_limit_bytes=_vmem_limit(est)),
        name="even_prompt",
    )(proj, proj, proj, proj, proj, proj, proj, gates, w_conv, b_gates, mh_g)


def _sample_gates(gt, bgate_ref, m):
    li = gt[:, 0:HEADS] + bgate_ref[0:1, :]
    lf = _log_sigmoid(gt[:, HEADS:2 * HEADS] + bgate_ref[1:2, :])
    inter = lf + m
    m_t = jnp.maximum(inter, li)
    return jnp.exp(inter - m_t), jnp.exp(li - m_t), m_t


def _qk_cols_kernel(x_ref, w_ref, o_ref):
    t = _dot_nt(w_ref[...], x_ref[...])
    for s in range(DEC_BATCH // SB):
        o_ref[s] = t[:, s * SB:(s + 1) * SB]


def _qk_cols(x, w_even_t, j):
    nqk = 2 * HEADS * DK
    est = 4 * (2 * DEC_BATCH * D_MODEL + 2 * nqk * D_MODEL + 3 * nqk * LANES
               + 2 * (DEC_BATCH // SB) * nqk * LANES)
    return pl.pallas_call(
        _qk_cols_kernel,
        grid=(1,),
        in_specs=[
            pl.BlockSpec((DEC_BATCH, D_MODEL), lambda i: (MP // DEC_BATCH, 0)),
            pl.BlockSpec((None, nqk, D_MODEL), lambda i: (j, 3 * D_CONV // nqk, 0)),
        ],
        out_specs=pl.BlockSpec((DEC_BATCH // SB, nqk, SB), lambda i: (0, 0, 0)),
        out_shape=jax.ShapeDtypeStruct((DEC_BATCH // SB, nqk, SB), F32),
        compiler_params=pltpu.CompilerParams(
            dimension_semantics=("arbitrary",), vmem_limit_bytes=_vmem_limit(est)),
        name="qk_cols",
    )(x, w_even_t)


def _even_sample_state_kernel(c_ref, qk_ref, v_ref, gt_ref, m_ref, bgate_ref, *rest):
    c_out_ref, num_ref = rest[-2], rest[-1]
    a, wgt, _ = _sample_gates(gt_ref[...], bgate_ref, m_ref[...])
    for b0 in range(0, SB, STATE_GROUP):
        ids = [(bi, h) for bi in range(b0, b0 + STATE_GROUP) for h in range(HEADS)]
        a_s = {k: a[k[0]:k[0] + 1, k[1]:k[1] + 1] for k in ids}
        w_s = {k: wgt[k[0]:k[0] + 1, k[1]:k[1] + 1] for k in ids}
        qc = {k: qk_ref[k[1] * DK:(k[1] + 1) * DK, k[0]:k[0] + 1] for k in ids}
        kc = {k: qk_ref[(HEADS + k[1]) * DK:(HEADS + k[1] + 1) * DK, k[0]:k[0] + 1] * (DK ** -0.5)
              for k in ids}
        wk = {k: w_s[k] * kc[k] for k in ids}
        vr = {k: v_ref[k[0]:k[0] + 1, k[1] * DV:(k[1] + 1) * DV] for k in ids}
        for k in ids:
            ch = c_ref[k[0], k[1]]
            c_out_ref[k[0], k[1]] = a_s[k] * ch + wk[k] * vr[k]
            num_ref[k[0]:k[0] + 1, k[1] * DV:(k[1] + 1) * DV] = jnp.sum(qc[k] * ch, axis=0, keepdims=True)


def _even_sample_state(state_c, qk_cols, proj, gates, state_m, b_gates, j, c_prev):
    blk5 = (None, SB, HEADS, DK, DV)
    rb = MP // SB
    in_specs = [
        pl.BlockSpec(blk5, lambda i: (j, i, 0, 0, 0)),
        pl.BlockSpec((None, 2 * HEADS * DK, SB), lambda i: (i, 0, 0)),
        pl.BlockSpec((SB, HEADS * DV), lambda i: (rb + i, 4)),
        pl.BlockSpec((SB, LANES), lambda i: (rb + i, 0)),
        pl.BlockSpec((None, SB, HEADS), lambda i: (j, i, 0)),
        pl.BlockSpec((None, 2, HEADS), lambda i: (j, 0, 0)),
    ]
    args = [state_c, qk_cols, proj, gates, state_m, b_gates]
    aliases = {}
    if c_prev is not None:
        in_specs.append(pl.BlockSpec(memory_space=pl.ANY))
        args.append(c_prev)
        aliases = {len(args) - 1: 0}
    est = 4 * (4 * SB * HEADS * DK * DV + 4 * HEADS * DK * LANES)
    return pl.pallas_call(
        _even_sample_state_kernel,
        grid=(DEC_BATCH // SB,),
        in_specs=in_specs,
        out_specs=[
            pl.BlockSpec(blk5, lambda i: (j, i, 0, 0, 0)),
            pl.BlockSpec((SB, HEADS * DV), lambda i: (i, 0)),
        ],
        out_shape=[
            jax.ShapeDtypeStruct(state_c.shape, F32),
            jax.ShapeDtypeStruct((DEC_BATCH, HEADS * DV), F32),
        ],
        input_output_aliases=aliases,
        compiler_params=pltpu.CompilerParams(
            dimension_semantics=("arbitrary",), vmem_limit_bytes=_vmem_limit(est)),
        name="even_sample_state",
    )(*args)


def _even_sample_kernel(bg_ref, cg_ref, xin_ref, q_ref, k_ref, v_ref, og_ref, gt_ref,
                        num_ref, cst_ref, n_ref, m_ref, wc_ref, bgate_ref, mhg_ref, mixin_hbm,
                        mix_ref, conv_ref, n_out_ref, m_out_ref):
    del mixin_hbm
    cx = cg_ref[...] * xin_ref[...]
    st0 = cst_ref[:, 0:D_CONV]
    st1 = cst_ref[:, D_CONV:2 * D_CONV]
    conv = st0 * wc_ref[0:1, :] + st1 * wc_ref[1:2, :] + cx * wc_ref[2:3, :]
    mix_ref[:, 0:D_CONV] = bg_ref[...] * conv
    conv_ref[:, 0:D_CONV] = st1
    conv_ref[:, D_CONV:2 * D_CONV] = cx

    a, wgt, m_t = _sample_gates(gt_ref[...], bgate_ref, m_ref[...])
    m_out_ref[...] = m_t
    floor = jnp.exp(-m_t)
    for h in range(HEADS):
        a_h = a[:, h:h + 1]
        w_h = wgt[:, h:h + 1]
        qh = q_ref[:, h * DK:(h + 1) * DK]
        kh = k_ref[:, h * DK:(h + 1) * DK] * (DK ** -0.5)
        vh = v_ref[:, h * DV:(h + 1) * DV]
        nh = n_ref[:, h * DK:(h + 1) * DK]
        wt = w_h * jnp.sum(qh * kh, axis=-1, keepdims=True)
        num = a_h * num_ref[:, h * DV:(h + 1) * DV] + wt * vh
        den = a_h * jnp.sum(qh * nh, axis=-1, keepdims=True) + wt
        hh = num / jnp.maximum(jnp.abs(den), floor[:, h:h + 1])
        n_out_ref[:, h * DK:(h + 1) * DK] = a_h * nh + w_h * kh
        hn = _layer_norm(hh, mhg_ref[0:1, h * DV:(h + 1) * DV], None)
        og = og_ref[:, h * DV:(h + 1) * DV]
        mix_ref[:, D_CONV + h * DV:D_CONV + (h + 1) * DV] = hn * jax.nn.sigmoid(og)


def _even_sample(proj, gates, num, conv_st, n_st, m_st, w_conv, b_gates, mh_g, mixin, j):
    nb = DEC_BATCH
    rb = MP // nb
    est = 4 * 2 * (5 * nb * 1024 + 2 * nb * 512 + nb * 128 + nb * 1024 + nb * 2048 + nb * 512
                   + nb * 2048 + nb * 2048 + nb * 512)
    return pl.pallas_call(
        _even_sample_kernel,
        grid=(1,),
        in_specs=[
            pl.BlockSpec((nb, D_CONV), lambda i: (rb, 0)),
            pl.BlockSpec((nb, D_CONV), lambda i: (rb, 1)),
            pl.BlockSpec((nb, D_CONV), lambda i: (rb, 2)),
            pl.BlockSpec((nb, HEADS * DK), lambda i: (rb, 6)),
            pl.BlockSpec((nb, HEADS * DK), lambda i: (rb, 7)),
            pl.BlockSpec((nb, HEADS * DV), lambda i: (rb, 4)),
            pl.BlockSpec((nb, HEADS * DV), lambda i: (rb, 5)),
            pl.BlockSpec((nb, LANES), lambda i: (rb, 0)),
            pl.BlockSpec((nb, HEADS * DV), lambda i: (0, 0)),
            pl.BlockSpec((None, nb, 2 * D_CONV), lambda i: (j, 0, 0)),
            pl.BlockSpec((None, nb, HEADS * DK), lambda i: (j, 0, 0)),
            pl.BlockSpec((None, nb, HEADS), lambda i: (j, 0, 0)),
            pl.BlockSpec((None, CONV_W, D_CONV), lambda i: (j, 0, 0)),
            pl.BlockSpec((None, 2, HEADS), lambda i: (j, 0, 0)),
            pl.BlockSpec((None, 1, HEADS * DV), lambda i: (j, 0, 0)),
            pl.BlockSpec(memory_space=pl.ANY),
        ],
        out_specs=[
            pl.BlockSpec((nb, D_MODEL), lambda i: (rb, 0)),
            pl.BlockSpec((nb, 2 * D_CONV), lambda i: (0, 0)),
            pl.BlockSpec((nb, HEADS * DK), lambda i: (0, 0)),
            pl.BlockSpec((nb, HEADS), lambda i: (0, 0)),
        ],
        out_shape=[
            jax.ShapeDtypeStruct((M, D_MODEL), F32),
            jax.ShapeDtypeStruct((nb, 2 * D_CONV), F32),
            jax.ShapeDtypeStruct((nb, HEADS * DK), F32),
            jax.ShapeDtypeStruct((nb, HEADS), F32),
        ],
        input_output_aliases={15: 0},
        compiler_params=pltpu.CompilerParams(
            dimension_semantics=("arbitrary",), vmem_limit_bytes=_vmem_limit(est)),
        name="even_sample",
    )(proj, proj, proj, proj, proj, proj, proj, gates, num, conv_st, n_st, m_st,
      w_conv, b_gates, mh_g, mixin)


def _gmlp_norm(u_raw, v_raw, g, b):
    u = jax.nn.gelu(u_raw)
    vn = _layer_norm(jax.nn.gelu(v_raw), g, b)
    return u, vn


def _odd_prompt_kernel(p_ref, u_ref, v_ref, wp_ref, sc_ref, gmg_ref, gmb_ref, ws_ref, bst_ref,
                       mix_ref, pool_ref, gv_ref, pbuf, sbuf):
    s = pl.program_id(1)
    R = RO

    @pl.when(s == 0)
    def _():
        pbuf[0:HIST, :] = jnp.zeros((HIST, D_POOL), F32)
        sbuf[0:SUBLANES, :] = jnp.zeros((SUBLANES, D_POOL), F32)

    p = p_ref[...]
    pbuf[HIST:HIST + R, :] = p
    pos = s * R + lax.broadcasted_iota(jnp.int32, (R, 1), 0)
    for g, w in enumerate(POOL_WINDOWS):
        lo, hi = g * POOL_GW, (g + 1) * POOL_GW
        cur = pbuf[0:HIST + R, lo:hi]
        d = 1
        while d < w:
            sbuf[SUBLANES:SUBLANES + HIST + R, lo:hi] = cur
            cur = cur + sbuf[SUBLANES - d:SUBLANES - d + HIST + R, lo:hi]
            d *= 2
        win = cur[HIST:, :]
        cnt = jnp.minimum(w, pos + 1).astype(F32)
        diff = win / cnt - p[:, lo:hi]
        mix_ref[:, lo:hi] = _dot(diff, wp_ref[g]) * sc_ref[0:1, lo:hi]
    pool_ref[...] = pbuf[HIST + R - POOL_BUF:HIST + R, :]
    pbuf[0:HIST, :] = pbuf[R:R + HIST, :]

    u, vn = _gmlp_norm(u_ref[...], v_ref[...], gmg_ref[...], gmb_ref[...])
    L = GMLP_CHUNK
    tril = lax.broadcasted_iota(jnp.int32, (L, L), 0) >= lax.broadcasted_iota(jnp.int32, (L, L), 1)
    for g in range(D_GMLP // GMLP_GW):
        lo, hi = g * GMLP_GW, (g + 1) * GMLP_GW
        ws = jnp.where(tril, ws_ref[g], 0.0)
        bcol = bst_ref[:, g:g + 1]
        for ck in range(R // L):
            r0, r1 = ck * L, (ck + 1) * L
            sv = _dot(ws, vn[r0:r1, lo:hi]) + bcol
            mix_ref[r0:r1, D_POOL + lo:D_POOL + hi] = u[r0:r1, lo:hi] * sv
    gv_ref[...] = vn[R - L:R, :]


def _odd_prompt(proj, w_pool, pool_scale, gm_g, gm_b, w_spatial, bs_t, j):
    nrb = SEQ // RO
    r = lambda b, s: b * nrb + s
    est = 4 * (2 * 3 * RO * 1024 + 2 * RO * 2048 + 2 * 4 * 256 * 256 + 2 * 4 * 128 * 128
               + (RO + HIST) * 1024 + 8 * RO * 1024)
    return pl.pallas_call(
        _odd_prompt_kernel,
        grid=(BATCH, nrb),
        in_specs=[
            pl.BlockSpec((RO, D_POOL), lambda b, s: (r(b, s), 0)),
            pl.BlockSpec((RO, D_GMLP), lambda b, s: (r(b, s), 1)),
            pl.BlockSpec((RO, D_GMLP), lambda b, s: (r(b, s), 2)),
            pl.BlockSpec((None, 4, POOL_GW, POOL_GW), lambda b, s: (j, 0, 0, 0)),
            pl.BlockSpec((None, 1, D_POOL), lambda b, s: (j, 0, 0)),
            pl.BlockSpec((None, 1, D_GMLP), lambda b, s: (j, 0, 0)),
            pl.BlockSpec((None, 1, D_GMLP), lambda b, s: (j, 0, 0)),
            pl.BlockSpec((None, 4, GMLP_CHUNK, GMLP_CHUNK), lambda b, s: (j, 0, 0, 0)),
            pl.BlockSpec((None, GMLP_CHUNK, 4), lambda b, s: (j, 0, 0)),
        ],
        out_specs=[
            pl.BlockSpec((RO, D_MODEL), lambda b, s: (r(b, s), 0)),
            pl.BlockSpec((None, POOL_BUF, D_POOL), lambda b, s: (b, 0, 0)),
            pl.BlockSpec((None, GMLP_CHUNK, D_GMLP), lambda b, s: (b, 0, 0)),
        ],
        out_shape=[
            jax.ShapeDtypeStruct((M, D_MODEL), F32),
            jax.ShapeDtypeStruct((BATCH, POOL_BUF, D_POOL), F32),
            jax.ShapeDtypeStruct((BATCH, GMLP_CHUNK, D_GMLP), F32),
        ],
        scratch_shapes=[pltpu.VMEM((RO + HIST, D_POOL), F32),
                        pltpu.VMEM((SUBLANES + RO + HIST, D_POOL), F32)],
        compiler_params=pltpu.CompilerParams(
            dimension_semantics=("arbitrary", "arbitrary"), vmem_limit_bytes=_vmem_limit(est)),
        name="odd_prompt",
    )(proj, proj, proj, w_pool, pool_scale, gm_g, gm_b, w_spatial, bs_t)


def _odd_sample_kernel(p_ref, u_ref, v_ref, st_ref, wp_ref, sc_ref, gmg_ref, gmb_ref, ws_ref, bst_ref,
                       *rest):
    mix_ref, pool_ref, gv_ref = rest[-3:]
    p = p_ref[...]
    for r in range(POOL_BUF - 1):
        pool_ref[r] = st_ref[r + 1]
    pool_ref[POOL_BUF - 1] = p
    for g, w in enumerate(POOL_WINDOWS):
        lo, hi = g * POOL_GW, (g + 1) * POOL_GW
        win = p[:, lo:hi]
        for jj in range(1, w):
            win = win + st_ref[POOL_BUF - jj, :, lo:hi]
        cnt = float(min(w, PAST_LEN + 1))
        diff = win / cnt - p[:, lo:hi]
        mix_ref[:, lo:hi] = _dot(diff, wp_ref[g]) * sc_ref[0:1, lo:hi]
    u, vn = _gmlp_norm(u_ref[...], v_ref[...], gmg_ref[...], gmb_ref[...])
    gv_ref[...] = vn
    for g in range(D_GMLP // GMLP_GW):
        lo, hi = g * GMLP_GW, (g + 1) * GMLP_GW
        sv = ws_ref[g, 0:1, 0:1] * vn[:, lo:hi] + bst_ref[0:1, g:g + 1]
        mix_ref[:, D_POOL + lo:D_POOL + hi] = u[:, lo:hi] * sv


def _odd_sample(proj, pool_st, w_pool, pool_scale, gm_g, gm_b, w_spatial, bs_t, mixin, pool_prev, j):
    nb = DEC_BATCH
    rb = MP // nb
    st_blk = (None, POOL_BUF, nb, D_POOL)
    in_specs = [
        pl.BlockSpec((nb, D_POOL), lambda i: (rb, 0)),
        pl.BlockSpec((nb, D_GMLP), lambda i: (rb, 1)),
        pl.BlockSpec((nb, D_GMLP), lambda i: (rb, 2)),
        pl.BlockSpec(st_blk, lambda i: (j, 0, 0, 0)),
        pl.BlockSpec((None, 4, POOL_GW, POOL_GW), lambda i: (j, 0, 0, 0)),
        pl.BlockSpec((None, 1, D_POOL), lambda i: (j, 0, 0)),
        pl.BlockSpec((None, 1, D_GMLP), lambda i: (j, 0, 0)),
        pl.BlockSpec((None, 1, D_GMLP), lambda i: (j, 0, 0)),
        pl.BlockSpec((None, 4, GMLP_CHUNK, GMLP_CHUNK), lambda i: (j, 0, 0, 0)),
        pl.BlockSpec((None, GMLP_CHUNK, 4), lambda i: (j, 0, 0)),
        pl.BlockSpec(memory_space=pl.ANY),
    ]
    args = [proj, proj, proj, pool_st, w_pool, pool_scale, gm_g, gm_b, w_spatial, bs_t, mixin]
    aliases = {len(args) - 1: 0}
    if pool_prev is not None:
        in_specs.append(pl.BlockSpec(memory_space=pl.ANY))
        args.append(pool_prev)
        aliases[len(args) - 1] = 1
    est = 4 * (4 * POOL_BUF * nb * D_POOL + 2 * (3 * nb * 1024 + 4 * 256 * 256 + 4 * 128 * 128
                                                + nb * 2048 + nb * 1024))
    return pl.pallas_call(
        _odd_sample_kernel,
        grid=(1,),
        in_specs=in_specs,
        out_specs=[
            pl.BlockSpec((nb, D_MODEL), lambda i: (rb, 0)),
            pl.BlockSpec(st_blk, lambda i: (j, 0, 0, 0)),
            pl.BlockSpec((nb, D_GMLP), lambda i: (0, 0)),
        ],
        out_shape=[
            jax.ShapeDtypeStruct((M, D_MODEL), F32),
            jax.ShapeDtypeStruct(pool_st.shape, F32),
            jax.ShapeDtypeStruct((nb, D_GMLP), F32),
        ],
        input_output_aliases=aliases,
        compiler_params=pltpu.CompilerParams(
            dimension_semantics=("arbitrary",), vmem_limit_bytes=_vmem_limit(est)),
        name="odd_sample",
    )(*args)


def kernel(x_prompt, x_sample, state_conv, state_mlstm_C, state_mlstm_n, state_mlstm_m, state_pool,
           ln_g, ln_b, w_ffn_in, w_ffn_out, w_in_even, b_gates_even, w_conv, mh_norm_g, w_out_even,
           w_in_odd, w_pool, pool_scale, gm_ln_g, gm_ln_b, w_spatial, b_spatial, w_out_odd):
    n_even, n_odd = w_in_even.shape[0], w_in_odd.shape[0]
    x = jnp.concatenate([x_prompt.reshape(MP, D_MODEL), x_sample.reshape(DEC_BATCH, D_MODEL)], axis=0)

    ln_g3 = ln_g.reshape(DEPTH * 3, 1, D_MODEL)
    ln_b3 = ln_b.reshape(DEPTH * 3, 1, D_MODEL)
    w_even_t = jnp.swapaxes(w_in_even, 1, 2)
    w_gate_t = jnp.pad(w_even_t[:, EVEN_MAIN:, :], ((0, 0), (0, LANES - 2 * HEADS), (0, 0)))
    pool_st = jnp.swapaxes(state_pool, 1, 2)
    mh_g3 = mh_norm_g.reshape(n_even, 1, HEADS * DV)
    conv_st = state_conv.reshape(n_even, DEC_BATCH, (CONV_W - 1) * D_CONV)
    n_st = state_mlstm_n.reshape(n_even, DEC_BATCH, HEADS * DK)
    scale3 = pool_scale.reshape(n_odd, 1, D_POOL)
    gm_g3 = gm_ln_g.reshape(n_odd, 1, D_GMLP)
    gm_b3 = gm_ln_b.reshape(n_odd, 1, D_GMLP)
    bs_t = jnp.swapaxes(b_spatial, 1, 2)

    conv_p, conv_s, c_p, n_p, n_s, m_p, m_s = [], [], [], [], [], [], []
    pool_p, gv_p, gv_s = [], [], []
    c_s = None
    pool_s = None

    for layer in range(DEPTH):
        j = layer // 2
        x = _ffn(x, w_ffn_in, w_ffn_out, ln_g3, ln_b3, layer, 0, 3 * layer)
        if layer % 2 == 0:
            proj, gates = _proj(x, w_even_t, j, 0, EVEN_MAIN, "proj_even", w_transposed=True,
                                w_gate_t=w_gate_t)
            proj = _proj(x, w_even_t, j, 1, EVEN_MAIN, "proj_even", w_transposed=True, prev=proj)
            mixin, cv, cc, nn, mm = _even_prompt(proj, gates, w_conv, b_gates_even, mh_g3, j)
            conv_p.append(cv)
            c_p.append(cc)
            n_p.append(nn)
            m_p.append(mm.reshape(BATCH, HEADS))
            qk_cols = _qk_cols(x, w_even_t, j)
            c_s, num = _even_sample_state(state_mlstm_C, qk_cols, proj, gates, state_mlstm_m,
                                          b_gates_even, j, c_s)
            mixin, cvs, nns, mms = _even_sample(proj, gates, num, conv_st, n_st, state_mlstm_m,
                                                w_conv, b_gates_even, mh_g3, mixin, j)
            conv_s.append(cvs.reshape(DEC_BATCH, CONV_W - 1, D_CONV))
            n_s.append(nns.reshape(DEC_BATCH, HEADS, DK))
            m_s.append(mms)
            x = _outproj(x, mixin, w_out_even, ln_g3, ln_b3, j, 3 * layer + 1)
        else:
            proj = _proj(x, w_in_odd, j, 0, ODD_IN, "proj_odd")
            mixin, pp, gv = _odd_prompt(proj, w_pool, scale3, gm_g3, gm_b3, w_spatial, bs_t, j)
            pool_p.append(pp)
            gv_p.append(gv)
            mixin, pool_s, gvs = _odd_sample(proj, pool_st, w_pool, scale3, gm_g3, gm_b3, w_spatial, bs_t,
                                             mixin, pool_s, j)
            gv_s.append(gvs.reshape(DEC_BATCH, 1, D_GMLP))
            x = _outproj(x, mixin, w_out_odd, ln_g3, ln_b3, j, 3 * layer + 1)
        x = _ffn(x, w_ffn_in, w_ffn_out, ln_g3, ln_b3, layer, 1, 3 * layer + 2,
                 split_out=layer == DEPTH - 1)

    y_prompt = x[0].reshape(BATCH, SEQ, D_MODEL)
    y_sample = x[1].reshape(DEC_BATCH, 1, D_MODEL)
    return (y_prompt, y_sample,
            jnp.stack(conv_p), jnp.stack(conv_s),
            jnp.stack(c_p), c_s,
            jnp.stack(n_p), jnp.stack(n_s),
            jnp.stack(m_p), jnp.stack(m_s),
            jnp.stack(pool_p), jnp.swapaxes(pool_s, 1, 2),
            jnp.stack(gv_p), jnp.stack(gv_s))
```

```python
import functools

import jax
import jax.numpy as jnp
from jax import lax
from jax.experimental import pallas as pl
from jax.experimental.pallas import tpu as pltpu

F32 = jnp.float32

D_MODEL = 2048
BATCH = 4
SEQ = 2048
DEPTH = 4
DEC_BATCH = 128
PAST_LEN = 16384
D_FF = 5632
D_CONV = 1024
CONV_W = 3
HEADS = 4
DK = 128
DV = 256
D_POOL = 1024
POOL_WINDOWS = (2, 4, 8, 16)
POOL_GW = 256
POOL_BUF = 15
D_GMLP = 1024
GMLP_GW = 256
GMLP_CHUNK = 128
ALPHA = (2 * DEPTH) ** 0.25
LN_EPS = 1e-5
EVEN_MAIN = 3 * D_CONV + 2 * HEADS * DK + 2 * HEADS * DV
ODD_IN = D_POOL + 2 * D_GMLP

MP = BATCH * SEQ
M = MP + DEC_BATCH

LANES = 128
SUBLANES = 8
VMEM_BYTES_V7X = 64 * 1024 * 1024
VMEM_RESERVED_BYTES = 4 * 1024 * 1024
VMEM_TEMP_BYTES = 8 * 1024 * 1024

TM = 832
TF = 512
TFS = 256
TN = 512
TMP = 520
PROJ_COLS = 3072
TMO = 416
LC = 256
RE = 256
HEAD_GROUP = 4
CONV_PARTS = 8
RO = 256
SB = 16
STATE_GROUP = 4
HIST = 16


def _vmem_limit(block_bytes):
    return int(min(VMEM_BYTES_V7X - VMEM_RESERVED_BYTES, block_bytes + VMEM_TEMP_BYTES))


def _layer_norm(y, g, b, eps=LN_EPS):
    mu = jnp.mean(y, axis=-1, keepdims=True)
    yc = y - mu
    var = jnp.mean(yc * yc, axis=-1, keepdims=True)
    out = yc * lax.rsqrt(var + eps) * g
    if b is not None:
        out = out + b
    return out


def _log_sigmoid(x):
    return -(jnp.maximum(-x, 0.0) + jnp.log1p(jnp.exp(-jnp.abs(x))))


def _dot(a, b):
    return jnp.dot(a, b, preferred_element_type=F32)


def _dot_nt(a, b):
    return lax.dot_general(a, b, (((1,), (1,)), ((), ())), preferred_element_type=F32)


def _ffn_kernel(x_ref, w_in_hbm, w_out_hbm, g_ref, b_ref, o_ref, *rest, nf, layer, which):
    *maybe_sample_ref, wg_buf, wu_buf, wo_buf, sem = rest
    i = pl.program_id(0)
    f = pl.program_id(1)
    n_steps = (M // TM) * nf
    step = i * nf + f
    slot = step % 2

    def weight_copies(chunk, s):
        col = pl.multiple_of(chunk * TF, TF)
        up_col = pl.multiple_of(D_FF + chunk * TF, TF)
        return (
            pltpu.make_async_copy(w_in_hbm.at[layer, which, :, pl.ds(col, TF)], wg_buf.at[s], sem.at[0, s]),
            pltpu.make_async_copy(w_in_hbm.at[layer, which, :, pl.ds(up_col, TF)], wu_buf.at[s], sem.at[1, s]),
            pltpu.make_async_copy(w_out_hbm.at[layer, which, pl.ds(col, TF), :], wo_buf.at[s], sem.at[2, s]),
        )

    def start_weights(chunk, s):
        for k, copy in enumerate(weight_copies(chunk, s)):
            copy.start(priority=k % 2)

    @pl.when(step == 0)
    def _():
        start_weights(0, 0)

    @pl.when(step + 1 < n_steps)
    def _():
        start_weights(jnp.where(f + 1 == nf, 0, f + 1), 1 - slot)

    for copy in weight_copies(f, slot):
        copy.wait()

    wg_ref = wg_buf.at[slot]
    wu_ref = wu_buf.at[slot]
    wo_ref = wo_buf.at[slot]

    def accumulate(first_chunk):
        x = x_ref[...]
        for c in range(TF // TFS):
            c0, c1 = c * TFS, (c + 1) * TFS
            gate = _dot(x, wg_ref[:, c0:c1])
            up = _dot(x, wu_ref[:, c0:c1])
            h = gate * jax.nn.sigmoid(gate) * up
            for n in range(D_MODEL // TN):
                n0, n1 = n * TN, (n + 1) * TN
                part = _dot(h, wo_ref[c0:c1, n0:n1])
                if first_chunk and c == 0:
                    o_ref[:, n0:n1] = (2.0 * ALPHA) * x_ref[:, n0:n1] + part
                else:
                    o_ref[:, n0:n1] += part

    @pl.when(f == 0)
    def _():
        accumulate(True)

    @pl.when(f > 0)
    def _():
        accumulate(False)

    @pl.when(f == nf - 1)
    def _():
        o_ref[...] = _layer_norm(o_ref[...], g_ref[...], b_ref[...], eps=4.0 * LN_EPS)

    if maybe_sample_ref:
        @pl.when((f == nf - 1) & (pl.program_id(0) == M // TM - 1))
        def _():
            maybe_sample_ref[0][...] = o_ref[TM - DEC_BATCH:, :]


def _ffn(x, w_ffn_in, w_ffn_out, ln_g, ln_b, layer, which, ln_idx, split_out=False):
    nf = D_FF // TF
    est = 4 * (4 * TM * D_MODEL + 2 * 3 * D_MODEL * TF + 4 * TM * TFS + TM * TN)
    out_specs = pl.BlockSpec((TM, D_MODEL), lambda i, f: (i, 0))
    out_shape = jax.ShapeDtypeStruct((M, D_MODEL), F32)
    if split_out:
        out_specs = [out_specs, pl.BlockSpec((DEC_BATCH, D_MODEL), lambda i, f: (0, 0))]
        out_shape = [jax.ShapeDtypeStruct((MP, D_MODEL), F32),
                     jax.ShapeDtypeStruct((DEC_BATCH, D_MODEL), F32)]
    return pl.pallas_call(
        functools.partial(_ffn_kernel, nf=nf, layer=layer, which=which),
        grid=(M // TM, nf),
        in_specs=[
            pl.BlockSpec((TM, D_MODEL), lambda i, f: (i, 0)),
            pl.BlockSpec(memory_space=pl.ANY),
            pl.BlockSpec(memory_space=pl.ANY),
            pl.BlockSpec((None, 1, D_MODEL), lambda i, f: (ln_idx, 0, 0)),
            pl.BlockSpec((None, 1, D_MODEL), lambda i, f: (ln_idx, 0, 0)),
        ],
        out_specs=out_specs,
        out_shape=out_shape,
        scratch_shapes=[
            pltpu.VMEM((2, D_MODEL, TF), F32),
            pltpu.VMEM((2, D_MODEL, TF), F32),
            pltpu.VMEM((2, TF, D_MODEL), F32),
            pltpu.SemaphoreType.DMA((3, 2)),
        ],
        compiler_params=pltpu.CompilerParams(
            dimension_semantics=("arbitrary", "arbitrary"), vmem_limit_bytes=_vmem_limit(est)),
        name="ffn_ln",
    )(x, w_ffn_in, w_ffn_out, ln_g, ln_b)


def _proj_kernel(x_ref, w_ref, *rest, w_transposed, with_gates, aliased):
    rest = rest[1:] if aliased else rest
    if with_gates:
        wgate_ref, o_ref, gate_ref = rest
        gate_ref[...] = _dot_nt(x_ref[...], wgate_ref[...])
    else:
        (o_ref,) = rest
    for n in range(PROJ_COLS // TN):
        lo, hi = n * TN, (n + 1) * TN
        if w_transposed:
            o_ref[:, lo:hi] = _dot_nt(x_ref[...], w_ref[lo:hi, :])
        else:
            o_ref[:, lo:hi] = _dot(x_ref[...], w_ref[:, lo:hi])


def _proj(x, w, layer_idx, col_block, n_total, name, w_transposed=False, w_gate_t=None, prev=None):
    est = 4 * (PROJ_COLS * D_MODEL + 2 * TMP * D_MODEL + 2 * TMP * PROJ_COLS + 2 * TMP * TN)
    if w_transposed:
        w_spec = pl.BlockSpec((None, PROJ_COLS, D_MODEL), lambda i: (layer_idx, col_block, 0),
                              pipeline_mode=pl.Buffered(1))
    else:
        w_spec = pl.BlockSpec((None, D_MODEL, PROJ_COLS), lambda i: (layer_idx, 0, col_block),
                              pipeline_mode=pl.Buffered(1))
    in_specs = [pl.BlockSpec((TMP, D_MODEL), lambda i: (i, 0)), w_spec]
    args = [x, w]
    aliases = {}
    if prev is not None:
        in_specs.append(pl.BlockSpec(memory_space=pl.ANY))
        args.append(prev)
        aliases = {2: 0}
    out_specs = pl.BlockSpec((TMP, PROJ_COLS), lambda i: (i, col_block))
    out_shape = jax.ShapeDtypeStruct((M, n_total), F32)
    if w_gate_t is not None:
        in_specs.append(pl.BlockSpec((None, LANES, D_MODEL), lambda i: (layer_idx, 0, 0)))
        args.append(w_gate_t)
        out_specs = [out_specs, pl.BlockSpec((TMP, LANES), lambda i: (i, 0))]
        out_shape = [out_shape, jax.ShapeDtypeStruct((M, LANES), F32)]
        est += 4 * (2 * LANES * D_MODEL + 3 * TMP * LANES)
    return pl.pallas_call(
        functools.partial(_proj_kernel, w_transposed=w_transposed, with_gates=w_gate_t is not None,
                          aliased=prev is not None),
        grid=(M // TMP,),
        in_specs=in_specs,
        out_specs=out_specs,
        out_shape=out_shape,
        input_output_aliases=aliases,
        compiler_params=pltpu.CompilerParams(
            dimension_semantics=("parallel",), vmem_limit_bytes=_vmem_limit(est)),
        name=name,
    )(*args)


def _outproj_kernel(x_ref, a_ref, w_ref, g_ref, b_ref, o_ref):
    y = ALPHA * x_ref[...] + _dot(a_ref[...], w_ref[...])
    o_ref[...] = _layer_norm(y, g_ref[...], b_ref[...])


def _outproj(x, mix, w_out, ln_g, ln_b, layer_idx, ln_idx):
    est = 4 * (D_MODEL * D_MODEL + 9 * TMO * D_MODEL)
    return pl.pallas_call(
        _outproj_kernel,
        grid=(M // TMO,),
        in_specs=[
            pl.BlockSpec((TMO, D_MODEL), lambda i: (i, 0)),
            pl.BlockSpec((TMO, D_MODEL), lambda i: (i, 0)),
            pl.BlockSpec((None, D_MODEL, D_MODEL), lambda i: (layer_idx, 0, 0),
                         pipeline_mode=pl.Buffered(1)),
            pl.BlockSpec((None, 1, D_MODEL), lambda i: (ln_idx, 0, 0)),
            pl.BlockSpec((None, 1, D_MODEL), lambda i: (ln_idx, 0, 0)),
        ],
        out_specs=pl.BlockSpec((TMO, D_MODEL), lambda i: (i, 0)),
        out_shape=jax.ShapeDtypeStruct((M, D_MODEL), F32),
        compiler_params=pltpu.CompilerParams(
            dimension_semantics=("parallel",), vmem_limit_bytes=_vmem_limit(est)),
        name="outproj_ln",
    )(x, mix, w_out, ln_g, ln_b)


def _split3(x):
    h1 = x.astype(jnp.bfloat16).astype(F32)
    r = x - h1
    h2 = r.astype(jnp.bfloat16).astype(F32)
    return h1, h2, r - h2


def _even_prompt_kernel(bg_ref, cg_ref, xin_ref, q_ref, k_ref, v_ref, og_ref, gt_ref,
                        wc_ref, bgate_ref, mhg_ref,
                        mix_ref, conv_ref, c_out_ref, n_out_ref, m_out_ref,
                        cbuf, c_sc, n_sc, m_sc):
    c = pl.program_id(1)
    L = LC
    R = RE

    @pl.when(c == 0)
    def _():
        cbuf[0:SUBLANES, :] = jnp.zeros((SUBLANES, D_CONV), F32)
        c_sc[...] = jnp.zeros_like(c_sc)
        n_sc[...] = jnp.zeros_like(n_sc)
        m_sc[...] = jnp.zeros_like(m_sc)

    def conv_piece(lo, hi):
        cx = cg_ref[:, lo:hi] * xin_ref[:, lo:hi]
        cbuf[SUBLANES:SUBLANES + R, lo:hi] = cx
        c1 = cbuf[SUBLANES - 1:SUBLANES - 1 + R, lo:hi]
        c2 = cbuf[SUBLANES - 2:SUBLANES - 2 + R, lo:hi]
        conv = c2 * wc_ref[0:1, lo:hi] + c1 * wc_ref[1:2, lo:hi] + cx * wc_ref[2:3, lo:hi]
        mix_ref[:, lo:hi] = bg_ref[:, lo:hi] * conv
        conv_ref[:, lo:hi] = cbuf[SUBLANES + R - 2:SUBLANES + R, lo:hi]
        cbuf[0:SUBLANES, lo:hi] = cbuf[R:R + SUBLANES, lo:hi]

    conv_todo = [functools.partial(conv_piece, p * (D_CONV // CONV_PARTS), (p + 1) * (D_CONV // CONV_PARTS))
                 for p in range(CONV_PARTS)]

    def conv_step():
        if conv_todo:
            conv_todo.pop(0)()

    row = lax.broadcasted_iota(jnp.int32, (L, L), 0)
    col = lax.broadcasted_iota(jnp.int32, (L, L), 1)
    causal = row >= col
    tril = causal.astype(F32)
    c_state = [c_sc[h] for h in range(HEADS)]
    n_state = [n_sc[h:h + 1, :] for h in range(HEADS)]
    m_state = [m_sc[h, 0:1, 0:1] for h in range(HEADS)]

    for s in range(R // L):
        r0, r1 = s * L, (s + 1) * L
        gt = gt_ref[r0:r1, :]
        li_all = gt[:, 0:HEADS] + bgate_ref[0:1, :]
        lf_all = _log_sigmoid(gt[:, HEADS:2 * HEADS] + bgate_ref[1:2, :])
        f1, f2, f3 = _split3(lf_all)
        b_all = _dot(tril, f1) + _dot(tril, f2) + _dot(tril, f3)
        z = jnp.concatenate([li_all, b_all, jnp.zeros((L, LANES - 2 * HEADS), F32)], axis=1)
        zt = z.T

        for g0 in range(0, HEADS, HEAD_GROUP):
            hs = range(g0, g0 + HEAD_GROUP)
            li_c = {h: li_all[:, h:h + 1] for h in hs}
            b_c = {h: b_all[:, h:h + 1] for h in hs}
            m0 = {h: m_state[h] for h in hs}
            e = {h: jnp.where(causal, zt[h:h + 1, :] - zt[HEADS + h:HEADS + h + 1, :], -jnp.inf)
                 for h in hs}
            inter = {h: b_c[h] + m0[h] for h in hs}
            m_t = {h: jnp.maximum(inter[h], b_c[h] + jnp.max(e[h], axis=-1, keepdims=True)) for h in hs}
            conv_step()
            a_int = {h: jnp.exp(inter[h] - m_t[h]) for h in hs}
            qh = {h: q_ref[r0:r1, h * DK:(h + 1) * DK] for h in hs}
            kh = {h: k_ref[r0:r1, h * DK:(h + 1) * DK] * (DK ** -0.5) for h in hs}
            vh = {h: v_ref[r0:r1, h * DV:(h + 1) * DV] for h in hs}
            w = {h: jnp.exp(e[h] + (b_c[h] - m_t[h])) * _dot_nt(qh[h], kh[h]) for h in hs}
            conv_step()
            num = {h: a_int[h] * _dot(qh[h], c_state[h]) + _dot(w[h], vh[h]) for h in hs}
            w_rows = {h: jnp.sum(sum(w[h][:, t * LANES:(t + 1) * LANES] for t in range(L // LANES)),
                                 axis=-1, keepdims=True) for h in hs}
            conv_step()
            den = {h: a_int[h] * jnp.sum(qh[h] * n_state[h], axis=-1, keepdims=True) + w_rows[h]
                   for h in hs}
            hh = {h: num[h] / jnp.maximum(jnp.abs(den[h]), jnp.exp(-m_t[h])) for h in hs}
            conv_step()
            m_new = {h: m_t[h][L - 1:L, :] for h in hs}
            b_last = {h: b_c[h][L - 1:L, :] for h in hs}
            w_end = {h: jnp.exp(b_last[h] - b_c[h] + li_c[h] - m_new[h]) for h in hs}
            decay = {h: jnp.exp(b_last[h] + m0[h] - m_new[h]) for h in hs}
            wk = {h: w_end[h] * kh[h] for h in hs}
            conv_step()
            c_new = {h: decay[h] * c_state[h] + _dot(wk[h].T, vh[h]) for h in hs}
            conv_step()
            n_new = {h: decay[h] * n_state[h] + jnp.sum(wk[h], axis=0, keepdims=True) for h in hs}
            conv_step()
            hn = {h: _layer_norm(hh[h], mhg_ref[0:1, h * DV:(h + 1) * DV], None) for h in hs}
            conv_step()
            for h in hs:
                c_state[h], n_state[h], m_state[h] = c_new[h], n_new[h], m_new[h]
                og = og_ref[r0:r1, h * DV:(h + 1) * DV]
                mix_ref[r0:r1, D_CONV + h * DV:D_CONV + (h + 1) * DV] = hn[h] * jax.nn.sigmoid(og)

    while conv_todo:
        conv_step()

    for h in range(HEADS):
        c_sc[h] = c_state[h]
        n_sc[h:h + 1, :] = n_state[h]
        m_sc[h] = jnp.broadcast_to(m_state[h], (SUBLANES, LANES))
        c_out_ref[h] = c_state[h]
        n_out_ref[h:h + 1, :] = n_state[h]
        m_out_ref[0:1, h:h + 1] = m_state[h]


def _even_prompt(proj, gates, w_conv, b_gates, mh_g, j):
    nck = SEQ // RE
    r = lambda b, c: b * nck + c
    est = 4 * (2 * (5 * RE * 1024 + 2 * RE * 512 + RE * 128) + 2 * RE * 2048 + 4 * RE * 1024
               + 3 * HEADS * DK * DV + 24 * LC * LC)
    return pl.pallas_call(
        _even_prompt_kernel,
        grid=(BATCH, nck),
        in_specs=[
            pl.BlockSpec((RE, D_CONV), lambda b, c: (r(b, c), 0)),
            pl.BlockSpec((RE, D_CONV), lambda b, c: (r(b, c), 1)),
            pl.BlockSpec((RE, D_CONV), lambda b, c: (r(b, c), 2)),
            pl.BlockSpec((RE, HEADS * DK), lambda b, c: (r(b, c), 6)),
            pl.BlockSpec((RE, HEADS * DK), lambda b, c: (r(b, c), 7)),
            pl.BlockSpec((RE, HEADS * DV), lambda b, c: (r(b, c), 4)),
            pl.BlockSpec((RE, HEADS * DV), lambda b, c: (r(b, c), 5)),
            pl.BlockSpec((RE, LANES), lambda b, c: (r(b, c), 0)),
            pl.BlockSpec((None, CONV_W, D_CONV), lambda b, c: (j, 0, 0)),
            pl.BlockSpec((None, 2, HEADS), lambda b, c: (j, 0, 0)),
            pl.BlockSpec((None, 1, HEADS * DV), lambda b, c: (j, 0, 0)),
        ],
        out_specs=[
            pl.BlockSpec((RE, D_MODEL), lambda b, c: (r(b, c), 0)),
            pl.BlockSpec((None, CONV_W - 1, D_CONV), lambda b, c: (b, 0, 0)),
            pl.BlockSpec((None, HEADS, DK, DV), lambda b, c: (b, 0, 0, 0)),
            pl.BlockSpec((None, HEADS, DK), lambda b, c: (b, 0, 0)),
            pl.BlockSpec((None, 1, HEADS), lambda b, c: (b, 0, 0)),
        ],
        out_shape=[
            jax.ShapeDtypeStruct((M, D_MODEL), F32),
            jax.ShapeDtypeStruct((BATCH, CONV_W - 1, D_CONV), F32),
            jax.ShapeDtypeStruct((BATCH, HEADS, DK, DV), F32),
            jax.ShapeDtypeStruct((BATCH, HEADS, DK), F32),
            jax.ShapeDtypeStruct((BATCH, 1, HEADS), F32),
        ],
        scratch_shapes=[
            pltpu.VMEM((RE + SUBLANES, D_CONV), F32),
            pltpu.VMEM((HEADS, DK, DV), F32),
            pltpu.VMEM((HEADS, DK), F32),
            pltpu.VMEM((HEADS, SUBLANES, LANES), F32),
        ],
        compiler_params=pltpu.CompilerParams(
            dimension_semantics=("arbitrary", "arbitrary"), vmem_limit_bytes=_vmem_limit(est)),
        name="even_prompt",
    )(proj, proj, proj, proj, proj, proj, proj, gates, w_conv, b_gates, mh_g)


def _sample_gates(gt, bgate_ref, m):
    li = gt[:, 0:HEADS] + bgate_ref[0:1, :]
    lf = _log_sigmoid(gt[:, HEADS:2 * HEADS] + bgate_ref[1:2, :])
    inter = lf + m
    m_t = jnp.maximum(inter, li)
    return jnp.exp(inter - m_t), jnp.exp(li - m_t), m_t


def _qk_cols_kernel(x_ref, w_ref, o_ref):
    t = _dot_nt(w_ref[...], x_ref[...])
    for s in range(DEC_BATCH // SB):
        o_ref[s] = t[:, s * SB:(s + 1) * SB]


def _qk_cols(x, w_even_t, j):
    nqk = 2 * HEADS * DK
    est = 4 * (2 * DEC_BATCH * D_MODEL + 2 * nqk * D_MODEL + 3 * nqk * LANES
               + 2 * (DEC_BATCH // SB) * nqk * LANES)
    return pl.pallas_call(
        _qk_cols_kernel,
        grid=(1,),
        in_specs=[
            pl.BlockSpec((DEC_BATCH, D_MODEL), lambda i: (MP // DEC_BATCH, 0)),
            pl.BlockSpec((None, nqk, D_MODEL), lambda i: (j, 3 * D_CONV // nqk, 0)),
        ],
        out_specs=pl.BlockSpec((DEC_BATCH // SB, nqk, SB), lambda i: (0, 0, 0)),
        out_shape=jax.ShapeDtypeStruct((DEC_BATCH // SB, nqk, SB), F32),
        compiler_params=pltpu.CompilerParams(
            dimension_semantics=("arbitrary",), vmem_limit_bytes=_vmem_limit(est)),
        name="qk_cols",
    )(x, w_even_t)


def _even_sample_state_kernel(c_ref, qk_ref, v_ref, gt_ref, m_ref, bgate_ref, *rest):
    c_out_ref, num_ref = rest[-2], rest[-1]
    a, wgt, _ = _sample_gates(gt_ref[...], bgate_ref, m_ref[...])
    for b0 in range(0, SB, STATE_GROUP):
        ids = [(bi, h) for bi in range(b0, b0 + STATE_GROUP) for h in range(HEADS)]
        a_s = {k: a[k[0]:k[0] + 1, k[1]:k[1] + 1] for k in ids}
        w_s = {k: wgt[k[0]:k[0] + 1, k[1]:k[1] + 1] for k in ids}
        qc = {k: qk_ref[k[1] * DK:(k[1] + 1) * DK, k[0]:k[0] + 1] for k in ids}
        kc = {k: qk_ref[(HEADS + k[1]) * DK:(HEADS + k[1] + 1) * DK, k[0]:k[0] + 1] * (DK ** -0.5)
              for k in ids}
        wk = {k: w_s[k] * kc[k] for k in ids}
        vr = {k: v_ref[k[0]:k[0] + 1, k[1] * DV:(k[1] + 1) * DV] for k in ids}
        for k in ids:
            ch = c_ref[k[0], k[1]]
            c_out_ref[k[0], k[1]] = a_s[k] * ch + wk[k] * vr[k]
            num_ref[k[0]:k[0] + 1, k[1] * DV:(k[1] + 1) * DV] = jnp.sum(qc[k] * ch, axis=0, keepdims=True)


def _even_sample_state(state_c, qk_cols, proj, gates, state_m, b_gates, j, c_prev):
    blk5 = (None, SB, HEADS, DK, DV)
    rb = MP // SB
    in_specs = [
        pl.BlockSpec(blk5, lambda i: (j, i, 0, 0, 0)),
        pl.BlockSpec((None, 2 * HEADS * DK, SB), lambda i: (i, 0, 0)),
        pl.BlockSpec((SB, HEADS * DV), lambda i: (rb + i, 4)),
        pl.BlockSpec((SB, LANES), lambda i: (rb + i, 0)),
        pl.BlockSpec((None, SB, HEADS), lambda i: (j, i, 0)),
        pl.BlockSpec((None, 2, HEADS), lambda i: (j, 0, 0)),
    ]
    args = [state_c, qk_cols, proj, gates, state_m, b_gates]
    aliases = {}
    if c_prev is not None:
        in_specs.append(pl.BlockSpec(memory_space=pl.ANY))
        args.append(c_prev)
        aliases = {len(args) - 1: 0}
    est = 4 * (4 * SB * HEADS * DK * DV + 4 * HEADS * DK * LANES)
    return pl.pallas_call(
        _even_sample_state_kernel,
        grid=(DEC_BATCH // SB,),
        in_specs=in_specs,
        out_specs=[
            pl.BlockSpec(blk5, lambda i: (j, i, 0, 0, 0)),
            pl.BlockSpec((SB, HEADS * DV), lambda i: (i, 0)),
        ],
        out_shape=[
            jax.ShapeDtypeStruct(state_c.shape, F32),
            jax.ShapeDtypeStruct((DEC_BATCH, HEADS * DV), F32),
        ],
        input_output_aliases=aliases,
        compiler_params=pltpu.CompilerParams(
            dimension_semantics=("arbitrary",), vmem_limit_bytes=_vmem_limit(est)),
        name="even_sample_state",
    )(*args)


def _even_sample_kernel(bg_ref, cg_ref, xin_ref, q_ref, k_ref, v_ref, og_ref, gt_ref,
                        num_ref, cst_ref, n_ref, m_ref, wc_ref, bgate_ref, mhg_ref, mixin_hbm,
                        mix_ref, conv_ref, n_out_ref, m_out_ref):
    del mixin_hbm
    cx = cg_ref[...] * xin_ref[...]
    st0 = cst_ref[:, 0:D_CONV]
    st1 = cst_ref[:, D_CONV:2 * D_CONV]
    conv = st0 * wc_ref[0:1, :] + st1 * wc_ref[1:2, :] + cx * wc_ref[2:3, :]
    mix_ref[:, 0:D_CONV] = bg_ref[...] * conv
    conv_ref[:, 0:D_CONV] = st1
    conv_ref[:, D_CONV:2 * D_CONV] = cx

    a, wgt, m_t = _sample_gates(gt_ref[...], bgate_ref, m_ref[...])
    m_out_ref[...] = m_t
    floor = jnp.exp(-m_t)
    for h in range(HEADS):
        a_h = a[:, h:h + 1]
        w_h = wgt[:, h:h + 1]
        qh = q_ref[:, h * DK:(h + 1) * DK]
        kh = k_ref[:, h * DK:(h + 1) * DK] * (DK ** -0.5)
        vh = v_ref[:, h * DV:(h + 1) * DV]
        nh = n_ref[:, h * DK:(h + 1) * DK]
        wt = w_h * jnp.sum(qh * kh, axis=-1, keepdims=True)
        num = a_h * num_ref[:, h * DV:(h + 1) * DV] + wt * vh
        den = a_h * jnp.sum(qh * nh, axis=-1, keepdims=True) + wt
        hh = num / jnp.maximum(jnp.abs(den), floor[:, h:h + 1])
        n_out_ref[:, h * DK:(h + 1) * DK] = a_h * nh + w_h * kh
        hn = _layer_norm(hh, mhg_ref[0:1, h * DV:(h + 1) * DV], None)
        og = og_ref[:, h * DV:(h + 1) * DV]
        mix_ref[:, D_CONV + h * DV:D_CONV + (h + 1) * DV] = hn * jax.nn.sigmoid(og)


def _even_sample(proj, gates, num, conv_st, n_st, m_st, w_conv, b_gates, mh_g, mixin, j):
    nb = DEC_BATCH
    rb = MP // nb
    est = 4 * 2 * (5 * nb * 1024 + 2 * nb * 512 + nb * 128 + nb * 1024 + nb * 2048 + nb * 512
                   + nb * 2048 + nb * 2048 + nb * 512)
    return pl.pallas_call(
        _even_sample_kernel,
        grid=(1,),
        in_specs=[
            pl.BlockSpec((nb, D_CONV), lambda i: (rb, 0)),
            pl.BlockSpec((nb, D_CONV), lambda i: (rb, 1)),
            pl.BlockSpec((nb, D_CONV), lambda i: (rb, 2)),
            pl.BlockSpec((nb, HEADS * DK), lambda i: (rb, 6)),
            pl.BlockSpec((nb, HEADS * DK), lambda i: (rb, 7)),
            pl.BlockSpec((nb, HEADS * DV), lambda i: (rb, 4)),
            pl.BlockSpec((nb, HEADS * DV), lambda i: (rb, 5)),
            pl.BlockSpec((nb, LANES), lambda i: (rb, 0)),
            pl.BlockSpec((nb, HEADS * DV), lambda i: (0, 0)),
            pl.BlockSpec((None, nb, 2 * D_CONV), lambda i: (j, 0, 0)),
            pl.BlockSpec((None, nb, HEADS * DK), lambda i: (j, 0, 0)),
            pl.BlockSpec((None, nb, HEADS), lambda i: (j, 0, 0)),
            pl.BlockSpec((None, CONV_W, D_CONV), lambda i: (j, 0, 0)),
            pl.BlockSpec((None, 2, HEADS), lambda i: (j, 0, 0)),
            pl.BlockSpec((None, 1, HEADS * DV), lambda i: (j, 0, 0)),
            pl.BlockSpec(memory_space=pl.ANY),
        ],
        out_specs=[
            pl.BlockSpec((nb, D_MODEL), lambda i: (rb, 0)),
            pl.BlockSpec((nb, 2 * D_CONV), lambda i: (0, 0)),
            pl.BlockSpec((nb, HEADS * DK), lambda i: (0, 0)),
            pl.BlockSpec((nb, HEADS), lambda i: (0, 0)),
        ],
        out_shape=[
            jax.ShapeDtypeStruct((M, D_MODEL), F32),
            jax.ShapeDtypeStruct((nb, 2 * D_CONV), F32),
            jax.ShapeDtypeStruct((nb, HEADS * DK), F32),
            jax.ShapeDtypeStruct((nb, HEADS), F32),
        ],
        input_output_aliases={15: 0},
        compiler_params=pltpu.CompilerParams(
            dimension_semantics=("arbitrary",), vmem_limit_bytes=_vmem_limit(est)),
        name="even_sample",
    )(proj, proj, proj, proj, proj, proj, proj, gates, num, conv_st, n_st, m_st,
      w_conv, b_gates, mh_g, mixin)


def _gmlp_norm(u_raw, v_raw, g, b):
    u = jax.nn.gelu(u_raw)
    vn = _layer_norm(jax.nn.gelu(v_raw), g, b)
    return u, vn


def _odd_prompt_kernel(p_ref, u_ref, v_ref, wp_ref, sc_ref, gmg_ref, gmb_ref, ws_ref, bst_ref,
                       mix_ref, pool_ref, gv_ref, pbuf, sbuf):
    s = pl.program_id(1)
    R = RO

    @pl.when(s == 0)
    def _():
        pbuf[0:HIST, :] = jnp.zeros((HIST, D_POOL), F32)
        sbuf[0:SUBLANES, :] = jnp.zeros((SUBLANES, D_POOL), F32)

    p = p_ref[...]
    pbuf[HIST:HIST + R, :] = p
    pos = s * R + lax.broadcasted_iota(jnp.int32, (R, 1), 0)
    for g, w in enumerate(POOL_WINDOWS):
        lo, hi = g * POOL_GW, (g + 1) * POOL_GW
        cur = pbuf[0:HIST + R, lo:hi]
        d = 1
        while d < w:
            sbuf[SUBLANES:SUBLANES + HIST + R, lo:hi] = cur
            cur = cur + sbuf[SUBLANES - d:SUBLANES - d + HIST + R, lo:hi]
            d *= 2
        win = cur[HIST:, :]
        cnt = jnp.minimum(w, pos + 1).astype(F32)
        diff = win / cnt - p[:, lo:hi]
        mix_ref[:, lo:hi] = _dot(diff, wp_ref[g]) * sc_ref[0:1, lo:hi]
    pool_ref[...] = pbuf[HIST + R - POOL_BUF:HIST + R, :]
    pbuf[0:HIST, :] = pbuf[R:R + HIST, :]

    u, vn = _gmlp_norm(u_ref[...], v_ref[...], gmg_ref[...], gmb_ref[...])
    L = GMLP_CHUNK
    tril = lax.broadcasted_iota(jnp.int32, (L, L), 0) >= lax.broadcasted_iota(jnp.int32, (L, L), 1)
    for g in range(D_GMLP // GMLP_GW):
        lo, hi = g * GMLP_GW, (g + 1) * GMLP_GW
        ws = jnp.where(tril, ws_ref[g], 0.0)
        bcol = bst_ref[:, g:g + 1]
        for ck in range(R // L):
            r0, r1 = ck * L, (ck + 1) * L
            sv = _dot(ws, vn[r0:r1, lo:hi]) + bcol
            mix_ref[r0:r1, D_POOL + lo:D_POOL + hi] = u[r0:r1, lo:hi] * sv
    gv_ref[...] = vn[R - L:R, :]


def _odd_prompt(proj, w_pool, pool_scale, gm_g, gm_b, w_spatial, bs_t, j):
    nrb = SEQ // RO
    r = lambda b, s: b * nrb + s
    est = 4 * (2 * 3 * RO * 1024 + 2 * RO * 2048 + 2 * 4 * 256 * 256 + 2 * 4 * 128 * 128
               + (RO + HIST) * 1024 + 8 * RO * 1024)
    return pl.pallas_call(
        _odd_prompt_kernel,
        grid=(BATCH, nrb),
        in_specs=[
            pl.BlockSpec((RO, D_POOL), lambda b, s: (r(b, s), 0)),
            pl.BlockSpec((RO, D_GMLP), lambda b, s: (r(b, s), 1)),
            pl.BlockSpec((RO, D_GMLP), lambda b, s: (r(b, s), 2)),
            pl.BlockSpec((None, 4, POOL_GW, POOL_GW), lambda b, s: (j, 0, 0, 0)),
            pl.BlockSpec((None, 1, D_POOL), lambda b, s: (j, 0, 0)),
            pl.BlockSpec((None, 1, D_GMLP), lambda b, s: (j, 0, 0)),
            pl.BlockSpec((None, 1, D_GMLP), lambda b, s: (j, 0, 0)),
            pl.BlockSpec((None, 4, GMLP_CHUNK, GMLP_CHUNK), lambda b, s: (j, 0, 0, 0)),
            pl.BlockSpec((None, GMLP_CHUNK, 4), lambda b, s: (j, 0, 0)),
        ],
        out_specs=[
            pl.BlockSpec((RO, D_MODEL), lambda b, s: (r(b, s), 0)),
            pl.BlockSpec((None, POOL_BUF, D_POOL), lambda b, s: (b, 0, 0)),
            pl.BlockSpec((None, GMLP_CHUNK, D_GMLP), lambda b, s: (b, 0, 0)),
        ],
        out_shape=[
            jax.ShapeDtypeStruct((M, D_MODEL), F32),
            jax.ShapeDtypeStruct((BATCH, POOL_BUF, D_POOL), F32),
            jax.ShapeDtypeStruct((BATCH, GMLP_CHUNK, D_GMLP), F32),
        ],
        scratch_shapes=[pltpu.VMEM((RO + HIST, D_POOL), F32),
                        pltpu.VMEM((SUBLANES + RO + HIST, D_POOL), F32)],
        compiler_params=pltpu.CompilerParams(
            dimension_semantics=("arbitrary", "arbitrary"), vmem_limit_bytes=_vmem_limit(est)),
        name="odd_prompt",
    )(proj, proj, proj, w_pool, pool_scale, gm_g, gm_b, w_spatial, bs_t)


def _odd_sample_kernel(p_ref, u_ref, v_ref, st_ref, wp_ref, sc_ref, gmg_ref, gmb_ref, ws_ref, bst_ref,
                       *rest):
    mix_ref, pool_ref, gv_ref = rest[-3:]
    p = p_ref[...]
    for r in range(POOL_BUF - 1):
        pool_ref[r] = st_ref[r + 1]
    pool_ref[POOL_BUF - 1] = p
    for g, w in enumerate(POOL_WINDOWS):
        lo, hi = g * POOL_GW, (g + 1) * POOL_GW
        win = p[:, lo:hi]
        for jj in range(1, w):
            win = win + st_ref[POOL_BUF - jj, :, lo:hi]
        cnt = float(min(w, PAST_LEN + 1))
        diff = win / cnt - p[:, lo:hi]
        mix_ref[:, lo:hi] = _dot(diff, wp_ref[g]) * sc_ref[0:1, lo:hi]
    u, vn = _gmlp_norm(u_ref[...], v_ref[...], gmg_ref[...], gmb_ref[...])
    gv_ref[...] = vn
    for g in range(D_GMLP // GMLP_GW):
        lo, hi = g * GMLP_GW, (g + 1) * GMLP_GW
        sv = ws_ref[g, 0:1, 0:1] * vn[:, lo:hi] + bst_ref[0:1, g:g + 1]
        mix_ref[:, D_POOL + lo:D_POOL + hi] = u[:, lo:hi] * sv


def _odd_sample(proj, pool_st, w_pool, pool_scale, gm_g, gm_b, w_spatial, bs_t, mixin, pool_prev, j):
    nb = DEC_BATCH
    rb = MP // nb
    st_blk = (None, POOL_BUF, nb, D_POOL)
    in_specs = [
        pl.BlockSpec((nb, D_POOL), lambda i: (rb, 0)),
        pl.BlockSpec((nb, D_GMLP), lambda i: (rb, 1)),
        pl.BlockSpec((nb, D_GMLP), lambda i: (rb, 2)),
        pl.BlockSpec(st_blk, lambda i: (j, 0, 0, 0)),
        pl.BlockSpec((None, 4, POOL_GW, POOL_GW), lambda i: (j, 0, 0, 0)),
        pl.BlockSpec((None, 1, D_POOL), lambda i: (j, 0, 0)),
        pl.BlockSpec((None, 1, D_GMLP), lambda i: (j, 0, 0)),
        pl.BlockSpec((None, 1, D_GMLP), lambda i: (j, 0, 0)),
        pl.BlockSpec((None, 4, GMLP_CHUNK, GMLP_CHUNK), lambda i: (j, 0, 0, 0)),
        pl.BlockSpec((None, GMLP_CHUNK, 4), lambda i: (j, 0, 0)),
        pl.BlockSpec(memory_space=pl.ANY),
    ]
    args = [proj, proj, proj, pool_st, w_pool, pool_scale, gm_g, gm_b, w_spatial, bs_t, mixin]
    aliases = {len(args) - 1: 0}
    if pool_prev is not None:
        in_specs.append(pl.BlockSpec(memory_space=pl.ANY))
        args.append(pool_prev)
        aliases[len(args) - 1] = 1
    est = 4 * (4 * POOL_BUF * nb * D_POOL + 2 * (3 * nb * 1024 + 4 * 256 * 256 + 4 * 128 * 128
                                                + nb * 2048 + nb * 1024))
    return pl.pallas_call(
        _odd_sample_kernel,
        grid=(1,),
        in_specs=in_specs,
        out_specs=[
            pl.BlockSpec((nb, D_MODEL), lambda i: (rb, 0)),
            pl.BlockSpec(st_blk, lambda i: (j, 0, 0, 0)),
            pl.BlockSpec((nb, D_GMLP), lambda i: (0, 0)),
        ],
        out_shape=[
            jax.ShapeDtypeStruct((M, D_MODEL), F32),
            jax.ShapeDtypeStruct(pool_st.shape, F32),
            jax.ShapeDtypeStruct((nb, D_GMLP), F32),
        ],
        input_output_aliases=aliases,
        compiler_params=pltpu.CompilerParams(
            dimension_semantics=("arbitrary",), vmem_limit_bytes=_vmem_limit(est)),
        name="odd_sample",
    )(*args)


def kernel(x_prompt, x_sample, state_conv, state_mlstm_C, state_mlstm_n, state_mlstm_m, state_pool,
           ln_g, ln_b, w_ffn_in, w_ffn_out, w_in_even, b_gates_even, w_conv, mh_norm_g, w_out_even,
           w_in_odd, w_pool, pool_scale, gm_ln_g, gm_ln_b, w_spatial, b_spatial, w_out_odd):
    n_even, n_odd = w_in_even.shape[0], w_in_odd.shape[0]
    x = jnp.concatenate([x_prompt.reshape(MP, D_MODEL), x_sample.reshape(DEC_BATCH, D_MODEL)], axis=0)

    ln_g3 = ln_g.reshape(DEPTH * 3, 1, D_MODEL)
    ln_b3 = ln_b.reshape(DEPTH * 3, 1, D_MODEL)
    w_even_t = jnp.swapaxes(w_in_even, 1, 2)
    w_gate_t = jnp.pad(w_even_t[:, EVEN_MAIN:, :], ((0, 0), (0, LANES - 2 * HEADS), (0, 0)))
    pool_st = jnp.swapaxes(state_pool, 1, 2)
    mh_g3 = mh_norm_g.reshape(n_even, 1, HEADS * DV)
    conv_st = state_conv.reshape(n_even, DEC_BATCH, (CONV_W - 1) * D_CONV)
    n_st = state_mlstm_n.reshape(n_even, DEC_BATCH, HEADS * DK)
    scale3 = pool_scale.reshape(n_odd, 1, D_POOL)
    gm_g3 = gm_ln_g.reshape(n_odd, 1, D_GMLP)
    gm_b3 = gm_ln_b.reshape(n_odd, 1, D_GMLP)
    bs_t = jnp.swapaxes(b_spatial, 1, 2)

    conv_p, conv_s, c_p, n_p, n_s, m_p, m_s = [], [], [], [], [], [], []
    pool_p, gv_p, gv_s = [], [], []
    c_s = None
    pool_s = None

    for layer in range(DEPTH):
        j = layer // 2
        x = _ffn(x, w_ffn_in, w_ffn_out, ln_g3, ln_b3, layer, 0, 3 * layer)
        if layer % 2 == 0:
            proj, gates = _proj(x, w_even_t, j, 0, EVEN_MAIN, "proj_even", w_transposed=True,
                                w_gate_t=w_gate_t)
            proj = _proj(x, w_even_t, j, 1, EVEN_MAIN, "proj_even", w_transposed=True, prev=proj)
            mixin, cv, cc, nn, mm = _even_prompt(proj, gates, w_conv, b_gates_even, mh_g3, j)
            conv_p.append(cv)
            c_p.append(cc)
            n_p.append(nn)
            m_p.append(mm.reshape(BATCH, HEADS))
            qk_cols = _qk_cols(x, w_even_t, j)
            c_s, num = _even_sample_state(state_mlstm_C, qk_cols, proj, gates, state_mlstm_m,
                                          b_gates_even, j, c_s)
            mixin, cvs, nns, mms = _even_sample(proj, gates, num, conv_st, n_st, state_mlstm_m,
                                                w_conv, b_gates_even, mh_g3, mixin, j)
            conv_s.append(cvs.reshape(DEC_BATCH, CONV_W - 1, D_CONV))
            n_s.append(nns.reshape(DEC_BATCH, HEADS, DK))
            m_s.append(mms)
            x = _outproj(x, mixin, w_out_even, ln_g3, ln_b3, j, 3 * layer + 1)
        else:
            proj = _proj(x, w_in_odd, j, 0, ODD_IN, "proj_odd")
            mixin, pp, gv = _odd_prompt(proj, w_pool, scale3, gm_g3, gm_b3, w_spatial, bs_t, j)
            pool_p.append(pp)
            gv_p.append(gv)
            mixin, pool_s, gvs = _odd_sample(proj, pool_st, w_pool, scale3, gm_g3, gm_b3, w_spatial, bs_t,
                                             mixin, pool_s, j)
            gv_s.append(gvs.reshape(DEC_BATCH, 1, D_GMLP))
            x = _outproj(x, mixin, w_out_odd, ln_g3, ln_b3, j, 3 * layer + 1)
        x = _ffn(x, w_ffn_in, w_ffn_out, ln_g3, ln_b3, layer, 1, 3 * layer + 2,
                 split_out=layer == DEPTH - 1)

    y_prompt = x[0].reshape(BATCH, SEQ, D_MODEL)
    y_sample = x[1].reshape(DEC_BATCH, 1, D_MODEL)
    return (y_prompt, y_sample,
            jnp.stack(conv_p), jnp.stack(conv_s),
            jnp.stack(c_p), c_s,
            jnp.stack(n_p), jnp.stack(n_s),
            jnp.stack(m_p), jnp.stack(m_s),
            jnp.stack(pool_p), jnp.swapaxes(pool_s, 1, 2),
            jnp.stack(gv_p), jnp.stack(gv_s))
```
